```python
import math
import jax, jax.numpy as jnp
from jax import lax
import numpy as np

D_MODEL = 2048
BATCH = 4
SEQ = 2048
DEPTH = 1

MIX_WIDTH = D_MODEL
ATTN_WIDTH = MIX_WIDTH // 2
POOL_WIDTH = MIX_WIDTH - ATTN_WIDTH
N_DIFF_HEADS = 8
DIFF_HEAD_DIM = ATTN_WIDTH // N_DIFF_HEADS // 2
DIFF_V_DIM = 2 * DIFF_HEAD_DIM
POOL_WINDOWS = (2, 4, 8, 16)
N_POOL_GROUPS = len(POOL_WINDOWS)
POOL_GROUP_DIM = POOL_WIDTH // N_POOL_GROUPS
IN_WIDTH = 3 * ATTN_WIDTH + POOL_WIDTH
N_EXPERT_GROUPS = 4
EXPERTS_PER_GROUP = 8
N_EXPERTS = N_EXPERT_GROUPS * EXPERTS_PER_GROUP
TOP_K_FINE = 2
D_FF_EXPERT = D_MODEL // 4
Q_BLOCK = 128
RMS_EPS = 1e-6
NEG_INF = -1e30

kernel_name = "hymba_diffattn_pool_hmoe_block"


def rmsnorm(x, g):
    xf = x.astype(jnp.float32)
    y = xf * lax.rsqrt(jnp.mean(xf * xf, axis=-1, keepdims=True) + RMS_EPS)
    return (y * g.astype(jnp.float32)).astype(x.dtype)


def alibi_slopes(n_heads):
    return jnp.exp2(-8.0 * jnp.arange(1, n_heads + 1, dtype=jnp.float32) / n_heads)


def diff_attention(q, k, v, lam, subln_g, lam_init):
    B, S = q.shape[0], q.shape[1]
    nb = S // Q_BLOCK
    q1 = q[:, :, :, 0].transpose(0, 2, 1, 3)
    q2 = q[:, :, :, 1].transpose(0, 2, 1, 3)
    k1 = k[:, :, :, 0].transpose(0, 2, 1, 3)
    k2 = k[:, :, :, 1].transpose(0, 2, 1, 3)
    vh = v.transpose(0, 2, 1, 3)
    scale = DIFF_HEAD_DIM ** -0.5
    slopes = alibi_slopes(N_DIFF_HEADS)
    key_pos = jnp.arange(S, dtype=jnp.int32)

    def to_blocks(t):
        return t.reshape(B, N_DIFF_HEADS, nb, Q_BLOCK, -1).transpose(2, 0, 1, 3, 4)

    def block(args):
        q1i, q2i, bi = args
        qpos = bi * Q_BLOCK + jnp.arange(Q_BLOCK, dtype=jnp.int32)
        dist = qpos[:, None] - key_pos[None, :]
        bias = jnp.where(dist[None] >= 0,
                         -slopes[:, None, None] * dist[None].astype(jnp.float32),
                         NEG_INF)
        s1 = jnp.einsum('bhqd,bhkd->bhqk', q1i, k1).astype(jnp.float32) * scale + bias
        s2 = jnp.einsum('bhqd,bhkd->bhqk', q2i, k2).astype(jnp.float32) * scale + bias
        a = jax.nn.softmax(s1, axis=-1) - lam * jax.nn.softmax(s2, axis=-1)
        return jnp.einsum('bhqk,bhkv->bhqv', a.astype(vh.dtype), vh)

    out = lax.map(block, (to_blocks(q1), to_blocks(q2), jnp.arange(nb, dtype=jnp.int32)))
    out = out.transpose(1, 2, 0, 3, 4).reshape(B, N_DIFF_HEADS, S, DIFF_V_DIM)
    out = rmsnorm(out, subln_g) * (1.0 - lam_init)
    return out.transpose(0, 2, 1, 3).reshape(B, S, ATTN_WIDTH)


def pool_mixer(u, pool_w, pool_scale):
    B, S = u.shape[0], u.shape[1]
    ug = u.reshape(B, S, N_POOL_GROUPS, POOL_GROUP_DIM)
    pos = jnp.arange(S, dtype=jnp.int32)
    outs = []
    for gi, w in enumerate(POOL_WINDOWS):
        ch = ug[:, :, gi, :].astype(jnp.float32)
        csum = jnp.pad(jnp.cumsum(ch, axis=1), ((0, 0), (1, 0), (0, 0)))
        lag = jnp.pad(csum, ((0, 0), (w - 1, 0), (0, 0)))[:, :S]
        count = jnp.minimum(pos + 1, w).astype(jnp.float32)
        mean = (csum[:, 1:] - lag) / count[None, :, None]
        outs.append(mean - ch)
    pooled = jnp.stack(outs, axis=2).astype(u.dtype)
    mixed = jnp.einsum('bsgc,gce->bsge', pooled, pool_w)
    return mixed.reshape(B, S, POOL_WIDTH) * pool_scale


def hierarchical_moe(xn, w_coarse, b_coarse, w_fine, b_fine, w_gate, w_up, w_down):
    B, S, D = xn.shape
    T = B * S
    xt = xn.reshape(T, D)
    tok = jnp.arange(T, dtype=jnp.int32)
    coarse = jnp.einsum('td,dg->tg', xt, w_coarse).astype(jnp.float32) + b_coarse.astype(jnp.float32)
    gsel = jnp.argmax(coarse, axis=-1)
    p_group = jax.nn.softmax(coarse, axis=-1)[tok, gsel]
    fine = jnp.einsum('td,gde->tge', xt, w_fine).astype(jnp.float32) + b_fine.astype(jnp.float32)[None]
    fine_sel = fine[tok, gsel]
    top_v, top_i = lax.top_k(fine_sel, TOP_K_FINE)
    wts = jax.nn.softmax(top_v, axis=-1) * p_group[:, None]
    eidx = gsel[:, None] * EXPERTS_PER_GROUP + top_i
    combine = jnp.sum(jax.nn.one_hot(eidx, N_EXPERTS, dtype=jnp.float32) * wts[..., None], axis=1)
    combine = combine.astype(xn.dtype)
    y = jnp.zeros((T, D), dtype=xn.dtype)
    for g in range(N_EXPERT_GROUPS):
        sl = slice(g * EXPERTS_PER_GROUP, (g + 1) * EXPERTS_PER_GROUP)
        hg = jnp.einsum('td,edf->tef', xt, w_gate[sl])
        hu = jnp.einsum('td,edf->tef', xt, w_up[sl])
        h = jax.nn.silu(hg) * hu * combine[:, sl, None]
        y = y + jnp.einsum('tef,efd->td', h, w_down[sl])
    return y.reshape(B, S, D)


def setup_inputs(seed: int = 0) -> dict:
    key = jax.random.key(seed)
    ks = jax.random.split(key, 20)
    f32 = jnp.float32
    L, D = DEPTH, D_MODEL

    def nrm(k, shape, scale):
        return jax.random.normal(k, shape, f32) * scale

    return {
        "x": jax.random.normal(ks[0], (BATCH, SEQ, D), f32),
        "norm1_g": 1.0 + nrm(ks[1], (L, D), 0.02),
        "w_in": nrm(ks[2], (L, D, IN_WIDTH), D ** -0.5),
        "lambda_q1": nrm(ks[3], (L, DIFF_HEAD_DIM), 0.1),
        "lambda_k1": nrm(ks[4], (L, DIFF_HEAD_DIM), 0.1),
        "lambda_q2": nrm(ks[5], (L, DIFF_HEAD_DIM), 0.1),
        "lambda_k2": nrm(ks[6], (L, DIFF_HEAD_DIM), 0.1),
        "subln_g": 1.0 + nrm(ks[7], (L, DIFF_V_DIM), 0.02),
        "pool_w": nrm(ks[8], (L, N_POOL_GROUPS, POOL_GROUP_DIM, POOL_GROUP_DIM), POOL_GROUP_DIM ** -0.5),
        "pool_scale": 1.0 + nrm(ks[9], (L, POOL_WIDTH), 0.1),
        "w_out": nrm(ks[10], (L, MIX_WIDTH, D), MIX_WIDTH ** -0.5),
        "norm2_g": 1.0 + nrm(ks[11], (L, D), 0.02),
        "w_coarse": nrm(ks[12], (L, D, N_EXPERT_GROUPS), D ** -0.5),
        "b_coarse": nrm(ks[13], (L, N_EXPERT_GROUPS), 0.01),
        "w_fine": nrm(ks[14], (L, N_EXPERT_GROUPS, D, EXPERTS_PER_GROUP), D ** -0.5),
        "b_fine": nrm(ks[15], (L, N_EXPERT_GROUPS, EXPERTS_PER_GROUP), 0.01),
        "w_gate": nrm(ks[16], (L, N_EXPERTS, D, D_FF_EXPERT), D ** -0.5),
        "w_up": nrm(ks[17], (L, N_EXPERTS, D, D_FF_EXPERT), D ** -0.5),
        "w_down": nrm(ks[18], (L, N_EXPERTS, D_FF_EXPERT, D), D_FF_EXPERT ** -0.5),
        "final_norm_g": 1.0 + nrm(ks[19], (D,), 0.02),
    }


def reference(x, norm1_g, w_in, lambda_q1, lambda_k1, lambda_q2, lambda_k2, subln_g,
              pool_w, pool_scale, w_out, norm2_g, w_coarse, b_coarse, w_fine, b_fine,
              w_gate, w_up, w_down, final_norm_g):
    B, S, D = x.shape
    h = x
    for l in range(DEPTH):
        lam_init = 0.8 - 0.6 * math.exp(-0.3 * l)
        xn = rmsnorm(h, norm1_g[l])
        proj = jnp.einsum('bsd,de->bse', xn, w_in[l])
        q = proj[..., :ATTN_WIDTH].reshape(B, S, N_DIFF_HEADS, 2, DIFF_HEAD_DIM)
        k = proj[..., ATTN_WIDTH:2 * ATTN_WIDTH].reshape(B, S, N_DIFF_HEADS, 2, DIFF_HEAD_DIM)
        v = proj[..., 2 * ATTN_WIDTH:3 * ATTN_WIDTH].reshape(B, S, N_DIFF_HEADS, DIFF_V_DIM)
        u = proj[..., 3 * ATTN_WIDTH:]
        lam = (jnp.exp(jnp.sum(lambda_q1[l].astype(jnp.float32) * lambda_k1[l].astype(jnp.float32)))
               - jnp.exp(jnp.sum(lambda_q2[l].astype(jnp.float32) * lambda_k2[l].astype(jnp.float32)))
               + lam_init)
        attn_out = diff_attention(q, k, v, lam, subln_g[l], lam_init)
        pool_out = pool_mixer(u, pool_w[l], pool_scale[l])
        mixed = jnp.concatenate([attn_out, pool_out], axis=-1)
        h = h + jnp.einsum('bsm,md->bsd', mixed, w_out[l])
        hn = rmsnorm(h, norm2_g[l])
        h = h + hierarchical_moe(hn, w_coarse[l], b_coarse[l], w_fine[l], b_fine[l],
                                 w_gate[l], w_up[l], w_down[l])
    return rmsnorm(h, final_norm_g)
```

```python
import functools
import math

import jax
import jax.numpy as jnp
from jax import lax
from jax.experimental import pallas as pl
from jax.experimental.pallas import tpu as pltpu

N_DIFF_HEADS = 8
DIFF_HEAD_DIM = 64
DIFF_V_DIM = 2 * DIFF_HEAD_DIM
ATTN_WIDTH = N_DIFF_HEADS * DIFF_V_DIM
POOL_WINDOWS = (2, 4, 8, 16)
POOL_GROUP_DIM = 256
POOL_WIDTH = len(POOL_WINDOWS) * POOL_GROUP_DIM
N_EXPERT_GROUPS = 4
EXPERTS_PER_GROUP = 8
N_EXPERTS = N_EXPERT_GROUPS * EXPERTS_PER_GROUP
TOP_K_FINE = 2
RMS_EPS = 1e-6
NEG_INF = -1e30
LAM_INIT = 0.8 - 0.6 * math.exp(-0.3 * 0)

LANES = 128
VMEM_LIMIT = 56 * 1024 * 1024
ROW_TILE = 256

F32 = jnp.float32
BF16 = jnp.bfloat16


def _rms(x, g):
    return x * lax.rsqrt(jnp.mean(x * x, axis=-1, keepdims=True) + RMS_EPS) * g


def _inproj_kernel(x_ref, g_ref, w_ref, o_ref, wb_ref):
    @pl.when(pl.program_id(1) == 0)
    def _():
        wb_ref[...] = w_ref[...].astype(BF16)

    xn = _rms(x_ref[...], g_ref[...]).astype(BF16)
    o_ref[...] = jnp.dot(xn, wb_ref[...], preferred_element_type=F32).astype(o_ref.dtype)


def _inproj(x2, g, w, tm=512, tn=1024):
    t, d = x2.shape
    n = w.shape[1]
    return pl.pallas_call(
        _inproj_kernel,
        grid=(n // tn, t // tm),
        in_specs=[
            pl.BlockSpec((tm, d), lambda j, i: (i, 0)),
            pl.BlockSpec((1, d), lambda j, i: (0, 0)),
            pl.BlockSpec((d, tn), lambda j, i: (0, j)),
        ],
        out_specs=pl.BlockSpec((tm, tn), lambda j, i: (i, j)),
        out_shape=jax.ShapeDtypeStruct((t, n), BF16),
        scratch_shapes=[pltpu.VMEM((d, tn), BF16)],
        compiler_params=pltpu.CompilerParams(
            dimension_semantics=("arbitrary", "arbitrary"), vmem_limit_bytes=VMEM_LIMIT),
        name="inproj",
    )(x2, g.reshape(1, d), w)


def _attn_kernel(lq1_ref, lk1_ref, lq2_ref, lk2_ref, q_ref, k_ref, v_ref, g_ref, o_ref,
                 m_ref, l_ref, acc_ref, *, tq):
    h = pl.program_id(1)
    qi = pl.program_id(2)
    d = DIFF_HEAD_DIM
    q = q_ref[...]
    lane = lax.broadcasted_iota(jnp.int32, q.shape, 1)
    qs = q * jnp.asarray(d ** -0.5, q.dtype)
    zero = jnp.zeros_like(qs)
    qh = (jnp.where(lane < d, qs, zero), jnp.where(lane >= d, qs, zero))

    slope = jnp.exp2(jnp.full((1, 1), -8.0 / N_DIFF_HEADS, F32) * (h + 1).astype(F32))
    rel = (lax.broadcasted_iota(jnp.int32, (tq, tq), 0)
           - lax.broadcasted_iota(jnp.int32, (tq, tq), 1))
    nb = -slope * rel.astype(F32)
    nb_diag = jnp.where(rel >= 0, nb, NEG_INF)
    dn = (((1,), (1,)), ((), ()))

    kb = k_ref[pl.ds(qi * tq, tq), :]
    vb = v_ref[pl.ds(qi * tq, tq), :]
    for a in range(2):
        s = lax.dot_general(qh[a], kb, dn, preferred_element_type=F32) + nb_diag
        m = jnp.max(s, axis=1, keepdims=True)
        p = jnp.exp(s - m)
        m_ref[a] = m
        l_ref[a] = jnp.sum(p, axis=1, keepdims=True)
        acc_ref[a] = jnp.dot(p.astype(BF16), vb, preferred_element_type=F32)

    def body(j, carry):
        kb = k_ref[pl.ds(j * tq, tq), :]
        vb = v_ref[pl.ds(j * tq, tq), :]
        off = -slope * ((qi - j) * tq).astype(F32)
        for a in range(2):
            s = lax.dot_general(qh[a], kb, dn, preferred_element_type=F32) + nb
            m_old = m_ref[a]
            m_new = jnp.maximum(m_old, jnp.max(s, axis=1, keepdims=True) + off)
            p = jnp.exp(s - (m_new - off))
            alpha = jnp.exp(m_old - m_new)
            l_ref[a] = alpha * l_ref[a] + jnp.sum(p, axis=1, keepdims=True)
            acc_ref[a] = alpha * acc_ref[a] + jnp.dot(p.astype(BF16), vb, preferred_element_type=F32)
            m_ref[a] = m_new
        return carry

    lax.fori_loop(0, qi, body, 0)

    lam = (jnp.exp(jnp.sum(lq1_ref[...] * lk1_ref[...], axis=1, keepdims=True))
           - jnp.exp(jnp.sum(lq2_ref[...] * lk2_ref[...], axis=1, keepdims=True)) + LAM_INIT)
    o = acc_ref[0] / l_ref[0] - lam * (acc_ref[1] / l_ref[1])
    o_ref[...] = (_rms(o, g_ref[...]) * (1.0 - LAM_INIT)).astype(o_ref.dtype)


def _attention(proj3, lq1, lk1, lq2, lk2, subln_g, tq=256):
    b, s, _ = proj3.shape
    dv = DIFF_V_DIM
    nh = N_DIFF_HEADS
    lam_spec = pl.BlockSpec((1, DIFF_HEAD_DIM), lambda bi, hi, qi: (0, 0))
    return pl.pallas_call(
        functools.partial(_attn_kernel, tq=tq),
        grid=(b, nh, s // tq),
        in_specs=[
            lam_spec, lam_spec, lam_spec, lam_spec,
            pl.BlockSpec((None, tq, dv), lambda bi, hi, qi: (bi, qi, hi)),
            pl.BlockSpec((None, s, dv), lambda bi, hi, qi: (bi, 0, nh + hi)),
            pl.BlockSpec((None, s, dv), lambda bi, hi, qi: (bi, 0, 2 * nh + hi)),
            pl.BlockSpec((1, dv), lambda bi, hi, qi: (0, 0)),
        ],
        out_specs=pl.BlockSpec((None, tq, dv), lambda bi, hi, qi: (bi, qi, hi)),
        out_shape=jax.ShapeDtypeStruct((b, s, ATTN_WIDTH), BF16),
        scratch_shapes=[
            pltpu.VMEM((2, tq, 1), F32),
            pltpu.VMEM((2, tq, 1), F32),
            pltpu.VMEM((2, tq, dv), F32),
        ],
        compiler_params=pltpu.CompilerParams(
            dimension_semantics=("arbitrary", "arbitrary", "arbitrary"),
            vmem_limit_bytes=VMEM_LIMIT),
        name="diff_attn",
    )(lq1, lk1, lq2, lk2, proj3, proj3, proj3, subln_g.reshape(1, dv))


def _pool_kernel(u_ref, w_ref, sc_ref, o_ref):
    s = u_ref.shape[0]
    c = POOL_GROUP_DIM
    t = lax.broadcasted_iota(jnp.int32, (s, 1), 0)
    for gi, win in enumerate(POOL_WINDOWS):
        ch = u_ref[:, gi * c:(gi + 1) * c].astype(F32)
        acc = ch
        span = 1
        while span < win:
            acc = acc + jnp.where(t >= span, pltpu.roll(acc, span, axis=0), 0.0)
            span *= 2
        count = jnp.minimum(t + 1, win).astype(F32)
        pooled = acc / count - ch
        mixed = jnp.dot(pooled.astype(BF16), w_ref[gi].astype(BF16), preferred_element_type=F32)
        o_ref[:, gi * c:(gi + 1) * c] = (mixed * sc_ref[:, gi * c:(gi + 1) * c]).astype(o_ref.dtype)


def _pool(proj3, pool_w, pool_scale):
    b, s, n = proj3.shape
    assert all(w & (w - 1) == 0 for w in POOL_WINDOWS)
    return pl.pallas_call(
        _pool_kernel,
        grid=(b,),
        in_specs=[
            pl.BlockSpec((None, s, POOL_WIDTH), lambda bi: (bi, 0, n // POOL_WIDTH - 1)),
            pl.BlockSpec(pool_w.shape, lambda bi: (0, 0, 0)),
            pl.BlockSpec((1, POOL_WIDTH), lambda bi: (0, 0)),
        ],
        out_specs=pl.BlockSpec((None, s, POOL_WIDTH), lambda bi: (bi, 0, 0)),
        out_shape=jax.ShapeDtypeStruct((b, s, POOL_WIDTH), BF16),
        compiler_params=pltpu.CompilerParams(
            dimension_semantics=("arbitrary",), vmem_limit_bytes=VMEM_LIMIT),
        name="pool_mixer",
    )(proj3, pool_w, pool_scale.reshape(1, POOL_WIDTH))


def _outproj_kernel(a_ref, p_ref, x_ref, w_ref, g_ref, wr_ref, br_ref,
                    h_ref, hn_ref, ri_ref, rw_ref, wb_ref):
    @pl.when(pl.program_id(0) == 0)
    def _():
        wb_ref[...] = w_ref[...].astype(BF16)

    ka = a_ref.shape[1]
    mixed = (jnp.dot(a_ref[...], wb_ref[:ka, :], preferred_element_type=F32)
             + jnp.dot(p_ref[...], wb_ref[ka:, :], preferred_element_type=F32))
    h = x_ref[...] + mixed
    h_ref[...] = h
    hn = _rms(h, g_ref[...])
    hn_ref[...] = hn

    logits = jnp.dot(hn, wr_ref[...], preferred_element_type=F32,
                     precision=lax.Precision.HIGHEST) + br_ref[...]
    ng, epg = N_EXPERT_GROUPS, EXPERTS_PER_GROUP
    lane = lax.broadcasted_iota(jnp.int32, logits.shape, 1)
    big = jnp.int32(LANES)
    low = jnp.float32(-3.0e38)
    cm = lane < ng
    c = jnp.where(cm, logits, low)
    cmax = jnp.max(c, axis=1, keepdims=True)
    gsel = jnp.min(jnp.where(c == cmax, lane, big), axis=1, keepdims=True)
    p_group = 1.0 / jnp.sum(jnp.where(cm, jnp.exp(c - cmax), 0.0), axis=1, keepdims=True)
    f_lo = ng + epg * gsel
    fm = (lane >= f_lo) & (lane < f_lo + epg)
    f = jnp.where(fm, logits, low)
    v1 = jnp.max(f, axis=1, keepdims=True)
    i1 = jnp.min(jnp.where(fm & (f == v1), lane, big), axis=1, keepdims=True)
    fm2 = fm & (lane != i1)
    f2 = jnp.where(fm2, logits, low)
    v2 = jnp.max(f2, axis=1, keepdims=True)
    i2 = jnp.min(jnp.where(fm2 & (f2 == v2), lane, big), axis=1, keepdims=True)
    e21 = jnp.exp(v2 - v1)
    w1 = p_group / (1.0 + e21)
    w2 = p_group * e21 / (1.0 + e21)
    ri_ref[...] = jnp.where(lane == 0, i1 - ng, jnp.where(lane == 1, i2 - ng, 0))
    rw_ref[...] = jnp.where(lane == 0, w1, jnp.where(lane == 1, w2, 0.0))


def _outproj(attn2, pool2, x2, w_out, g2, wr, br, tm=256):
    t, d = x2.shape
    ka, kp = attn2.shape[1], pool2.shape[1]
    row = lambda i: (i, 0)
    const = lambda i: (0, 0)
    return pl.pallas_call(
        _outproj_kernel,
        grid=(t // tm,),
        in_specs=[
            pl.BlockSpec((tm, ka), row),
            pl.BlockSpec((tm, kp), row),
            pl.BlockSpec((tm, d), row),
            pl.BlockSpec((ka + kp, d), const, pipeline_mode=pl.Buffered(1)),
            pl.BlockSpec((1, d), const),
            pl.BlockSpec((d, LANES), const),
            pl.BlockSpec((1, LANES), const),
        ],
        out_specs=[
            pl.BlockSpec((tm, d), row),
            pl.BlockSpec((tm, d), row),
            pl.BlockSpec((tm, LANES), row),
            pl.BlockSpec((tm, LANES), row),
        ],
        out_shape=[
            jax.ShapeDtypeStruct((t, d), F32),
            jax.ShapeDtypeStruct((t, d), F32),
            jax.ShapeDtypeStruct((t, LANES), jnp.int32),
            jax.ShapeDtypeStruct((t, LANES), F32),
        ],
        scratch_shapes=[pltpu.VMEM((ka + kp, d), BF16)],
        compiler_params=pltpu.CompilerParams(
            dimension_semantics=("arbitrary",), vmem_limit_bytes=VMEM_LIMIT),
        name="outproj_router",
    )(attn2, pool2, x2, w_out, g2.reshape(1, d), wr, br)


def _expert_kernel(te_ref, nt_ref, tok_ref, hn_hbm, wg_ref, wu_ref, wd_ref, y_ref, xbuf, sem):
    j = pl.program_id(0)
    tm = xbuf.shape[0]

    @pl.when(j < nt_ref[0])
    def _():
        def issue(r, c):
            tok = tok_ref[j * tm + r]
            pltpu.make_async_copy(hn_hbm.at[pl.ds(tok, 1), :], xbuf.at[pl.ds(r, 1), :], sem).start()
            return c

        lax.fori_loop(0, tm, issue, 0, unroll=8)
        pltpu.make_async_copy(hn_hbm.at[pl.ds(0, tm), :], xbuf, sem).wait()
        x = xbuf[...].astype(BF16)
        hg = jnp.dot(x, wg_ref[...].astype(BF16), preferred_element_type=F32)
        hu = jnp.dot(x, wu_ref[...].astype(BF16), preferred_element_type=F32)
        act = (hg / (1.0 + jnp.exp(-hg))) * hu
        y_ref[...] = jnp.dot(act.astype(BF16), wd_ref[...].astype(BF16), preferred_element_type=F32)

    @pl.when(j >= nt_ref[0])
    def _():
        y_ref[...] = jnp.zeros_like(y_ref)


def _experts(tile_expert, n_tiles, tok_of, hn, w_gate, w_up, w_down, tm=ROW_TILE):
    t, d = hn.shape
    f = w_gate.shape[2]
    nt = tile_expert.shape[0]
    wmap = lambda j, te, n, tok: (te[j], 0, 0)
    grid_spec = pltpu.PrefetchScalarGridSpec(
        num_scalar_prefetch=3,
        grid=(nt,),
        in_specs=[
            pl.BlockSpec(memory_space=pl.ANY),
            pl.BlockSpec((None, d, f), wmap),
            pl.BlockSpec((None, d, f), wmap),
            pl.BlockSpec((None, f, d), wmap),
        ],
        out_specs=pl.BlockSpec((tm, d), lambda j, te, n, tok: (j, 0)),
        scratch_shapes=[pltpu.VMEM((tm, d), F32), pltpu.SemaphoreType.DMA(())],
    )
    return pl.pallas_call(
        _expert_kernel,
        grid_spec=grid_spec,
        out_shape=jax.ShapeDtypeStruct((nt * tm, d), F32),
        compiler_params=pltpu.CompilerParams(
            dimension_semantics=("arbitrary",), vmem_limit_bytes=VMEM_LIMIT),
        name="experts",
    )(tile_expert, n_tiles, tok_of, hn, w_gate, w_up, w_down)


def _combine_kernel(pos_ref, h_ref, rw_ref, ys_hbm, g_ref, o_ref, ybuf, sem):
    i = pl.program_id(0)
    tm = h_ref.shape[0]

    def issue(r, c):
        for k in range(TOP_K_FINE):
            p = pos_ref[(i * tm + r) * TOP_K_FINE + k]
            pltpu.make_async_copy(ys_hbm.at[pl.ds(p, 1), :], ybuf.at[k, pl.ds(r, 1), :], sem).start()
        return c

    lax.fori_loop(0, tm, issue, 0, unroll=8)
    for k in range(TOP_K_FINE):
        pltpu.make_async_copy(ys_hbm.at[pl.ds(0, tm), :], ybuf.at[k], sem).wait()
    w = rw_ref[...]
    y = w[:, 0:1] * ybuf[0] + w[:, 1:2] * ybuf[1]
    o_ref[...] = _rms(h_ref[...] + y, g_ref[...])


def _combine(pos, h, rw, ys, g, tm=256):
    t, d = h.shape
    grid_spec = pltpu.PrefetchScalarGridSpec(
        num_scalar_prefetch=1,
        grid=(t // tm,),
        in_specs=[
            pl.BlockSpec((tm, d), lambda i, p: (i, 0)),
            pl.BlockSpec((tm, LANES), lambda i, p: (i, 0)),
            pl.BlockSpec(memory_space=pl.ANY),
            pl.BlockSpec((1, d), lambda i, p: (0, 0)),
        ],
        out_specs=pl.BlockSpec((tm, d), lambda i, p: (i, 0)),
        scratch_shapes=[pltpu.VMEM((TOP_K_FINE, tm, d), F32), pltpu.SemaphoreType.DMA(())],
    )
    return pl.pallas_call(
        _combine_kernel,
        grid_spec=grid_spec,
        out_shape=jax.ShapeDtypeStruct((t, d), F32),
        compiler_params=pltpu.CompilerParams(
            dimension_semantics=("arbitrary",), vmem_limit_bytes=VMEM_LIMIT),
        name="combine",
    )(pos, h, rw, ys, g.reshape(1, d))


def _sort_plan(eidx, tm=ROW_TILE):
    t, k = eidx.shape
    a = t * k
    nt = a // tm + N_EXPERTS
    ef = eidx.reshape(a)
    onehot = (ef[:, None] == jnp.arange(N_EXPERTS, dtype=jnp.int32)[None, :]).astype(jnp.int32)
    cum = jnp.cumsum(onehot, axis=0)
    rank = jnp.sum(onehot * cum, axis=1) - 1
    counts = cum[-1]
    padded = ((counts + tm - 1) // tm) * tm
    off_end = jnp.cumsum(padded)
    off = off_end - padded
    pos = jnp.sum(onehot * off[None, :], axis=1) + rank
    tok_of = jnp.zeros((nt * tm,), jnp.int32).at[pos].set(jnp.arange(a, dtype=jnp.int32) // k)
    n_tiles = off_end[-1] // tm
    tile_start = jnp.arange(nt, dtype=jnp.int32) * tm
    te = jnp.sum((off_end[None, :] <= tile_start[:, None]).astype(jnp.int32), axis=1)
    te = jnp.minimum(te, N_EXPERTS - 1)
    te = jnp.where(jnp.arange(nt) < n_tiles, te, te[n_tiles - 1])
    return pos.astype(jnp.int32), tok_of, te.astype(jnp.int32), n_tiles.reshape(1).astype(jnp.int32)


def kernel(x, norm1_g, w_in, lambda_q1, lambda_k1, lambda_q2, lambda_k2, subln_g, pool_w, pool_scale, w_out, norm2_g, w_coarse, b_coarse, w_fine, b_fine, w_gate, w_up, w_down, final_norm_g):
    b, s, d = x.shape
    t = b * s
    assert norm1_g.shape[0] == 1
    x2 = x.reshape(t, d)
    proj = _inproj(x2, norm1_g[0], w_in[0])
    proj3 = proj.reshape(b, s, proj.shape[1])
    attn = _attention(proj3, lambda_q1, lambda_k1, lambda_q2, lambda_k2, subln_g[0])
    pool = _pool(proj3, pool_w[0], pool_scale[0])

    ng, epg = N_EXPERT_GROUPS, EXPERTS_PER_GROUP
    wr = jnp.concatenate(
        [w_coarse[0], jnp.transpose(w_fine[0], (1, 0, 2)).reshape(d, ng * epg)], axis=1)
    wr = jnp.pad(wr, ((0, 0), (0, LANES - wr.shape[1])))
    br = jnp.concatenate([b_coarse[0], b_fine[0].reshape(ng * epg)])
    br = jnp.pad(br, (0, LANES - br.shape[0])).reshape(1, LANES)
    h, hn, ri, rw = _outproj(attn.reshape(t, ATTN_WIDTH), pool.reshape(t, POOL_WIDTH), x2,
                             w_out[0], norm2_g[0], wr, br)

    pos, tok_of, tile_expert, n_tiles = _sort_plan(ri[:, :TOP_K_FINE])
    ys = _experts(tile_expert, n_tiles, tok_of, hn, w_gate[0], w_up[0], w_down[0])
    out = _combine(pos, h, rw, ys, final_norm_g)
    return out.reshape(b, s, d)
```

```python
import functools
import math

import jax
import jax.numpy as jnp
from jax import lax
from jax.experimental import pallas as pl
from jax.experimental.pallas import tpu as pltpu

N_DIFF_HEADS = 8
DIFF_HEAD_DIM = 64
DIFF_V_DIM = 2 * DIFF_HEAD_DIM
ATTN_WIDTH = N_DIFF_HEADS * DIFF_V_DIM
POOL_WINDOWS = (2, 4, 8, 16)
POOL_GROUP_DIM = 256
POOL_WIDTH = len(POOL_WINDOWS) * POOL_GROUP_DIM
N_EXPERT_GROUPS = 4
EXPERTS_PER_GROUP = 8
N_EXPERTS = N_EXPERT_GROUPS * EXPERTS_PER_GROUP
TOP_K_FINE = 2
RMS_EPS = 1e-6
NEG_INF = -1e30
LAM_INIT = 0.8 - 0.6 * math.exp(-0.3 * 0)
LOG2E = math.log2(math.e)
Q_SCALE = DIFF_HEAD_DIM ** -0.5 * LOG2E

LANES = 128
VMEM_LIMIT = 56 * 1024 * 1024
ROW_TILE = 256

F32 = jnp.float32
BF16 = jnp.bfloat16


def _rms(x, g):
    return x * lax.rsqrt(jnp.mean(x * x, axis=-1, keepdims=True) + RMS_EPS) * g


def _inproj_kernel(x_ref, g_ref, w_ref, o_ref, wb_ref):
    @pl.when(pl.program_id(1) == 0)
    def _():
        wb_ref[...] = w_ref[...].astype(BF16)

    xn = _rms(x_ref[...], g_ref[...]).astype(BF16)
    acc = jnp.dot(xn, wb_ref[...], preferred_element_type=F32)
    scale = jnp.where(pl.program_id(0) == 0, jnp.float32(Q_SCALE), jnp.float32(1.0))
    o_ref[...] = (acc * scale).astype(o_ref.dtype)


def _inproj(x2, g, w, tm=512, tn=ATTN_WIDTH):
    t, d = x2.shape
    n = w.shape[1]
    return pl.pallas_call(
        _inproj_kernel,
        grid=(n // tn, t // tm),
        in_specs=[
            pl.BlockSpec((tm, d), lambda j, i: (i, 0)),
            pl.BlockSpec((1, d), lambda j, i: (0, 0)),
            pl.BlockSpec((d, tn), lambda j, i: (0, j)),
        ],
        out_specs=pl.BlockSpec((tm, tn), lambda j, i: (i, j)),
        out_shape=jax.ShapeDtypeStruct((t, n), BF16),
        scratch_shapes=[pltpu.VMEM((d, tn), BF16)],
        compiler_params=pltpu.CompilerParams(
            dimension_semantics=("arbitrary", "arbitrary"), vmem_limit_bytes=VMEM_LIMIT),
        name="inproj",
    )(x2, g.reshape(1, d), w)


def _attn_kernel(lq1_ref, lk1_ref, lq2_ref, lk2_ref, q_ref, k_ref, v_ref, g_ref, o_ref,
                 tab_ref, vt_ref, *, tq):
    h = pl.program_id(0)
    s_len = q_ref.shape[0]
    nq = s_len // tq
    d, dv = DIFF_HEAD_DIM, DIFF_V_DIM

    @pl.when(pl.program_id(1) == 0)
    def _():
        slope = jnp.exp2(jnp.full((1, 1), -8.0 / N_DIFF_HEADS, F32) * (h + 1).astype(F32))
        rel = ((nq - 1) * tq + lax.broadcasted_iota(jnp.int32, (s_len, tq), 1)
               - lax.broadcasted_iota(jnp.int32, (s_len, tq), 0))
        tab_ref[...] = jnp.where(rel >= 0, (-LOG2E * slope) * rel.astype(F32), NEG_INF)

    vt_ref[:dv, :] = v_ref[...].astype(F32).T.astype(BF16)
    vt_ref[dv:, :] = jnp.ones((vt_ref.shape[0] - dv, s_len), BF16)

    lam = (jnp.exp(jnp.sum(lq1_ref[...] * lk1_ref[...], axis=1, keepdims=True))
           - jnp.exp(jnp.sum(lq2_ref[...] * lk2_ref[...], axis=1, keepdims=True)) + LAM_INIT)
    dn = (((1,), (1,)), ((), ()))
    lane = lax.broadcasted_iota(jnp.int32, (tq, 2 * d), 1)

    def scores(qi):
        n = (qi + 1) * tq
        q = q_ref[qi * tq:(qi + 1) * tq, :]
        zero = jnp.zeros_like(q)
        qh = (jnp.where(lane < d, q, zero), jnp.where(lane >= d, q, zero))
        kk = k_ref[:n, :]
        bias = tab_ref[(nq - 1 - qi) * tq:(nq - 1 - qi) * tq + n, :]
        return [lax.dot_general(kk, qh[a], dn, preferred_element_type=F32) + bias
                for a in range(2)]

    s_next = scores(0)
    for qi in range(nq):
        n = (qi + 1) * tq
        s_cur = s_next
        if qi + 1 < nq:
            s_next = scores(qi + 1)
        outs = []
        for a in range(2):
            s = s_cur[a]
            m = jnp.max(s, axis=0, keepdims=True)
            p = jnp.exp2(s - m).astype(BF16)
            acc = jnp.dot(vt_ref[:, :n], p, preferred_element_type=F32)
            outs.append(acc[:dv] / acc[dv:dv + 1])
        ot = outs[0] - lam * outs[1]
        yt = ot * lax.rsqrt(jnp.mean(ot * ot, axis=0, keepdims=True) + RMS_EPS) * g_ref[...]
        o_ref[qi * tq:(qi + 1) * tq, :] = (yt * (1.0 - LAM_INIT)).T.astype(o_ref.dtype)


def _attention(proj3, lq1, lk1, lq2, lk2, subln_g, tq=256):
    b, s, _ = proj3.shape
    dv = DIFF_V_DIM
    nh = N_DIFF_HEADS
    ones_rows = 16
    lam_spec = pl.BlockSpec((1, DIFF_HEAD_DIM), lambda hi, bi: (0, 0))
    return pl.pallas_call(
        functools.partial(_attn_kernel, tq=tq),
        grid=(nh, b),
        in_specs=[
            lam_spec, lam_spec, lam_spec, lam_spec,
            pl.BlockSpec((None, s, dv), lambda hi, bi: (bi, 0, hi)),
            pl.BlockSpec((None, s, dv), lambda hi, bi: (bi, 0, nh + hi)),
            pl.BlockSpec((None, s, dv), lambda hi, bi: (bi, 0, 2 * nh + hi)),
            pl.BlockSpec((dv, 1), lambda hi, bi: (0, 0)),
        ],
        out_specs=pl.BlockSpec((None, s, dv), lambda hi, bi: (bi, 0, hi)),
        out_shape=jax.ShapeDtypeStruct((b, s, ATTN_WIDTH), BF16),
        scratch_shapes=[
            pltpu.VMEM((s, tq), F32),
            pltpu.VMEM((dv + ones_rows, s), BF16),
        ],
        compiler_params=pltpu.CompilerParams(
            dimension_semantics=("arbitrary", "arbitrary"),
            vmem_limit_bytes=VMEM_LIMIT),
        name="diff_attn",
    )(lq1, lk1, lq2, lk2, proj3, proj3, proj3, subln_g.reshape(dv, 1))


def _pool_kernel(u_ref, w_ref, sc_ref, o_ref):
    s = u_ref.shape[0]
    c = POOL_GROUP_DIM
    t = lax.broadcasted_iota(jnp.int32, (s, 1), 0)
    for gi, win in enumerate(POOL_WINDOWS):
        ch = u_ref[:, gi * c:(gi + 1) * c].astype(F32)
        acc = ch
        span = 1
        while span < win:
            acc = acc + jnp.where(t >= span, pltpu.roll(acc, span, axis=0), 0.0)
            span *= 2
        count = jnp.minimum(t + 1, win).astype(F32)
        pooled = acc / count - ch
        mixed = jnp.dot(pooled.astype(BF16), w_ref[gi].astype(BF16), preferred_element_type=F32)
        o_ref[:, gi * c:(gi + 1) * c] = (mixed * sc_ref[:, gi * c:(gi + 1) * c]).astype(o_ref.dtype)


def _pool(proj3, pool_w, pool_scale):
    b, s, n = proj3.shape
    assert all(w & (w - 1) == 0 for w in POOL_WINDOWS)
    return pl.pallas_call(
        _pool_kernel,
        grid=(b,),
        in_specs=[
            pl.BlockSpec((None, s, POOL_WIDTH), lambda bi: (bi, 0, n // POOL_WIDTH - 1)),
            pl.BlockSpec(pool_w.shape, lambda bi: (0, 0, 0)),
            pl.BlockSpec((1, POOL_WIDTH), lambda bi: (0, 0)),
        ],
        out_specs=pl.BlockSpec((None, s, POOL_WIDTH), lambda bi: (bi, 0, 0)),
        out_shape=jax.ShapeDtypeStruct((b, s, POOL_WIDTH), BF16),
        compiler_params=pltpu.CompilerParams(
            dimension_semantics=("arbitrary",), vmem_limit_bytes=VMEM_LIMIT),
        name="pool_mixer",
    )(proj3, pool_w, pool_scale.reshape(1, POOL_WIDTH))


def _outproj_kernel(a_ref, p_ref, x_ref, w_ref, g_ref, wr_ref, br_ref,
                    h_ref, hn_ref, ri_ref, rw_ref, wb_ref):
    @pl.when(pl.program_id(0) == 0)
    def _():
        wb_ref[...] = w_ref[...].astype(BF16)

    ka = a_ref.shape[1]
    mixed = (jnp.dot(a_ref[...], wb_ref[:ka, :], preferred_element_type=F32)
             + jnp.dot(p_ref[...], wb_ref[ka:, :], preferred_element_type=F32))
    h = x_ref[...] + mixed
    h_ref[...] = h
    hn = _rms(h, g_ref[...])
    hn_ref[...] = hn

    hn_hi = hn.astype(BF16)
    hn_lo = (hn - hn_hi.astype(F32)).astype(BF16)
    wr_hi = wr_ref[0]
    logits = (jnp.dot(hn_hi, wr_hi, preferred_element_type=F32)
              + jnp.dot(hn_lo, wr_hi, preferred_element_type=F32)
              + jnp.dot(hn_hi, wr_ref[1], preferred_element_type=F32)) + br_ref[...]
    ng, epg = N_EXPERT_GROUPS, EXPERTS_PER_GROUP
    lane = lax.broadcasted_iota(jnp.int32, logits.shape, 1)
    big = jnp.int32(LANES)
    low = jnp.float32(-3.0e38)
    cm = lane < ng
    c = jnp.where(cm, logits, low)
    cmax = jnp.max(c, axis=1, keepdims=True)
    gsel = jnp.min(jnp.where(c == cmax, lane, big), axis=1, keepdims=True)
    p_group = 1.0 / jnp.sum(jnp.where(cm, jnp.exp(c - cmax), 0.0), axis=1, keepdims=True)
    f_lo = ng + epg * gsel
    fm = (lane >= f_lo) & (lane < f_lo + epg)
    f = jnp.where(fm, logits, low)
    v1 = jnp.max(f, axis=1, keepdims=True)
    i1 = jnp.min(jnp.where(fm & (f == v1), lane, big), axis=1, keepdims=True)
    fm2 = fm & (lane != i1)
    f2 = jnp.where(fm2, logits, low)
    v2 = jnp.max(f2, axis=1, keepdims=True)
    i2 = jnp.min(jnp.where(fm2 & (f2 == v2), lane, big), axis=1, keepdims=True)
    e21 = jnp.exp(v2 - v1)
    w1 = p_group / (1.0 + e21)
    w2 = p_group * e21 / (1.0 + e21)
    ri_ref[...] = jnp.where(lane == 0, i1 - ng, jnp.where(lane == 1, i2 - ng, 0))
    rw_ref[...] = jnp.where(lane == 0, w1, jnp.where(lane == 1, w2, 0.0))


def _outproj(attn2, pool2, x2, w_out, g2, wr, br, tm=256):
    t, d = x2.shape
    ka, kp = attn2.shape[1], pool2.shape[1]
    row = lambda i: (i, 0)
    const = lambda i: (0, 0)
    return pl.pallas_call(
        _outproj_kernel,
        grid=(t // tm,),
        in_specs=[
            pl.BlockSpec((tm, ka), row),
            pl.BlockSpec((tm, kp), row),
            pl.BlockSpec((tm, d), row),
            pl.BlockSpec((ka + kp, d), const, pipeline_mode=pl.Buffered(1)),
            pl.BlockSpec((1, d), const),
            pl.BlockSpec((2, d, LANES), lambda i: (0, 0, 0)),
            pl.BlockSpec((1, LANES), const),
        ],
        out_specs=[
            pl.BlockSpec((tm, d), row),
            pl.BlockSpec((tm, d), row),
            pl.BlockSpec((tm, LANES), row),
            pl.BlockSpec((tm, LANES), row),
        ],
        out_shape=[
            jax.ShapeDtypeStruct((t, d), F32),
            jax.ShapeDtypeStruct((t, d), F32),
            jax.ShapeDtypeStruct((t, LANES), jnp.int32),
            jax.ShapeDtypeStruct((t, LANES), F32),
        ],
        scratch_shapes=[pltpu.VMEM((ka + kp, d), BF16)],
        compiler_params=pltpu.CompilerParams(
            dimension_semantics=("arbitrary",), vmem_limit_bytes=VMEM_LIMIT),
        name="outproj_router",
    )(attn2, pool2, x2, w_out, g2.reshape(1, d), wr, br)


def _expert_kernel(te_ref, nt_ref, first_ref, slot_ref, nxt_ref, tok_ref,
                   hn_hbm, wg_hbm, wu_hbm, wd_hbm, y_ref,
                   xbuf, wgb, wub, wdb, gsem, wsem):
    j = pl.program_id(0)
    nt = nt_ref[0]
    tm = xbuf.shape[1]

    def weight_copies(e, s):
        return (pltpu.make_async_copy(wg_hbm.at[e], wgb.at[s], wsem.at[s]),
                pltpu.make_async_copy(wu_hbm.at[e], wub.at[s], wsem.at[s]),
                pltpu.make_async_copy(wd_hbm.at[e], wdb.at[s], wsem.at[s]))

    def start_gather(tile, s):
        def issue(r, c):
            tok = tok_ref[tile * tm + r]
            pltpu.make_async_copy(hn_hbm.at[pl.ds(tok, 1), :], xbuf.at[s, pl.ds(r, 1), :],
                                  gsem.at[s]).start()
            return c

        lax.fori_loop(0, tm, issue, 0, unroll=True)

    @pl.when(j == 0)
    def _():
        for c in weight_copies(te_ref[0], 0):
            c.start()
        start_gather(0, 0)

    @pl.when(j < nt)
    def _():
        s = j % 2
        ws = slot_ref[j]

        @pl.when(first_ref[j] == 1)
        def _():
            for c in weight_copies(te_ref[j], ws):
                c.wait()

            @pl.when(nxt_ref[j] >= 0)
            def _():
                for c in weight_copies(nxt_ref[j], 1 - ws):
                    c.start()

        @pl.when(j + 1 < nt)
        def _():
            start_gather(j + 1, 1 - s)

        pltpu.make_async_copy(hn_hbm.at[pl.ds(0, tm), :], xbuf.at[s], gsem.at[s]).wait()
        x = xbuf[s].astype(BF16)
        hg = jnp.dot(x, wgb[ws].astype(BF16), preferred_element_type=F32)
        hu = jnp.dot(x, wub[ws].astype(BF16), preferred_element_type=F32)
        act = (hg / (1.0 + jnp.exp(-hg))) * hu
        y_ref[...] = jnp.dot(act.astype(BF16), wdb[ws].astype(BF16), preferred_element_type=F32)

    @pl.when(j >= nt)
    def _():
        y_ref[...] = jnp.zeros_like(y_ref)


def _experts(plan, hn, w_gate, w_up, w_down, tm=ROW_TILE):
    t, d = hn.shape
    f = w_gate.shape[2]
    nt = plan["tile_expert"].shape[0]
    any_spec = pl.BlockSpec(memory_space=pl.ANY)
    grid_spec = pltpu.PrefetchScalarGridSpec(
        num_scalar_prefetch=6,
        grid=(nt,),
        in_specs=[any_spec, any_spec, any_spec, any_spec],
        out_specs=pl.BlockSpec((tm, d), lambda j, *_: (j, 0)),
        scratch_shapes=[
            pltpu.VMEM((2, tm, d), F32),
            pltpu.VMEM((2, d, f), F32),
            pltpu.VMEM((2, d, f), F32),
            pltpu.VMEM((2, f, d), F32),
            pltpu.SemaphoreType.DMA((2,)),
            pltpu.SemaphoreType.DMA((2,)),
        ],
    )
    return pl.pallas_call(
        _expert_kernel,
        grid_spec=grid_spec,
        out_shape=jax.ShapeDtypeStruct((nt * tm, d), F32),
        compiler_params=pltpu.CompilerParams(
            dimension_semantics=("arbitrary",), vmem_limit_bytes=VMEM_LIMIT),
        name="experts",
    )(plan["tile_expert"], plan["n_tiles"], plan["first"], plan["slot"], plan["next_expert"],
      plan["tok_of"], hn, w_gate, w_up, w_down)


def _combine_kernel(pos_ref, h_ref, rw_ref, ys_hbm, g_ref, o_ref, ybuf, sem):
    i = pl.program_id(0)
    tm = h_ref.shape[0]

    def start_gather(tile, s):
        def issue(r, c):
            for k in range(TOP_K_FINE):
                p = pos_ref[(tile * tm + r) * TOP_K_FINE + k]
                pltpu.make_async_copy(ys_hbm.at[pl.ds(p, 1), :], ybuf.at[s, k, pl.ds(r, 1), :],
                                      sem.at[s]).start()
            return c

        lax.fori_loop(0, tm, issue, 0, unroll=True)

    @pl.when(i == 0)
    def _():
        start_gather(0, 0)

    s = i % 2

    @pl.when(i + 1 < pl.num_programs(0))
    def _():
        start_gather(i + 1, 1 - s)

    for k in range(TOP_K_FINE):
        pltpu.make_async_copy(ys_hbm.at[pl.ds(0, tm), :], ybuf.at[s, k], sem.at[s]).wait()
    w = rw_ref[...]
    y = w[:, 0:1] * ybuf[s, 0] + w[:, 1:2] * ybuf[s, 1]
    o_ref[...] = _rms(h_ref[...] + y, g_ref[...])


def _combine(pos, h, rw, ys, g, tm=256):
    t, d = h.shape
    grid_spec = pltpu.PrefetchScalarGridSpec(
        num_scalar_prefetch=1,
        grid=(t // tm,),
        in_specs=[
            pl.BlockSpec((tm, d), lambda i, p: (i, 0)),
            pl.BlockSpec((tm, LANES), lambda i, p: (i, 0)),
            pl.BlockSpec(memory_space=pl.ANY),
            pl.BlockSpec((1, d), lambda i, p: (0, 0)),
        ],
        out_specs=pl.BlockSpec((tm, d), lambda i, p: (i, 0)),
        scratch_shapes=[pltpu.VMEM((2, TOP_K_FINE, tm, d), F32), pltpu.SemaphoreType.DMA((2,))],
    )
    return pl.pallas_call(
        _combine_kernel,
        grid_spec=grid_spec,
        out_shape=jax.ShapeDtypeStruct((t, d), F32),
        compiler_params=pltpu.CompilerParams(
            dimension_semantics=("arbitrary",), vmem_limit_bytes=VMEM_LIMIT),
        name="combine",
    )(pos, h, rw, ys, g.reshape(1, d))


def _sort_plan(eidx, tm=ROW_TILE):
    t, k = eidx.shape
    a = t * k
    nt = a // tm + N_EXPERTS
    ef = eidx.reshape(a)
    onehot = (ef[:, None] == jnp.arange(N_EXPERTS, dtype=jnp.int32)[None, :]).astype(jnp.int32)
    cum = jnp.cumsum(onehot, axis=0)
    rank = jnp.sum(onehot * cum, axis=1) - 1
    counts = cum[-1]
    padded = ((counts + tm - 1) // tm) * tm
    off_end = jnp.cumsum(padded)
    off = off_end - padded
    pos = jnp.sum(onehot * off[None, :], axis=1) + rank
    tok_of = jnp.zeros((nt * tm,), jnp.int32).at[pos].set(jnp.arange(a, dtype=jnp.int32) // k)
    n_tiles = off_end[-1] // tm
    tile_idx = jnp.arange(nt, dtype=jnp.int32)
    te = jnp.sum((off_end[None, :] <= (tile_idx * tm)[:, None]).astype(jnp.int32), axis=1)
    te = jnp.minimum(te, N_EXPERTS - 1)
    te = jnp.where(tile_idx < n_tiles, te, te[n_tiles - 1]).astype(jnp.int32)
    first = ((tile_idx == 0) | (te != jnp.roll(te, 1))) & (tile_idx < n_tiles)
    slot = (jnp.cumsum(first.astype(jnp.int32)) - 1) % 2
    eid = jnp.arange(N_EXPERTS, dtype=jnp.int32)
    later = (eid[None, :] > eid[:, None]) & (counts[None, :] > 0)
    next_e = jnp.min(jnp.where(later, eid[None, :], N_EXPERTS), axis=1)
    next_e = jnp.where(next_e < N_EXPERTS, next_e, -1)
    return dict(
        pos=pos.astype(jnp.int32), tok_of=tok_of, tile_expert=te,
        n_tiles=n_tiles.reshape(1).astype(jnp.int32), first=first.astype(jnp.int32),
        slot=slot.astype(jnp.int32), next_expert=next_e[te].astype(jnp.int32))


def kernel(x, norm1_g, w_in, lambda_q1, lambda_k1, lambda_q2, lambda_k2, subln_g, pool_w, pool_scale, w_out, norm2_g, w_coarse, b_coarse, w_fine, b_fine, w_gate, w_up, w_down, final_norm_g):
    b, s, d = x.shape
    t = b * s
    assert norm1_g.shape[0] == 1
    x2 = x.reshape(t, d)
    proj = _inproj(x2, norm1_g[0], w_in[0])
    proj3 = proj.reshape(b, s, proj.shape[1])
    attn = _attention(proj3, lambda_q1, lambda_k1, lambda_q2, lambda_k2, subln_g[0])
    pool = _pool(proj3, pool_w[0], pool_scale[0])

    ng, epg = N_EXPERT_GROUPS, EXPERTS_PER_GROUP
    wr = jnp.concatenate(
        [w_coarse[0], jnp.transpose(w_fine[0], (1, 0, 2)).reshape(d, ng * epg)], axis=1)
    wr = jnp.pad(wr, ((0, 0), (0, LANES - wr.shape[1])))
    br = jnp.concatenate([b_coarse[0], b_fine[0].reshape(ng * epg)])
    br = jnp.pad(br, (0, LANES - br.shape[0])).reshape(1, LANES)
    wr_hi = wr.astype(BF16)
    wr_lo = (wr - wr_hi.astype(F32)).astype(BF16)
    h, hn, ri, rw = _outproj(attn.reshape(t, ATTN_WIDTH), pool.reshape(t, POOL_WIDTH), x2,
                             w_out[0], norm2_g[0], jnp.stack([wr_hi, wr_lo]), br)

    plan = _sort_plan(ri[:, :TOP_K_FINE])
    ys = _experts(plan, hn, w_gate[0], w_up[0], w_down[0])
    out = _combine(plan["pos"], h, rw, ys, final_norm_g)
    return out.reshape(b, s, d)
```

```python
import functools
import math

import jax
import jax.numpy as jnp
from jax import lax
from jax.experimental import pallas as pl
from jax.experimental.pallas import tpu as pltpu

N_DIFF_HEADS = 8
DIFF_HEAD_DIM = 64
DIFF_V_DIM = 2 * DIFF_HEAD_DIM
ATTN_WIDTH = N_DIFF_HEADS * DIFF_V_DIM
POOL_WINDOWS = (2, 4, 8, 16)
POOL_GROUP_DIM = 256
POOL_WIDTH = len(POOL_WINDOWS) * POOL_GROUP_DIM
N_EXPERT_GROUPS = 4
EXPERTS_PER_GROUP = 8
N_EXPERTS = N_EXPERT_GROUPS * EXPERTS_PER_GROUP
TOP_K_FINE = 2
RMS_EPS = 1e-6
NEG_INF = -1e30
LAM_INIT = 0.8 - 0.6 * math.exp(-0.3 * 0)
LOG2E = math.log2(math.e)
Q_SCALE = DIFF_HEAD_DIM ** -0.5 * LOG2E

LANES = 128
VMEM_LIMIT = 56 * 1024 * 1024
ROW_TILE = 256

F32 = jnp.float32
BF16 = jnp.bfloat16


def _rms(x, g):
    return x * lax.rsqrt(jnp.mean(x * x, axis=-1, keepdims=True) + RMS_EPS) * g


def _inproj_kernel(x_ref, g_ref, w_ref, o_ref, wb_ref):
    @pl.when(pl.program_id(1) == 0)
    def _():
        wb_ref[...] = w_ref[...].astype(BF16)

    xn = _rms(x_ref[...], g_ref[...]).astype(BF16)
    acc = jnp.dot(xn, wb_ref[...], preferred_element_type=F32)
    scale = jnp.where(pl.program_id(0) == 0, jnp.float32(Q_SCALE), jnp.float32(1.0))
    o_ref[...] = (acc * scale).astype(o_ref.dtype)


def _inproj(x2, g, w, tm=512, tn=ATTN_WIDTH):
    t, d = x2.shape
    n = w.shape[1]
    return pl.pallas_call(
        _inproj_kernel,
        grid=(n // tn, t // tm),
        in_specs=[
            pl.BlockSpec((tm, d), lambda j, i: (i, 0)),
            pl.BlockSpec((1, d), lambda j, i: (0, 0)),
            pl.BlockSpec((d, tn), lambda j, i: (0, j)),
        ],
        out_specs=pl.BlockSpec((tm, tn), lambda j, i: (i, j)),
        out_shape=jax.ShapeDtypeStruct((t, n), BF16),
        scratch_shapes=[pltpu.VMEM((d, tn), BF16)],
        compiler_params=pltpu.CompilerParams(
            dimension_semantics=("arbitrary", "arbitrary"), vmem_limit_bytes=VMEM_LIMIT),
        name="inproj",
    )(x2, g.reshape(1, d), w)


def _attn_kernel(lq1_ref, lk1_ref, lq2_ref, lk2_ref, q_ref, k_ref, v_ref, g_ref, o_ref,
                 tab_ref, vt_ref, *, tq):
    h = pl.program_id(0)
    s_len = q_ref.shape[0]
    nq = s_len // tq
    d, dv = DIFF_HEAD_DIM, DIFF_V_DIM

    @pl.when(pl.program_id(1) == 0)
    def _():
        slope = jnp.exp2(jnp.full((1, 1), -8.0 / N_DIFF_HEADS, F32) * (h + 1).astype(F32))
        rel = ((nq - 1) * tq + lax.broadcasted_iota(jnp.int32, (s_len, tq), 1)
               - lax.broadcasted_iota(jnp.int32, (s_len, tq), 0))
        tab_ref[...] = jnp.where(rel >= 0, (-LOG2E * slope) * rel.astype(F32), NEG_INF)

    vt_ref[:dv, :] = v_ref[...].astype(F32).T.astype(BF16)
    vt_ref[dv:, :] = jnp.ones((vt_ref.shape[0] - dv, s_len), BF16)

    lam = (jnp.exp(jnp.sum(lq1_ref[...] * lk1_ref[...], axis=1, keepdims=True))
           - jnp.exp(jnp.sum(lq2_ref[...] * lk2_ref[...], axis=1, keepdims=True)) + LAM_INIT)
    dn = (((1,), (1,)), ((), ()))
    lane = lax.broadcasted_iota(jnp.int32, (tq, 2 * d), 1)

    def scores(qi):
        n = (qi + 1) * tq
        q = q_ref[qi * tq:(qi + 1) * tq, :]
        zero = jnp.zeros_like(q)
        qh = (jnp.where(lane < d, q, zero), jnp.where(lane >= d, q, zero))
        kk = k_ref[:n, :]
        bias = tab_ref[(nq - 1 - qi) * tq:(nq - 1 - qi) * tq + n, :]
        return [lax.dot_general(kk, qh[a], dn, preferred_element_type=F32) + bias
                for a in range(2)]

    s_next = scores(0)
    for qi in range(nq):
        n = (qi + 1) * tq
        s_cur = s_next
        if qi + 1 < nq:
            s_next = scores(qi + 1)
        outs = []
        for a in range(2):
            s = s_cur[a]
            m = jnp.max(s, axis=0, keepdims=True)
            p = jnp.exp2(s - m).astype(BF16)
            acc = jnp.dot(vt_ref[:, :n], p, preferred_element_type=F32)
            outs.append(acc[:dv] / acc[dv:dv + 1])
        ot = outs[0] - lam * outs[1]
        yt = ot * lax.rsqrt(jnp.mean(ot * ot, axis=0, keepdims=True) + RMS_EPS) * g_ref[...]
        o_ref[qi * tq:(qi + 1) * tq, :] = (yt * (1.0 - LAM_INIT)).T.astype(o_ref.dtype)


def _attention(proj3, lq1, lk1, lq2, lk2, subln_g, tq=256):
    b, s, _ = proj3.shape
    dv = DIFF_V_DIM
    nh = N_DIFF_HEADS
    ones_rows = 16
    lam_spec = pl.BlockSpec((1, DIFF_HEAD_DIM), lambda hi, bi: (0, 0))
    return pl.pallas_call(
        functools.partial(_attn_kernel, tq=tq),
        grid=(nh, b),
        in_specs=[
            lam_spec, lam_spec, lam_spec, lam_spec,
            pl.BlockSpec((None, s, dv), lambda hi, bi: (bi, 0, hi)),
            pl.BlockSpec((None, s, dv), lambda hi, bi: (bi, 0, nh + hi)),
            pl.BlockSpec((None, s, dv), lambda hi, bi: (bi, 0, 2 * nh + hi)),
            pl.BlockSpec((dv, 1), lambda hi, bi: (0, 0)),
        ],
        out_specs=pl.BlockSpec((None, s, dv), lambda hi, bi: (bi, 0, hi)),
        out_shape=jax.ShapeDtypeStruct((b, s, ATTN_WIDTH), BF16),
        scratch_shapes=[
            pltpu.VMEM((s, tq), F32),
            pltpu.VMEM((dv + ones_rows, s), BF16),
        ],
        compiler_params=pltpu.CompilerParams(
            dimension_semantics=("arbitrary", "arbitrary"),
            vmem_limit_bytes=VMEM_LIMIT),
        name="diff_attn",
    )(lq1, lk1, lq2, lk2, proj3, proj3, proj3, subln_g.reshape(dv, 1))


def _pool_kernel(u_ref, w_ref, sc_ref, o_ref):
    s = u_ref.shape[0]
    c = POOL_GROUP_DIM
    t = lax.broadcasted_iota(jnp.int32, (s, 1), 0)
    for gi, win in enumerate(POOL_WINDOWS):
        ch = u_ref[:, gi * c:(gi + 1) * c].astype(F32)
        acc = ch
        span = 1
        while span < win:
            acc = acc + jnp.where(t >= span, pltpu.roll(acc, span, axis=0), 0.0)
            span *= 2
        count = jnp.minimum(t + 1, win).astype(F32)
        pooled = acc / count - ch
        mixed = jnp.dot(pooled.astype(BF16), w_ref[gi].astype(BF16), preferred_element_type=F32)
        o_ref[:, gi * c:(gi + 1) * c] = (mixed * sc_ref[:, gi * c:(gi + 1) * c]).astype(o_ref.dtype)


def _pool(proj3, pool_w, pool_scale):
    b, s, n = proj3.shape
    assert all(w & (w - 1) == 0 for w in POOL_WINDOWS)
    return pl.pallas_call(
        _pool_kernel,
        grid=(b,),
        in_specs=[
            pl.BlockSpec((None, s, POOL_WIDTH), lambda bi: (bi, 0, n // POOL_WIDTH - 1)),
            pl.BlockSpec(pool_w.shape, lambda bi: (0, 0, 0)),
            pl.BlockSpec((1, POOL_WIDTH), lambda bi: (0, 0)),
        ],
        out_specs=pl.BlockSpec((None, s, POOL_WIDTH), lambda bi: (bi, 0, 0)),
        out_shape=jax.ShapeDtypeStruct((b, s, POOL_WIDTH), BF16),
        compiler_params=pltpu.CompilerParams(
            dimension_semantics=("arbitrary",), vmem_limit_bytes=VMEM_LIMIT),
        name="pool_mixer",
    )(proj3, pool_w, pool_scale.reshape(1, POOL_WIDTH))


def _outproj_kernel(a_ref, p_ref, x_ref, w_ref, g_ref, wr_ref, br_ref,
                    h_ref, hn_ref, ri_ref, rw_ref, wb_ref):
    @pl.when(pl.program_id(0) == 0)
    def _():
        wb_ref[...] = w_ref[...].astype(BF16)

    ka = a_ref.shape[1]
    mixed = (jnp.dot(a_ref[...], wb_ref[:ka, :], preferred_element_type=F32)
             + jnp.dot(p_ref[...], wb_ref[ka:, :], preferred_element_type=F32))
    h = x_ref[...] + mixed
    h_ref[...] = h
    hn = _rms(h, g_ref[...])
    hn_ref[...] = hn

    hn_hi = hn.astype(BF16)
    hn_lo = (hn - hn_hi.astype(F32)).astype(BF16)
    wr_hi = wr_ref[0]
    logits = (jnp.dot(hn_hi, wr_hi, preferred_element_type=F32)
              + jnp.dot(hn_lo, wr_hi, preferred_element_type=F32)
              + jnp.dot(hn_hi, wr_ref[1], preferred_element_type=F32)) + br_ref[...]
    ng, epg = N_EXPERT_GROUPS, EXPERTS_PER_GROUP
    lane = lax.broadcasted_iota(jnp.int32, logits.shape, 1)
    big = jnp.int32(LANES)
    low = jnp.float32(-3.0e38)
    cm = lane < ng
    c = jnp.where(cm, logits, low)
    cmax = jnp.max(c, axis=1, keepdims=True)
    gsel = jnp.min(jnp.where(c == cmax, lane, big), axis=1, keepdims=True)
    p_group = 1.0 / jnp.sum(jnp.where(cm, jnp.exp(c - cmax), 0.0), axis=1, keepdims=True)
    f_lo = ng + epg * gsel
    fm = (lane >= f_lo) & (lane < f_lo + epg)
    f = jnp.where(fm, logits, low)
    v1 = jnp.max(f, axis=1, keepdims=True)
    i1 = jnp.min(jnp.where(fm & (f == v1), lane, big), axis=1, keepdims=True)
    fm2 = fm & (lane != i1)
    f2 = jnp.where(fm2, logits, low)
    v2 = jnp.max(f2, axis=1, keepdims=True)
    i2 = jnp.min(jnp.where(fm2 & (f2 == v2), lane, big), axis=1, keepdims=True)
    e21 = jnp.exp(v2 - v1)
    w1 = p_group / (1.0 + e21)
    w2 = p_group * e21 / (1.0 + e21)
    ri_ref[...] = jnp.where(lane == 0, i1 - ng, jnp.where(lane == 1, i2 - ng, 0))
    rw_ref[...] = jnp.where(lane == 0, w1, jnp.where(lane == 1, w2, 0.0))


def _outproj(attn2, pool2, x2, w_out, g2, wr, br, tm=256):
    t, d = x2.shape
    ka, kp = attn2.shape[1], pool2.shape[1]
    row = lambda i: (i, 0)
    const = lambda i: (0, 0)
    return pl.pallas_call(
        _outproj_kernel,
        grid=(t // tm,),
        in_specs=[
            pl.BlockSpec((tm, ka), row),
            pl.BlockSpec((tm, kp), row),
            pl.BlockSpec((tm, d), row),
            pl.BlockSpec((ka + kp, d), const, pipeline_mode=pl.Buffered(1)),
            pl.BlockSpec((1, d), const),
            pl.BlockSpec((2, d, LANES), lambda i: (0, 0, 0)),
            pl.BlockSpec((1, LANES), const),
        ],
        out_specs=[
            pl.BlockSpec((tm, d), row),
            pl.BlockSpec((tm, d), row),
            pl.BlockSpec((tm, LANES), row),
            pl.BlockSpec((tm, LANES), row),
        ],
        out_shape=[
            jax.ShapeDtypeStruct((t, d), F32),
            jax.ShapeDtypeStruct((t, d), F32),
            jax.ShapeDtypeStruct((t, LANES), jnp.int32),
            jax.ShapeDtypeStruct((t, LANES), F32),
        ],
        scratch_shapes=[pltpu.VMEM((ka + kp, d), BF16)],
        compiler_params=pltpu.CompilerParams(
            dimension_semantics=("arbitrary",), vmem_limit_bytes=VMEM_LIMIT),
        name="outproj_router",
    )(attn2, pool2, x2, w_out, g2.reshape(1, d), wr, br)


def _expert_kernel(te_ref, nt_ref, first_ref, slot_ref, nxt_ref, tok_ref,
                   hn_hbm, wg_hbm, wu_hbm, wd_hbm, y_ref,
                   xbuf, wgb, wub, wdb, gsem, wsem):
    j = pl.program_id(0)
    nt = nt_ref[0]
    tm = xbuf.shape[1]

    def weight_copies(e, s):
        return (pltpu.make_async_copy(wg_hbm.at[e], wgb.at[s], wsem.at[s]),
                pltpu.make_async_copy(wu_hbm.at[e], wub.at[s], wsem.at[s]),
                pltpu.make_async_copy(wd_hbm.at[e], wdb.at[s], wsem.at[s]))

    def start_gather(tile, s):
        def issue(r, c):
            tok = tok_ref[tile * tm + r]
            pltpu.make_async_copy(hn_hbm.at[pl.ds(tok, 1), :], xbuf.at[s, pl.ds(r, 1), :],
                                  gsem.at[s]).start()
            return c

        lax.fori_loop(0, tm, issue, 0, unroll=True)

    @pl.when(j == 0)
    def _():
        start_gather(0, 0)
        for c in weight_copies(te_ref[0], 0):
            c.start(priority=1)

    @pl.when(j < nt)
    def _():
        s = j % 2
        ws = slot_ref[j]

        @pl.when(j + 1 < nt)
        def _():
            start_gather(j + 1, 1 - s)

        @pl.when(first_ref[j] == 1)
        def _():
            for c in weight_copies(te_ref[j], ws):
                c.wait()

            @pl.when(nxt_ref[j] >= 0)
            def _():
                for c in weight_copies(nxt_ref[j], 1 - ws):
                    c.start(priority=1)

        pltpu.make_async_copy(hn_hbm.at[pl.ds(0, tm), :], xbuf.at[s], gsem.at[s]).wait()
        x = xbuf[s].astype(BF16)
        hg = jnp.dot(x, wgb[ws].astype(BF16), preferred_element_type=F32)
        hu = jnp.dot(x, wub[ws].astype(BF16), preferred_element_type=F32)
        act = (hg / (1.0 + jnp.exp(-hg))) * hu
        y_ref[...] = jnp.dot(act.astype(BF16), wdb[ws].astype(BF16), preferred_element_type=F32)

    @pl.when(j >= nt)
    def _():
        y_ref[...] = jnp.zeros_like(y_ref)


def _experts(plan, hn, w_gate, w_up, w_down, tm=ROW_TILE):
    t, d = hn.shape
    f = w_gate.shape[2]
    nt = plan["tile_expert"].shape[0]
    any_spec = pl.BlockSpec(memory_space=pl.ANY)
    grid_spec = pltpu.PrefetchScalarGridSpec(
        num_scalar_prefetch=6,
        grid=(nt,),
        in_specs=[any_spec, any_spec, any_spec, any_spec],
        out_specs=pl.BlockSpec((tm, d), lambda j, *_: (j, 0)),
        scratch_shapes=[
            pltpu.VMEM((2, tm, d), F32),
            pltpu.VMEM((2, d, f), F32),
            pltpu.VMEM((2, d, f), F32),
            pltpu.VMEM((2, f, d), F32),
            pltpu.SemaphoreType.DMA((2,)),
            pltpu.SemaphoreType.DMA((2,)),
        ],
    )
    return pl.pallas_call(
        _expert_kernel,
        grid_spec=grid_spec,
        out_shape=jax.ShapeDtypeStruct((nt * tm, d), F32),
        compiler_params=pltpu.CompilerParams(
            dimension_semantics=("arbitrary",), vmem_limit_bytes=VMEM_LIMIT),
        name="experts",
    )(plan["tile_expert"], plan["n_tiles"], plan["first"], plan["slot"], plan["next_expert"],
      plan["tok_of"], hn, w_gate, w_up, w_down)


def _combine_kernel(pos_ref, h_ref, rw_ref, ys_hbm, g_ref, o_ref, ybuf, sem):
    i = pl.program_id(0)
    tm = h_ref.shape[0]

    def start_gather(tile, s):
        def issue(r, c):
            for k in range(TOP_K_FINE):
                p = pos_ref[(tile * tm + r) * TOP_K_FINE + k]
                pltpu.make_async_copy(ys_hbm.at[pl.ds(p, 1), :], ybuf.at[s, k, pl.ds(r, 1), :],
                                      sem.at[s]).start()
            return c

        lax.fori_loop(0, tm, issue, 0, unroll=True)

    @pl.when(i == 0)
    def _():
        start_gather(0, 0)

    s = i % 2

    @pl.when(i + 1 < pl.num_programs(0))
    def _():
        start_gather(i + 1, 1 - s)

    for k in range(TOP_K_FINE):
        pltpu.make_async_copy(ys_hbm.at[pl.ds(0, tm), :], ybuf.at[s, k], sem.at[s]).wait()
    w = rw_ref[...]
    y = w[:, 0:1] * ybuf[s, 0] + w[:, 1:2] * ybuf[s, 1]
    o_ref[...] = _rms(h_ref[...] + y, g_ref[...])


def _combine(pos, h, rw, ys, g, tm=256):
    t, d = h.shape
    grid_spec = pltpu.PrefetchScalarGridSpec(
        num_scalar_prefetch=1,
        grid=(t // tm,),
        in_specs=[
            pl.BlockSpec((tm, d), lambda i, p: (i, 0)),
            pl.BlockSpec((tm, LANES), lambda i, p: (i, 0)),
            pl.BlockSpec(memory_space=pl.ANY),
            pl.BlockSpec((1, d), lambda i, p: (0, 0)),
        ],
        out_specs=pl.BlockSpec((tm, d), lambda i, p: (i, 0)),
        scratch_shapes=[pltpu.VMEM((2, TOP_K_FINE, tm, d), F32), pltpu.SemaphoreType.DMA((2,))],
    )
    return pl.pallas_call(
        _combine_kernel,
        grid_spec=grid_spec,
        out_shape=jax.ShapeDtypeStruct((t, d), F32),
        compiler_params=pltpu.CompilerParams(
            dimension_semantics=("arbitrary",), vmem_limit_bytes=VMEM_LIMIT),
        name="combine",
    )(pos, h, rw, ys, g.reshape(1, d))


def _sort_plan(eidx, tm=ROW_TILE):
    t, k = eidx.shape
    a = t * k
    nt = a // tm + N_EXPERTS
    ef = eidx.reshape(a)
    onehot = (ef[:, None] == jnp.arange(N_EXPERTS, dtype=jnp.int32)[None, :]).astype(jnp.int32)
    cum = jnp.cumsum(onehot, axis=0)
    rank = jnp.sum(onehot * cum, axis=1) - 1
    counts = cum[-1]
    padded = ((counts + tm - 1) // tm) * tm
    off_end = jnp.cumsum(padded)
    off = off_end - padded
    pos = jnp.sum(onehot * off[None, :], axis=1) + rank
    tok_of = jnp.zeros((nt * tm,), jnp.int32).at[pos].set(jnp.arange(a, dtype=jnp.int32) // k)
    n_tiles = off_end[-1] // tm
    tile_idx = jnp.arange(nt, dtype=jnp.int32)
    te = jnp.sum((off_end[None, :] <= (tile_idx * tm)[:, None]).astype(jnp.int32), axis=1)
    te = jnp.minimum(te, N_EXPERTS - 1)
    te = jnp.where(tile_idx < n_tiles, te, te[n_tiles - 1]).astype(jnp.int32)
    first = ((tile_idx == 0) | (te != jnp.roll(te, 1))) & (tile_idx < n_tiles)
    slot = (jnp.cumsum(first.astype(jnp.int32)) - 1) % 2
    eid = jnp.arange(N_EXPERTS, dtype=jnp.int32)
    later = (eid[None, :] > eid[:, None]) & (counts[None, :] > 0)
    next_e = jnp.min(jnp.where(later, eid[None, :], N_EXPERTS), axis=1)
    next_e = jnp.where(next_e < N_EXPERTS, next_e, -1)
    return dict(
        pos=pos.astype(jnp.int32), tok_of=tok_of, tile_expert=te,
        n_tiles=n_tiles.reshape(1).astype(jnp.int32), first=first.astype(jnp.int32),
        slot=slot.astype(jnp.int32), next_expert=next_e[te].astype(jnp.int32))


def kernel(x, norm1_g, w_in, lambda_q1, lambda_k1, lambda_q2, lambda_k2, subln_g, pool_w, pool_scale, w_out, norm2_g, w_coarse, b_coarse, w_fine, b_fine, w_gate, w_up, w_down, final_norm_g):
    b, s, d = x.shape
    t = b * s
    assert norm1_g.shape[0] == 1
    x2 = x.reshape(t, d)
    proj = _inproj(x2, norm1_g[0], w_in[0])
    proj3 = proj.reshape(b, s, proj.shape[1])
    attn = _attention(proj3, lambda_q1, lambda_k1, lambda_q2, lambda_k2, subln_g[0])
    pool = _pool(proj3, pool_w[0], pool_scale[0])

    ng, epg = N_EXPERT_GROUPS, EXPERTS_PER_GROUP
    wr = jnp.concatenate(
        [w_coarse[0], jnp.transpose(w_fine[0], (1, 0, 2)).reshape(d, ng * epg)], axis=1)
    wr = jnp.pad(wr, ((0, 0), (0, LANES - wr.shape[1])))
    br = jnp.concatenate([b_coarse[0], b_fine[0].reshape(ng * epg)])
    br = jnp.pad(br, (0, LANES - br.shape[0])).reshape(1, LANES)
    wr_hi = wr.astype(BF16)
    wr_lo = (wr - wr_hi.astype(F32)).astype(BF16)
    h, hn, ri, rw = _outproj(attn.reshape(t, ATTN_WIDTH), pool.reshape(t, POOL_WIDTH), x2,
                             w_out[0], norm2_g[0], jnp.stack([wr_hi, wr_lo]), br)

    plan = _sort_plan(ri[:, :TOP_K_FINE])
    ys = _experts(plan, hn, w_gate[0], w_up[0], w_down[0])
    out = _combine(plan["pos"], h, rw, ys, final_norm_g)
    return out.reshape(b, s, d)
```

```python
import functools
import math

import jax
import jax.numpy as jnp
from jax import lax
from jax.experimental import pallas as pl
from jax.experimental.pallas import tpu as pltpu

N_DIFF_HEADS = 8
DIFF_HEAD_DIM = 64
DIFF_V_DIM = 2 * DIFF_HEAD_DIM
ATTN_WIDTH = N_DIFF_HEADS * DIFF_V_DIM
POOL_WINDOWS = (2, 4, 8, 16)
POOL_GROUP_DIM = 256
POOL_WIDTH = len(POOL_WINDOWS) * POOL_GROUP_DIM
N_EXPERT_GROUPS = 4
EXPERTS_PER_GROUP = 8
N_EXPERTS = N_EXPERT_GROUPS * EXPERTS_PER_GROUP
TOP_K_FINE = 2
RMS_EPS = 1e-6
NEG_INF = -1e30
LAM_INIT = 0.8 - 0.6 * math.exp(-0.3 * 0)
LOG2E = math.log2(math.e)
Q_SCALE = DIFF_HEAD_DIM ** -0.5 * LOG2E

LANES = 128
VMEM_LIMIT = 56 * 1024 * 1024
ROW_TILE = 256

F32 = jnp.float32
BF16 = jnp.bfloat16


def _rms(x, g):
    return x * lax.rsqrt(jnp.mean(x * x, axis=-1, keepdims=True) + RMS_EPS) * g


def _inproj_kernel(x_ref, g_ref, w_ref, o_ref, wb_ref):
    @pl.when(pl.program_id(1) == 0)
    def _():
        wb_ref[...] = w_ref[...].astype(BF16)

    xn = _rms(x_ref[...], g_ref[...]).astype(BF16)
    acc = jnp.dot(xn, wb_ref[...], preferred_element_type=F32)
    scale = jnp.where(pl.program_id(0) == 0, jnp.float32(Q_SCALE), jnp.float32(1.0))
    o_ref[...] = (acc * scale).astype(o_ref.dtype)


def _inproj(x2, g, w, tm=512, tn=ATTN_WIDTH):
    t, d = x2.shape
    n = w.shape[1]
    return pl.pallas_call(
        _inproj_kernel,
        grid=(n // tn, t // tm),
        in_specs=[
            pl.BlockSpec((tm, d), lambda j, i: (i, 0)),
            pl.BlockSpec((1, d), lambda j, i: (0, 0)),
            pl.BlockSpec((d, tn), lambda j, i: (0, j)),
        ],
        out_specs=pl.BlockSpec((tm, tn), lambda j, i: (i, j)),
        out_shape=jax.ShapeDtypeStruct((t, n), BF16),
        scratch_shapes=[pltpu.VMEM((d, tn), BF16)],
        compiler_params=pltpu.CompilerParams(
            dimension_semantics=("arbitrary", "arbitrary"), vmem_limit_bytes=VMEM_LIMIT),
        name="inproj",
    )(x2, g.reshape(1, d), w)


def _attn_kernel(lq1_ref, lk1_ref, lq2_ref, lk2_ref, q_ref, k_ref, v_ref, g_ref, o_ref,
                 tab_ref, vt_ref, *, tq):
    h = pl.program_id(0)
    s_len = q_ref.shape[0]
    nq = s_len // tq
    d, dv = DIFF_HEAD_DIM, DIFF_V_DIM

    @pl.when(pl.program_id(1) == 0)
    def _():
        slope = jnp.exp2(jnp.full((1, 1), -8.0 / N_DIFF_HEADS, F32) * (h + 1).astype(F32))
        rel = ((nq - 1) * tq + lax.broadcasted_iota(jnp.int32, (s_len, tq), 1)
               - lax.broadcasted_iota(jnp.int32, (s_len, tq), 0))
        tab_ref[...] = jnp.where(rel >= 0, (-LOG2E * slope) * rel.astype(F32), NEG_INF)

    vt_ref[:dv, :] = v_ref[...].astype(F32).T.astype(BF16)
    vt_ref[dv:, :] = jnp.ones((vt_ref.shape[0] - dv, s_len), BF16)

    lam = (jnp.exp(jnp.sum(lq1_ref[...] * lk1_ref[...], axis=1, keepdims=True))
           - jnp.exp(jnp.sum(lq2_ref[...] * lk2_ref[...], axis=1, keepdims=True)) + LAM_INIT)
    dn = (((1,), (1,)), ((), ()))
    lane = lax.broadcasted_iota(jnp.int32, (tq, 2 * d), 1)

    def scores(qi):
        n = (qi + 1) * tq
        q = q_ref[qi * tq:(qi + 1) * tq, :]
        zero = jnp.zeros_like(q)
        qh = (jnp.where(lane < d, q, zero), jnp.where(lane >= d, q, zero))
        kk = k_ref[:n, :]
        bias = tab_ref[(nq - 1 - qi) * tq:(nq - 1 - qi) * tq + n, :]
        return [lax.dot_general(kk, qh[a], dn, preferred_element_type=F32) + bias
                for a in range(2)]

    s_next = scores(0)
    for qi in range(nq):
        n = (qi + 1) * tq
        s_cur = s_next
        if qi + 1 < nq:
            s_next = scores(qi + 1)
        outs = []
        for a in range(2):
            s = s_cur[a]
            m = jnp.max(s, axis=0, keepdims=True)
            p = jnp.exp2(s - m).astype(BF16)
            acc = jnp.dot(vt_ref[:, :n], p, preferred_element_type=F32)
            outs.append(acc[:dv] / acc[dv:dv + 1])
        ot = outs[0] - lam * outs[1]
        yt = ot * lax.rsqrt(jnp.mean(ot * ot, axis=0, keepdims=True) + RMS_EPS) * g_ref[...]
        o_ref[qi * tq:(qi + 1) * tq, :] = (yt * (1.0 - LAM_INIT)).T.astype(o_ref.dtype)


def _attention(proj3, lq1, lk1, lq2, lk2, subln_g, tq=256):
    b, s, _ = proj3.shape
    dv = DIFF_V_DIM
    nh = N_DIFF_HEADS
    ones_rows = 16
    lam_spec = pl.BlockSpec((1, DIFF_HEAD_DIM), lambda hi, bi: (0, 0))
    return pl.pallas_call(
        functools.partial(_attn_kernel, tq=tq),
        grid=(nh, b),
        in_specs=[
            lam_spec, lam_spec, lam_spec, lam_spec,
            pl.BlockSpec((None, s, dv), lambda hi, bi: (bi, 0, hi)),
            pl.BlockSpec((None, s, dv), lambda hi, bi: (bi, 0, nh + hi)),
            pl.BlockSpec((None, s, dv), lambda hi, bi: (bi, 0, 2 * nh + hi)),
            pl.BlockSpec((dv, 1), lambda hi, bi: (0, 0)),
        ],
        out_specs=pl.BlockSpec((None, s, dv), lambda hi, bi: (bi, 0, hi)),
        out_shape=jax.ShapeDtypeStruct((b, s, ATTN_WIDTH), BF16),
        scratch_shapes=[
            pltpu.VMEM((s, tq), F32),
            pltpu.VMEM((dv + ones_rows, s), BF16),
        ],
        compiler_params=pltpu.CompilerParams(
            dimension_semantics=("arbitrary", "arbitrary"),
            vmem_limit_bytes=VMEM_LIMIT),
        name="diff_attn",
    )(lq1, lk1, lq2, lk2, proj3, proj3, proj3, subln_g.reshape(dv, 1))


def _pool_kernel(u_ref, w_ref, sc_ref, o_ref):
    s = u_ref.shape[0]
    c = POOL_GROUP_DIM
    t = lax.broadcasted_iota(jnp.int32, (s, 1), 0)
    for gi, win in enumerate(POOL_WINDOWS):
        ch = u_ref[:, gi * c:(gi + 1) * c].astype(F32)
        acc = ch
        span = 1
        while span < win:
            acc = acc + jnp.where(t >= span, pltpu.roll(acc, span, axis=0), 0.0)
            span *= 2
        count = jnp.minimum(t + 1, win).astype(F32)
        pooled = acc / count - ch
        mixed = jnp.dot(pooled.astype(BF16), w_ref[gi].astype(BF16), preferred_element_type=F32)
        o_ref[:, gi * c:(gi + 1) * c] = (mixed * sc_ref[:, gi * c:(gi + 1) * c]).astype(o_ref.dtype)


def _pool(proj3, pool_w, pool_scale):
    b, s, n = proj3.shape
    assert all(w & (w - 1) == 0 for w in POOL_WINDOWS)
    return pl.pallas_call(
        _pool_kernel,
        grid=(b,),
        in_specs=[
            pl.BlockSpec((None, s, POOL_WIDTH), lambda bi: (bi, 0, n // POOL_WIDTH - 1)),
            pl.BlockSpec(pool_w.shape, lambda bi: (0, 0, 0)),
            pl.BlockSpec((1, POOL_WIDTH), lambda bi: (0, 0)),
        ],
        out_specs=pl.BlockSpec((None, s, POOL_WIDTH), lambda bi: (bi, 0, 0)),
        out_shape=jax.ShapeDtypeStruct((b, s, POOL_WIDTH), BF16),
        compiler_params=pltpu.CompilerParams(
            dimension_semantics=("arbitrary",), vmem_limit_bytes=VMEM_LIMIT),
        name="pool_mixer",
    )(proj3, pool_w, pool_scale.reshape(1, POOL_WIDTH))


def _outproj_kernel(a_ref, p_ref, x_ref, w_ref, g_ref, wr_ref, br_ref,
                    h_ref, hn_ref, ri_ref, rw_ref, wb_ref):
    @pl.when(pl.program_id(0) == 0)
    def _():
        wb_ref[...] = w_ref[...].astype(BF16)

    ka = a_ref.shape[1]
    mixed = (jnp.dot(a_ref[...], wb_ref[:ka, :], preferred_element_type=F32)
             + jnp.dot(p_ref[...], wb_ref[ka:, :], preferred_element_type=F32))
    h = x_ref[...] + mixed
    h_ref[...] = h
    hn = _rms(h, g_ref[...])
    hn_hi = hn.astype(BF16)
    bits = lax.bitcast_convert_type(hn_hi.astype(F32), jnp.uint32)
    half = bits.shape[1] // 2
    hn_ref[...] = bits[:, :half] | (bits[:, half:] >> 16)

    hn_lo = (hn - hn_hi.astype(F32)).astype(BF16)
    wr_hi = wr_ref[0]
    logits = (jnp.dot(hn_hi, wr_hi, preferred_element_type=F32)
              + jnp.dot(hn_lo, wr_hi, preferred_element_type=F32)
              + jnp.dot(hn_hi, wr_ref[1], preferred_element_type=F32)) + br_ref[...]
    ng, epg = N_EXPERT_GROUPS, EXPERTS_PER_GROUP
    lane = lax.broadcasted_iota(jnp.int32, logits.shape, 1)
    big = jnp.int32(LANES)
    low = jnp.float32(-3.0e38)
    cm = lane < ng
    c = jnp.where(cm, logits, low)
    cmax = jnp.max(c, axis=1, keepdims=True)
    gsel = jnp.min(jnp.where(c == cmax, lane, big), axis=1, keepdims=True)
    p_group = 1.0 / jnp.sum(jnp.where(cm, jnp.exp(c - cmax), 0.0), axis=1, keepdims=True)
    f_lo = ng + epg * gsel
    fm = (lane >= f_lo) & (lane < f_lo + epg)
    f = jnp.where(fm, logits, low)
    v1 = jnp.max(f, axis=1, keepdims=True)
    i1 = jnp.min(jnp.where(fm & (f == v1), lane, big), axis=1, keepdims=True)
    fm2 = fm & (lane != i1)
    f2 = jnp.where(fm2, logits, low)
    v2 = jnp.max(f2, axis=1, keepdims=True)
    i2 = jnp.min(jnp.where(fm2 & (f2 == v2), lane, big), axis=1, keepdims=True)
    e21 = jnp.exp(v2 - v1)
    w1 = p_group / (1.0 + e21)
    w2 = p_group * e21 / (1.0 + e21)
    ri_ref[...] = jnp.where(lane == 0, i1 - ng, jnp.where(lane == 1, i2 - ng, 0))
    rw_ref[...] = jnp.where(lane == 0, w1, jnp.where(lane == 1, w2, 0.0))


def _outproj(attn2, pool2, x2, w_out, g2, wr, br, tm=256):
    t, d = x2.shape
    ka, kp = attn2.shape[1], pool2.shape[1]
    row = lambda i: (i, 0)
    const = lambda i: (0, 0)
    return pl.pallas_call(
        _outproj_kernel,
        grid=(t // tm,),
        in_specs=[
            pl.BlockSpec((tm, ka), row),
            pl.BlockSpec((tm, kp), row),
            pl.BlockSpec((tm, d), row),
            pl.BlockSpec((ka + kp, d), const, pipeline_mode=pl.Buffered(1)),
            pl.BlockSpec((1, d), const),
            pl.BlockSpec((2, d, LANES), lambda i: (0, 0, 0)),
            pl.BlockSpec((1, LANES), const),
        ],
        out_specs=[
            pl.BlockSpec((tm, d), row),
            pl.BlockSpec((tm, d // 2), row),
            pl.BlockSpec((tm, LANES), row),
            pl.BlockSpec((tm, LANES), row),
        ],
        out_shape=[
            jax.ShapeDtypeStruct((t, d), F32),
            jax.ShapeDtypeStruct((t, d // 2), jnp.uint32),
            jax.ShapeDtypeStruct((t, LANES), jnp.int32),
            jax.ShapeDtypeStruct((t, LANES), F32),
        ],
        scratch_shapes=[pltpu.VMEM((ka + kp, d), BF16)],
        compiler_params=pltpu.CompilerParams(
            dimension_semantics=("arbitrary",), vmem_limit_bytes=VMEM_LIMIT),
        name="outproj_router",
    )(attn2, pool2, x2, w_out, g2.reshape(1, d), wr, br)


def _unpack_bf16_pair(words):
    hi = lax.bitcast_convert_type(words & jnp.uint32(0xFFFF0000), F32).astype(BF16)
    lo = lax.bitcast_convert_type(words << 16, F32).astype(BF16)
    return hi, lo


def _expert_kernel(te_ref, nt_ref, first_ref, slot_ref, nxt_ref, nv_ref, tok_ref,
                   hn_hbm, wg_hbm, wu_hbm, wd_hbm, y_ref,
                   xbuf, wgb, wub, wdb, gsem, wsem):
    j = pl.program_id(0)
    nt = nt_ref[0]
    tm, kh = xbuf.shape[1], xbuf.shape[2]

    def weight_copies(e, s):
        return (pltpu.make_async_copy(wg_hbm.at[e], wgb.at[s], wsem.at[s]),
                pltpu.make_async_copy(wu_hbm.at[e], wub.at[s], wsem.at[s]),
                pltpu.make_async_copy(wd_hbm.at[e], wdb.at[s], wsem.at[s]))

    def start_gather(tile, s):
        def issue(r, c):
            tok = tok_ref[tile * tm + r]
            pltpu.make_async_copy(hn_hbm.at[pl.ds(tok, 1), :], xbuf.at[s, pl.ds(r, 1), :],
                                  gsem.at[s]).start()
            return c

        full = nv_ref[tile] == tm

        @pl.when(full)
        def _():
            lax.fori_loop(0, tm, issue, 0, unroll=True)

        @pl.when(jnp.logical_not(full))
        def _():
            lax.fori_loop(0, nv_ref[tile], issue, 0)

    @pl.when(j == 0)
    def _():
        xbuf[...] = jnp.zeros_like(xbuf)
        start_gather(0, 0)
        for c in weight_copies(te_ref[0], 0):
            c.start(priority=1)

    @pl.when(j < nt)
    def _():
        s = j % 2
        ws = slot_ref[j]

        @pl.when(j + 1 < nt)
        def _():
            start_gather(j + 1, 1 - s)

        @pl.when(first_ref[j] == 1)
        def _():
            for c in weight_copies(te_ref[j], ws):
                c.wait()

            @pl.when(nxt_ref[j] >= 0)
            def _():
                for c in weight_copies(nxt_ref[j], 1 - ws):
                    c.start(priority=1)

        nv = nv_ref[j]

        @pl.when(nv == tm)
        def _():
            pltpu.make_async_copy(hn_hbm.at[pl.ds(0, tm), :], xbuf.at[s], gsem.at[s]).wait()

        @pl.when(nv < tm)
        def _():
            def wait_row(r, c):
                pltpu.make_async_copy(hn_hbm.at[pl.ds(0, 1), :], xbuf.at[s, pl.ds(0, 1), :],
                                      gsem.at[s]).wait()
                return c

            lax.fori_loop(0, nv, wait_row, 0)

        xa, xb = _unpack_bf16_pair(xbuf[s])
        hg = (jnp.dot(xa, wgb[ws, :kh, :].astype(BF16), preferred_element_type=F32)
              + jnp.dot(xb, wgb[ws, kh:, :].astype(BF16), preferred_element_type=F32))
        hu = (jnp.dot(xa, wub[ws, :kh, :].astype(BF16), preferred_element_type=F32)
              + jnp.dot(xb, wub[ws, kh:, :].astype(BF16), preferred_element_type=F32))
        act = (hg / (1.0 + jnp.exp(-hg))) * hu
        y_ref[...] = jnp.dot(act.astype(BF16), wdb[ws].astype(BF16), preferred_element_type=F32)

    @pl.when(j >= nt)
    def _():
        y_ref[...] = jnp.zeros_like(y_ref)


def _experts(plan, hn_packed, w_gate, w_up, w_down, tm=ROW_TILE):
    d, f = w_gate.shape[1], w_gate.shape[2]
    assert hn_packed.shape[1] * 2 == d
    nt = plan["tile_expert"].shape[0]
    any_spec = pl.BlockSpec(memory_space=pl.ANY)
    grid_spec = pltpu.PrefetchScalarGridSpec(
        num_scalar_prefetch=7,
        grid=(nt,),
        in_specs=[any_spec, any_spec, any_spec, any_spec],
        out_specs=pl.BlockSpec((tm, d), lambda j, *_: (j, 0)),
        scratch_shapes=[
            pltpu.VMEM((2, tm, d // 2), jnp.uint32),
            pltpu.VMEM((2, d, f), F32),
            pltpu.VMEM((2, d, f), F32),
            pltpu.VMEM((2, f, d), F32),
            pltpu.SemaphoreType.DMA((2,)),
            pltpu.SemaphoreType.DMA((2,)),
        ],
    )
    return pl.pallas_call(
        _expert_kernel,
        grid_spec=grid_spec,
        out_shape=jax.ShapeDtypeStruct((nt * tm, d), F32),
        compiler_params=pltpu.CompilerParams(
            dimension_semantics=("arbitrary",), vmem_limit_bytes=VMEM_LIMIT),
        name="experts",
    )(plan["tile_expert"], plan["n_tiles"], plan["first"], plan["slot"], plan["next_expert"],
      plan["n_valid"], plan["tok_of"], hn_packed, w_gate, w_up, w_down)


def _combine_kernel(pos_ref, h_ref, rw_ref, ys_hbm, g_ref, o_ref, ybuf, sem):
    i = pl.program_id(0)
    tm = h_ref.shape[0]

    def start_gather(tile, s):
        def issue(r, c):
            for k in range(TOP_K_FINE):
                p = pos_ref[(tile * tm + r) * TOP_K_FINE + k]
                pltpu.make_async_copy(ys_hbm.at[pl.ds(p, 1), :], ybuf.at[s, k, pl.ds(r, 1), :],
                                      sem.at[s]).start()
            return c

        lax.fori_loop(0, tm, issue, 0, unroll=True)

    @pl.when(i == 0)
    def _():
        start_gather(0, 0)

    s = i % 2

    @pl.when(i + 1 < pl.num_programs(0))
    def _():
        start_gather(i + 1, 1 - s)

    for k in range(TOP_K_FINE):
        pltpu.make_async_copy(ys_hbm.at[pl.ds(0, tm), :], ybuf.at[s, k], sem.at[s]).wait()
    w = rw_ref[...]
    y = w[:, 0:1] * ybuf[s, 0] + w[:, 1:2] * ybuf[s, 1]
    o_ref[...] = _rms(h_ref[...] + y, g_ref[...])


def _combine(pos, h, rw, ys, g, tm=256):
    t, d = h.shape
    grid_spec = pltpu.PrefetchScalarGridSpec(
        num_scalar_prefetch=1,
        grid=(t // tm,),
        in_specs=[
            pl.BlockSpec((tm, d), lambda i, p: (i, 0)),
            pl.BlockSpec((tm, LANES), lambda i, p: (i, 0)),
            pl.BlockSpec(memory_space=pl.ANY),
            pl.BlockSpec((1, d), lambda i, p: (0, 0)),
        ],
        out_specs=pl.BlockSpec((tm, d), lambda i, p: (i, 0)),
        scratch_shapes=[pltpu.VMEM((2, TOP_K_FINE, tm, d), F32), pltpu.SemaphoreType.DMA((2,))],
    )
    return pl.pallas_call(
        _combine_kernel,
        grid_spec=grid_spec,
        out_shape=jax.ShapeDtypeStruct((t, d), F32),
        compiler_params=pltpu.CompilerParams(
            dimension_semantics=("arbitrary",), vmem_limit_bytes=VMEM_LIMIT),
        name="combine",
    )(pos, h, rw, ys, g.reshape(1, d))


def _sort_plan(eidx, tm=ROW_TILE):
    t, k = eidx.shape
    a = t * k
    nt = a // tm + N_EXPERTS
    ef = eidx.reshape(a)
    onehot = (ef[:, None] == jnp.arange(N_EXPERTS, dtype=jnp.int32)[None, :]).astype(jnp.int32)
    cum = jnp.cumsum(onehot, axis=0)
    rank = jnp.sum(onehot * cum, axis=1) - 1
    counts = cum[-1]
    padded = ((counts + tm - 1) // tm) * tm
    off_end = jnp.cumsum(padded)
    off = off_end - padded
    pos = jnp.sum(onehot * off[None, :], axis=1) + rank
    tok_of = jnp.zeros((nt * tm,), jnp.int32).at[pos].set(jnp.arange(a, dtype=jnp.int32) // k)
    n_tiles = off_end[-1] // tm
    tile_idx = jnp.arange(nt, dtype=jnp.int32)
    te = jnp.sum((off_end[None, :] <= (tile_idx * tm)[:, None]).astype(jnp.int32), axis=1)
    te = jnp.minimum(te, N_EXPERTS - 1)
    te = jnp.where(tile_idx < n_tiles, te, te[n_tiles - 1]).astype(jnp.int32)
    first = ((tile_idx == 0) | (te != jnp.roll(te, 1))) & (tile_idx < n_tiles)
    slot = (jnp.cumsum(first.astype(jnp.int32)) - 1) % 2
    eid = jnp.arange(N_EXPERTS, dtype=jnp.int32)
    later = (eid[None, :] > eid[:, None]) & (counts[None, :] > 0)
    next_e = jnp.min(jnp.where(later, eid[None, :], N_EXPERTS), axis=1)
    next_e = jnp.where(next_e < N_EXPERTS, next_e, -1)
    n_valid = jnp.clip(off[te] + counts[te] - tile_idx * tm, 0, tm)
    n_valid = jnp.where(tile_idx < n_tiles, n_valid, 0).astype(jnp.int32)
    return dict(
        n_valid=n_valid,
        pos=pos.astype(jnp.int32), tok_of=tok_of, tile_expert=te,
        n_tiles=n_tiles.reshape(1).astype(jnp.int32), first=first.astype(jnp.int32),
        slot=slot.astype(jnp.int32), next_expert=next_e[te].astype(jnp.int32))


def kernel(x, norm1_g, w_in, lambda_q1, lambda_k1, lambda_q2, lambda_k2, subln_g, pool_w, pool_scale, w_out, norm2_g, w_coarse, b_coarse, w_fine, b_fine, w_gate, w_up, w_down, final_norm_g):
    b, s, d = x.shape
    t = b * s
    assert norm1_g.shape[0] == 1
    x2 = x.reshape(t, d)
    proj = _inproj(x2, norm1_g[0], w_in[0])
    proj3 = proj.reshape(b, s, proj.shape[1])
    attn = _attention(proj3, lambda_q1, lambda_k1, lambda_q2, lambda_k2, subln_g[0])
    pool = _pool(proj3, pool_w[0], pool_scale[0])

    ng, epg = N_EXPERT_GROUPS, EXPERTS_PER_GROUP
    wr = jnp.concatenate(
        [w_coarse[0], jnp.transpose(w_fine[0], (1, 0, 2)).reshape(d, ng * epg)], axis=1)
    wr = jnp.pad(wr, ((0, 0), (0, LANES - wr.shape[1])))
    br = jnp.concatenate([b_coarse[0], b_fine[0].reshape(ng * epg)])
    br = jnp.pad(br, (0, LANES - br.shape[0])).reshape(1, LANES)
    wr_hi = wr.astype(BF16)
    wr_lo = (wr - wr_hi.astype(F32)).astype(BF16)
    h, hn, ri, rw = _outproj(attn.reshape(t, ATTN_WIDTH), pool.reshape(t, POOL_WIDTH), x2,
                             w_out[0], norm2_g[0], jnp.stack([wr_hi, wr_lo]), br)

    plan = _sort_plan(ri[:, :TOP_K_FINE])
    ys = _experts(plan, hn, w_gate[0], w_up[0], w_down[0])
    out = _combine(plan["pos"], h, rw, ys, final_norm_g)
    return out.reshape(b, s, d)
```

```python
import functools
import math

import jax
import jax.numpy as jnp
from jax import lax
from jax.experimental import pallas as pl
from jax.experimental.pallas import tpu as pltpu

N_DIFF_HEADS = 8
DIFF_HEAD_DIM = 64
DIFF_V_DIM = 2 * DIFF_HEAD_DIM
ATTN_WIDTH = N_DIFF_HEADS * DIFF_V_DIM
POOL_WINDOWS = (2, 4, 8, 16)
POOL_GROUP_DIM = 256
POOL_WIDTH = len(POOL_WINDOWS) * POOL_GROUP_DIM
N_EXPERT_GROUPS = 4
EXPERTS_PER_GROUP = 8
N_EXPERTS = N_EXPERT_GROUPS * EXPERTS_PER_GROUP
TOP_K_FINE = 2
RMS_EPS = 1e-6
NEG_INF = -1e30
LAM_INIT = 0.8 - 0.6 * math.exp(-0.3 * 0)
LOG2E = math.log2(math.e)
Q_SCALE = DIFF_HEAD_DIM ** -0.5 * LOG2E

LANES = 128
VMEM_LIMIT = 56 * 1024 * 1024
ROW_TILE = 256

F32 = jnp.float32
BF16 = jnp.bfloat16


def _rms(x, g):
    return x * lax.rsqrt(jnp.mean(x * x, axis=-1, keepdims=True) + RMS_EPS) * g


def _inproj_kernel(x_ref, g_ref, w_ref, o_ref, wb_ref):
    @pl.when(pl.program_id(1) == 0)
    def _():
        wb_ref[...] = w_ref[...].astype(BF16)

    xn = _rms(x_ref[...], g_ref[...]).astype(BF16)
    acc = jnp.dot(xn, wb_ref[...], preferred_element_type=F32)
    scale = jnp.where(pl.program_id(0) == 0, jnp.float32(Q_SCALE), jnp.float32(1.0))
    o_ref[...] = (acc * scale).astype(o_ref.dtype)


def _inproj(x2, g, w, tm=512, tn=ATTN_WIDTH):
    t, d = x2.shape
    n = w.shape[1]
    return pl.pallas_call(
        _inproj_kernel,
        grid=(n // tn, t // tm),
        in_specs=[
            pl.BlockSpec((tm, d), lambda j, i: (i, 0)),
            pl.BlockSpec((1, d), lambda j, i: (0, 0)),
            pl.BlockSpec((d, tn), lambda j, i: (0, j)),
        ],
        out_specs=pl.BlockSpec((tm, tn), lambda j, i: (i, j)),
        out_shape=jax.ShapeDtypeStruct((t, n), BF16),
        scratch_shapes=[pltpu.VMEM((d, tn), BF16)],
        compiler_params=pltpu.CompilerParams(
            dimension_semantics=("arbitrary", "arbitrary"), vmem_limit_bytes=VMEM_LIMIT),
        name="inproj",
    )(x2, g.reshape(1, d), w)


def _attn_kernel(lq1_ref, lk1_ref, lq2_ref, lk2_ref, q_ref, k_ref, v_ref, g_ref, o_ref,
                 tab_ref, vt_ref, *, tq):
    h = pl.program_id(0)
    s_len = q_ref.shape[0]
    nq = s_len // tq
    d, dv = DIFF_HEAD_DIM, DIFF_V_DIM

    @pl.when(pl.program_id(1) == 0)
    def _():
        slope = jnp.exp2(jnp.full((1, 1), -8.0 / N_DIFF_HEADS, F32) * (h + 1).astype(F32))
        rel = ((nq - 1) * tq + lax.broadcasted_iota(jnp.int32, (s_len, tq), 1)
               - lax.broadcasted_iota(jnp.int32, (s_len, tq), 0))
        tab_ref[...] = jnp.where(rel >= 0, (-LOG2E * slope) * rel.astype(F32), NEG_INF)

    vt_ref[:dv, :] = v_ref[...].astype(F32).T.astype(BF16)
    vt_ref[dv:, :] = jnp.ones((vt_ref.shape[0] - dv, s_len), BF16)

    lam = (jnp.exp(jnp.sum(lq1_ref[...] * lk1_ref[...], axis=1, keepdims=True))
           - jnp.exp(jnp.sum(lq2_ref[...] * lk2_ref[...], axis=1, keepdims=True)) + LAM_INIT)
    dn = (((1,), (1,)), ((), ()))
    lane = lax.broadcasted_iota(jnp.int32, (tq, 2 * d), 1)

    def scores(qi):
        n = (qi + 1) * tq
        q = q_ref[qi * tq:(qi + 1) * tq, :]
        zero = jnp.zeros_like(q)
        qh = (jnp.where(lane < d, q, zero), jnp.where(lane >= d, q, zero))
        kk = k_ref[:n, :]
        bias = tab_ref[(nq - 1 - qi) * tq:(nq - 1 - qi) * tq + n, :]
        return [lax.dot_general(kk, qh[a], dn, preferred_element_type=F32) + bias
                for a in range(2)]

    s_next = scores(0)
    for qi in range(nq):
        n = (qi + 1) * tq
        s_cur = s_next
        if qi + 1 < nq:
            s_next = scores(qi + 1)
        outs = []
        for a in range(2):
            s = s_cur[a]
            m = jnp.max(s, axis=0, keepdims=True)
            p = jnp.exp2(s - m).astype(BF16)
            acc = jnp.dot(vt_ref[:, :n], p, preferred_element_type=F32)
            outs.append(acc[:dv] / acc[dv:dv + 1])
        ot = outs[0] - lam * outs[1]
        yt = ot * lax.rsqrt(jnp.mean(ot * ot, axis=0, keepdims=True) + RMS_EPS) * g_ref[...]
        o_ref[qi * tq:(qi + 1) * tq, :] = (yt * (1.0 - LAM_INIT)).T.astype(o_ref.dtype)


def _attention(proj3, lq1, lk1, lq2, lk2, subln_g, tq=256):
    b, s, _ = proj3.shape
    dv = DIFF_V_DIM
    nh = N_DIFF_HEADS
    ones_rows = 16
    lam_spec = pl.BlockSpec((1, DIFF_HEAD_DIM), lambda hi, bi: (0, 0))
    return pl.pallas_call(
        functools.partial(_attn_kernel, tq=tq),
        grid=(nh, b),
        in_specs=[
            lam_spec, lam_spec, lam_spec, lam_spec,
            pl.BlockSpec((None, s, dv), lambda hi, bi: (bi, 0, hi)),
            pl.BlockSpec((None, s, dv), lambda hi, bi: (bi, 0, nh + hi)),
            pl.BlockSpec((None, s, dv), lambda hi, bi: (bi, 0, 2 * nh + hi)),
            pl.BlockSpec((dv, 1), lambda hi, bi: (0, 0)),
        ],
        out_specs=pl.BlockSpec((None, s, dv), lambda hi, bi: (bi, 0, hi)),
        out_shape=jax.ShapeDtypeStruct((b, s, ATTN_WIDTH), BF16),
        scratch_shapes=[
            pltpu.VMEM((s, tq), F32),
            pltpu.VMEM((dv + ones_rows, s), BF16),
        ],
        compiler_params=pltpu.CompilerParams(
            dimension_semantics=("arbitrary", "arbitrary"),
            vmem_limit_bytes=VMEM_LIMIT),
        name="diff_attn",
    )(lq1, lk1, lq2, lk2, proj3, proj3, proj3, subln_g.reshape(dv, 1))


def _pool_kernel(u_ref, w_ref, sc_ref, o_ref):
    s = u_ref.shape[0]
    c = POOL_GROUP_DIM
    t = lax.broadcasted_iota(jnp.int32, (s, 1), 0)
    for gi, win in enumerate(POOL_WINDOWS):
        ch = u_ref[:, gi * c:(gi + 1) * c].astype(F32)
        acc = ch
        span = 1
        while span < win:
            acc = acc + jnp.where(t >= span, pltpu.roll(acc, span, axis=0), 0.0)
            span *= 2
        count = jnp.minimum(t + 1, win).astype(F32)
        pooled = acc / count - ch
        mixed = jnp.dot(pooled.astype(BF16), w_ref[gi].astype(BF16), preferred_element_type=F32)
        o_ref[:, gi * c:(gi + 1) * c] = (mixed * sc_ref[:, gi * c:(gi + 1) * c]).astype(o_ref.dtype)


def _pool(proj3, pool_w, pool_scale):
    b, s, n = proj3.shape
    assert all(w & (w - 1) == 0 for w in POOL_WINDOWS)
    return pl.pallas_call(
        _pool_kernel,
        grid=(b,),
        in_specs=[
            pl.BlockSpec((None, s, POOL_WIDTH), lambda bi: (bi, 0, n // POOL_WIDTH - 1)),
            pl.BlockSpec(pool_w.shape, lambda bi: (0, 0, 0)),
            pl.BlockSpec((1, POOL_WIDTH), lambda bi: (0, 0)),
        ],
        out_specs=pl.BlockSpec((None, s, POOL_WIDTH), lambda bi: (bi, 0, 0)),
        out_shape=jax.ShapeDtypeStruct((b, s, POOL_WIDTH), BF16),
        compiler_params=pltpu.CompilerParams(
            dimension_semantics=("arbitrary",), vmem_limit_bytes=VMEM_LIMIT),
        name="pool_mixer",
    )(proj3, pool_w, pool_scale.reshape(1, POOL_WIDTH))


def _pack_bf16_pair(x):
    bits = lax.bitcast_convert_type(x.astype(BF16).astype(F32), jnp.uint32)
    half = bits.shape[1] // 2
    return bits[:, :half] | (bits[:, half:] >> 16)


def _outproj_kernel(a_ref, p_ref, x_ref, w_ref, g_ref, wr_ref, br_ref,
                    h_ref, hn_ref, ri_ref, rw_ref, *, n_sub):
    ka = a_ref.shape[1]
    hs = x_ref.shape[0] // n_sub
    normed = []
    for r in range(n_sub):
        rows = pl.ds(r * hs, hs)
        mixed = (jnp.dot(a_ref[rows, :], w_ref[:ka, :], preferred_element_type=F32)
                 + jnp.dot(p_ref[rows, :], w_ref[ka:, :], preferred_element_type=F32))
        h = x_ref[rows, :] + mixed
        h_ref[rows, :] = h
        hn = _rms(h, g_ref[...])
        hn_ref[rows, :] = _pack_bf16_pair(hn)
        hn_hi = hn.astype(BF16)
        normed.append((hn_hi, (hn - hn_hi.astype(F32)).astype(BF16)))
    for r in range(n_sub):
        rows = pl.ds(r * hs, hs)
        ri, rw = _route(normed[r][0], normed[r][1], wr_ref, br_ref)
        ri_ref[rows, :] = ri
        rw_ref[rows, :] = rw


def _route(hn_hi, hn_lo, wr_ref, br_ref):
    wr_hi = wr_ref[0]
    logits = (jnp.dot(hn_hi, wr_hi, preferred_element_type=F32)
              + jnp.dot(hn_lo, wr_hi, preferred_element_type=F32)
              + jnp.dot(hn_hi, wr_ref[1], preferred_element_type=F32)) + br_ref[...]
    ng, epg = N_EXPERT_GROUPS, EXPERTS_PER_GROUP
    lane = lax.broadcasted_iota(jnp.int32, logits.shape, 1)
    big = jnp.int32(LANES)
    low = jnp.float32(-3.0e38)
    cm = lane < ng
    c = jnp.where(cm, logits, low)
    cmax = jnp.max(c, axis=1, keepdims=True)
    gsel = jnp.min(jnp.where(c == cmax, lane, big), axis=1, keepdims=True)
    p_group = 1.0 / jnp.sum(jnp.where(cm, jnp.exp(c - cmax), 0.0), axis=1, keepdims=True)
    f_lo = ng + epg * gsel
    fm = (lane >= f_lo) & (lane < f_lo + epg)
    f = jnp.where(fm, logits, low)
    v1 = jnp.max(f, axis=1, keepdims=True)
    i1 = jnp.min(jnp.where(fm & (f == v1), lane, big), axis=1, keepdims=True)
    fm2 = fm & (lane != i1)
    f2 = jnp.where(fm2, logits, low)
    v2 = jnp.max(f2, axis=1, keepdims=True)
    i2 = jnp.min(jnp.where(fm2 & (f2 == v2), lane, big), axis=1, keepdims=True)
    e21 = jnp.exp(v2 - v1)
    w1 = p_group / (1.0 + e21)
    w2 = p_group * e21 / (1.0 + e21)
    return (jnp.where(lane == 0, i1 - ng, jnp.where(lane == 1, i2 - ng, 0)),
            jnp.where(lane == 0, w1, jnp.where(lane == 1, w2, 0.0)))


def _outproj(attn2, pool2, x2, w_out_bf16, g2, wr, br, tm=512, n_sub=2):
    t, d = x2.shape
    ka, kp = attn2.shape[1], pool2.shape[1]
    row = lambda i: (i, 0)
    const = lambda i: (0, 0)
    return pl.pallas_call(
        functools.partial(_outproj_kernel, n_sub=n_sub),
        grid=(t // tm,),
        in_specs=[
            pl.BlockSpec((tm, ka), row),
            pl.BlockSpec((tm, kp), row),
            pl.BlockSpec((tm, d), row),
            pl.BlockSpec((ka + kp, d), const, pipeline_mode=pl.Buffered(1)),
            pl.BlockSpec((1, d), const),
            pl.BlockSpec((2, d, LANES), lambda i: (0, 0, 0)),
            pl.BlockSpec((1, LANES), const),
        ],
        out_specs=[
            pl.BlockSpec((tm, d), row),
            pl.BlockSpec((tm, d // 2), row),
            pl.BlockSpec((tm, LANES), row),
            pl.BlockSpec((tm, LANES), row),
        ],
        out_shape=[
            jax.ShapeDtypeStruct((t, d), F32),
            jax.ShapeDtypeStruct((t, d // 2), jnp.uint32),
            jax.ShapeDtypeStruct((t, LANES), jnp.int32),
            jax.ShapeDtypeStruct((t, LANES), F32),
        ],
        compiler_params=pltpu.CompilerParams(
            dimension_semantics=("arbitrary",), vmem_limit_bytes=VMEM_LIMIT),
        name="outproj_router",
    )(attn2, pool2, x2, w_out_bf16, g2.reshape(1, d), wr, br)


def _unpack_bf16_pair(words):
    hi = lax.bitcast_convert_type(words & jnp.uint32(0xFFFF0000), F32).astype(BF16)
    lo = lax.bitcast_convert_type(words << 16, F32).astype(BF16)
    return hi, lo


def _expert_kernel(te_ref, nt_ref, first_ref, slot_ref, nxt_ref, nv_ref, tok_ref,
                   hn_hbm, wg_hbm, wu_hbm, wd_hbm, y_ref,
                   xbuf, wgb, wub, wdb, gsem, wsem):
    j = pl.program_id(0)
    nt = nt_ref[0]
    tm, kh = xbuf.shape[1], xbuf.shape[2]

    def weight_copies(e, s):
        return (pltpu.make_async_copy(wg_hbm.at[e], wgb.at[s], wsem.at[s]),
                pltpu.make_async_copy(wu_hbm.at[e], wub.at[s], wsem.at[s]),
                pltpu.make_async_copy(wd_hbm.at[e], wdb.at[s], wsem.at[s]))

    def start_gather(tile, s):
        def issue(r, c):
            tok = tok_ref[tile * tm + r]
            pltpu.make_async_copy(hn_hbm.at[pl.ds(tok, 1), :], xbuf.at[s, pl.ds(r, 1), :],
                                  gsem.at[s]).start()
            return c

        full = nv_ref[tile] == tm

        @pl.when(full)
        def _():
            lax.fori_loop(0, tm, issue, 0, unroll=True)

        @pl.when(jnp.logical_not(full))
        def _():
            lax.fori_loop(0, nv_ref[tile], issue, 0)

    @pl.when(j == 0)
    def _():
        xbuf[...] = jnp.zeros_like(xbuf)
        start_gather(0, 0)
        for c in weight_copies(te_ref[0], 0):
            c.start(priority=1)

    @pl.when(j < nt)
    def _():
        s = j % 2
        ws = slot_ref[j]

        @pl.when(j + 1 < nt)
        def _():
            start_gather(j + 1, 1 - s)

        @pl.when(first_ref[j] == 1)
        def _():
            for c in weight_copies(te_ref[j], ws):
                c.wait()

            @pl.when(nxt_ref[j] >= 0)
            def _():
                for c in weight_copies(nxt_ref[j], 1 - ws):
                    c.start(priority=1)

        nv = nv_ref[j]

        @pl.when(nv == tm)
        def _():
            pltpu.make_async_copy(hn_hbm.at[pl.ds(0, tm), :], xbuf.at[s], gsem.at[s]).wait()

        @pl.when(nv < tm)
        def _():
            def wait_row(r, c):
                pltpu.make_async_copy(hn_hbm.at[pl.ds(0, 1), :], xbuf.at[s, pl.ds(0, 1), :],
                                      gsem.at[s]).wait()
                return c

            lax.fori_loop(0, nv, wait_row, 0)

        xa, xb = _unpack_bf16_pair(xbuf[s])
        hg = (jnp.dot(xa, wgb[ws, :kh, :].astype(BF16), preferred_element_type=F32)
              + jnp.dot(xb, wgb[ws, kh:, :].astype(BF16), preferred_element_type=F32))
        hu = (jnp.dot(xa, wub[ws, :kh, :].astype(BF16), preferred_element_type=F32)
              + jnp.dot(xb, wub[ws, kh:, :].astype(BF16), preferred_element_type=F32))
        act = (hg / (1.0 + jnp.exp(-hg))) * hu
        y_ref[...] = _pack_bf16_pair(
            jnp.dot(act.astype(BF16), wdb[ws].astype(BF16), preferred_element_type=F32))

    @pl.when(j >= nt)
    def _():
        y_ref[...] = jnp.zeros_like(y_ref)


def _experts(plan, hn_packed, w_gate, w_up, w_down, tm=ROW_TILE):
    d, f = w_gate.shape[1], w_gate.shape[2]
    assert hn_packed.shape[1] * 2 == d
    nt = plan["tile_expert"].shape[0]
    any_spec = pl.BlockSpec(memory_space=pl.ANY)
    grid_spec = pltpu.PrefetchScalarGridSpec(
        num_scalar_prefetch=7,
        grid=(nt,),
        in_specs=[any_spec, any_spec, any_spec, any_spec],
        out_specs=pl.BlockSpec((tm, d // 2), lambda j, *_: (j, 0)),
        scratch_shapes=[
            pltpu.VMEM((2, tm, d // 2), jnp.uint32),
            pltpu.VMEM((2, d, f), F32),
            pltpu.VMEM((2, d, f), F32),
            pltpu.VMEM((2, f, d), F32),
            pltpu.SemaphoreType.DMA((2,)),
            pltpu.SemaphoreType.DMA((2,)),
        ],
    )
    return pl.pallas_call(
        _expert_kernel,
        grid_spec=grid_spec,
        out_shape=jax.ShapeDtypeStruct((nt * tm, d // 2), jnp.uint32),
        compiler_params=pltpu.CompilerParams(
            dimension_semantics=("arbitrary",), vmem_limit_bytes=VMEM_LIMIT),
        name="experts",
    )(plan["tile_expert"], plan["n_tiles"], plan["first"], plan["slot"], plan["next_expert"],
      plan["n_valid"], plan["tok_of"], hn_packed, w_gate, w_up, w_down)


def _combine_kernel(pos_ref, h_ref, rw_ref, ys_hbm, g_ref, o_ref, ybuf, sem):
    i = pl.program_id(0)
    tm = h_ref.shape[0]

    def start_gather(tile, s):
        def issue(r, c):
            for k in range(TOP_K_FINE):
                p = pos_ref[(tile * tm + r) * TOP_K_FINE + k]
                pltpu.make_async_copy(ys_hbm.at[pl.ds(p, 1), :], ybuf.at[s, k, pl.ds(r, 1), :],
                                      sem.at[s]).start()
            return c

        lax.fori_loop(0, tm, issue, 0, unroll=True)

    @pl.when(i == 0)
    def _():
        start_gather(0, 0)

    s = i % 2

    @pl.when(i + 1 < pl.num_programs(0))
    def _():
        start_gather(i + 1, 1 - s)

    for k in range(TOP_K_FINE):
        pltpu.make_async_copy(ys_hbm.at[pl.ds(0, tm), :], ybuf.at[s, k], sem.at[s]).wait()
    w = rw_ref[...]
    kh = ybuf.shape[3]
    halves = []
    for part in range(2):
        y = None
        for k in range(TOP_K_FINE):
            words = ybuf[s, k]
            bits = (words & jnp.uint32(0xFFFF0000)) if part == 0 else (words << 16)
            term = w[:, k:k + 1] * lax.bitcast_convert_type(bits, F32)
            y = term if y is None else y + term
        halves.append(h_ref[:, part * kh:(part + 1) * kh] + y)
    ms = sum(jnp.sum(v * v, axis=-1, keepdims=True) for v in halves) / (2 * kh)
    inv = lax.rsqrt(ms + RMS_EPS)
    for part in range(2):
        o_ref[:, part * kh:(part + 1) * kh] = halves[part] * inv * g_ref[:, part * kh:(part + 1) * kh]


def _combine(pos, h, rw, ys, g, tm=256):
    t, d = h.shape
    grid_spec = pltpu.PrefetchScalarGridSpec(
        num_scalar_prefetch=1,
        grid=(t // tm,),
        in_specs=[
            pl.BlockSpec((tm, d), lambda i, p: (i, 0)),
            pl.BlockSpec((tm, LANES), lambda i, p: (i, 0)),
            pl.BlockSpec(memory_space=pl.ANY),
            pl.BlockSpec((1, d), lambda i, p: (0, 0)),
        ],
        out_specs=pl.BlockSpec((tm, d), lambda i, p: (i, 0)),
        scratch_shapes=[pltpu.VMEM((2, TOP_K_FINE, tm, d // 2), jnp.uint32),
                        pltpu.SemaphoreType.DMA((2,))],
    )
    return pl.pallas_call(
        _combine_kernel,
        grid_spec=grid_spec,
        out_shape=jax.ShapeDtypeStruct((t, d), F32),
        compiler_params=pltpu.CompilerParams(
            dimension_semantics=("arbitrary",), vmem_limit_bytes=VMEM_LIMIT),
        name="combine",
    )(pos, h, rw, ys, g.reshape(1, d))


def _sort_plan(eidx, tm=ROW_TILE):
    t, k = eidx.shape
    a = t * k
    nt = a // tm + N_EXPERTS
    ef = eidx.reshape(a)
    onehot = (ef[:, None] == jnp.arange(N_EXPERTS, dtype=jnp.int32)[None, :]).astype(jnp.int32)
    cum = jnp.cumsum(onehot, axis=0)
    rank = jnp.sum(onehot * cum, axis=1) - 1
    counts = cum[-1]
    padded = ((counts + tm - 1) // tm) * tm
    off_end = jnp.cumsum(padded)
    off = off_end - padded
    pos = jnp.sum(onehot * off[None, :], axis=1) + rank
    tok_of = jnp.zeros((nt * tm,), jnp.int32).at[pos].set(jnp.arange(a, dtype=jnp.int32) // k)
    n_tiles = off_end[-1] // tm
    tile_idx = jnp.arange(nt, dtype=jnp.int32)
    te = jnp.sum((off_end[None, :] <= (tile_idx * tm)[:, None]).astype(jnp.int32), axis=1)
    te = jnp.minimum(te, N_EXPERTS - 1)
    te = jnp.where(tile_idx < n_tiles, te, te[n_tiles - 1]).astype(jnp.int32)
    first = ((tile_idx == 0) | (te != jnp.roll(te, 1))) & (tile_idx < n_tiles)
    slot = (jnp.cumsum(first.astype(jnp.int32)) - 1) % 2
    eid = jnp.arange(N_EXPERTS, dtype=jnp.int32)
    later = (eid[None, :] > eid[:, None]) & (counts[None, :] > 0)
    next_e = jnp.min(jnp.where(later, eid[None, :], N_EXPERTS), axis=1)
    next_e = jnp.where(next_e < N_EXPERTS, next_e, -1)
    n_valid = jnp.clip(off[te] + counts[te] - tile_idx * tm, 0, tm)
    n_valid = jnp.where(tile_idx < n_tiles, n_valid, 0).astype(jnp.int32)
    return dict(
        n_valid=n_valid,
        pos=pos.astype(jnp.int32), tok_of=tok_of, tile_expert=te,
        n_tiles=n_tiles.reshape(1).astype(jnp.int32), first=first.astype(jnp.int32),
        slot=slot.astype(jnp.int32), next_expert=next_e[te].astype(jnp.int32))


def kernel(x, norm1_g, w_in, lambda_q1, lambda_k1, lambda_q2, lambda_k2, subln_g, pool_w, pool_scale, w_out, norm2_g, w_coarse, b_coarse, w_fine, b_fine, w_gate, w_up, w_down, final_norm_g):
    b, s, d = x.shape
    t = b * s
    assert norm1_g.shape[0] == 1
    x2 = x.reshape(t, d)
    proj = _inproj(x2, norm1_g[0], w_in[0])
    proj3 = proj.reshape(b, s, proj.shape[1])
    attn = _attention(proj3, lambda_q1, lambda_k1, lambda_q2, lambda_k2, subln_g[0])
    pool = _pool(proj3, pool_w[0], pool_scale[0])

    ng, epg = N_EXPERT_GROUPS, EXPERTS_PER_GROUP
    wr = jnp.concatenate(
        [w_coarse[0], jnp.transpose(w_fine[0], (1, 0, 2)).reshape(d, ng * epg)], axis=1)
    wr = jnp.pad(wr, ((0, 0), (0, LANES - wr.shape[1])))
    br = jnp.concatenate([b_coarse[0], b_fine[0].reshape(ng * epg)])
    br = jnp.pad(br, (0, LANES - br.shape[0])).reshape(1, LANES)
    wr_hi = wr.astype(BF16)
    wr_lo = (wr - wr_hi.astype(F32)).astype(BF16)
    h, hn, ri, rw = _outproj(attn.reshape(t, ATTN_WIDTH), pool.reshape(t, POOL_WIDTH), x2,
                             w_out[0].astype(BF16), norm2_g[0], jnp.stack([wr_hi, wr_lo]), br)

    plan = _sort_plan(ri[:, :TOP_K_FINE])
    ys = _experts(plan, hn, w_gate[0], w_up[0], w_down[0])
    out = _combine(plan["pos"], h, rw, ys, final_norm_g)
    return out.reshape(b, s, d)
```

```python
import functools
import math

import jax
import jax.numpy as jnp
from jax import lax
from jax.experimental import pallas as pl
from jax.experimental.pallas import tpu as pltpu

N_DIFF_HEADS = 8
DIFF_HEAD_DIM = 64
DIFF_V_DIM = 2 * DIFF_HEAD_DIM
ATTN_WIDTH = N_DIFF_HEADS * DIFF_V_DIM
POOL_WINDOWS = (2, 4, 8, 16)
POOL_GROUP_DIM = 256
POOL_WIDTH = len(POOL_WINDOWS) * POOL_GROUP_DIM
N_EXPERT_GROUPS = 4
EXPERTS_PER_GROUP = 8
N_EXPERTS = N_EXPERT_GROUPS * EXPERTS_PER_GROUP
TOP_K_FINE = 2
RMS_EPS = 1e-6
NEG_INF = -1e30
LAM_INIT = 0.8 - 0.6 * math.exp(-0.3 * 0)
LOG2E = math.log2(math.e)
Q_SCALE = DIFF_HEAD_DIM ** -0.5 * LOG2E

LANES = 128
VMEM_LIMIT = 56 * 1024 * 1024
ROW_TILE = 256

F32 = jnp.float32
BF16 = jnp.bfloat16


def _rms(x, g):
    return x * lax.rsqrt(jnp.mean(x * x, axis=-1, keepdims=True) + RMS_EPS) * g


def _inproj_kernel(x_ref, g_ref, w_ref, o_ref, wb_ref):
    @pl.when(pl.program_id(1) == 0)
    def _():
        wb_ref[...] = w_ref[...].astype(BF16)

    xn = _rms(x_ref[...], g_ref[...]).astype(BF16)
    acc = jnp.dot(xn, wb_ref[...], preferred_element_type=F32)
    scale = jnp.where(pl.program_id(0) == 0, jnp.float32(Q_SCALE), jnp.float32(1.0))
    o_ref[...] = (acc * scale).astype(o_ref.dtype)


def _inproj(x2, g, w, tm=512, tn=ATTN_WIDTH):
    t, d = x2.shape
    n = w.shape[1]
    return pl.pallas_call(
        _inproj_kernel,
        grid=(n // tn, t // tm),
        in_specs=[
            pl.BlockSpec((tm, d), lambda j, i: (i, 0)),
            pl.BlockSpec((1, d), lambda j, i: (0, 0)),
            pl.BlockSpec((d, tn), lambda j, i: (0, j)),
        ],
        out_specs=pl.BlockSpec((tm, tn), lambda j, i: (i, j)),
        out_shape=jax.ShapeDtypeStruct((t, n), BF16),
        scratch_shapes=[pltpu.VMEM((d, tn), BF16)],
        compiler_params=pltpu.CompilerParams(
            dimension_semantics=("arbitrary", "arbitrary"), vmem_limit_bytes=VMEM_LIMIT),
        name="inproj",
    )(x2, g.reshape(1, d), w)


def _attn_kernel(lq1_ref, lk1_ref, lq2_ref, lk2_ref, q_ref, k_ref, v_ref, g_ref, o_ref,
                 tab_ref, vt_ref, *, tq):
    h = pl.program_id(0)
    s_len = q_ref.shape[0]
    nq = s_len // tq
    d, dv = DIFF_HEAD_DIM, DIFF_V_DIM

    @pl.when(pl.program_id(1) == 0)
    def _():
        slope = jnp.exp2(jnp.full((1, 1), -8.0 / N_DIFF_HEADS, F32) * (h + 1).astype(F32))
        rel = ((nq - 1) * tq + lax.broadcasted_iota(jnp.int32, (s_len, tq), 1)
               - lax.broadcasted_iota(jnp.int32, (s_len, tq), 0))
        tab_ref[...] = jnp.where(rel >= 0, (-LOG2E * slope) * rel.astype(F32), NEG_INF)

    vt_ref[:dv, :] = v_ref[...].astype(F32).T.astype(BF16)
    vt_ref[dv:, :] = jnp.ones((vt_ref.shape[0] - dv, s_len), BF16)

    lam = (jnp.exp(jnp.sum(lq1_ref[...] * lk1_ref[...], axis=1, keepdims=True))
           - jnp.exp(jnp.sum(lq2_ref[...] * lk2_ref[...], axis=1, keepdims=True)) + LAM_INIT)
    dn = (((1,), (1,)), ((), ()))
    lane = lax.broadcasted_iota(jnp.int32, (tq, 2 * d), 1)

    def scores(qi):
        n = (qi + 1) * tq
        q = q_ref[qi * tq:(qi + 1) * tq, :]
        zero = jnp.zeros_like(q)
        qh = (jnp.where(lane < d, q, zero), jnp.where(lane >= d, q, zero))
        kk = k_ref[:n, :]
        bias = tab_ref[(nq - 1 - qi) * tq:(nq - 1 - qi) * tq + n, :]
        return [lax.dot_general(kk, qh[a], dn, preferred_element_type=F32) + bias
                for a in range(2)]

    s_next = scores(0)
    for qi in range(nq):
        n = (qi + 1) * tq
        s_cur = s_next
        if qi + 1 < nq:
            s_next = scores(qi + 1)
        outs = []
        for a in range(2):
            s = s_cur[a]
            m = jnp.max(s, axis=0, keepdims=True)
            p = jnp.exp2(s - m).astype(BF16)
            acc = jnp.dot(vt_ref[:, :n], p, preferred_element_type=F32)
            outs.append(acc[:dv] / acc[dv:dv + 1])
        ot = outs[0] - lam * outs[1]
        yt = ot * lax.rsqrt(jnp.mean(ot * ot, axis=0, keepdims=True) + RMS_EPS) * g_ref[...]
        o_ref[qi * tq:(qi + 1) * tq, :] = (yt * (1.0 - LAM_INIT)).T.astype(o_ref.dtype)


def _attention(proj3, lq1, lk1, lq2, lk2, subln_g, tq=256):
    b, s, _ = proj3.shape
    dv = DIFF_V_DIM
    nh = N_DIFF_HEADS
    ones_rows = 16
    lam_spec = pl.BlockSpec((1, DIFF_HEAD_DIM), lambda hi, bi: (0, 0))
    return pl.pallas_call(
        functools.partial(_attn_kernel, tq=tq),
        grid=(nh, b),
        in_specs=[
            lam_spec, lam_spec, lam_spec, lam_spec,
            pl.BlockSpec((None, s, dv), lambda hi, bi: (bi, 0, hi)),
            pl.BlockSpec((None, s, dv), lambda hi, bi: (bi, 0, nh + hi)),
            pl.BlockSpec((None, s, dv), lambda hi, bi: (bi, 0, 2 * nh + hi)),
            pl.BlockSpec((dv, 1), lambda hi, bi: (0, 0)),
        ],
        out_specs=pl.BlockSpec((None, s, dv), lambda hi, bi: (bi, 0, hi)),
        out_shape=jax.ShapeDtypeStruct((b, s, ATTN_WIDTH), BF16),
        scratch_shapes=[
            pltpu.VMEM((s, tq), F32),
            pltpu.VMEM((dv + ones_rows, s), BF16),
        ],
        compiler_params=pltpu.CompilerParams(
            dimension_semantics=("arbitrary", "arbitrary"),
            vmem_limit_bytes=VMEM_LIMIT),
        name="diff_attn",
    )(lq1, lk1, lq2, lk2, proj3, proj3, proj3, subln_g.reshape(dv, 1))


def _pool_kernel(u_ref, w_ref, sc_ref, o_ref):
    s = u_ref.shape[0]
    c = POOL_GROUP_DIM
    t = lax.broadcasted_iota(jnp.int32, (s, 1), 0)
    for gi, win in enumerate(POOL_WINDOWS):
        ch = u_ref[:, gi * c:(gi + 1) * c].astype(F32)
        acc = ch
        span = 1
        while span < win:
            acc = acc + jnp.where(t >= span, pltpu.roll(acc, span, axis=0), 0.0)
            span *= 2
        count = jnp.minimum(t + 1, win).astype(F32)
        pooled = acc / count - ch
        mixed = jnp.dot(pooled.astype(BF16), w_ref[gi].astype(BF16), preferred_element_type=F32)
        o_ref[:, gi * c:(gi + 1) * c] = (mixed * sc_ref[:, gi * c:(gi + 1) * c]).astype(o_ref.dtype)


def _pool(proj3, pool_w, pool_scale):
    b, s, n = proj3.shape
    assert all(w & (w - 1) == 0 for w in POOL_WINDOWS)
    return pl.pallas_call(
        _pool_kernel,
        grid=(b,),
        in_specs=[
            pl.BlockSpec((None, s, POOL_WIDTH), lambda bi: (bi, 0, n // POOL_WIDTH - 1)),
            pl.BlockSpec(pool_w.shape, lambda bi: (0, 0, 0)),
            pl.BlockSpec((1, POOL_WIDTH), lambda bi: (0, 0)),
        ],
        out_specs=pl.BlockSpec((None, s, POOL_WIDTH), lambda bi: (bi, 0, 0)),
        out_shape=jax.ShapeDtypeStruct((b, s, POOL_WIDTH), BF16),
        compiler_params=pltpu.CompilerParams(
            dimension_semantics=("arbitrary",), vmem_limit_bytes=VMEM_LIMIT),
        name="pool_mixer",
    )(proj3, pool_w, pool_scale.reshape(1, POOL_WIDTH))


def _pack_bf16_pair(x):
    bits = lax.bitcast_convert_type(x.astype(BF16).astype(F32), jnp.uint32)
    half = bits.shape[1] // 2
    return bits[:, :half] | (bits[:, half:] >> 16)


def _outproj_kernel(a_ref, p_ref, x_ref, w_ref, g_ref, wr_ref, br_ref,
                    h_ref, hn_ref, ri_ref, rw_ref, *, n_sub):
    ka = a_ref.shape[1]
    hs = x_ref.shape[0] // n_sub
    normed = []
    for r in range(n_sub):
        rows = pl.ds(r * hs, hs)
        mixed = (jnp.dot(a_ref[rows, :], w_ref[:ka, :], preferred_element_type=F32)
                 + jnp.dot(p_ref[rows, :], w_ref[ka:, :], preferred_element_type=F32))
        h = x_ref[rows, :] + mixed
        h_ref[rows, :] = h
        hn = _rms(h, g_ref[...])
        hn_ref[rows, :] = _pack_bf16_pair(hn)
        hn_hi = hn.astype(BF16)
        normed.append((hn_hi, (hn - hn_hi.astype(F32)).astype(BF16)))
    for r in range(n_sub):
        rows = pl.ds(r * hs, hs)
        ri, rw = _route(normed[r][0], normed[r][1], wr_ref, br_ref)
        ri_ref[rows, :] = ri
        rw_ref[rows, :] = rw


def _route(hn_hi, hn_lo, wr_ref, br_ref):
    wr_hi = wr_ref[0]
    logits = (jnp.dot(hn_hi, wr_hi, preferred_element_type=F32)
              + jnp.dot(hn_lo, wr_hi, preferred_element_type=F32)
              + jnp.dot(hn_hi, wr_ref[1], preferred_element_type=F32)) + br_ref[...]
    ng, epg = N_EXPERT_GROUPS, EXPERTS_PER_GROUP
    lane = lax.broadcasted_iota(jnp.int32, logits.shape, 1)
    big = jnp.int32(LANES)
    low = jnp.float32(-3.0e38)
    cm = lane < ng
    c = jnp.where(cm, logits, low)
    cmax = jnp.max(c, axis=1, keepdims=True)
    gsel = jnp.min(jnp.where(c == cmax, lane, big), axis=1, keepdims=True)
    p_group = 1.0 / jnp.sum(jnp.where(cm, jnp.exp(c - cmax), 0.0), axis=1, keepdims=True)
    f_lo = ng + epg * gsel
    fm = (lane >= f_lo) & (lane < f_lo + epg)
    f = jnp.where(fm, logits, low)
    v1 = jnp.max(f, axis=1, keepdims=True)
    i1 = jnp.min(jnp.where(fm & (f == v1), lane, big), axis=1, keepdims=True)
    fm2 = fm & (lane != i1)
    f2 = jnp.where(fm2, logits, low)
    v2 = jnp.max(f2, axis=1, keepdims=True)
    i2 = jnp.min(jnp.where(fm2 & (f2 == v2), lane, big), axis=1, keepdims=True)
    e21 = jnp.exp(v2 - v1)
    w1 = p_group / (1.0 + e21)
    w2 = p_group * e21 / (1.0 + e21)
    return (jnp.where(lane == 0, i1 - ng, jnp.where(lane == 1, i2 - ng, 0)),
            jnp.where(lane == 0, w1, jnp.where(lane == 1, w2, 0.0)))


def _outproj(attn2, pool2, x2, w_out_bf16, g2, wr, br, tm=512, n_sub=2):
    t, d = x2.shape
    ka, kp = attn2.shape[1], pool2.shape[1]
    row = lambda i: (i, 0)
    const = lambda i: (0, 0)
    return pl.pallas_call(
        functools.partial(_outproj_kernel, n_sub=n_sub),
        grid=(t // tm,),
        in_specs=[
            pl.BlockSpec((tm, ka), row),
            pl.BlockSpec((tm, kp), row),
            pl.BlockSpec((tm, d), row),
            pl.BlockSpec((ka + kp, d), const, pipeline_mode=pl.Buffered(1)),
            pl.BlockSpec((1, d), const),
            pl.BlockSpec((2, d, LANES), lambda i: (0, 0, 0)),
            pl.BlockSpec((1, LANES), const),
        ],
        out_specs=[
            pl.BlockSpec((tm, d), row),
            pl.BlockSpec((tm, d // 2), row),
            pl.BlockSpec((tm, LANES), row),
            pl.BlockSpec((tm, LANES), row),
        ],
        out_shape=[
            jax.ShapeDtypeStruct((t, d), F32),
            jax.ShapeDtypeStruct((t, d // 2), jnp.uint32),
            jax.ShapeDtypeStruct((t, LANES), jnp.int32),
            jax.ShapeDtypeStruct((t, LANES), F32),
        ],
        compiler_params=pltpu.CompilerParams(
            dimension_semantics=("arbitrary",), vmem_limit_bytes=VMEM_LIMIT),
        name="outproj_router",
    )(attn2, pool2, x2, w_out_bf16, g2.reshape(1, d), wr, br)


def _unpack_bf16_pair(words):
    hi = lax.bitcast_convert_type(words & jnp.uint32(0xFFFF0000), F32).astype(BF16)
    lo = lax.bitcast_convert_type(words << 16, F32).astype(BF16)
    return hi, lo


def _expert_kernel(te_ref, nt_ref, first_ref, slot_ref, nxt_ref, nv_ref, tok_ref,
                   hn_hbm, wg_hbm, wu_hbm, wd_hbm, y_ref,
                   xbuf, wgb, wub, wdb, gsem, wsem):
    j = pl.program_id(0)
    nt = nt_ref[0]
    tm, kh = xbuf.shape[1], xbuf.shape[2]

    def weight_copies(e, s):
        return (pltpu.make_async_copy(wg_hbm.at[e], wgb.at[s], wsem.at[s]),
                pltpu.make_async_copy(wu_hbm.at[e], wub.at[s], wsem.at[s]),
                pltpu.make_async_copy(wd_hbm.at[e], wdb.at[s], wsem.at[s]))

    def start_gather(tile, s):
        def issue(r, priority):
            tok = tok_ref[tile * tm + r]
            pltpu.make_async_copy(hn_hbm.at[pl.ds(tok, 1), :], xbuf.at[s, pl.ds(r, 1), :],
                                  gsem.at[s]).start(priority=priority)

        full = nv_ref[tile] == tm

        @pl.when(full)
        def _():
            for r in range(tm):
                issue(r, r % 2)

        @pl.when(jnp.logical_not(full))
        def _():
            def body(r, c):
                issue(r, 0)
                return c

            lax.fori_loop(0, nv_ref[tile], body, 0)

    @pl.when(j == 0)
    def _():
        xbuf[...] = jnp.zeros_like(xbuf)
        start_gather(0, 0)
        for c in weight_copies(te_ref[0], 0):
            c.start(priority=1)

    @pl.when(j < nt)
    def _():
        s = j % 2
        ws = slot_ref[j]

        @pl.when(j + 1 < nt)
        def _():
            start_gather(j + 1, 1 - s)

        @pl.when(first_ref[j] == 1)
        def _():
            for c in weight_copies(te_ref[j], ws):
                c.wait()

            @pl.when(nxt_ref[j] >= 0)
            def _():
                for c in weight_copies(nxt_ref[j], 1 - ws):
                    c.start(priority=1)

        nv = nv_ref[j]

        @pl.when(nv == tm)
        def _():
            pltpu.make_async_copy(hn_hbm.at[pl.ds(0, tm), :], xbuf.at[s], gsem.at[s]).wait()

        @pl.when(nv < tm)
        def _():
            def wait_row(r, c):
                pltpu.make_async_copy(hn_hbm.at[pl.ds(0, 1), :], xbuf.at[s, pl.ds(0, 1), :],
                                      gsem.at[s]).wait()
                return c

            lax.fori_loop(0, nv, wait_row, 0)

        xa, xb = _unpack_bf16_pair(xbuf[s])
        hg = (jnp.dot(xa, wgb[ws, :kh, :].astype(BF16), preferred_element_type=F32)
              + jnp.dot(xb, wgb[ws, kh:, :].astype(BF16), preferred_element_type=F32))
        hu = (jnp.dot(xa, wub[ws, :kh, :].astype(BF16), preferred_element_type=F32)
              + jnp.dot(xb, wub[ws, kh:, :].astype(BF16), preferred_element_type=F32))
        act = (hg / (1.0 + jnp.exp(-hg))) * hu
        y_ref[...] = _pack_bf16_pair(
            jnp.dot(act.astype(BF16), wdb[ws].astype(BF16), preferred_element_type=F32))

    @pl.when(j >= nt)
    def _():
        y_ref[...] = jnp.zeros_like(y_ref)


def _experts(plan, hn_packed, w_gate, w_up, w_down, tm=ROW_TILE):
    d, f = w_gate.shape[1], w_gate.shape[2]
    assert hn_packed.shape[1] * 2 == d
    nt = plan["tile_expert"].shape[0]
    any_spec = pl.BlockSpec(memory_space=pl.ANY)
    grid_spec = pltpu.PrefetchScalarGridSpec(
        num_scalar_prefetch=7,
        grid=(nt,),
        in_specs=[any_spec, any_spec, any_spec, any_spec],
        out_specs=pl.BlockSpec((tm, d // 2), lambda j, *_: (j, 0)),
        scratch_shapes=[
            pltpu.VMEM((2, tm, d // 2), jnp.uint32),
            pltpu.VMEM((2, d, f), F32),
            pltpu.VMEM((2, d, f), F32),
            pltpu.VMEM((2, f, d), F32),
            pltpu.SemaphoreType.DMA((2,)),
            pltpu.SemaphoreType.DMA((2,)),
        ],
    )
    return pl.pallas_call(
        _expert_kernel,
        grid_spec=grid_spec,
        out_shape=jax.ShapeDtypeStruct((nt * tm, d // 2), jnp.uint32),
        compiler_params=pltpu.CompilerParams(
            dimension_semantics=("arbitrary",), vmem_limit_bytes=VMEM_LIMIT),
        name="experts",
    )(plan["tile_expert"], plan["n_tiles"], plan["first"], plan["slot"], plan["next_expert"],
      plan["n_valid"], plan["tok_of"], hn_packed, w_gate, w_up, w_down)


def _combine_kernel(pos_ref, h_ref, rw_ref, ys_hbm, g_ref, o_ref, ybuf, sem):
    i = pl.program_id(0)
    tm = h_ref.shape[0]

    def start_gather(tile, s):
        def issue(r, c):
            for k in range(TOP_K_FINE):
                p = pos_ref[(tile * tm + r) * TOP_K_FINE + k]
                pltpu.make_async_copy(ys_hbm.at[pl.ds(p, 1), :], ybuf.at[s, k, pl.ds(r, 1), :],
                                      sem.at[s]).start(priority=k % 2)
            return c

        lax.fori_loop(0, tm, issue, 0, unroll=True)

    @pl.when(i == 0)
    def _():
        start_gather(0, 0)

    s = i % 2

    @pl.when(i + 1 < pl.num_programs(0))
    def _():
        start_gather(i + 1, 1 - s)

    for k in range(TOP_K_FINE):
        pltpu.make_async_copy(ys_hbm.at[pl.ds(0, tm), :], ybuf.at[s, k], sem.at[s]).wait()
    w = rw_ref[...]
    kh = ybuf.shape[3]
    halves = []
    for part in range(2):
        y = None
        for k in range(TOP_K_FINE):
            words = ybuf[s, k]
            bits = (words & jnp.uint32(0xFFFF0000)) if part == 0 else (words << 16)
            term = w[:, k:k + 1] * lax.bitcast_convert_type(bits, F32)
            y = term if y is None else y + term
        halves.append(h_ref[:, part * kh:(part + 1) * kh] + y)
    ms = sum(jnp.sum(v * v, axis=-1, keepdims=True) for v in halves) / (2 * kh)
    inv = lax.rsqrt(ms + RMS_EPS)
    for part in range(2):
        o_ref[:, part * kh:(part + 1) * kh] = halves[part] * inv * g_ref[:, part * kh:(part + 1) * kh]


def _combine(pos, h, rw, ys, g, tm=256):
    t, d = h.shape
    grid_spec = pltpu.PrefetchScalarGridSpec(
        num_scalar_prefetch=1,
        grid=(t // tm,),
        in_specs=[
            pl.BlockSpec((tm, d), lambda i, p: (i, 0)),
            pl.BlockSpec((tm, LANES), lambda i, p: (i, 0)),
            pl.BlockSpec(memory_space=pl.ANY),
            pl.BlockSpec((1, d), lambda i, p: (0, 0)),
        ],
        out_specs=pl.BlockSpec((tm, d), lambda i, p: (i, 0)),
        scratch_shapes=[pltpu.VMEM((2, TOP_K_FINE, tm, d // 2), jnp.uint32),
                        pltpu.SemaphoreType.DMA((2,))],
    )
    return pl.pallas_call(
        _combine_kernel,
        grid_spec=grid_spec,
        out_shape=jax.ShapeDtypeStruct((t, d), F32),
        compiler_params=pltpu.CompilerParams(
            dimension_semantics=("arbitrary",), vmem_limit_bytes=VMEM_LIMIT),
        name="combine",
    )(pos, h, rw, ys, g.reshape(1, d))


def _sort_plan(eidx, tm=ROW_TILE):
    t, k = eidx.shape
    a = t * k
    nt = a // tm + N_EXPERTS
    ef = eidx.reshape(a)
    onehot = (ef[:, None] == jnp.arange(N_EXPERTS, dtype=jnp.int32)[None, :]).astype(jnp.int32)
    cum = jnp.cumsum(onehot, axis=0)
    rank = jnp.sum(onehot * cum, axis=1) - 1
    counts = cum[-1]
    padded = ((counts + tm - 1) // tm) * tm
    off_end = jnp.cumsum(padded)
    off = off_end - padded
    pos = jnp.sum(onehot * off[None, :], axis=1) + rank
    tok_of = jnp.zeros((nt * tm,), jnp.int32).at[pos].set(jnp.arange(a, dtype=jnp.int32) // k)
    n_tiles = off_end[-1] // tm
    tile_idx = jnp.arange(nt, dtype=jnp.int32)
    te = jnp.sum((off_end[None, :] <= (tile_idx * tm)[:, None]).astype(jnp.int32), axis=1)
    te = jnp.minimum(te, N_EXPERTS - 1)
    te = jnp.where(tile_idx < n_tiles, te, te[n_tiles - 1]).astype(jnp.int32)
    first = ((tile_idx == 0) | (te != jnp.roll(te, 1))) & (tile_idx < n_tiles)
    slot = (jnp.cumsum(first.astype(jnp.int32)) - 1) % 2
    eid = jnp.arange(N_EXPERTS, dtype=jnp.int32)
    later = (eid[None, :] > eid[:, None]) & (counts[None, :] > 0)
    next_e = jnp.min(jnp.where(later, eid[None, :], N_EXPERTS), axis=1)
    next_e = jnp.where(next_e < N_EXPERTS, next_e, -1)
    n_valid = jnp.clip(off[te] + counts[te] - tile_idx * tm, 0, tm)
    n_valid = jnp.where(tile_idx < n_tiles, n_valid, 0).astype(jnp.int32)
    return dict(
        n_valid=n_valid,
        pos=pos.astype(jnp.int32), tok_of=tok_of, tile_expert=te,
        n_tiles=n_tiles.reshape(1).astype(jnp.int32), first=first.astype(jnp.int32),
        slot=slot.astype(jnp.int32), next_expert=next_e[te].astype(jnp.int32))


def kernel(x, norm1_g, w_in, lambda_q1, lambda_k1, lambda_q2, lambda_k2, subln_g, pool_w, pool_scale, w_out, norm2_g, w_coarse, b_coarse, w_fine, b_fine, w_gate, w_up, w_down, final_norm_g):
    b, s, d = x.shape
    t = b * s
    assert norm1_g.shape[0] == 1
    x2 = x.reshape(t, d)
    proj = _inproj(x2, norm1_g[0], w_in[0])
    proj3 = proj.reshape(b, s, proj.shape[1])
    attn = _attention(proj3, lambda_q1, lambda_k1, lambda_q2, lambda_k2, subln_g[0])
    pool = _pool(proj3, pool_w[0], pool_scale[0])

    ng, epg = N_EXPERT_GROUPS, EXPERTS_PER_GROUP
    wr = jnp.concatenate(
        [w_coarse[0], jnp.transpose(w_fine[0], (1, 0, 2)).reshape(d, ng * epg)], axis=1)
    wr = jnp.pad(wr, ((0, 0), (0, LANES - wr.shape[1])))
    br = jnp.concatenate([b_coarse[0], b_fine[0].reshape(ng * epg)])
    br = jnp.pad(br, (0, LANES - br.shape[0])).reshape(1, LANES)
    wr_hi = wr.astype(BF16)
    wr_lo = (wr - wr_hi.astype(F32)).astype(BF16)
    h, hn, ri, rw = _outproj(attn.reshape(t, ATTN_WIDTH), pool.reshape(t, POOL_WIDTH), x2,
                             w_out[0].astype(BF16), norm2_g[0], jnp.stack([wr_hi, wr_lo]), br)

    plan = _sort_plan(ri[:, :TOP_K_FINE])
    ys = _experts(plan, hn, w_gate[0], w_up[0], w_down[0])
    out = _combine(plan["pos"], h, rw, ys, final_norm_g)
    return out.reshape(b, s, d)
```

```python
import functools
import math

import jax
import jax.numpy as jnp
from jax import lax
from jax.experimental import pallas as pl
from jax.experimental.pallas import tpu as pltpu

N_DIFF_HEADS = 8
DIFF_HEAD_DIM = 64
DIFF_V_DIM = 2 * DIFF_HEAD_DIM
ATTN_WIDTH = N_DIFF_HEADS * DIFF_V_DIM
POOL_WINDOWS = (2, 4, 8, 16)
POOL_GROUP_DIM = 256
POOL_WIDTH = len(POOL_WINDOWS) * POOL_GROUP_DIM
N_EXPERT_GROUPS = 4
EXPERTS_PER_GROUP = 8
N_EXPERTS = N_EXPERT_GROUPS * EXPERTS_PER_GROUP
TOP_K_FINE = 2
RMS_EPS = 1e-6
NEG_INF = -1e30
LAM_INIT = 0.8 - 0.6 * math.exp(-0.3 * 0)
LOG2E = math.log2(math.e)
Q_SCALE = DIFF_HEAD_DIM ** -0.5 * LOG2E

LANES = 128
VMEM_LIMIT = 56 * 1024 * 1024
ROW_TILE = 256

F32 = jnp.float32
BF16 = jnp.bfloat16


def _rms(x, g):
    return x * lax.rsqrt(jnp.mean(x * x, axis=-1, keepdims=True) + RMS_EPS) * g


def _inproj_kernel(x_ref, g_ref, w_ref, o_ref, wb_ref):
    @pl.when(pl.program_id(1) == 0)
    def _():
        wb_ref[...] = w_ref[...].astype(BF16)

    xn = _rms(x_ref[...], g_ref[...]).astype(BF16)
    acc = jnp.dot(xn, wb_ref[...], preferred_element_type=F32)
    scale = jnp.where(pl.program_id(0) == 0, jnp.float32(Q_SCALE), jnp.float32(1.0))
    o_ref[...] = (acc * scale).astype(o_ref.dtype)


def _inproj(x2, g, w, tm=512, tn=ATTN_WIDTH):
    t, d = x2.shape
    n = w.shape[1]
    return pl.pallas_call(
        _inproj_kernel,
        grid=(n // tn, t // tm),
        in_specs=[
            pl.BlockSpec((tm, d), lambda j, i: (i, 0)),
            pl.BlockSpec((1, d), lambda j, i: (0, 0)),
            pl.BlockSpec((d, tn), lambda j, i: (0, j)),
        ],
        out_specs=pl.BlockSpec((tm, tn), lambda j, i: (i, j)),
        out_shape=jax.ShapeDtypeStruct((t, n), BF16),
        scratch_shapes=[pltpu.VMEM((d, tn), BF16)],
        compiler_params=pltpu.CompilerParams(
            dimension_semantics=("arbitrary", "arbitrary"), vmem_limit_bytes=VMEM_LIMIT),
        name="inproj",
    )(x2, g.reshape(1, d), w)


def _attn_kernel(lq1_ref, lk1_ref, lq2_ref, lk2_ref, q_ref, k_ref, v_ref, g_ref, o_ref,
                 tab_ref, vt_ref, *, tq):
    h = pl.program_id(0)
    s_len = q_ref.shape[0]
    nq = s_len // tq
    d, dv = DIFF_HEAD_DIM, DIFF_V_DIM

    @pl.when(pl.program_id(1) == 0)
    def _():
        slope = jnp.exp2(jnp.full((1, 1), -8.0 / N_DIFF_HEADS, F32) * (h + 1).astype(F32))
        rel = ((nq - 1) * tq + lax.broadcasted_iota(jnp.int32, (s_len, tq), 1)
               - lax.broadcasted_iota(jnp.int32, (s_len, tq), 0))
        tab_ref[...] = jnp.where(rel >= 0, (-LOG2E * slope) * rel.astype(F32), NEG_INF)

    vt_ref[:dv, :] = v_ref[...].astype(F32).T.astype(BF16)
    vt_ref[dv:, :] = jnp.ones((vt_ref.shape[0] - dv, s_len), BF16)

    lam = (jnp.exp(jnp.sum(lq1_ref[...] * lk1_ref[...], axis=1, keepdims=True))
           - jnp.exp(jnp.sum(lq2_ref[...] * lk2_ref[...], axis=1, keepdims=True)) + LAM_INIT)
    dn = (((1,), (1,)), ((), ()))
    lane = lax.broadcasted_iota(jnp.int32, (tq, 2 * d), 1)

    def scores(qi):
        n = (qi + 1) * tq
        q = q_ref[qi * tq:(qi + 1) * tq, :]
        zero = jnp.zeros_like(q)
        qh = (jnp.where(lane < d, q, zero), jnp.where(lane >= d, q, zero))
        kk = k_ref[:n, :]
        bias = tab_ref[(nq - 1 - qi) * tq:(nq - 1 - qi) * tq + n, :]
        return [lax.dot_general(kk, qh[a], dn, preferred_element_type=F32) + bias
                for a in range(2)]

    s_next = scores(0)
    for qi in range(nq):
        n = (qi + 1) * tq
        s_cur = s_next
        if qi + 1 < nq:
            s_next = scores(qi + 1)
        outs = []
        for a in range(2):
            s = s_cur[a]
            m = jnp.max(s, axis=0, keepdims=True)
            p = jnp.exp2(s - m).astype(BF16)
            acc = jnp.dot(vt_ref[:, :n], p, preferred_element_type=F32)
            outs.append(acc[:dv] / acc[dv:dv + 1])
        ot = outs[0] - lam * outs[1]
        yt = ot * lax.rsqrt(jnp.mean(ot * ot, axis=0, keepdims=True) + RMS_EPS) * g_ref[...]
        o_ref[qi * tq:(qi + 1) * tq, :] = (yt * (1.0 - LAM_INIT)).T.astype(o_ref.dtype)


def _attention(proj3, lq1, lk1, lq2, lk2, subln_g, tq=256):
    b, s, _ = proj3.shape
    dv = DIFF_V_DIM
    nh = N_DIFF_HEADS
    ones_rows = 16
    lam_spec = pl.BlockSpec((1, DIFF_HEAD_DIM), lambda hi, bi: (0, 0))
    return pl.pallas_call(
        functools.partial(_attn_kernel, tq=tq),
        grid=(nh, b),
        in_specs=[
            lam_spec, lam_spec, lam_spec, lam_spec,
            pl.BlockSpec((None, s, dv), lambda hi, bi: (bi, 0, hi)),
            pl.BlockSpec((None, s, dv), lambda hi, bi: (bi, 0, nh + hi)),
            pl.BlockSpec((None, s, dv), lambda hi, bi: (bi, 0, 2 * nh + hi)),
            pl.BlockSpec((dv, 1), lambda hi, bi: (0, 0)),
        ],
        out_specs=pl.BlockSpec((None, s, dv), lambda hi, bi: (bi, 0, hi)),
        out_shape=jax.ShapeDtypeStruct((b, s, ATTN_WIDTH), BF16),
        scratch_shapes=[
            pltpu.VMEM((s, tq), F32),
            pltpu.VMEM((dv + ones_rows, s), BF16),
        ],
        compiler_params=pltpu.CompilerParams(
            dimension_semantics=("arbitrary", "arbitrary"),
            vmem_limit_bytes=VMEM_LIMIT),
        name="diff_attn",
    )(lq1, lk1, lq2, lk2, proj3, proj3, proj3, subln_g.reshape(dv, 1))


def _pool_kernel(u_ref, w_ref, sc_ref, o_ref):
    s = u_ref.shape[0]
    c = POOL_GROUP_DIM
    t = lax.broadcasted_iota(jnp.int32, (s, 1), 0)
    for gi, win in enumerate(POOL_WINDOWS):
        ch = u_ref[:, gi * c:(gi + 1) * c].astype(F32)
        acc = ch
        span = 1
        while span < win:
            acc = acc + jnp.where(t >= span, pltpu.roll(acc, span, axis=0), 0.0)
            span *= 2
        count = jnp.minimum(t + 1, win).astype(F32)
        pooled = acc / count - ch
        mixed = jnp.dot(pooled.astype(BF16), w_ref[gi].astype(BF16), preferred_element_type=F32)
        o_ref[:, gi * c:(gi + 1) * c] = (mixed * sc_ref[:, gi * c:(gi + 1) * c]).astype(o_ref.dtype)


def _pool(proj3, pool_w, pool_scale):
    b, s, n = proj3.shape
    assert all(w & (w - 1) == 0 for w in POOL_WINDOWS)
    return pl.pallas_call(
        _pool_kernel,
        grid=(b,),
        in_specs=[
            pl.BlockSpec((None, s, POOL_WIDTH), lambda bi: (bi, 0, n // POOL_WIDTH - 1)),
            pl.BlockSpec(pool_w.shape, lambda bi: (0, 0, 0)),
            pl.BlockSpec((1, POOL_WIDTH), lambda bi: (0, 0)),
        ],
        out_specs=pl.BlockSpec((None, s, POOL_WIDTH), lambda bi: (bi, 0, 0)),
        out_shape=jax.ShapeDtypeStruct((b, s, POOL_WIDTH), BF16),
        compiler_params=pltpu.CompilerParams(
            dimension_semantics=("arbitrary",), vmem_limit_bytes=VMEM_LIMIT),
        name="pool_mixer",
    )(proj3, pool_w, pool_scale.reshape(1, POOL_WIDTH))


def _pack_bf16_pair(x):
    bits = lax.bitcast_convert_type(x.astype(BF16).astype(F32), jnp.uint32)
    half = bits.shape[1] // 2
    return bits[:, :half] | (bits[:, half:] >> 16)


def _outproj_kernel(a_ref, p_ref, x_ref, w_ref, g_ref, wr_ref, br_ref,
                    h_ref, hn_ref, ri_ref, rw_ref, *, n_sub):
    ka = a_ref.shape[1]
    hs = x_ref.shape[0] // n_sub
    normed = []
    for r in range(n_sub):
        rows = pl.ds(r * hs, hs)
        mixed = (jnp.dot(a_ref[rows, :], w_ref[:ka, :], preferred_element_type=F32)
                 + jnp.dot(p_ref[rows, :], w_ref[ka:, :], preferred_element_type=F32))
        h = x_ref[rows, :] + mixed
        h_ref[rows, :] = h
        hn = _rms(h, g_ref[...])
        hn_ref[rows, :] = _pack_bf16_pair(hn)
        hn_hi = hn.astype(BF16)
        normed.append((hn_hi, (hn - hn_hi.astype(F32)).astype(BF16)))
    for r in range(n_sub):
        rows = pl.ds(r * hs, hs)
        ri, rw = _route(normed[r][0], normed[r][1], wr_ref, br_ref)
        ri_ref[rows, :] = ri
        rw_ref[rows, :] = rw


def _route(hn_hi, hn_lo, wr_ref, br_ref):
    wr_hi = wr_ref[0]
    logits = (jnp.dot(hn_hi, wr_hi, preferred_element_type=F32)
              + jnp.dot(hn_lo, wr_hi, preferred_element_type=F32)
              + jnp.dot(hn_hi, wr_ref[1], preferred_element_type=F32)) + br_ref[...]
    ng, epg = N_EXPERT_GROUPS, EXPERTS_PER_GROUP
    lane = lax.broadcasted_iota(jnp.int32, logits.shape, 1)
    big = jnp.int32(LANES)
    low = jnp.float32(-3.0e38)
    cm = lane < ng
    c = jnp.where(cm, logits, low)
    cmax = jnp.max(c, axis=1, keepdims=True)
    gsel = jnp.min(jnp.where(c == cmax, lane, big), axis=1, keepdims=True)
    p_group = 1.0 / jnp.sum(jnp.where(cm, jnp.exp(c - cmax), 0.0), axis=1, keepdims=True)
    f_lo = ng + epg * gsel
    fm = (lane >= f_lo) & (lane < f_lo + epg)
    f = jnp.where(fm, logits, low)
    v1 = jnp.max(f, axis=1, keepdims=True)
    i1 = jnp.min(jnp.where(fm & (f == v1), lane, big), axis=1, keepdims=True)
    fm2 = fm & (lane != i1)
    f2 = jnp.where(fm2, logits, low)
    v2 = jnp.max(f2, axis=1, keepdims=True)
    i2 = jnp.min(jnp.where(fm2 & (f2 == v2), lane, big), axis=1, keepdims=True)
    e21 = jnp.exp(v2 - v1)
    w1 = p_group / (1.0 + e21)
    w2 = p_group * e21 / (1.0 + e21)
    return (jnp.where(lane == 0, i1 - ng, jnp.where(lane == 1, i2 - ng, 0)),
            jnp.where(lane == 0, w1, jnp.where(lane == 1, w2, 0.0)))


def _outproj(attn2, pool2, x2, w_out_bf16, g2, wr, br, tm=512, n_sub=2):
    t, d = x2.shape
    ka, kp = attn2.shape[1], pool2.shape[1]
    row = lambda i: (i, 0)
    const = lambda i: (0, 0)
    return pl.pallas_call(
        functools.partial(_outproj_kernel, n_sub=n_sub),
        grid=(t // tm,),
        in_specs=[
            pl.BlockSpec((tm, ka), row),
            pl.BlockSpec((tm, kp), row),
            pl.BlockSpec((tm, d), row),
            pl.BlockSpec((ka + kp, d), const, pipeline_mode=pl.Buffered(1)),
            pl.BlockSpec((1, d), const),
            pl.BlockSpec((2, d, LANES), lambda i: (0, 0, 0)),
            pl.BlockSpec((1, LANES), const),
        ],
        out_specs=[
            pl.BlockSpec((tm, d), row),
            pl.BlockSpec((tm, d // 2), row),
            pl.BlockSpec((tm, LANES), row),
            pl.BlockSpec((tm, LANES), row),
        ],
        out_shape=[
            jax.ShapeDtypeStruct((t, d), F32),
            jax.ShapeDtypeStruct((t, d // 2), jnp.uint32),
            jax.ShapeDtypeStruct((t, LANES), jnp.int32),
            jax.ShapeDtypeStruct((t, LANES), F32),
        ],
        compiler_params=pltpu.CompilerParams(
            dimension_semantics=("arbitrary",), vmem_limit_bytes=VMEM_LIMIT),
        name="outproj_router",
    )(attn2, pool2, x2, w_out_bf16, g2.reshape(1, d), wr, br)


def _unpack_bf16_pair(words):
    hi = lax.bitcast_convert_type(words & jnp.uint32(0xFFFF0000), F32).astype(BF16)
    lo = lax.bitcast_convert_type(words << 16, F32).astype(BF16)
    return hi, lo


def _expert_kernel(te_ref, nt_ref, first_ref, slot_ref, nxt_ref, tok_ref,
                   hn_hbm, wg_hbm, wu_hbm, wd_hbm, y_ref,
                   xbuf0, xbuf1, wgb, wub, wdb, gsem, wsem):
    j = pl.program_id(0)
    nt = nt_ref[0]
    tm, kh = xbuf0.shape
    bufs = (xbuf0, xbuf1)

    def weight_copies(e, s):
        return (pltpu.make_async_copy(wg_hbm.at[e], wgb.at[s], wsem.at[s]),
                pltpu.make_async_copy(wu_hbm.at[e], wub.at[s], wsem.at[s]),
                pltpu.make_async_copy(wd_hbm.at[e], wdb.at[s], wsem.at[s]))

    def start_gather(tile, buf, sm):
        for r in range(tm):
            tok = tok_ref[tile * tm + r]
            pltpu.make_async_copy(hn_hbm.at[pl.ds(tok, 1), :], buf.at[pl.ds(r, 1), :], sm).start()

    def wait_gather(buf, sm):
        pltpu.make_async_copy(hn_hbm.at[pl.ds(0, tm), :], buf, sm).wait()

    @pl.when(j == 0)
    def _():
        start_gather(0, xbuf0, gsem.at[0])
        for c in weight_copies(te_ref[0], 0):
            c.start(priority=1)

    @pl.when(j < nt)
    def _():
        ws = slot_ref[j]

        @pl.when(first_ref[j] == 1)
        def _():
            for c in weight_copies(te_ref[j], ws):
                c.wait()

            @pl.when(nxt_ref[j] >= 0)
            def _():
                for c in weight_copies(nxt_ref[j], 1 - ws):
                    c.start(priority=1)

        def step(par):
            cur, nxt = bufs[par], bufs[1 - par]
            wait_gather(cur, gsem.at[par])
            start_gather(j + 1, nxt, gsem.at[1 - par])
            xa, xb = _unpack_bf16_pair(cur[...])
            hg = (jnp.dot(xa, wgb[ws, :kh, :].astype(BF16), preferred_element_type=F32)
                  + jnp.dot(xb, wgb[ws, kh:, :].astype(BF16), preferred_element_type=F32))
            hu = (jnp.dot(xa, wub[ws, :kh, :].astype(BF16), preferred_element_type=F32)
                  + jnp.dot(xb, wub[ws, kh:, :].astype(BF16), preferred_element_type=F32))
            act = (hg / (1.0 + jnp.exp(-hg))) * hu
            y_ref[...] = _pack_bf16_pair(
                jnp.dot(act.astype(BF16), wdb[ws].astype(BF16), preferred_element_type=F32))

            @pl.when(j == nt - 1)
            def _():
                wait_gather(nxt, gsem.at[1 - par])

        for par in range(2):
            pl.when(j % 2 == par)(functools.partial(step, par))

    @pl.when(j >= nt)
    def _():
        y_ref[...] = jnp.zeros_like(y_ref)


def _experts(plan, hn_packed, w_gate, w_up, w_down, tm=ROW_TILE):
    d, f = w_gate.shape[1], w_gate.shape[2]
    assert hn_packed.shape[1] * 2 == d
    nt = plan["tile_expert"].shape[0]
    any_spec = pl.BlockSpec(memory_space=pl.ANY)
    grid_spec = pltpu.PrefetchScalarGridSpec(
        num_scalar_prefetch=6,
        grid=(nt,),
        in_specs=[any_spec, any_spec, any_spec, any_spec],
        out_specs=pl.BlockSpec((tm, d // 2), lambda j, *_: (j, 0)),
        scratch_shapes=[
            pltpu.VMEM((tm, d // 2), jnp.uint32),
            pltpu.VMEM((tm, d // 2), jnp.uint32),
            pltpu.VMEM((2, d, f), F32),
            pltpu.VMEM((2, d, f), F32),
            pltpu.VMEM((2, f, d), F32),
            pltpu.SemaphoreType.DMA((2,)),
            pltpu.SemaphoreType.DMA((2,)),
        ],
    )
    return pl.pallas_call(
        _expert_kernel,
        grid_spec=grid_spec,
        out_shape=jax.ShapeDtypeStruct((nt * tm, d // 2), jnp.uint32),
        compiler_params=pltpu.CompilerParams(
            dimension_semantics=("arbitrary",), vmem_limit_bytes=VMEM_LIMIT),
        name="experts",
    )(plan["tile_expert"], plan["n_tiles"], plan["first"], plan["slot"], plan["next_expert"],
      plan["tok_of"], hn_packed, w_gate, w_up, w_down)


def _combine_kernel(pos_ref, h_ref, rw_ref, ys_hbm, g_ref, o_ref, ybuf0, ybuf1, sem):
    i = pl.program_id(0)
    last = pl.num_programs(0) - 1
    tm = h_ref.shape[0]
    kh = ybuf0.shape[2]
    bufs = (ybuf0, ybuf1)

    def start_gather(tile, buf, sm):
        for r in range(tm):
            for k in range(TOP_K_FINE):
                p = pos_ref[(tile * tm + r) * TOP_K_FINE + k]
                pltpu.make_async_copy(ys_hbm.at[pl.ds(p, 1), :], buf.at[k, pl.ds(r, 1), :], sm).start()

    def wait_gather(buf, sm):
        for k in range(TOP_K_FINE):
            pltpu.make_async_copy(ys_hbm.at[pl.ds(0, tm), :], buf.at[k], sm).wait()

    @pl.when(i == 0)
    def _():
        start_gather(0, ybuf0, sem.at[0])

    def step(par):
        cur, nxt = bufs[par], bufs[1 - par]
        wait_gather(cur, sem.at[par])
        start_gather(jnp.minimum(i + 1, last), nxt, sem.at[1 - par])
        w = rw_ref[...]
        halves = []
        for part in range(2):
            y = None
            for k in range(TOP_K_FINE):
                words = cur[k]
                bits = (words & jnp.uint32(0xFFFF0000)) if part == 0 else (words << 16)
                term = w[:, k:k + 1] * lax.bitcast_convert_type(bits, F32)
                y = term if y is None else y + term
            halves.append(h_ref[:, part * kh:(part + 1) * kh] + y)
        ms = sum(jnp.sum(v * v, axis=-1, keepdims=True) for v in halves) / (2 * kh)
        inv = lax.rsqrt(ms + RMS_EPS)
        for part in range(2):
            o_ref[:, part * kh:(part + 1) * kh] = (
                halves[part] * inv * g_ref[:, part * kh:(part + 1) * kh])

        @pl.when(i == last)
        def _():
            wait_gather(nxt, sem.at[1 - par])

    for par in range(2):
        pl.when(i % 2 == par)(functools.partial(step, par))


def _combine(pos, h, rw, ys, g, tm=256):
    t, d = h.shape
    grid_spec = pltpu.PrefetchScalarGridSpec(
        num_scalar_prefetch=1,
        grid=(t // tm,),
        in_specs=[
            pl.BlockSpec((tm, d), lambda i, p: (i, 0)),
            pl.BlockSpec((tm, LANES), lambda i, p: (i, 0)),
            pl.BlockSpec(memory_space=pl.ANY),
            pl.BlockSpec((1, d), lambda i, p: (0, 0)),
        ],
        out_specs=pl.BlockSpec((tm, d), lambda i, p: (i, 0)),
        scratch_shapes=[pltpu.VMEM((TOP_K_FINE, tm, d // 2), jnp.uint32),
                        pltpu.VMEM((TOP_K_FINE, tm, d // 2), jnp.uint32),
                        pltpu.SemaphoreType.DMA((2,))],
    )
    return pl.pallas_call(
        _combine_kernel,
        grid_spec=grid_spec,
        out_shape=jax.ShapeDtypeStruct((t, d), F32),
        compiler_params=pltpu.CompilerParams(
            dimension_semantics=("arbitrary",), vmem_limit_bytes=VMEM_LIMIT),
        name="combine",
    )(pos, h, rw, ys, g.reshape(1, d))


def _sort_plan(eidx, tm=ROW_TILE):
    t, k = eidx.shape
    a = t * k
    nt = a // tm + N_EXPERTS
    ef = eidx.reshape(a)
    onehot = (ef[:, None] == jnp.arange(N_EXPERTS, dtype=jnp.int32)[None, :]).astype(jnp.int32)
    cum = jnp.cumsum(onehot, axis=0)
    rank = jnp.sum(onehot * cum, axis=1) - 1
    counts = cum[-1]
    padded = ((counts + tm - 1) // tm) * tm
    off_end = jnp.cumsum(padded)
    off = off_end - padded
    pos = jnp.sum(onehot * off[None, :], axis=1) + rank
    tok_of = jnp.zeros(((nt + 1) * tm,), jnp.int32).at[pos].set(jnp.arange(a, dtype=jnp.int32) // k)
    n_tiles = off_end[-1] // tm
    tile_idx = jnp.arange(nt, dtype=jnp.int32)
    te = jnp.sum((off_end[None, :] <= (tile_idx * tm)[:, None]).astype(jnp.int32), axis=1)
    te = jnp.minimum(te, N_EXPERTS - 1)
    te = jnp.where(tile_idx < n_tiles, te, te[n_tiles - 1]).astype(jnp.int32)
    first = ((tile_idx == 0) | (te != jnp.roll(te, 1))) & (tile_idx < n_tiles)
    slot = (jnp.cumsum(first.astype(jnp.int32)) - 1) % 2
    eid = jnp.arange(N_EXPERTS, dtype=jnp.int32)
    later = (eid[None, :] > eid[:, None]) & (counts[None, :] > 0)
    next_e = jnp.min(jnp.where(later, eid[None, :], N_EXPERTS), axis=1)
    next_e = jnp.where(next_e < N_EXPERTS, next_e, -1)
    return dict(
        pos=pos.astype(jnp.int32), tok_of=tok_of, tile_expert=te,
        n_tiles=n_tiles.reshape(1).astype(jnp.int32), first=first.astype(jnp.int32),
        slot=slot.astype(jnp.int32), next_expert=next_e[te].astype(jnp.int32))


def kernel(x, norm1_g, w_in, lambda_q1, lambda_k1, lambda_q2, lambda_k2, subln_g, pool_w, pool_scale, w_out, norm2_g, w_coarse, b_coarse, w_fine, b_fine, w_gate, w_up, w_down, final_norm_g):
    b, s, d = x.shape
    t = b * s
    assert norm1_g.shape[0] == 1
    x2 = x.reshape(t, d)
    proj = _inproj(x2, norm1_g[0], w_in[0])
    proj3 = proj.reshape(b, s, proj.shape[1])
    attn = _attention(proj3, lambda_q1, lambda_k1, lambda_q2, lambda_k2, subln_g[0])
    pool = _pool(proj3, pool_w[0], pool_scale[0])

    ng, epg = N_EXPERT_GROUPS, EXPERTS_PER_GROUP
    wr = jnp.concatenate(
        [w_coarse[0], jnp.transpose(w_fine[0], (1, 0, 2)).reshape(d, ng * epg)], axis=1)
    wr = jnp.pad(wr, ((0, 0), (0, LANES - wr.shape[1])))
    br = jnp.concatenate([b_coarse[0], b_fine[0].reshape(ng * epg)])
    br = jnp.pad(br, (0, LANES - br.shape[0])).reshape(1, LANES)
    wr_hi = wr.astype(BF16)
    wr_lo = (wr - wr_hi.astype(F32)).astype(BF16)
    h, hn, ri, rw = _outproj(attn.reshape(t, ATTN_WIDTH), pool.reshape(t, POOL_WIDTH), x2,
                             w_out[0].astype(BF16), norm2_g[0], jnp.stack([wr_hi, wr_lo]), br)

    plan = _sort_plan(ri[:, :TOP_K_FINE])
    ys = _experts(plan, hn, w_gate[0], w_up[0], w_down[0])
    out = _combine(plan["pos"], h, rw, ys, final_norm_g)
    return out.reshape(b, s, d)
```

```python
import functools
import math

import jax
import jax.numpy as jnp
from jax import lax
from jax.experimental import pallas as pl
from jax.experimental.pallas import tpu as pltpu

N_DIFF_HEADS = 8
DIFF_HEAD_DIM = 64
DIFF_V_DIM = 2 * DIFF_HEAD_DIM
ATTN_WIDTH = N_DIFF_HEADS * DIFF_V_DIM
POOL_WINDOWS = (2, 4, 8, 16)
POOL_GROUP_DIM = 256
POOL_WIDTH = len(POOL_WINDOWS) * POOL_GROUP_DIM
N_EXPERT_GROUPS = 4
EXPERTS_PER_GROUP = 8
N_EXPERTS = N_EXPERT_GROUPS * EXPERTS_PER_GROUP
TOP_K_FINE = 2
RMS_EPS = 1e-6
NEG_INF = -1e30
LAM_INIT = 0.8 - 0.6 * math.exp(-0.3 * 0)
LOG2E = math.log2(math.e)
Q_SCALE = DIFF_HEAD_DIM ** -0.5 * LOG2E

LANES = 128
VMEM_LIMIT = 56 * 1024 * 1024
ROW_TILE = 256

F32 = jnp.float32
BF16 = jnp.bfloat16


def _rms(x, g):
    return x * lax.rsqrt(jnp.mean(x * x, axis=-1, keepdims=True) + RMS_EPS) * g


def _inproj_kernel(x_ref, g_ref, w_ref, o_ref, wb_ref):
    @pl.when(pl.program_id(1) == 0)
    def _():
        wb_ref[...] = w_ref[...].astype(BF16)

    xn = _rms(x_ref[...], g_ref[...]).astype(BF16)
    acc = jnp.dot(xn, wb_ref[...], preferred_element_type=F32)
    scale = jnp.where(pl.program_id(0) == 0, jnp.float32(Q_SCALE), jnp.float32(1.0))
    o_ref[...] = (acc * scale).astype(o_ref.dtype)


def _inproj(x2, g, w, tm=512, tn=ATTN_WIDTH):
    t, d = x2.shape
    n = w.shape[1]
    return pl.pallas_call(
        _inproj_kernel,
        grid=(n // tn, t // tm),
        in_specs=[
            pl.BlockSpec((tm, d), lambda j, i: (i, 0)),
            pl.BlockSpec((1, d), lambda j, i: (0, 0)),
            pl.BlockSpec((d, tn), lambda j, i: (0, j)),
        ],
        out_specs=pl.BlockSpec((tm, tn), lambda j, i: (i, j)),
        out_shape=jax.ShapeDtypeStruct((t, n), BF16),
        scratch_shapes=[pltpu.VMEM((d, tn), BF16)],
        compiler_params=pltpu.CompilerParams(
            dimension_semantics=("arbitrary", "arbitrary"), vmem_limit_bytes=VMEM_LIMIT),
        name="inproj",
    )(x2, g.reshape(1, d), w)


def _attn_kernel(lq1_ref, lk1_ref, lq2_ref, lk2_ref, q_ref, k_ref, v_ref, g_ref, o_ref,
                 tab_ref, vt_ref, *, tq):
    h = pl.program_id(0)
    s_len = q_ref.shape[0]
    nq = s_len // tq
    d, dv = DIFF_HEAD_DIM, DIFF_V_DIM

    @pl.when(pl.program_id(1) == 0)
    def _():
        slope = jnp.exp2(jnp.full((1, 1), -8.0 / N_DIFF_HEADS, F32) * (h + 1).astype(F32))
        rel = ((nq - 1) * tq + lax.broadcasted_iota(jnp.int32, (s_len, tq), 1)
               - lax.broadcasted_iota(jnp.int32, (s_len, tq), 0))
        tab_ref[...] = jnp.where(rel >= 0, (-LOG2E * slope) * rel.astype(F32), NEG_INF)

    vt_ref[:dv, :] = v_ref[...].astype(F32).T.astype(BF16)
    vt_ref[dv:, :] = jnp.ones((vt_ref.shape[0] - dv, s_len), BF16)

    lam = (jnp.exp(jnp.sum(lq1_ref[...] * lk1_ref[...], axis=1, keepdims=True))
           - jnp.exp(jnp.sum(lq2_ref[...] * lk2_ref[...], axis=1, keepdims=True)) + LAM_INIT)
    dn = (((1,), (1,)), ((), ()))
    lane = lax.broadcasted_iota(jnp.int32, (tq, 2 * d), 1)

    def scores(qi):
        n = (qi + 1) * tq
        q = q_ref[qi * tq:(qi + 1) * tq, :]
        zero = jnp.zeros_like(q)
        qh = (jnp.where(lane < d, q, zero), jnp.where(lane >= d, q, zero))
        kk = k_ref[:n, :]
        bias = tab_ref[(nq - 1 - qi) * tq:(nq - 1 - qi) * tq + n, :]
        return [lax.dot_general(kk, qh[a], dn, preferred_element_type=F32) + bias
                for a in range(2)]

    s_next = scores(0)
    for qi in range(nq):
        n = (qi + 1) * tq
        s_cur = s_next
        if qi + 1 < nq:
            s_next = scores(qi + 1)
        outs = []
        for a in range(2):
            s = s_cur[a]
            m = jnp.max(s, axis=0, keepdims=True)
            p = jnp.exp2(s - m).astype(BF16)
            acc = jnp.dot(vt_ref[:, :n], p, preferred_element_type=F32)
            outs.append(acc[:dv] / acc[dv:dv + 1])
        ot = outs[0] - lam * outs[1]
        yt = ot * lax.rsqrt(jnp.mean(ot * ot, axis=0, keepdims=True) + RMS_EPS) * g_ref[...]
        o_ref[qi * tq:(qi + 1) * tq, :] = (yt * (1.0 - LAM_INIT)).T.astype(o_ref.dtype)


def _attention(proj3, lq1, lk1, lq2, lk2, subln_g, tq=256):
    b, s, _ = proj3.shape
    dv = DIFF_V_DIM
    nh = N_DIFF_HEADS
    ones_rows = 16
    lam_spec = pl.BlockSpec((1, DIFF_HEAD_DIM), lambda hi, bi: (0, 0))
    return pl.pallas_call(
        functools.partial(_attn_kernel, tq=tq),
        grid=(nh, b),
        in_specs=[
            lam_spec, lam_spec, lam_spec, lam_spec,
            pl.BlockSpec((None, s, dv), lambda hi, bi: (bi, 0, hi)),
            pl.BlockSpec((None, s, dv), lambda hi, bi: (bi, 0, nh + hi)),
            pl.BlockSpec((None, s, dv), lambda hi, bi: (bi, 0, 2 * nh + hi)),
            pl.BlockSpec((dv, 1), lambda hi, bi: (0, 0)),
        ],
        out_specs=pl.BlockSpec((None, s, dv), lambda hi, bi: (bi, 0, hi)),
        out_shape=jax.ShapeDtypeStruct((b, s, ATTN_WIDTH), BF16),
        scratch_shapes=[
            pltpu.VMEM((s, tq), F32),
            pltpu.VMEM((dv + ones_rows, s), BF16),
        ],
        compiler_params=pltpu.CompilerParams(
            dimension_semantics=("arbitrary", "arbitrary"),
            vmem_limit_bytes=VMEM_LIMIT),
        name="diff_attn",
    )(lq1, lk1, lq2, lk2, proj3, proj3, proj3, subln_g.reshape(dv, 1))


def _pool_kernel(u_ref, w_ref, sc_ref, o_ref):
    s = u_ref.shape[0]
    c = POOL_GROUP_DIM
    t = lax.broadcasted_iota(jnp.int32, (s, 1), 0)
    for gi, win in enumerate(POOL_WINDOWS):
        ch = u_ref[:, gi * c:(gi + 1) * c].astype(F32)
        acc = ch
        span = 1
        while span < win:
            acc = acc + jnp.where(t >= span, pltpu.roll(acc, span, axis=0), 0.0)
            span *= 2
        count = jnp.minimum(t + 1, win).astype(F32)
        pooled = acc / count - ch
        mixed = jnp.dot(pooled.astype(BF16), w_ref[gi].astype(BF16), preferred_element_type=F32)
        o_ref[:, gi * c:(gi + 1) * c] = (mixed * sc_ref[:, gi * c:(gi + 1) * c]).astype(o_ref.dtype)


def _pool(proj3, pool_w, pool_scale):
    b, s, n = proj3.shape
    assert all(w & (w - 1) == 0 for w in POOL_WINDOWS)
    return pl.pallas_call(
        _pool_kernel,
        grid=(b,),
        in_specs=[
            pl.BlockSpec((None, s, POOL_WIDTH), lambda bi: (bi, 0, n // POOL_WIDTH - 1)),
            pl.BlockSpec(pool_w.shape, lambda bi: (0, 0, 0)),
            pl.BlockSpec((1, POOL_WIDTH), lambda bi: (0, 0)),
        ],
        out_specs=pl.BlockSpec((None, s, POOL_WIDTH), lambda bi: (bi, 0, 0)),
        out_shape=jax.ShapeDtypeStruct((b, s, POOL_WIDTH), BF16),
        compiler_params=pltpu.CompilerParams(
            dimension_semantics=("arbitrary",), vmem_limit_bytes=VMEM_LIMIT),
        name="pool_mixer",
    )(proj3, pool_w, pool_scale.reshape(1, POOL_WIDTH))


def _pack_bf16_pair(x):
    bits = lax.bitcast_convert_type(x.astype(BF16).astype(F32), jnp.uint32)
    half = bits.shape[1] // 2
    return bits[:, :half] | (bits[:, half:] >> 16)


def _outproj_kernel(a_ref, p_ref, x_ref, w_ref, g_ref, wr_ref, br_ref,
                    h_ref, hn_ref, ri_ref, rw_ref, *, n_sub):
    ka = a_ref.shape[1]
    hs = x_ref.shape[0] // n_sub
    normed = []
    for r in range(n_sub):
        rows = pl.ds(r * hs, hs)
        mixed = (jnp.dot(a_ref[rows, :], w_ref[:ka, :], preferred_element_type=F32)
                 + jnp.dot(p_ref[rows, :], w_ref[ka:, :], preferred_element_type=F32))
        h = x_ref[rows, :] + mixed
        h_ref[rows, :] = h
        hn = _rms(h, g_ref[...])
        hn_ref[rows, :] = _pack_bf16_pair(hn)
        hn_hi = hn.astype(BF16)
        normed.append((hn_hi, (hn - hn_hi.astype(F32)).astype(BF16)))
    for r in range(n_sub):
        rows = pl.ds(r * hs, hs)
        ri, rw = _route(normed[r][0], normed[r][1], wr_ref, br_ref)
        ri_ref[rows, :] = ri
        rw_ref[rows, :] = rw


def _route(hn_hi, hn_lo, wr_ref, br_ref):
    wr_hi = wr_ref[0]
    logits = (jnp.dot(hn_hi, wr_hi, preferred_element_type=F32)
              + jnp.dot(hn_lo, wr_hi, preferred_element_type=F32)
              + jnp.dot(hn_hi, wr_ref[1], preferred_element_type=F32)) + br_ref[...]
    ng, epg = N_EXPERT_GROUPS, EXPERTS_PER_GROUP
    lane = lax.broadcasted_iota(jnp.int32, logits.shape, 1)
    big = jnp.int32(LANES)
    low = jnp.float32(-3.0e38)
    cm = lane < ng
    c = jnp.where(cm, logits, low)
    cmax = jnp.max(c, axis=1, keepdims=True)
    gsel = jnp.min(jnp.where(c == cmax, lane, big), axis=1, keepdims=True)
    p_group = 1.0 / jnp.sum(jnp.where(cm, jnp.exp(c - cmax), 0.0), axis=1, keepdims=True)
    f_lo = ng + epg * gsel
    fm = (lane >= f_lo) & (lane < f_lo + epg)
    f = jnp.where(fm, logits, low)
    v1 = jnp.max(f, axis=1, keepdims=True)
    i1 = jnp.min(jnp.where(fm & (f == v1), lane, big), axis=1, keepdims=True)
    fm2 = fm & (lane != i1)
    f2 = jnp.where(fm2, logits, low)
    v2 = jnp.max(f2, axis=1, keepdims=True)
    i2 = jnp.min(jnp.where(fm2 & (f2 == v2), lane, big), axis=1, keepdims=True)
    e21 = jnp.exp(v2 - v1)
    w1 = p_group / (1.0 + e21)
    w2 = p_group * e21 / (1.0 + e21)
    return (jnp.where(lane == 0, i1 - ng, jnp.where(lane == 1, i2 - ng, 0)),
            jnp.where(lane == 0, w1, jnp.where(lane == 1, w2, 0.0)))


def _outproj(attn2, pool2, x2, w_out_bf16, g2, wr, br, tm=512, n_sub=2):
    t, d = x2.shape
    ka, kp = attn2.shape[1], pool2.shape[1]
    row = lambda i: (i, 0)
    const = lambda i: (0, 0)
    return pl.pallas_call(
        functools.partial(_outproj_kernel, n_sub=n_sub),
        grid=(t // tm,),
        in_specs=[
            pl.BlockSpec((tm, ka), row),
            pl.BlockSpec((tm, kp), row),
            pl.BlockSpec((tm, d), row),
            pl.BlockSpec((ka + kp, d), const, pipeline_mode=pl.Buffered(1)),
            pl.BlockSpec((1, d), const),
            pl.BlockSpec((2, d, LANES), lambda i: (0, 0, 0)),
            pl.BlockSpec((1, LANES), const),
        ],
        out_specs=[
            pl.BlockSpec((tm, d), row),
            pl.BlockSpec((tm, d // 2), row),
            pl.BlockSpec((tm, LANES), row),
            pl.BlockSpec((tm, LANES), row),
        ],
        out_shape=[
            jax.ShapeDtypeStruct((t, d), F32),
            jax.ShapeDtypeStruct((t, d // 2), jnp.uint32),
            jax.ShapeDtypeStruct((t, LANES), jnp.int32),
            jax.ShapeDtypeStruct((t, LANES), F32),
        ],
        compiler_params=pltpu.CompilerParams(
            dimension_semantics=("arbitrary",), vmem_limit_bytes=VMEM_LIMIT),
        name="outproj_router",
    )(attn2, pool2, x2, w_out_bf16, g2.reshape(1, d), wr, br)


def _unpack_bf16_pair(words):
    hi = lax.bitcast_convert_type(words & jnp.uint32(0xFFFF0000), F32).astype(BF16)
    lo = lax.bitcast_convert_type(words << 16, F32).astype(BF16)
    return hi, lo


def _expert_kernel(te_ref, nt_ref, first_ref, slot_ref, nxt_ref, tok_ref,
                   hn_hbm, wg_hbm, wu_hbm, wd_hbm, y_ref,
                   xbuf0, xbuf1, xbuf2, wgb, wub, wdb, gsem, wsem):
    j = pl.program_id(0)
    nt = nt_ref[0]
    tm, kh = xbuf0.shape
    bufs = (xbuf0, xbuf1, xbuf2)
    nbuf = len(bufs)

    def weight_copies(e, s):
        return (pltpu.make_async_copy(wg_hbm.at[e], wgb.at[s], wsem.at[s]),
                pltpu.make_async_copy(wu_hbm.at[e], wub.at[s], wsem.at[s]),
                pltpu.make_async_copy(wd_hbm.at[e], wdb.at[s], wsem.at[s]))

    def start_gather(tile, buf, sm):
        for r in range(tm):
            tok = tok_ref[tile * tm + r]
            pltpu.make_async_copy(hn_hbm.at[pl.ds(tok, 1), :], buf.at[pl.ds(r, 1), :], sm).start()

    def wait_gather(buf, sm):
        pltpu.make_async_copy(hn_hbm.at[pl.ds(0, tm), :], buf, sm).wait()

    @pl.when(j == 0)
    def _():
        start_gather(0, xbuf0, gsem.at[0])
        start_gather(1, xbuf1, gsem.at[1])
        for c in weight_copies(te_ref[0], 0):
            c.start(priority=1)

    @pl.when(j < nt)
    def _():
        ws = slot_ref[j]

        @pl.when(first_ref[j] == 1)
        def _():
            for c in weight_copies(te_ref[j], ws):
                c.wait()

            @pl.when(nxt_ref[j] >= 0)
            def _():
                for c in weight_copies(nxt_ref[j], 1 - ws):
                    c.start(priority=1)

        def step(par):
            cur = bufs[par]
            ahead = (par + nbuf - 1) % nbuf
            wait_gather(cur, gsem.at[par])
            start_gather(j + nbuf - 1, bufs[ahead], gsem.at[ahead])
            xa, xb = _unpack_bf16_pair(cur[...])
            hg = (jnp.dot(xa, wgb[ws, :kh, :].astype(BF16), preferred_element_type=F32)
                  + jnp.dot(xb, wgb[ws, kh:, :].astype(BF16), preferred_element_type=F32))
            hu = (jnp.dot(xa, wub[ws, :kh, :].astype(BF16), preferred_element_type=F32)
                  + jnp.dot(xb, wub[ws, kh:, :].astype(BF16), preferred_element_type=F32))
            act = (hg / (1.0 + jnp.exp(-hg))) * hu
            y_ref[...] = _pack_bf16_pair(
                jnp.dot(act.astype(BF16), wdb[ws].astype(BF16), preferred_element_type=F32))

            @pl.when(j == nt - 1)
            def _():
                for other in range(nbuf):
                    if other != par:
                        wait_gather(bufs[other], gsem.at[other])

        for par in range(nbuf):
            pl.when(j % nbuf == par)(functools.partial(step, par))

    @pl.when(j >= nt)
    def _():
        y_ref[...] = jnp.zeros_like(y_ref)


def _experts(plan, hn_packed, w_gate, w_up, w_down, tm=ROW_TILE):
    d, f = w_gate.shape[1], w_gate.shape[2]
    assert hn_packed.shape[1] * 2 == d
    nt = plan["tile_expert"].shape[0]
    any_spec = pl.BlockSpec(memory_space=pl.ANY)
    grid_spec = pltpu.PrefetchScalarGridSpec(
        num_scalar_prefetch=6,
        grid=(nt,),
        in_specs=[any_spec, any_spec, any_spec, any_spec],
        out_specs=pl.BlockSpec((tm, d // 2), lambda j, *_: (j, 0)),
        scratch_shapes=[
            pltpu.VMEM((tm, d // 2), jnp.uint32),
            pltpu.VMEM((tm, d // 2), jnp.uint32),
            pltpu.VMEM((tm, d // 2), jnp.uint32),
            pltpu.VMEM((2, d, f), F32),
            pltpu.VMEM((2, d, f), F32),
            pltpu.VMEM((2, f, d), F32),
            pltpu.SemaphoreType.DMA((3,)),
            pltpu.SemaphoreType.DMA((2,)),
        ],
    )
    return pl.pallas_call(
        _expert_kernel,
        grid_spec=grid_spec,
        out_shape=jax.ShapeDtypeStruct((nt * tm, d // 2), jnp.uint32),
        compiler_params=pltpu.CompilerParams(
            dimension_semantics=("arbitrary",), vmem_limit_bytes=VMEM_LIMIT),
        name="experts",
    )(plan["tile_expert"], plan["n_tiles"], plan["first"], plan["slot"], plan["next_expert"],
      plan["tok_of"], hn_packed, w_gate, w_up, w_down)


def _combine_kernel(pos_ref, h_ref, rw_ref, ys_hbm, g_ref, o_ref, ybuf0, ybuf1, ybuf2, sem):
    i = pl.program_id(0)
    last = pl.num_programs(0) - 1
    tm = h_ref.shape[0]
    kh = ybuf0.shape[2]
    bufs = (ybuf0, ybuf1, ybuf2)
    nbuf = len(bufs)

    def start_gather(tile, buf, sm):
        for r in range(tm):
            for k in range(TOP_K_FINE):
                p = pos_ref[(tile * tm + r) * TOP_K_FINE + k]
                pltpu.make_async_copy(ys_hbm.at[pl.ds(p, 1), :], buf.at[k, pl.ds(r, 1), :], sm).start()

    def wait_gather(buf, sm):
        for k in range(TOP_K_FINE):
            pltpu.make_async_copy(ys_hbm.at[pl.ds(0, tm), :], buf.at[k], sm).wait()

    @pl.when(i == 0)
    def _():
        start_gather(0, ybuf0, sem.at[0])
        start_gather(1, ybuf1, sem.at[1])

    def step(par):
        cur = bufs[par]
        ahead = (par + nbuf - 1) % nbuf
        wait_gather(cur, sem.at[par])
        start_gather(jnp.minimum(i + nbuf - 1, last), bufs[ahead], sem.at[ahead])
        w = rw_ref[...]
        halves = []
        for part in range(2):
            y = None
            for k in range(TOP_K_FINE):
                words = cur[k]
                bits = (words & jnp.uint32(0xFFFF0000)) if part == 0 else (words << 16)
                term = w[:, k:k + 1] * lax.bitcast_convert_type(bits, F32)
                y = term if y is None else y + term
            halves.append(h_ref[:, part * kh:(part + 1) * kh] + y)
        ms = sum(jnp.sum(v * v, axis=-1, keepdims=True) for v in halves) / (2 * kh)
        inv = lax.rsqrt(ms + RMS_EPS)
        for part in range(2):
            o_ref[:, part * kh:(part + 1) * kh] = (
                halves[part] * inv * g_ref[:, part * kh:(part + 1) * kh])

        @pl.when(i == last)
        def _():
            for other in range(nbuf):
                if other != par:
                    wait_gather(bufs[other], sem.at[other])

    for par in range(nbuf):
        pl.when(i % nbuf == par)(functools.partial(step, par))


def _combine(pos, h, rw, ys, g, tm=256):
    t, d = h.shape
    grid_spec = pltpu.PrefetchScalarGridSpec(
        num_scalar_prefetch=1,
        grid=(t // tm,),
        in_specs=[
            pl.BlockSpec((tm, d), lambda i, p: (i, 0)),
            pl.BlockSpec((tm, LANES), lambda i, p: (i, 0)),
            pl.BlockSpec(memory_space=pl.ANY),
            pl.BlockSpec((1, d), lambda i, p: (0, 0)),
        ],
        out_specs=pl.BlockSpec((tm, d), lambda i, p: (i, 0)),
        scratch_shapes=[pltpu.VMEM((TOP_K_FINE, tm, d // 2), jnp.uint32),
                        pltpu.VMEM((TOP_K_FINE, tm, d // 2), jnp.uint32),
                        pltpu.VMEM((TOP_K_FINE, tm, d // 2), jnp.uint32),
                        pltpu.SemaphoreType.DMA((3,))],
    )
    return pl.pallas_call(
        _combine_kernel,
        grid_spec=grid_spec,
        out_shape=jax.ShapeDtypeStruct((t, d), F32),
        compiler_params=pltpu.CompilerParams(
            dimension_semantics=("arbitrary",), vmem_limit_bytes=VMEM_LIMIT),
        name="combine",
    )(pos, h, rw, ys, g.reshape(1, d))


def _sort_plan(eidx, tm=ROW_TILE):
    t, k = eidx.shape
    a = t * k
    nt = a // tm + N_EXPERTS
    ef = eidx.reshape(a)
    onehot = (ef[:, None] == jnp.arange(N_EXPERTS, dtype=jnp.int32)[None, :]).astype(jnp.int32)
    cum = jnp.cumsum(onehot, axis=0)
    rank = jnp.sum(onehot * cum, axis=1) - 1
    counts = cum[-1]
    padded = ((counts + tm - 1) // tm) * tm
    off_end = jnp.cumsum(padded)
    off = off_end - padded
    pos = jnp.sum(onehot * off[None, :], axis=1) + rank
    tok_of = jnp.zeros(((nt + 2) * tm,), jnp.int32).at[pos].set(jnp.arange(a, dtype=jnp.int32) // k)
    n_tiles = off_end[-1] // tm
    tile_idx = jnp.arange(nt, dtype=jnp.int32)
    te = jnp.sum((off_end[None, :] <= (tile_idx * tm)[:, None]).astype(jnp.int32), axis=1)
    te = jnp.minimum(te, N_EXPERTS - 1)
    te = jnp.where(tile_idx < n_tiles, te, te[n_tiles - 1]).astype(jnp.int32)
    first = ((tile_idx == 0) | (te != jnp.roll(te, 1))) & (tile_idx < n_tiles)
    slot = (jnp.cumsum(first.astype(jnp.int32)) - 1) % 2
    eid = jnp.arange(N_EXPERTS, dtype=jnp.int32)
    later = (eid[None, :] > eid[:, None]) & (counts[None, :] > 0)
    next_e = jnp.min(jnp.where(later, eid[None, :], N_EXPERTS), axis=1)
    next_e = jnp.where(next_e < N_EXPERTS, next_e, -1)
    return dict(
        pos=pos.astype(jnp.int32), tok_of=tok_of, tile_expert=te,
        n_tiles=n_tiles.reshape(1).astype(jnp.int32), first=first.astype(jnp.int32),
        slot=slot.astype(jnp.int32), next_expert=next_e[te].astype(jnp.int32))


def kernel(x, norm1_g, w_in, lambda_q1, lambda_k1, lambda_q2, lambda_k2, subln_g, pool_w, pool_scale, w_out, norm2_g, w_coarse, b_coarse, w_fine, b_fine, w_gate, w_up, w_down, final_norm_g):
    b, s, d = x.shape
    t = b * s
    assert norm1_g.shape[0] == 1
    x2 = x.reshape(t, d)
    proj = _inproj(x2, norm1_g[0], w_in[0])
    proj3 = proj.reshape(b, s, proj.shape[1])
    attn = _attention(proj3, lambda_q1, lambda_k1, lambda_q2, lambda_k2, subln_g[0])
    pool = _pool(proj3, pool_w[0], pool_scale[0])

    ng, epg = N_EXPERT_GROUPS, EXPERTS_PER_GROUP
    wr = jnp.concatenate(
        [w_coarse[0], jnp.transpose(w_fine[0], (1, 0, 2)).reshape(d, ng * epg)], axis=1)
    wr = jnp.pad(wr, ((0, 0), (0, LANES - wr.shape[1])))
    br = jnp.concatenate([b_coarse[0], b_fine[0].reshape(ng * epg)])
    br = jnp.pad(br, (0, LANES - br.shape[0])).reshape(1, LANES)
    wr_hi = wr.astype(BF16)
    wr_lo = (wr - wr_hi.astype(F32)).astype(BF16)
    h, hn, ri, rw = _outproj(attn.reshape(t, ATTN_WIDTH), pool.reshape(t, POOL_WIDTH), x2,
                             w_out[0].astype(BF16), norm2_g[0], jnp.stack([wr_hi, wr_lo]), br)

    plan = _sort_plan(ri[:, :TOP_K_FINE])
    ys = _experts(plan, hn, w_gate[0], w_up[0], w_down[0])
    out = _combine(plan["pos"], h, rw, ys, final_norm_g)
    return out.reshape(b, s, d)
```

```python
import functools
import math

import jax
import jax.numpy as jnp
from jax import lax
from jax.experimental import pallas as pl
from jax.experimental.pallas import tpu as pltpu

N_DIFF_HEADS = 8
DIFF_HEAD_DIM = 64
DIFF_V_DIM = 2 * DIFF_HEAD_DIM
ATTN_WIDTH = N_DIFF_HEADS * DIFF_V_DIM
POOL_WINDOWS = (2, 4, 8, 16)
POOL_GROUP_DIM = 256
POOL_WIDTH = len(POOL_WINDOWS) * POOL_GROUP_DIM
N_EXPERT_GROUPS = 4
EXPERTS_PER_GROUP = 8
N_EXPERTS = N_EXPERT_GROUPS * EXPERTS_PER_GROUP
TOP_K_FINE = 2
RMS_EPS = 1e-6
NEG_INF = -1e30
LAM_INIT = 0.8 - 0.6 * math.exp(-0.3 * 0)
LOG2E = math.log2(math.e)
Q_SCALE = DIFF_HEAD_DIM ** -0.5 * LOG2E

LANES = 128
VMEM_LIMIT = 56 * 1024 * 1024
ROW_TILE = 256

F32 = jnp.float32
BF16 = jnp.bfloat16


def _rms(x, g):
    return x * lax.rsqrt(jnp.mean(x * x, axis=-1, keepdims=True) + RMS_EPS) * g


def _inproj_kernel(x_ref, g_ref, w_ref, o_ref, wb_ref):
    @pl.when(pl.program_id(1) == 0)
    def _():
        wb_ref[...] = w_ref[...].astype(BF16)

    xn = _rms(x_ref[...], g_ref[...]).astype(BF16)
    acc = jnp.dot(xn, wb_ref[...], preferred_element_type=F32)
    scale = jnp.where(pl.program_id(0) == 0, jnp.float32(Q_SCALE), jnp.float32(1.0))
    o_ref[...] = (acc * scale).astype(o_ref.dtype)


def _inproj(x2, g, w, tm=512, tn=ATTN_WIDTH):
    t, d = x2.shape
    n = w.shape[1]
    return pl.pallas_call(
        _inproj_kernel,
        grid=(n // tn, t // tm),
        in_specs=[
            pl.BlockSpec((tm, d), lambda j, i: (i, 0)),
            pl.BlockSpec((1, d), lambda j, i: (0, 0)),
            pl.BlockSpec((d, tn), lambda j, i: (0, j)),
        ],
        out_specs=pl.BlockSpec((tm, tn), lambda j, i: (i, j)),
        out_shape=jax.ShapeDtypeStruct((t, n), BF16),
        scratch_shapes=[pltpu.VMEM((d, tn), BF16)],
        compiler_params=pltpu.CompilerParams(
            dimension_semantics=("arbitrary", "arbitrary"), vmem_limit_bytes=VMEM_LIMIT),
        name="inproj",
    )(x2, g.reshape(1, d), w)


def _attn_kernel(lq1_ref, lk1_ref, lq2_ref, lk2_ref, q_ref, k_ref, v_ref, g_ref, o_ref,
                 tab_ref, vt_ref, *, tq):
    h = pl.program_id(0)
    s_len = q_ref.shape[0]
    nq = s_len // tq
    d, dv = DIFF_HEAD_DIM, DIFF_V_DIM

    @pl.when(pl.program_id(1) == 0)
    def _():
        slope = jnp.exp2(jnp.full((1, 1), -8.0 / N_DIFF_HEADS, F32) * (h + 1).astype(F32))
        rel = ((nq - 1) * tq + lax.broadcasted_iota(jnp.int32, (s_len, tq), 1)
               - lax.broadcasted_iota(jnp.int32, (s_len, tq), 0))
        tab_ref[...] = jnp.where(rel >= 0, (-LOG2E * slope) * rel.astype(F32), NEG_INF)

    vt_ref[:dv, :] = v_ref[...].astype(F32).T.astype(BF16)
    vt_ref[dv:, :] = jnp.ones((vt_ref.shape[0] - dv, s_len), BF16)

    lam = (jnp.exp(jnp.sum(lq1_ref[...] * lk1_ref[...], axis=1, keepdims=True))
           - jnp.exp(jnp.sum(lq2_ref[...] * lk2_ref[...], axis=1, keepdims=True)) + LAM_INIT)
    dn = (((1,), (1,)), ((), ()))
    lane = lax.broadcasted_iota(jnp.int32, (tq, 2 * d), 1)

    def scores(qi):
        n = (qi + 1) * tq
        q = q_ref[qi * tq:(qi + 1) * tq, :]
        zero = jnp.zeros_like(q)
        qh = (jnp.where(lane < d, q, zero), jnp.where(lane >= d, q, zero))
        kk = k_ref[:n, :]
        bias = tab_ref[(nq - 1 - qi) * tq:(nq - 1 - qi) * tq + n, :]
        return [lax.dot_general(kk, qh[a], dn, preferred_element_type=F32) + bias
                for a in range(2)]

    s_next = scores(0)
    for qi in range(nq):
        n = (qi + 1) * tq
        s_cur = s_next
        if qi + 1 < nq:
            s_next = scores(qi + 1)
        outs = []
        for a in range(2):
            s = s_cur[a]
            m = jnp.max(s, axis=0, keepdims=True)
            p = jnp.exp2(s - m).astype(BF16)
            acc = jnp.dot(vt_ref[:, :n], p, preferred_element_type=F32)
            outs.append(acc[:dv] / acc[dv:dv + 1])
        ot = outs[0] - lam * outs[1]
        yt = ot * lax.rsqrt(jnp.mean(ot * ot, axis=0, keepdims=True) + RMS_EPS) * g_ref[...]
        o_ref[qi * tq:(qi + 1) * tq, :] = (yt * (1.0 - LAM_INIT)).T.astype(o_ref.dtype)


def _attention(proj3, lq1, lk1, lq2, lk2, subln_g, tq=256):
    b, s, _ = proj3.shape
    dv = DIFF_V_DIM
    nh = N_DIFF_HEADS
    ones_rows = 16
    lam_spec = pl.BlockSpec((1, DIFF_HEAD_DIM), lambda hi, bi: (0, 0))
    return pl.pallas_call(
        functools.partial(_attn_kernel, tq=tq),
        grid=(nh, b),
        in_specs=[
            lam_spec, lam_spec, lam_spec, lam_spec,
            pl.BlockSpec((None, s, dv), lambda hi, bi: (bi, 0, hi)),
            pl.BlockSpec((None, s, dv), lambda hi, bi: (bi, 0, nh + hi)),
            pl.BlockSpec((None, s, dv), lambda hi, bi: (bi, 0, 2 * nh + hi)),
            pl.BlockSpec((dv, 1), lambda hi, bi: (0, 0)),
        ],
        out_specs=pl.BlockSpec((None, s, dv), lambda hi, bi: (bi, 0, hi)),
        out_shape=jax.ShapeDtypeStruct((b, s, ATTN_WIDTH), BF16),
        scratch_shapes=[
            pltpu.VMEM((s, tq), F32),
            pltpu.VMEM((dv + ones_rows, s), BF16),
        ],
        compiler_params=pltpu.CompilerParams(
            dimension_semantics=("arbitrary", "arbitrary"),
            vmem_limit_bytes=VMEM_LIMIT),
        name="diff_attn",
    )(lq1, lk1, lq2, lk2, proj3, proj3, proj3, subln_g.reshape(dv, 1))


def _pool_kernel(u_ref, w_ref, sc_ref, o_ref):
    s = u_ref.shape[0]
    c = POOL_GROUP_DIM
    t = lax.broadcasted_iota(jnp.int32, (s, 1), 0)
    for gi, win in enumerate(POOL_WINDOWS):
        ch = u_ref[:, gi * c:(gi + 1) * c].astype(F32)
        acc = ch
        span = 1
        while span < win:
            acc = acc + jnp.where(t >= span, pltpu.roll(acc, span, axis=0), 0.0)
            span *= 2
        count = jnp.minimum(t + 1, win).astype(F32)
        pooled = acc / count - ch
        mixed = jnp.dot(pooled.astype(BF16), w_ref[gi].astype(BF16), preferred_element_type=F32)
        o_ref[:, gi * c:(gi + 1) * c] = (mixed * sc_ref[:, gi * c:(gi + 1) * c]).astype(o_ref.dtype)


def _pool(proj3, pool_w, pool_scale):
    b, s, n = proj3.shape
    assert all(w & (w - 1) == 0 for w in POOL_WINDOWS)
    return pl.pallas_call(
        _pool_kernel,
        grid=(b,),
        in_specs=[
            pl.BlockSpec((None, s, POOL_WIDTH), lambda bi: (bi, 0, n // POOL_WIDTH - 1)),
            pl.BlockSpec(pool_w.shape, lambda bi: (0, 0, 0)),
            pl.BlockSpec((1, POOL_WIDTH), lambda bi: (0, 0)),
        ],
        out_specs=pl.BlockSpec((None, s, POOL_WIDTH), lambda bi: (bi, 0, 0)),
        out_shape=jax.ShapeDtypeStruct((b, s, POOL_WIDTH), BF16),
        compiler_params=pltpu.CompilerParams(
            dimension_semantics=("arbitrary",), vmem_limit_bytes=VMEM_LIMIT),
        name="pool_mixer",
    )(proj3, pool_w, pool_scale.reshape(1, POOL_WIDTH))


def _pack_bf16_pair(x):
    bits = lax.bitcast_convert_type(x.astype(BF16).astype(F32), jnp.uint32)
    half = bits.shape[1] // 2
    return bits[:, :half] | (bits[:, half:] >> 16)


def _outproj_kernel(a_ref, p_ref, x_ref, w_ref, g_ref, wr_ref, br_ref,
                    h_ref, hn_ref, ri_ref, rw_ref, *, n_sub):
    ka = a_ref.shape[1]
    hs = x_ref.shape[0] // n_sub
    normed = []
    for r in range(n_sub):
        rows = pl.ds(r * hs, hs)
        mixed = (jnp.dot(a_ref[rows, :], w_ref[:ka, :], preferred_element_type=F32)
                 + jnp.dot(p_ref[rows, :], w_ref[ka:, :], preferred_element_type=F32))
        h = x_ref[rows, :] + mixed
        h_ref[rows, :] = h
        hn = _rms(h, g_ref[...])
        hn_ref[rows, :] = _pack_bf16_pair(hn)
        hn_hi = hn.astype(BF16)
        normed.append((hn_hi, (hn - hn_hi.astype(F32)).astype(BF16)))
    for r in range(n_sub):
        rows = pl.ds(r * hs, hs)
        ri, rw = _route(normed[r][0], normed[r][1], wr_ref, br_ref)
        ri_ref[rows, :] = ri
        rw_ref[rows, :] = rw


def _route(hn_hi, hn_lo, wr_ref, br_ref):
    wr_hi = wr_ref[0]
    logits = (jnp.dot(hn_hi, wr_hi, preferred_element_type=F32)
              + jnp.dot(hn_lo, wr_hi, preferred_element_type=F32)
              + jnp.dot(hn_hi, wr_ref[1], preferred_element_type=F32)) + br_ref[...]
    ng, epg = N_EXPERT_GROUPS, EXPERTS_PER_GROUP
    lane = lax.broadcasted_iota(jnp.int32, logits.shape, 1)
    big = jnp.int32(LANES)
    low = jnp.float32(-3.0e38)
    cm = lane < ng
    c = jnp.where(cm, logits, low)
    cmax = jnp.max(c, axis=1, keepdims=True)
    gsel = jnp.min(jnp.where(c == cmax, lane, big), axis=1, keepdims=True)
    p_group = 1.0 / jnp.sum(jnp.where(cm, jnp.exp(c - cmax), 0.0), axis=1, keepdims=True)
    f_lo = ng + epg * gsel
    fm = (lane >= f_lo) & (lane < f_lo + epg)
    f = jnp.where(fm, logits, low)
    v1 = jnp.max(f, axis=1, keepdims=True)
    i1 = jnp.min(jnp.where(fm & (f == v1), lane, big), axis=1, keepdims=True)
    fm2 = fm & (lane != i1)
    f2 = jnp.where(fm2, logits, low)
    v2 = jnp.max(f2, axis=1, keepdims=True)
    i2 = jnp.min(jnp.where(fm2 & (f2 == v2), lane, big), axis=1, keepdims=True)
    e21 = jnp.exp(v2 - v1)
    w1 = p_group / (1.0 + e21)
    w2 = p_group * e21 / (1.0 + e21)
    return (jnp.where(lane == 0, i1 - ng, jnp.where(lane == 1, i2 - ng, 0)),
            jnp.where(lane == 0, w1, jnp.where(lane == 1, w2, 0.0)))


def _outproj(attn2, pool2, x2, w_out_bf16, g2, wr, br, tm=512, n_sub=2):
    t, d = x2.shape
    ka, kp = attn2.shape[1], pool2.shape[1]
    row = lambda i: (i, 0)
    const = lambda i: (0, 0)
    return pl.pallas_call(
        functools.partial(_outproj_kernel, n_sub=n_sub),
        grid=(t // tm,),
        in_specs=[
            pl.BlockSpec((tm, ka), row),
            pl.BlockSpec((tm, kp), row),
            pl.BlockSpec((tm, d), row),
            pl.BlockSpec((ka + kp, d), const, pipeline_mode=pl.Buffered(1)),
            pl.BlockSpec((1, d), const),
            pl.BlockSpec((2, d, LANES), lambda i: (0, 0, 0)),
            pl.BlockSpec((1, LANES), const),
        ],
        out_specs=[
            pl.BlockSpec((tm, d), row),
            pl.BlockSpec((tm, d // 2), row),
            pl.BlockSpec((tm, LANES), row),
            pl.BlockSpec((tm, LANES), row),
        ],
        out_shape=[
            jax.ShapeDtypeStruct((t, d), F32),
            jax.ShapeDtypeStruct((t, d // 2), jnp.uint32),
            jax.ShapeDtypeStruct((t, LANES), jnp.int32),
            jax.ShapeDtypeStruct((t, LANES), F32),
        ],
        compiler_params=pltpu.CompilerParams(
            dimension_semantics=("arbitrary",), vmem_limit_bytes=VMEM_LIMIT),
        name="outproj_router",
    )(attn2, pool2, x2, w_out_bf16, g2.reshape(1, d), wr, br)


def _unpack_bf16_pair(words):
    hi = lax.bitcast_convert_type(words & jnp.uint32(0xFFFF0000), F32).astype(BF16)
    lo = lax.bitcast_convert_type(words << 16, F32).astype(BF16)
    return hi, lo


def _expert_kernel(te_ref, nt_ref, first_ref, slot_ref, nxt_ref,
                   x_ref, wg_hbm, wu_hbm, wd_hbm, y_ref,
                   wgb, wub, wdb, wsem):
    j = pl.program_id(0)
    nt = nt_ref[0]
    kh = x_ref.shape[1]

    def weight_copies(e, s):
        return (pltpu.make_async_copy(wg_hbm.at[e], wgb.at[s], wsem.at[s]),
                pltpu.make_async_copy(wu_hbm.at[e], wub.at[s], wsem.at[s]),
                pltpu.make_async_copy(wd_hbm.at[e], wdb.at[s], wsem.at[s]))

    @pl.when(j == 0)
    def _():
        for c in weight_copies(te_ref[0], 0):
            c.start(priority=1)

    @pl.when(j < nt)
    def _():
        ws = slot_ref[j]

        @pl.when(first_ref[j] == 1)
        def _():
            for c in weight_copies(te_ref[j], ws):
                c.wait()

            @pl.when(nxt_ref[j] >= 0)
            def _():
                for c in weight_copies(nxt_ref[j], 1 - ws):
                    c.start(priority=1)

        xa, xb = _unpack_bf16_pair(x_ref[...])
        hg = (jnp.dot(xa, wgb[ws, :kh, :].astype(BF16), preferred_element_type=F32)
              + jnp.dot(xb, wgb[ws, kh:, :].astype(BF16), preferred_element_type=F32))
        hu = (jnp.dot(xa, wub[ws, :kh, :].astype(BF16), preferred_element_type=F32)
              + jnp.dot(xb, wub[ws, kh:, :].astype(BF16), preferred_element_type=F32))
        act = (hg / (1.0 + jnp.exp(-hg))) * hu
        y_ref[...] = _pack_bf16_pair(
            jnp.dot(act.astype(BF16), wdb[ws].astype(BF16), preferred_element_type=F32))

    @pl.when(j >= nt)
    def _():
        y_ref[...] = jnp.zeros_like(y_ref)


def _experts(plan, xs, w_gate, w_up, w_down, tm=ROW_TILE):
    d, f = w_gate.shape[1], w_gate.shape[2]
    assert xs.shape[1] * 2 == d
    nt = plan["tile_expert"].shape[0]
    any_spec = pl.BlockSpec(memory_space=pl.ANY)
    grid_spec = pltpu.PrefetchScalarGridSpec(
        num_scalar_prefetch=5,
        grid=(nt,),
        in_specs=[
            pl.BlockSpec((tm, d // 2), lambda j, te, n, *_: (jnp.minimum(j, n[0] - 1), 0)),
            any_spec, any_spec, any_spec],
        out_specs=pl.BlockSpec((tm, d // 2), lambda j, *_: (j, 0)),
        scratch_shapes=[
            pltpu.VMEM((2, d, f), F32),
            pltpu.VMEM((2, d, f), F32),
            pltpu.VMEM((2, f, d), F32),
            pltpu.SemaphoreType.DMA((2,)),
        ],
    )
    return pl.pallas_call(
        _expert_kernel,
        grid_spec=grid_spec,
        out_shape=jax.ShapeDtypeStruct((nt * tm, d // 2), jnp.uint32),
        compiler_params=pltpu.CompilerParams(
            dimension_semantics=("arbitrary",), vmem_limit_bytes=VMEM_LIMIT),
        name="experts",
    )(plan["tile_expert"], plan["n_tiles"], plan["first"], plan["slot"], plan["next_expert"],
      xs, w_gate, w_up, w_down)


def _combine_kernel(pos_ref, h_ref, rw_ref, ys_hbm, g_ref, o_ref, ybuf0, ybuf1, ybuf2, sem):
    i = pl.program_id(0)
    last = pl.num_programs(0) - 1
    tm = h_ref.shape[0]
    kh = ybuf0.shape[2]
    bufs = (ybuf0, ybuf1, ybuf2)
    nbuf = len(bufs)

    def start_gather(tile, buf, sm):
        for r in range(tm):
            for k in range(TOP_K_FINE):
                p = pos_ref[(tile * tm + r) * TOP_K_FINE + k]
                pltpu.make_async_copy(ys_hbm.at[pl.ds(p, 1), :], buf.at[k, pl.ds(r, 1), :],
                                      sm).start(priority=k % 2)

    def wait_gather(buf, sm):
        for k in range(TOP_K_FINE):
            pltpu.make_async_copy(ys_hbm.at[pl.ds(0, tm), :], buf.at[k], sm).wait()

    @pl.when(i == 0)
    def _():
        start_gather(0, ybuf0, sem.at[0])
        start_gather(1, ybuf1, sem.at[1])

    def step(par):
        cur = bufs[par]
        ahead = (par + nbuf - 1) % nbuf
        wait_gather(cur, sem.at[par])
        start_gather(jnp.minimum(i + nbuf - 1, last), bufs[ahead], sem.at[ahead])
        w = rw_ref[...]
        halves = []
        for part in range(2):
            y = None
            for k in range(TOP_K_FINE):
                words = cur[k]
                bits = (words & jnp.uint32(0xFFFF0000)) if part == 0 else (words << 16)
                term = w[:, k:k + 1] * lax.bitcast_convert_type(bits, F32)
                y = term if y is None else y + term
            halves.append(h_ref[:, part * kh:(part + 1) * kh] + y)
        ms = sum(jnp.sum(v * v, axis=-1, keepdims=True) for v in halves) / (2 * kh)
        inv = lax.rsqrt(ms + RMS_EPS)
        for part in range(2):
            o_ref[:, part * kh:(part + 1) * kh] = (
                halves[part] * inv * g_ref[:, part * kh:(part + 1) * kh])

        @pl.when(i == last)
        def _():
            for other in range(nbuf):
                if other != par:
                    wait_gather(bufs[other], sem.at[other])

    for par in range(nbuf):
        pl.when(i % nbuf == par)(functools.partial(step, par))


def _combine(pos, h, rw, ys, g, tm=256):
    t, d = h.shape
    grid_spec = pltpu.PrefetchScalarGridSpec(
        num_scalar_prefetch=1,
        grid=(t // tm,),
        in_specs=[
            pl.BlockSpec((tm, d), lambda i, p: (i, 0)),
            pl.BlockSpec((tm, LANES), lambda i, p: (i, 0)),
            pl.BlockSpec(memory_space=pl.ANY),
            pl.BlockSpec((1, d), lambda i, p: (0, 0)),
        ],
        out_specs=pl.BlockSpec((tm, d), lambda i, p: (i, 0)),
        scratch_shapes=[pltpu.VMEM((TOP_K_FINE, tm, d // 2), jnp.uint32),
                        pltpu.VMEM((TOP_K_FINE, tm, d // 2), jnp.uint32),
                        pltpu.VMEM((TOP_K_FINE, tm, d // 2), jnp.uint32),
                        pltpu.SemaphoreType.DMA((3,))],
    )
    return pl.pallas_call(
        _combine_kernel,
        grid_spec=grid_spec,
        out_shape=jax.ShapeDtypeStruct((t, d), F32),
        compiler_params=pltpu.CompilerParams(
            dimension_semantics=("arbitrary",), vmem_limit_bytes=VMEM_LIMIT),
        name="combine",
    )(pos, h, rw, ys, g.reshape(1, d))


def _rank_kernel(ri_ref, meta_ref, cnt_ref, carry_ref, tri_ref):
    tt = ri_ref.shape[0]
    ne = cnt_ref.shape[0]

    @pl.when(pl.program_id(0) == 0)
    def _():
        carry_ref[...] = jnp.zeros_like(carry_ref)
        earlier = (lax.broadcasted_iota(jnp.int32, (tt, tt), 0)
                   < lax.broadcasted_iota(jnp.int32, (tt, tt), 1))
        tri_ref[...] = jnp.where(earlier, 1.0, 0.0).astype(BF16)

    rit = ri_ref[...].astype(F32).T
    expert = lax.broadcasted_iota(jnp.int32, (ne, tt), 0).astype(F32)
    ranks = []
    hits = []
    for k in range(TOP_K_FINE):
        hits.append(expert == rit[k:k + 1, :])
    chosen = jnp.where(hits[0] | hits[1], 1.0, 0.0)
    before = (jnp.dot(chosen.astype(BF16), tri_ref[...], preferred_element_type=F32)
              + carry_ref[:, 0:1])
    for k in range(TOP_K_FINE):
        meta_ref[k:k + 1, :] = rit[k:k + 1, :].astype(jnp.int32)
        rank = jnp.sum(jnp.where(hits[k], before, 0.0), axis=0, keepdims=True)
        meta_ref[TOP_K_FINE + k:TOP_K_FINE + k + 1, :] = rank.astype(jnp.int32)
    meta_ref[2 * TOP_K_FINE:, :] = jnp.zeros((meta_ref.shape[0] - 2 * TOP_K_FINE, tt), jnp.int32)
    carry_ref[...] = carry_ref[...] + jnp.sum(chosen, axis=1, keepdims=True)
    cnt_ref[...] = carry_ref[...].astype(jnp.int32)


def _rank(ri, tt=1024):
    t = ri.shape[0]
    assert TOP_K_FINE == 2
    return pl.pallas_call(
        _rank_kernel,
        grid=(t // tt,),
        in_specs=[pl.BlockSpec((tt, LANES), lambda i: (i, 0))],
        out_specs=[pl.BlockSpec((None, 8, tt), lambda i: (i, 0, 0)),
                   pl.BlockSpec((N_EXPERTS, LANES), lambda i: (0, 0))],
        out_shape=[jax.ShapeDtypeStruct((t // tt, 8, tt), jnp.int32),
                   jax.ShapeDtypeStruct((N_EXPERTS, LANES), jnp.int32)],
        scratch_shapes=[pltpu.VMEM((N_EXPERTS, LANES), F32), pltpu.VMEM((tt, tt), BF16)],
        compiler_params=pltpu.CompilerParams(
            dimension_semantics=("arbitrary",), vmem_limit_bytes=VMEM_LIMIT),
        name="route_rank",
    )(ri)


def _scatter_rows_kernel(pos_ref, end_ref, nt_ref, src_hbm, dst_hbm, zbuf, st0, st1, st2,
                         sem, lsem, zsem, *, tm, row_tile):
    i = pl.program_id(0)
    stage = (st0, st1, st2)
    n_tiles_max = dst_hbm.shape[0] // row_tile

    def zero_copy(tile):
        row0 = pl.multiple_of(tile * row_tile, row_tile)
        return pltpu.make_async_copy(zbuf, dst_hbm.at[pl.ds(row0, row_tile), :], zsem)

    @pl.when(i == 0)
    def _():
        zbuf[...] = jnp.zeros_like(zbuf)
        n_exp = end_ref.shape[0]

        def has_rows(e):
            return end_ref[e] > (end_ref[e - 1] if e else 0)

        for e in range(n_exp):
            pl.when(has_rows(e))(lambda e=e: zero_copy(end_ref[e] - 1).start())

        def start_tail(tile, c):
            zero_copy(tile).start()
            return c

        def wait_tail(tile, c):
            zero_copy(tile).wait()
            return c

        lax.fori_loop(nt_ref[0], n_tiles_max, start_tail, 0)
        for e in range(n_exp):
            pl.when(has_rows(e))(lambda: zero_copy(0).wait())
        lax.fori_loop(nt_ref[0], n_tiles_max, wait_tail, 0)

    last = pl.num_programs(0) - 1
    nbuf = len(stage)

    def load(tile, b):
        row0 = pl.multiple_of(tile * tm, tm)
        return pltpu.make_async_copy(src_hbm.at[pl.ds(row0, tm), :], stage[b], lsem.at[b])

    def wait_rows(b):
        for _ in range(TOP_K_FINE):
            pltpu.make_async_copy(stage[b], dst_hbm.at[pl.ds(0, tm), :], sem.at[b]).wait()

    @pl.when(i == 0)
    def _():
        load(0, 0).start()

    def step(b):
        nb = (b + 1) % nbuf

        @pl.when(i >= nbuf - 1)
        def _():
            wait_rows(nb)

        @pl.when(i < last)
        def _():
            load(i + 1, nb).start()

        load(i, b).wait()
        for r in range(tm):
            for k in range(TOP_K_FINE):
                p = pos_ref[(i * tm + r) * TOP_K_FINE + k]
                pltpu.make_async_copy(stage[b].at[pl.ds(r, 1), :], dst_hbm.at[pl.ds(p, 1), :],
                                      sem.at[b]).start(priority=k % 2)

        @pl.when(i == last)
        def _():
            for d in range(nbuf - 1):
                @pl.when(i >= d)
                def _(d=d):
                    wait_rows((b - d) % nbuf)

    for b in range(nbuf):
        pl.when(i % nbuf == b)(functools.partial(step, b))


def _scatter_rows(plan, src, tm=256, row_tile=ROW_TILE):
    t, w = src.shape
    any_spec = pl.BlockSpec(memory_space=pl.ANY)
    grid_spec = pltpu.PrefetchScalarGridSpec(
        num_scalar_prefetch=3,
        grid=(t // tm,),
        in_specs=[any_spec],
        out_specs=any_spec,
        scratch_shapes=[pltpu.VMEM((row_tile, w), src.dtype),
                        pltpu.VMEM((tm, w), src.dtype), pltpu.VMEM((tm, w), src.dtype),
                        pltpu.VMEM((tm, w), src.dtype),
                        pltpu.SemaphoreType.DMA((3,)), pltpu.SemaphoreType.DMA((3,)),
                        pltpu.SemaphoreType.DMA(())],
    )
    return pl.pallas_call(
        functools.partial(_scatter_rows_kernel, tm=tm, row_tile=row_tile),
        grid_spec=grid_spec,
        out_shape=jax.ShapeDtypeStruct((plan["n_rows"], w), src.dtype),
        compiler_params=pltpu.CompilerParams(dimension_semantics=("arbitrary",)),
        name="scatter_rows",
    )(plan["pos"], plan["end_tile"], plan["n_tiles"], src)


def _sort_plan(ri, tm=ROW_TILE):
    t = ri.shape[0]
    k = TOP_K_FINE
    a = t * k
    nt = a // tm + N_EXPERTS
    meta, cnt = _rank(ri)
    counts = cnt[:, 0]
    padded = ((counts + tm - 1) // tm) * tm
    off_end = jnp.cumsum(padded)
    off = off_end - padded
    ef = meta[:, 0:k, :]
    pos = off[ef] + meta[:, k:2 * k, :]
    pos = jnp.transpose(pos, (0, 2, 1)).reshape(a)
    n_tiles = off_end[-1] // tm
    tile_idx = jnp.arange(nt, dtype=jnp.int32)
    te = jnp.sum((off_end[None, :] <= (tile_idx * tm)[:, None]).astype(jnp.int32), axis=1)
    te = jnp.minimum(te, N_EXPERTS - 1)
    te = jnp.where(tile_idx < n_tiles, te, te[n_tiles - 1]).astype(jnp.int32)
    first = ((tile_idx == 0) | (te != jnp.roll(te, 1))) & (tile_idx < n_tiles)
    slot = (jnp.cumsum(first.astype(jnp.int32)) - 1) % 2
    eid = jnp.arange(N_EXPERTS, dtype=jnp.int32)
    later = (eid[None, :] > eid[:, None]) & (counts[None, :] > 0)
    next_e = jnp.min(jnp.where(later, eid[None, :], N_EXPERTS), axis=1)
    next_e = jnp.where(next_e < N_EXPERTS, next_e, -1)
    return dict(
        end_tile=(off_end // tm).astype(jnp.int32),
        pos=pos.astype(jnp.int32), n_rows=nt * tm, tile_expert=te,
        n_tiles=n_tiles.reshape(1).astype(jnp.int32), first=first.astype(jnp.int32),
        slot=slot.astype(jnp.int32), next_expert=next_e[te].astype(jnp.int32))


def kernel(x, norm1_g, w_in, lambda_q1, lambda_k1, lambda_q2, lambda_k2, subln_g, pool_w, pool_scale, w_out, norm2_g, w_coarse, b_coarse, w_fine, b_fine, w_gate, w_up, w_down, final_norm_g):
    b, s, d = x.shape
    t = b * s
    assert norm1_g.shape[0] == 1
    x2 = x.reshape(t, d)
    proj = _inproj(x2, norm1_g[0], w_in[0])
    proj3 = proj.reshape(b, s, proj.shape[1])
    attn = _attention(proj3, lambda_q1, lambda_k1, lambda_q2, lambda_k2, subln_g[0])
    pool = _pool(proj3, pool_w[0], pool_scale[0])

    ng, epg = N_EXPERT_GROUPS, EXPERTS_PER_GROUP
    wr = jnp.concatenate(
        [w_coarse[0], jnp.transpose(w_fine[0], (1, 0, 2)).reshape(d, ng * epg)], axis=1)
    wr = jnp.pad(wr, ((0, 0), (0, LANES - wr.shape[1])))
    br = jnp.concatenate([b_coarse[0], b_fine[0].reshape(ng * epg)])
    br = jnp.pad(br, (0, LANES - br.shape[0])).reshape(1, LANES)
    wr_hi = wr.astype(BF16)
    wr_lo = (wr - wr_hi.astype(F32)).astype(BF16)
    h, hn, ri, rw = _outproj(attn.reshape(t, ATTN_WIDTH), pool.reshape(t, POOL_WIDTH), x2,
                             w_out[0].astype(BF16), norm2_g[0], jnp.stack([wr_hi, wr_lo]), br)

    plan = _sort_plan(ri)
    xs = _scatter_rows(plan, hn)
    ys = _experts(plan, xs, w_gate[0], w_up[0], w_down[0])
    out = _combine(plan["pos"], h, rw, ys, final_norm_g)
    return out.reshape(b, s, d)
```

```python
import functools
import math

import jax
import jax.numpy as jnp
from jax import lax
from jax.experimental import pallas as pl
from jax.experimental.pallas import tpu as pltpu

N_DIFF_HEADS = 8
DIFF_HEAD_DIM = 64
DIFF_V_DIM = 2 * DIFF_HEAD_DIM
ATTN_WIDTH = N_DIFF_HEADS * DIFF_V_DIM
POOL_WINDOWS = (2, 4, 8, 16)
POOL_GROUP_DIM = 256
POOL_WIDTH = len(POOL_WINDOWS) * POOL_GROUP_DIM
N_EXPERT_GROUPS = 4
EXPERTS_PER_GROUP = 8
N_EXPERTS = N_EXPERT_GROUPS * EXPERTS_PER_GROUP
TOP_K_FINE = 2
RMS_EPS = 1e-6
NEG_INF = -1e30
LAM_INIT = 0.8 - 0.6 * math.exp(-0.3 * 0)
LOG2E = math.log2(math.e)
Q_SCALE = DIFF_HEAD_DIM ** -0.5 * LOG2E

LANES = 128
VMEM_LIMIT = 56 * 1024 * 1024
ROW_TILE = 256

F32 = jnp.float32
BF16 = jnp.bfloat16


def _rms(x, g):
    return x * lax.rsqrt(jnp.mean(x * x, axis=-1, keepdims=True) + RMS_EPS) * g


def _inproj_kernel(x_ref, g_ref, w_ref, o_ref, wb_ref):
    @pl.when(pl.program_id(1) == 0)
    def _():
        wb_ref[...] = w_ref[...].astype(BF16)

    xn = _rms(x_ref[...], g_ref[...]).astype(BF16)
    acc = jnp.dot(xn, wb_ref[...], preferred_element_type=F32)
    scale = jnp.where(pl.program_id(0) == 0, jnp.float32(Q_SCALE), jnp.float32(1.0))
    o_ref[...] = (acc * scale).astype(o_ref.dtype)


def _inproj(x2, g, w, tm=512, tn=ATTN_WIDTH):
    t, d = x2.shape
    n = w.shape[1]
    return pl.pallas_call(
        _inproj_kernel,
        grid=(n // tn, t // tm),
        in_specs=[
            pl.BlockSpec((tm, d), lambda j, i: (i, 0)),
            pl.BlockSpec((1, d), lambda j, i: (0, 0)),
            pl.BlockSpec((d, tn), lambda j, i: (0, j)),
        ],
        out_specs=pl.BlockSpec((tm, tn), lambda j, i: (i, j)),
        out_shape=jax.ShapeDtypeStruct((t, n), BF16),
        scratch_shapes=[pltpu.VMEM((d, tn), BF16)],
        compiler_params=pltpu.CompilerParams(
            dimension_semantics=("arbitrary", "arbitrary"), vmem_limit_bytes=VMEM_LIMIT),
        name="inproj",
    )(x2, g.reshape(1, d), w)


def _attn_kernel(lq1_ref, lk1_ref, lq2_ref, lk2_ref, q_ref, k_ref, v_ref, g_ref, o_ref,
                 tab_ref, vt_ref, *, tq):
    h = pl.program_id(0)
    s_len = q_ref.shape[0]
    nq = s_len // tq
    d, dv = DIFF_HEAD_DIM, DIFF_V_DIM

    @pl.when(pl.program_id(1) == 0)
    def _():
        slope = jnp.exp2(jnp.full((1, 1), -8.0 / N_DIFF_HEADS, F32) * (h + 1).astype(F32))
        rel = ((nq - 1) * tq + lax.broadcasted_iota(jnp.int32, (s_len, tq), 1)
               - lax.broadcasted_iota(jnp.int32, (s_len, tq), 0))
        tab_ref[...] = jnp.where(rel >= 0, (-LOG2E * slope) * rel.astype(F32), NEG_INF)

    vt_ref[:dv, :] = v_ref[...].astype(F32).T.astype(BF16)
    vt_ref[dv:, :] = jnp.ones((vt_ref.shape[0] - dv, s_len), BF16)

    lam = (jnp.exp(jnp.sum(lq1_ref[...] * lk1_ref[...], axis=1, keepdims=True))
           - jnp.exp(jnp.sum(lq2_ref[...] * lk2_ref[...], axis=1, keepdims=True)) + LAM_INIT)
    dn = (((1,), (1,)), ((), ()))
    lane = lax.broadcasted_iota(jnp.int32, (tq, 2 * d), 1)

    def scores(qi):
        n = (qi + 1) * tq
        q = q_ref[qi * tq:(qi + 1) * tq, :]
        zero = jnp.zeros_like(q)
        qh = (jnp.where(lane < d, q, zero), jnp.where(lane >= d, q, zero))
        kk = k_ref[:n, :]
        bias = tab_ref[(nq - 1 - qi) * tq:(nq - 1 - qi) * tq + n, :]
        return [lax.dot_general(kk, qh[a], dn, preferred_element_type=F32) + bias
                for a in range(2)]

    s_next = scores(0)
    for qi in range(nq):
        n = (qi + 1) * tq
        s_cur = s_next
        if qi + 1 < nq:
            s_next = scores(qi + 1)
        outs = []
        for a in range(2):
            s = s_cur[a]
            m = jnp.max(s, axis=0, keepdims=True)
            p = jnp.exp2(s - m).astype(BF16)
            acc = jnp.dot(vt_ref[:, :n], p, preferred_element_type=F32)
            outs.append(acc[:dv] / acc[dv:dv + 1])
        ot = outs[0] - lam * outs[1]
        yt = ot * lax.rsqrt(jnp.mean(ot * ot, axis=0, keepdims=True) + RMS_EPS) * g_ref[...]
        o_ref[qi * tq:(qi + 1) * tq, :] = (yt * (1.0 - LAM_INIT)).T.astype(o_ref.dtype)


def _attention(proj3, lq1, lk1, lq2, lk2, subln_g, tq=256):
    b, s, _ = proj3.shape
    dv = DIFF_V_DIM
    nh = N_DIFF_HEADS
    ones_rows = 16
    lam_spec = pl.BlockSpec((1, DIFF_HEAD_DIM), lambda hi, bi: (0, 0))
    return pl.pallas_call(
        functools.partial(_attn_kernel, tq=tq),
        grid=(nh, b),
        in_specs=[
            lam_spec, lam_spec, lam_spec, lam_spec,
            pl.BlockSpec((None, s, dv), lambda hi, bi: (bi, 0, hi)),
            pl.BlockSpec((None, s, dv), lambda hi, bi: (bi, 0, nh + hi)),
            pl.BlockSpec((None, s, dv), lambda hi, bi: (bi, 0, 2 * nh + hi)),
            pl.BlockSpec((dv, 1), lambda hi, bi: (0, 0)),
        ],
        out_specs=pl.BlockSpec((None, s, dv), lambda hi, bi: (bi, 0, hi)),
        out_shape=jax.ShapeDtypeStruct((b, s, ATTN_WIDTH), BF16),
        scratch_shapes=[
            pltpu.VMEM((s, tq), F32),
            pltpu.VMEM((dv + ones_rows, s), BF16),
        ],
        compiler_params=pltpu.CompilerParams(
            dimension_semantics=("arbitrary", "arbitrary"),
            vmem_limit_bytes=VMEM_LIMIT),
        name="diff_attn",
    )(lq1, lk1, lq2, lk2, proj3, proj3, proj3, subln_g.reshape(dv, 1))


def _pool_kernel(u_ref, w_ref, sc_ref, o_ref):
    s = u_ref.shape[0]
    c = POOL_GROUP_DIM
    t = lax.broadcasted_iota(jnp.int32, (s, 1), 0)
    for gi, win in enumerate(POOL_WINDOWS):
        ch = u_ref[:, gi * c:(gi + 1) * c].astype(F32)
        acc = ch
        span = 1
        while span < win:
            acc = acc + jnp.where(t >= span, pltpu.roll(acc, span, axis=0), 0.0)
            span *= 2
        count = jnp.minimum(t + 1, win).astype(F32)
        pooled = acc / count - ch
        mixed = jnp.dot(pooled.astype(BF16), w_ref[gi].astype(BF16), preferred_element_type=F32)
        o_ref[:, gi * c:(gi + 1) * c] = (mixed * sc_ref[:, gi * c:(gi + 1) * c]).astype(o_ref.dtype)


def _pool(proj3, pool_w, pool_scale):
    b, s, n = proj3.shape
    assert all(w & (w - 1) == 0 for w in POOL_WINDOWS)
    return pl.pallas_call(
        _pool_kernel,
        grid=(b,),
        in_specs=[
            pl.BlockSpec((None, s, POOL_WIDTH), lambda bi: (bi, 0, n // POOL_WIDTH - 1)),
            pl.BlockSpec(pool_w.shape, lambda bi: (0, 0, 0)),
            pl.BlockSpec((1, POOL_WIDTH), lambda bi: (0, 0)),
        ],
        out_specs=pl.BlockSpec((None, s, POOL_WIDTH), lambda bi: (bi, 0, 0)),
        out_shape=jax.ShapeDtypeStruct((b, s, POOL_WIDTH), BF16),
        compiler_params=pltpu.CompilerParams(
            dimension_semantics=("arbitrary",), vmem_limit_bytes=VMEM_LIMIT),
        name="pool_mixer",
    )(proj3, pool_w, pool_scale.reshape(1, POOL_WIDTH))


def _pack_bf16_pair(x):
    bits = lax.bitcast_convert_type(x.astype(BF16).astype(F32), jnp.uint32)
    half = bits.shape[1] // 2
    return bits[:, :half] | (bits[:, half:] >> 16)


def _outproj_kernel(a_ref, p_ref, x_ref, w_ref, g_ref, wr_ref, br_ref,
                    h_ref, hn_ref, ri_ref, rw_ref, *, n_sub):
    ka = a_ref.shape[1]
    hs = x_ref.shape[0] // n_sub
    normed = []
    for r in range(n_sub):
        rows = pl.ds(r * hs, hs)
        mixed = (jnp.dot(a_ref[rows, :], w_ref[:ka, :], preferred_element_type=F32)
                 + jnp.dot(p_ref[rows, :], w_ref[ka:, :], preferred_element_type=F32))
        h = x_ref[rows, :] + mixed
        h_ref[rows, :] = h
        hn = _rms(h, g_ref[...])
        hn_ref[rows, :] = _pack_bf16_pair(hn)
        hn_hi = hn.astype(BF16)
        normed.append((hn_hi, (hn - hn_hi.astype(F32)).astype(BF16)))
    for r in range(n_sub):
        rows = pl.ds(r * hs, hs)
        ri, rw = _route(normed[r][0], normed[r][1], wr_ref, br_ref)
        ri_ref[rows, :] = ri
        rw_ref[rows, :] = rw


def _route(hn_hi, hn_lo, wr_ref, br_ref):
    wr_hi = wr_ref[0]
    logits = (jnp.dot(hn_hi, wr_hi, preferred_element_type=F32)
              + jnp.dot(hn_lo, wr_hi, preferred_element_type=F32)
              + jnp.dot(hn_hi, wr_ref[1], preferred_element_type=F32)) + br_ref[...]
    ng, epg = N_EXPERT_GROUPS, EXPERTS_PER_GROUP
    lane = lax.broadcasted_iota(jnp.int32, logits.shape, 1)
    big = jnp.int32(LANES)
    low = jnp.float32(-3.0e38)
    cm = lane < ng
    c = jnp.where(cm, logits, low)
    cmax = jnp.max(c, axis=1, keepdims=True)
    gsel = jnp.min(jnp.where(c == cmax, lane, big), axis=1, keepdims=True)
    p_group = 1.0 / jnp.sum(jnp.where(cm, jnp.exp(c - cmax), 0.0), axis=1, keepdims=True)
    f_lo = ng + epg * gsel
    fm = (lane >= f_lo) & (lane < f_lo + epg)
    f = jnp.where(fm, logits, low)
    v1 = jnp.max(f, axis=1, keepdims=True)
    i1 = jnp.min(jnp.where(fm & (f == v1), lane, big), axis=1, keepdims=True)
    fm2 = fm & (lane != i1)
    f2 = jnp.where(fm2, logits, low)
    v2 = jnp.max(f2, axis=1, keepdims=True)
    i2 = jnp.min(jnp.where(fm2 & (f2 == v2), lane, big), axis=1, keepdims=True)
    e21 = jnp.exp(v2 - v1)
    w1 = p_group / (1.0 + e21)
    w2 = p_group * e21 / (1.0 + e21)
    return (jnp.where(lane == 0, i1 - ng, jnp.where(lane == 1, i2 - ng, 0)),
            jnp.where(lane == 0, w1, jnp.where(lane == 1, w2, 0.0)))


def _outproj(attn2, pool2, x2, w_out_bf16, g2, wr, br, tm=512, n_sub=2):
    t, d = x2.shape
    ka, kp = attn2.shape[1], pool2.shape[1]
    row = lambda i: (i, 0)
    const = lambda i: (0, 0)
    return pl.pallas_call(
        functools.partial(_outproj_kernel, n_sub=n_sub),
        grid=(t // tm,),
        in_specs=[
            pl.BlockSpec((tm, ka), row),
            pl.BlockSpec((tm, kp), row),
            pl.BlockSpec((tm, d), row),
            pl.BlockSpec((ka + kp, d), const, pipeline_mode=pl.Buffered(1)),
            pl.BlockSpec((1, d), const),
            pl.BlockSpec((2, d, LANES), lambda i: (0, 0, 0)),
            pl.BlockSpec((1, LANES), const),
        ],
        out_specs=[
            pl.BlockSpec((tm, d), row),
            pl.BlockSpec((tm, d // 2), row),
            pl.BlockSpec((tm, LANES), row),
            pl.BlockSpec((tm, LANES), row),
        ],
        out_shape=[
            jax.ShapeDtypeStruct((t, d), F32),
            jax.ShapeDtypeStruct((t, d // 2), jnp.uint32),
            jax.ShapeDtypeStruct((t, LANES), jnp.int32),
            jax.ShapeDtypeStruct((t, LANES), F32),
        ],
        compiler_params=pltpu.CompilerParams(
            dimension_semantics=("arbitrary",), vmem_limit_bytes=VMEM_LIMIT),
        name="outproj_router",
    )(attn2, pool2, x2, w_out_bf16, g2.reshape(1, d), wr, br)


def _unpack_bf16_pair(words):
    hi = lax.bitcast_convert_type(words & jnp.uint32(0xFFFF0000), F32).astype(BF16)
    lo = lax.bitcast_convert_type(words << 16, F32).astype(BF16)
    return hi, lo


def _expert_kernel(te_ref, nt_ref, first_ref, slot_ref, nxt_ref,
                   x_ref, wg_hbm, wu_hbm, wd_hbm, y_ref,
                   wgb, wub, wdb, wsem):
    j = pl.program_id(0)
    nt = nt_ref[0]
    kh = x_ref.shape[1]

    def weight_copies(e, s):
        return (pltpu.make_async_copy(wg_hbm.at[e], wgb.at[s], wsem.at[s]),
                pltpu.make_async_copy(wu_hbm.at[e], wub.at[s], wsem.at[s]),
                pltpu.make_async_copy(wd_hbm.at[e], wdb.at[s], wsem.at[s]))

    @pl.when(j == 0)
    def _():
        for c in weight_copies(te_ref[0], 0):
            c.start(priority=1)

    @pl.when(j < nt)
    def _():
        ws = slot_ref[j]

        @pl.when(first_ref[j] == 1)
        def _():
            for c in weight_copies(te_ref[j], ws):
                c.wait()

            @pl.when(nxt_ref[j] >= 0)
            def _():
                for c in weight_copies(nxt_ref[j], 1 - ws):
                    c.start(priority=1)

        xa, xb = _unpack_bf16_pair(x_ref[...])
        hg = (jnp.dot(xa, wgb[ws, :kh, :].astype(BF16), preferred_element_type=F32)
              + jnp.dot(xb, wgb[ws, kh:, :].astype(BF16), preferred_element_type=F32))
        hu = (jnp.dot(xa, wub[ws, :kh, :].astype(BF16), preferred_element_type=F32)
              + jnp.dot(xb, wub[ws, kh:, :].astype(BF16), preferred_element_type=F32))
        act = (hg / (1.0 + jnp.exp(-hg))) * hu
        y_ref[...] = _pack_bf16_pair(
            jnp.dot(act.astype(BF16), wdb[ws].astype(BF16), preferred_element_type=F32))

    @pl.when(j >= nt)
    def _():
        y_ref[...] = jnp.zeros_like(y_ref)


def _experts(plan, xs, w_gate, w_up, w_down, tm=ROW_TILE):
    d, f = w_gate.shape[1], w_gate.shape[2]
    assert xs.shape[1] * 2 == d
    nt = plan["tile_expert"].shape[0]
    any_spec = pl.BlockSpec(memory_space=pl.ANY)
    grid_spec = pltpu.PrefetchScalarGridSpec(
        num_scalar_prefetch=5,
        grid=(nt,),
        in_specs=[
            pl.BlockSpec((tm, d // 2), lambda j, te, n, *_: (jnp.minimum(j, n[0] - 1), 0)),
            any_spec, any_spec, any_spec],
        out_specs=pl.BlockSpec((tm, d // 2), lambda j, *_: (j, 0)),
        scratch_shapes=[
            pltpu.VMEM((2, d, f), F32),
            pltpu.VMEM((2, d, f), F32),
            pltpu.VMEM((2, f, d), F32),
            pltpu.SemaphoreType.DMA((2,)),
        ],
    )
    return pl.pallas_call(
        _expert_kernel,
        grid_spec=grid_spec,
        out_shape=jax.ShapeDtypeStruct((nt * tm, d // 2), jnp.uint32),
        compiler_params=pltpu.CompilerParams(
            dimension_semantics=("arbitrary",), vmem_limit_bytes=VMEM_LIMIT),
        name="experts",
    )(plan["tile_expert"], plan["n_tiles"], plan["first"], plan["slot"], plan["next_expert"],
      xs, w_gate, w_up, w_down)


def _combine_kernel(pos_ref, h_ref, rw_ref, ys_hbm, g_ref, o_ref, ybuf0, ybuf1, ybuf2, sem):
    i = pl.program_id(0)
    last = pl.num_programs(0) - 1
    tm = h_ref.shape[0]
    kh = ybuf0.shape[2]
    bufs = (ybuf0, ybuf1, ybuf2)
    nbuf = len(bufs)

    def start_gather(tile, buf, sm):
        for r in range(tm):
            for k in range(TOP_K_FINE):
                p = pos_ref[(tile * tm + r) * TOP_K_FINE + k]
                pltpu.make_async_copy(ys_hbm.at[pl.ds(p, 1), :], buf.at[k, pl.ds(r, 1), :],
                                      sm).start(priority=k % 2)

    def wait_gather(buf, sm):
        for k in range(TOP_K_FINE):
            pltpu.make_async_copy(ys_hbm.at[pl.ds(0, tm), :], buf.at[k], sm).wait()

    @pl.when(i == 0)
    def _():
        start_gather(0, ybuf0, sem.at[0])
        start_gather(1, ybuf1, sem.at[1])

    def step(par):
        cur = bufs[par]
        ahead = (par + nbuf - 1) % nbuf
        wait_gather(cur, sem.at[par])
        start_gather(jnp.minimum(i + nbuf - 1, last), bufs[ahead], sem.at[ahead])
        w = rw_ref[...]
        halves = []
        for part in range(2):
            y = None
            for k in range(TOP_K_FINE):
                words = cur[k]
                bits = (words & jnp.uint32(0xFFFF0000)) if part == 0 else (words << 16)
                term = w[:, k:k + 1] * lax.bitcast_convert_type(bits, F32)
                y = term if y is None else y + term
            halves.append(h_ref[:, part * kh:(part + 1) * kh] + y)
        ms = sum(jnp.sum(v * v, axis=-1, keepdims=True) for v in halves) / (2 * kh)
        inv = lax.rsqrt(ms + RMS_EPS)
        for part in range(2):
            o_ref[:, part * kh:(part + 1) * kh] = (
                halves[part] * inv * g_ref[:, part * kh:(part + 1) * kh])

        @pl.when(i == last)
        def _():
            for other in range(nbuf):
                if other != par:
                    wait_gather(bufs[other], sem.at[other])

    for par in range(nbuf):
        pl.when(i % nbuf == par)(functools.partial(step, par))


def _combine(pos, h, rw, ys, g, tm=256):
    t, d = h.shape
    grid_spec = pltpu.PrefetchScalarGridSpec(
        num_scalar_prefetch=1,
        grid=(t // tm,),
        in_specs=[
            pl.BlockSpec((tm, d), lambda i, p: (i, 0)),
            pl.BlockSpec((tm, LANES), lambda i, p: (i, 0)),
            pl.BlockSpec(memory_space=pl.ANY),
            pl.BlockSpec((1, d), lambda i, p: (0, 0)),
        ],
        out_specs=pl.BlockSpec((tm, d), lambda i, p: (i, 0)),
        scratch_shapes=[pltpu.VMEM((TOP_K_FINE, tm, d // 2), jnp.uint32),
                        pltpu.VMEM((TOP_K_FINE, tm, d // 2), jnp.uint32),
                        pltpu.VMEM((TOP_K_FINE, tm, d // 2), jnp.uint32),
                        pltpu.SemaphoreType.DMA((3,))],
    )
    return pl.pallas_call(
        _combine_kernel,
        grid_spec=grid_spec,
        out_shape=jax.ShapeDtypeStruct((t, d), F32),
        compiler_params=pltpu.CompilerParams(
            dimension_semantics=("arbitrary",), vmem_limit_bytes=VMEM_LIMIT),
        name="combine",
    )(pos, h, rw, ys, g.reshape(1, d))


def _rank_kernel(ri_ref, meta_ref, cnt_ref, carry_ref, tri_ref):
    tt = ri_ref.shape[0]
    ne = cnt_ref.shape[0]

    @pl.when(pl.program_id(0) == 0)
    def _():
        carry_ref[...] = jnp.zeros_like(carry_ref)
        earlier = (lax.broadcasted_iota(jnp.int32, (tt, tt), 0)
                   < lax.broadcasted_iota(jnp.int32, (tt, tt), 1))
        tri_ref[...] = jnp.where(earlier, 1.0, 0.0).astype(BF16)

    rit = ri_ref[...].astype(F32).T
    expert = lax.broadcasted_iota(jnp.int32, (ne, tt), 0).astype(F32)
    ranks = []
    hits = []
    for k in range(TOP_K_FINE):
        hits.append(expert == rit[k:k + 1, :])
    chosen = jnp.where(hits[0] | hits[1], 1.0, 0.0)
    before = (jnp.dot(chosen.astype(BF16), tri_ref[...], preferred_element_type=F32)
              + carry_ref[:, 0:1])
    for k in range(TOP_K_FINE):
        meta_ref[k:k + 1, :] = rit[k:k + 1, :].astype(jnp.int32)
        rank = jnp.sum(jnp.where(hits[k], before, 0.0), axis=0, keepdims=True)
        meta_ref[TOP_K_FINE + k:TOP_K_FINE + k + 1, :] = rank.astype(jnp.int32)
    meta_ref[2 * TOP_K_FINE:, :] = jnp.zeros((meta_ref.shape[0] - 2 * TOP_K_FINE, tt), jnp.int32)
    carry_ref[...] = carry_ref[...] + jnp.sum(chosen, axis=1, keepdims=True)
    cnt_ref[...] = carry_ref[...].astype(jnp.int32)


def _rank(ri, tt=1024):
    t = ri.shape[0]
    assert TOP_K_FINE == 2
    return pl.pallas_call(
        _rank_kernel,
        grid=(t // tt,),
        in_specs=[pl.BlockSpec((tt, LANES), lambda i: (i, 0))],
        out_specs=[pl.BlockSpec((None, 8, tt), lambda i: (i, 0, 0)),
                   pl.BlockSpec((N_EXPERTS, LANES), lambda i: (0, 0))],
        out_shape=[jax.ShapeDtypeStruct((t // tt, 8, tt), jnp.int32),
                   jax.ShapeDtypeStruct((N_EXPERTS, LANES), jnp.int32)],
        scratch_shapes=[pltpu.VMEM((N_EXPERTS, LANES), F32), pltpu.VMEM((tt, tt), BF16)],
        compiler_params=pltpu.CompilerParams(
            dimension_semantics=("arbitrary",), vmem_limit_bytes=VMEM_LIMIT),
        name="route_rank",
    )(ri)


def _scatter_rows_kernel(pos_ref, end_ref, nt_ref, src_hbm, dst_hbm, zbuf, st0, st1, st2,
                         sem, lsem, zsem, *, tm, row_tile):
    i = pl.program_id(0)
    stage = (st0, st1, st2)
    n_tiles_max = dst_hbm.shape[0] // row_tile

    def zero_copy(tile):
        row0 = pl.multiple_of(tile * row_tile, row_tile)
        return pltpu.make_async_copy(zbuf, dst_hbm.at[pl.ds(row0, row_tile), :], zsem)

    @pl.when(i == 0)
    def _():
        zbuf[...] = jnp.zeros_like(zbuf)
        n_exp = end_ref.shape[0]

        def has_rows(e):
            return end_ref[e] > (end_ref[e - 1] if e else 0)

        for e in range(n_exp):
            pl.when(has_rows(e))(lambda e=e: zero_copy(end_ref[e] - 1).start())

        def start_tail(tile, c):
            zero_copy(tile).start()
            return c

        def wait_tail(tile, c):
            zero_copy(tile).wait()
            return c

        lax.fori_loop(nt_ref[0], n_tiles_max, start_tail, 0)
        for e in range(n_exp):
            pl.when(has_rows(e))(lambda: zero_copy(0).wait())
        lax.fori_loop(nt_ref[0], n_tiles_max, wait_tail, 0)

    last = pl.num_programs(0) - 1
    nbuf = len(stage)

    def load(tile, b):
        row0 = pl.multiple_of(tile * tm, tm)
        return pltpu.make_async_copy(src_hbm.at[pl.ds(row0, tm), :], stage[b], lsem.at[b])

    def wait_rows(b):
        for _ in range(TOP_K_FINE):
            pltpu.make_async_copy(stage[b], dst_hbm.at[pl.ds(0, tm), :], sem.at[b]).wait()

    @pl.when(i == 0)
    def _():
        load(0, 0).start()

    def step(b):
        nb = (b + 1) % nbuf

        @pl.when(i >= nbuf - 1)
        def _():
            wait_rows(nb)

        @pl.when(i < last)
        def _():
            load(i + 1, nb).start()

        load(i, b).wait()
        for r in range(tm):
            for k in range(TOP_K_FINE):
                p = pos_ref[(i * tm + r) * TOP_K_FINE + k]
                pltpu.make_async_copy(stage[b].at[pl.ds(r, 1), :], dst_hbm.at[pl.ds(p, 1), :],
                                      sem.at[b]).start(priority=k % 2)

        @pl.when(i == last)
        def _():
            for d in range(nbuf - 1):
                @pl.when(i >= d)
                def _(d=d):
                    wait_rows((b - d) % nbuf)

    for b in range(nbuf):
        pl.when(i % nbuf == b)(functools.partial(step, b))


def _scatter_rows(plan, src, tm=256, row_tile=ROW_TILE):
    t, w = src.shape
    any_spec = pl.BlockSpec(memory_space=pl.ANY)
    grid_spec = pltpu.PrefetchScalarGridSpec(
        num_scalar_prefetch=3,
        grid=(t // tm,),
        in_specs=[any_spec],
        out_specs=any_spec,
        scratch_shapes=[pltpu.VMEM((row_tile, w), src.dtype),
                        pltpu.VMEM((tm, w), src.dtype), pltpu.VMEM((tm, w), src.dtype),
                        pltpu.VMEM((tm, w), src.dtype),
                        pltpu.SemaphoreType.DMA((3,)), pltpu.SemaphoreType.DMA((3,)),
                        pltpu.SemaphoreType.DMA(())],
    )
    return pl.pallas_call(
        functools.partial(_scatter_rows_kernel, tm=tm, row_tile=row_tile),
        grid_spec=grid_spec,
        out_shape=jax.ShapeDtypeStruct((plan["n_rows"], w), src.dtype),
        compiler_params=pltpu.CompilerParams(dimension_semantics=("arbitrary",)),
        name="scatter_rows",
    )(plan["pos"], plan["end_tile"], plan["n_tiles"], src)


def _sort_plan(ri, tm=ROW_TILE):
    t = ri.shape[0]
    k = TOP_K_FINE
    a = t * k
    nt = a // tm + N_EXPERTS
    meta, cnt = _rank(ri)
    counts = cnt[:, 0]
    padded = ((counts + tm - 1) // tm) * tm
    off_end = jnp.cumsum(padded)
    off = off_end - padded
    ef = meta[:, 0:k, :]
    eid = jnp.arange(N_EXPERTS, dtype=jnp.int32)
    pos = jnp.sum(jnp.where(ef[..., None] == eid, off, 0), axis=-1) + meta[:, k:2 * k, :]
    pos = jnp.transpose(pos, (0, 2, 1)).reshape(a)
    n_tiles = off_end[-1] // tm
    tile_idx = jnp.arange(nt, dtype=jnp.int32)
    te = jnp.sum((off_end[None, :] <= (tile_idx * tm)[:, None]).astype(jnp.int32), axis=1)
    te = jnp.minimum(te, N_EXPERTS - 1)
    te = jnp.where(tile_idx < n_tiles, te, te[n_tiles - 1]).astype(jnp.int32)
    first = ((tile_idx == 0) | (te != jnp.roll(te, 1))) & (tile_idx < n_tiles)
    slot = (jnp.cumsum(first.astype(jnp.int32)) - 1) % 2
    eid = jnp.arange(N_EXPERTS, dtype=jnp.int32)
    later = (eid[None, :] > eid[:, None]) & (counts[None, :] > 0)
    next_e = jnp.min(jnp.where(later, eid[None, :], N_EXPERTS), axis=1)
    next_e = jnp.where(next_e < N_EXPERTS, next_e, -1)
    return dict(
        end_tile=(off_end // tm).astype(jnp.int32),
        pos=pos.astype(jnp.int32), n_rows=nt * tm, tile_expert=te,
        n_tiles=n_tiles.reshape(1).astype(jnp.int32), first=first.astype(jnp.int32),
        slot=slot.astype(jnp.int32), next_expert=next_e[te].astype(jnp.int32))


def kernel(x, norm1_g, w_in, lambda_q1, lambda_k1, lambda_q2, lambda_k2, subln_g, pool_w, pool_scale, w_out, norm2_g, w_coarse, b_coarse, w_fine, b_fine, w_gate, w_up, w_down, final_norm_g):
    b, s, d = x.shape
    t = b * s
    assert norm1_g.shape[0] == 1
    x2 = x.reshape(t, d)
    proj = _inproj(x2, norm1_g[0], w_in[0])
    proj3 = proj.reshape(b, s, proj.shape[1])
    attn = _attention(proj3, lambda_q1, lambda_k1, lambda_q2, lambda_k2, subln_g[0])
    pool = _pool(proj3, pool_w[0], pool_scale[0])

    ng, epg = N_EXPERT_GROUPS, EXPERTS_PER_GROUP
    wr = jnp.concatenate(
        [w_coarse[0], jnp.transpose(w_fine[0], (1, 0, 2)).reshape(d, ng * epg)], axis=1)
    wr = jnp.pad(wr, ((0, 0), (0, LANES - wr.shape[1])))
    br = jnp.concatenate([b_coarse[0], b_fine[0].reshape(ng * epg)])
    br = jnp.pad(br, (0, LANES - br.shape[0])).reshape(1, LANES)
    wr_hi = wr.astype(BF16)
    wr_lo = (wr - wr_hi.astype(F32)).astype(BF16)
    h, hn, ri, rw = _outproj(attn.reshape(t, ATTN_WIDTH), pool.reshape(t, POOL_WIDTH), x2,
                             w_out[0].astype(BF16), norm2_g[0], jnp.stack([wr_hi, wr_lo]), br)

    plan = _sort_plan(ri)
    xs = _scatter_rows(plan, hn)
    ys = _experts(plan, xs, w_gate[0], w_up[0], w_down[0])
    out = _combine(plan["pos"], h, rw, ys, final_norm_g)
    return out.reshape(b, s, d)
```

```python
import functools
import math

import jax
import jax.numpy as jnp
from jax import lax
from jax.experimental import pallas as pl
from jax.experimental.pallas import tpu as pltpu

N_DIFF_HEADS = 8
DIFF_HEAD_DIM = 64
DIFF_V_DIM = 2 * DIFF_HEAD_DIM
ATTN_WIDTH = N_DIFF_HEADS * DIFF_V_DIM
POOL_WINDOWS = (2, 4, 8, 16)
POOL_GROUP_DIM = 256
POOL_WIDTH = len(POOL_WINDOWS) * POOL_GROUP_DIM
N_EXPERT_GROUPS = 4
EXPERTS_PER_GROUP = 8
N_EXPERTS = N_EXPERT_GROUPS * EXPERTS_PER_GROUP
TOP_K_FINE = 2
RMS_EPS = 1e-6
NEG_INF = -1e30
LAM_INIT = 0.8 - 0.6 * math.exp(-0.3 * 0)
LOG2E = math.log2(math.e)
Q_SCALE = DIFF_HEAD_DIM ** -0.5 * LOG2E

LANES = 128
VMEM_LIMIT = 56 * 1024 * 1024
ROW_TILE = 256

F32 = jnp.float32
BF16 = jnp.bfloat16


def _rms(x, g):
    return x * lax.rsqrt(jnp.mean(x * x, axis=-1, keepdims=True) + RMS_EPS) * g


def _inproj_kernel(x_ref, g_ref, w_ref, o_ref, *, tn):
    xn = _rms(x_ref[...], g_ref[...]).astype(BF16)
    for c in range(w_ref.shape[1] // tn):
        cols = slice(c * tn, (c + 1) * tn)
        acc = jnp.dot(xn, w_ref[:, cols], preferred_element_type=F32)
        if c == 0:
            acc = acc * Q_SCALE
        o_ref[:, cols] = acc.astype(o_ref.dtype)


def _inproj(x2, g, w_bf16, tm=512, tn=ATTN_WIDTH):
    t, d = x2.shape
    n = w_bf16.shape[1]
    return pl.pallas_call(
        functools.partial(_inproj_kernel, tn=tn),
        grid=(t // tm,),
        in_specs=[
            pl.BlockSpec((tm, d), lambda i: (i, 0)),
            pl.BlockSpec((1, d), lambda i: (0, 0)),
            pl.BlockSpec((d, n), lambda i: (0, 0), pipeline_mode=pl.Buffered(1)),
        ],
        out_specs=pl.BlockSpec((tm, n), lambda i: (i, 0)),
        out_shape=jax.ShapeDtypeStruct((t, n), BF16),
        compiler_params=pltpu.CompilerParams(
            dimension_semantics=("arbitrary",), vmem_limit_bytes=VMEM_LIMIT),
        name="inproj",
    )(x2, g.reshape(1, d), w_bf16)


def _attn_kernel(lq1_ref, lk1_ref, lq2_ref, lk2_ref, q_ref, k_ref, v_ref, g_ref, o_ref,
                 tab_ref, vt_ref, *, tq):
    h = pl.program_id(0)
    s_len = q_ref.shape[0]
    nq = s_len // tq
    d, dv = DIFF_HEAD_DIM, DIFF_V_DIM

    @pl.when(pl.program_id(1) == 0)
    def _():
        slope = jnp.exp2(jnp.full((1, 1), -8.0 / N_DIFF_HEADS, F32) * (h + 1).astype(F32))
        rel = ((nq - 1) * tq + lax.broadcasted_iota(jnp.int32, (s_len, tq), 1)
               - lax.broadcasted_iota(jnp.int32, (s_len, tq), 0))
        tab_ref[...] = jnp.where(rel >= 0, (-LOG2E * slope) * rel.astype(F32), NEG_INF)

    vt_ref[:dv, :] = v_ref[...].astype(F32).T.astype(BF16)
    vt_ref[dv:, :] = jnp.ones((vt_ref.shape[0] - dv, s_len), BF16)

    lam = (jnp.exp(jnp.sum(lq1_ref[...] * lk1_ref[...], axis=1, keepdims=True))
           - jnp.exp(jnp.sum(lq2_ref[...] * lk2_ref[...], axis=1, keepdims=True)) + LAM_INIT)
    dn = (((1,), (1,)), ((), ()))
    lane = lax.broadcasted_iota(jnp.int32, (tq, 2 * d), 1)

    def scores(qi):
        n = (qi + 1) * tq
        q = q_ref[qi * tq:(qi + 1) * tq, :]
        zero = jnp.zeros_like(q)
        qh = (jnp.where(lane < d, q, zero), jnp.where(lane >= d, q, zero))
        kk = k_ref[:n, :]
        bias = tab_ref[(nq - 1 - qi) * tq:(nq - 1 - qi) * tq + n, :]
        return [lax.dot_general(kk, qh[a], dn, preferred_element_type=F32) + bias
                for a in range(2)]

    s_next = scores(0)
    for qi in range(nq):
        n = (qi + 1) * tq
        s_cur = s_next
        if qi + 1 < nq:
            s_next = scores(qi + 1)
        outs = []
        for a in range(2):
            s = s_cur[a]
            m = jnp.max(s, axis=0, keepdims=True)
            p = jnp.exp2(s - m).astype(BF16)
            acc = jnp.dot(vt_ref[:, :n], p, preferred_element_type=F32)
            outs.append(acc[:dv] / acc[dv:dv + 1])
        ot = outs[0] - lam * outs[1]
        yt = ot * lax.rsqrt(jnp.mean(ot * ot, axis=0, keepdims=True) + RMS_EPS) * g_ref[...]
        o_ref[qi * tq:(qi + 1) * tq, :] = (yt * (1.0 - LAM_INIT)).T.astype(o_ref.dtype)


def _attention(proj3, lq1, lk1, lq2, lk2, subln_g, tq=256):
    b, s, _ = proj3.shape
    dv = DIFF_V_DIM
    nh = N_DIFF_HEADS
    ones_rows = 16
    lam_spec = pl.BlockSpec((1, DIFF_HEAD_DIM), lambda hi, bi: (0, 0))
    return pl.pallas_call(
        functools.partial(_attn_kernel, tq=tq),
        grid=(nh, b),
        in_specs=[
            lam_spec, lam_spec, lam_spec, lam_spec,
            pl.BlockSpec((None, s, dv), lambda hi, bi: (bi, 0, hi)),
            pl.BlockSpec((None, s, dv), lambda hi, bi: (bi, 0, nh + hi)),
            pl.BlockSpec((None, s, dv), lambda hi, bi: (bi, 0, 2 * nh + hi)),
            pl.BlockSpec((dv, 1), lambda hi, bi: (0, 0)),
        ],
        out_specs=pl.BlockSpec((None, s, dv), lambda hi, bi: (bi, 0, hi)),
        out_shape=jax.ShapeDtypeStruct((b, s, ATTN_WIDTH), BF16),
        scratch_shapes=[
            pltpu.VMEM((s, tq), F32),
            pltpu.VMEM((dv + ones_rows, s), BF16),
        ],
        compiler_params=pltpu.CompilerParams(
            dimension_semantics=("arbitrary", "arbitrary"),
            vmem_limit_bytes=VMEM_LIMIT),
        name="diff_attn",
    )(lq1, lk1, lq2, lk2, proj3, proj3, proj3, subln_g.reshape(dv, 1))


def _pool_kernel(u_ref, w_ref, sc_ref, o_ref):
    s = u_ref.shape[0]
    c = POOL_GROUP_DIM
    t = lax.broadcasted_iota(jnp.int32, (s, 1), 0)
    for gi, win in enumerate(POOL_WINDOWS):
        ch = u_ref[:, gi * c:(gi + 1) * c].astype(F32)
        acc = ch
        span = 1
        while span < win:
            acc = acc + jnp.where(t >= span, pltpu.roll(acc, span, axis=0), 0.0)
            span *= 2
        count = jnp.minimum(t + 1, win).astype(F32)
        pooled = acc / count - ch
        mixed = jnp.dot(pooled.astype(BF16), w_ref[gi].astype(BF16), preferred_element_type=F32)
        o_ref[:, gi * c:(gi + 1) * c] = (mixed * sc_ref[:, gi * c:(gi + 1) * c]).astype(o_ref.dtype)


def _pool(proj3, pool_w, pool_scale):
    b, s, n = proj3.shape
    assert all(w & (w - 1) == 0 for w in POOL_WINDOWS)
    return pl.pallas_call(
        _pool_kernel,
        grid=(b,),
        in_specs=[
            pl.BlockSpec((None, s, POOL_WIDTH), lambda bi: (bi, 0, n // POOL_WIDTH - 1)),
            pl.BlockSpec(pool_w.shape, lambda bi: (0, 0, 0)),
            pl.BlockSpec((1, POOL_WIDTH), lambda bi: (0, 0)),
        ],
        out_specs=pl.BlockSpec((None, s, POOL_WIDTH), lambda bi: (bi, 0, 0)),
        out_shape=jax.ShapeDtypeStruct((b, s, POOL_WIDTH), BF16),
        compiler_params=pltpu.CompilerParams(
            dimension_semantics=("arbitrary",), vmem_limit_bytes=VMEM_LIMIT),
        name="pool_mixer",
    )(proj3, pool_w, pool_scale.reshape(1, POOL_WIDTH))


def _pack_bf16_pair(x):
    bits = lax.bitcast_convert_type(x.astype(BF16).astype(F32), jnp.uint32)
    half = bits.shape[1] // 2
    return bits[:, :half] | (bits[:, half:] >> 16)


def _outproj_kernel(a_ref, p_ref, x_ref, w_ref, g_ref, wr_ref, br_ref,
                    h_ref, hn_ref, ri_ref, rw_ref, *, n_sub):
    ka = a_ref.shape[1]
    hs = x_ref.shape[0] // n_sub
    normed = []
    for r in range(n_sub):
        rows = pl.ds(r * hs, hs)
        mixed = (jnp.dot(a_ref[rows, :], w_ref[:ka, :], preferred_element_type=F32)
                 + jnp.dot(p_ref[rows, :], w_ref[ka:, :], preferred_element_type=F32))
        h = x_ref[rows, :] + mixed
        h_ref[rows, :] = h
        hn = _rms(h, g_ref[...])
        hn_ref[rows, :] = _pack_bf16_pair(hn)
        hn_hi = hn.astype(BF16)
        normed.append((hn_hi, (hn - hn_hi.astype(F32)).astype(BF16)))
    for r in range(n_sub):
        rows = pl.ds(r * hs, hs)
        ri, rw = _route(normed[r][0], normed[r][1], wr_ref, br_ref)
        ri_ref[rows, :] = ri
        rw_ref[rows, :] = rw


def _route(hn_hi, hn_lo, wr_ref, br_ref):
    wr_hi = wr_ref[0]
    logits = (jnp.dot(hn_hi, wr_hi, preferred_element_type=F32)
              + jnp.dot(hn_lo, wr_hi, preferred_element_type=F32)
              + jnp.dot(hn_hi, wr_ref[1], preferred_element_type=F32)) + br_ref[...]
    ng, epg = N_EXPERT_GROUPS, EXPERTS_PER_GROUP
    lane = lax.broadcasted_iota(jnp.int32, logits.shape, 1)
    big = jnp.int32(LANES)
    low = jnp.float32(-3.0e38)
    cm = lane < ng
    c = jnp.where(cm, logits, low)
    cmax = jnp.max(c, axis=1, keepdims=True)
    gsel = jnp.min(jnp.where(c == cmax, lane, big), axis=1, keepdims=True)
    p_group = 1.0 / jnp.sum(jnp.where(cm, jnp.exp(c - cmax), 0.0), axis=1, keepdims=True)
    f_lo = ng + epg * gsel
    fm = (lane >= f_lo) & (lane < f_lo + epg)
    f = jnp.where(fm, logits, low)
    v1 = jnp.max(f, axis=1, keepdims=True)
    i1 = jnp.min(jnp.where(fm & (f == v1), lane, big), axis=1, keepdims=True)
    fm2 = fm & (lane != i1)
    f2 = jnp.where(fm2, logits, low)
    v2 = jnp.max(f2, axis=1, keepdims=True)
    i2 = jnp.min(jnp.where(fm2 & (f2 == v2), lane, big), axis=1, keepdims=True)
    e21 = jnp.exp(v2 - v1)
    w1 = p_group / (1.0 + e21)
    w2 = p_group * e21 / (1.0 + e21)
    return (jnp.where(lane == 0, i1 - ng, jnp.where(lane == 1, i2 - ng, 0)),
            jnp.where(lane == 0, w1, jnp.where(lane == 1, w2, 0.0)))


def _outproj(attn2, pool2, x2, w_out_bf16, g2, wr, br, tm=512, n_sub=2):
    t, d = x2.shape
    ka, kp = attn2.shape[1], pool2.shape[1]
    row = lambda i: (i, 0)
    const = lambda i: (0, 0)
    return pl.pallas_call(
        functools.partial(_outproj_kernel, n_sub=n_sub),
        grid=(t // tm,),
        in_specs=[
            pl.BlockSpec((tm, ka), row),
            pl.BlockSpec((tm, kp), row),
            pl.BlockSpec((tm, d), row),
            pl.BlockSpec((ka + kp, d), const, pipeline_mode=pl.Buffered(1)),
            pl.BlockSpec((1, d), const),
            pl.BlockSpec((2, d, LANES), lambda i: (0, 0, 0)),
            pl.BlockSpec((1, LANES), const),
        ],
        out_specs=[
            pl.BlockSpec((tm, d), row),
            pl.BlockSpec((tm, d // 2), row),
            pl.BlockSpec((tm, LANES), row),
            pl.BlockSpec((tm, LANES), row),
        ],
        out_shape=[
            jax.ShapeDtypeStruct((t, d), F32),
            jax.ShapeDtypeStruct((t, d // 2), jnp.uint32),
            jax.ShapeDtypeStruct((t, LANES), jnp.int32),
            jax.ShapeDtypeStruct((t, LANES), F32),
        ],
        compiler_params=pltpu.CompilerParams(
            dimension_semantics=("arbitrary",), vmem_limit_bytes=VMEM_LIMIT),
        name="outproj_router",
    )(attn2, pool2, x2, w_out_bf16, g2.reshape(1, d), wr, br)


def _unpack_bf16_pair(words):
    hi = lax.bitcast_convert_type(words & jnp.uint32(0xFFFF0000), F32).astype(BF16)
    lo = lax.bitcast_convert_type(words << 16, F32).astype(BF16)
    return hi, lo


def _expert_kernel(te_ref, nt_ref, first_ref, slot_ref, nxt_ref, nxt2_ref,
                   x_ref, wg_hbm, wu_hbm, wd_hbm, y_ref,
                   wgl, wul, wdl, wg, wu, wd, wsem):
    j = pl.program_id(0)
    nt = nt_ref[0]
    kh = x_ref.shape[1]

    def weight_copies(e, s):
        return (pltpu.make_async_copy(wg_hbm.at[e], wgl.at[s], wsem.at[s]),
                pltpu.make_async_copy(wu_hbm.at[e], wul.at[s], wsem.at[s]),
                pltpu.make_async_copy(wd_hbm.at[e], wdl.at[s], wsem.at[s]))

    def start_weights(e, s):
        for c in weight_copies(e, s):
            c.start(priority=1)

    @pl.when(j == 0)
    def _():
        start_weights(te_ref[0], 0)

        @pl.when(nxt_ref[0] >= 0)
        def _():
            start_weights(nxt_ref[0], 1)

    @pl.when(j < nt)
    def _():
        @pl.when(first_ref[j] == 1)
        def _():
            ws = slot_ref[j]
            for c in weight_copies(te_ref[j], ws):
                c.wait()
            wg[...] = wgl[ws].astype(BF16)
            wu[...] = wul[ws].astype(BF16)
            wd[...] = wdl[ws].astype(BF16)

            @pl.when(nxt2_ref[j] >= 0)
            def _():
                start_weights(nxt2_ref[j], ws)

        xa, xb = _unpack_bf16_pair(x_ref[...])
        hg = (jnp.dot(xa, wg[:kh, :], preferred_element_type=F32)
              + jnp.dot(xb, wg[kh:, :], preferred_element_type=F32))
        hu = (jnp.dot(xa, wu[:kh, :], preferred_element_type=F32)
              + jnp.dot(xb, wu[kh:, :], preferred_element_type=F32))
        act = (hg / (1.0 + jnp.exp(-hg))) * hu
        y_ref[...] = _pack_bf16_pair(
            jnp.dot(act.astype(BF16), wd[...], preferred_element_type=F32))

    @pl.when(j >= nt)
    def _():
        y_ref[...] = jnp.zeros_like(y_ref)


def _experts(plan, xs, w_gate, w_up, w_down, tm=ROW_TILE):
    d, f = w_gate.shape[1], w_gate.shape[2]
    assert xs.shape[1] * 2 == d
    nt = plan["tile_expert"].shape[0]
    any_spec = pl.BlockSpec(memory_space=pl.ANY)
    grid_spec = pltpu.PrefetchScalarGridSpec(
        num_scalar_prefetch=6,
        grid=(nt,),
        in_specs=[
            pl.BlockSpec((tm, d // 2), lambda j, te, n, *_: (jnp.minimum(j, n[0] - 1), 0)),
            any_spec, any_spec, any_spec],
        out_specs=pl.BlockSpec((tm, d // 2), lambda j, *_: (j, 0)),
        scratch_shapes=[
            pltpu.VMEM((2, d, f), F32),
            pltpu.VMEM((2, d, f), F32),
            pltpu.VMEM((2, f, d), F32),
            pltpu.VMEM((d, f), BF16),
            pltpu.VMEM((d, f), BF16),
            pltpu.VMEM((f, d), BF16),
            pltpu.SemaphoreType.DMA((2,)),
        ],
    )
    return pl.pallas_call(
        _expert_kernel,
        grid_spec=grid_spec,
        out_shape=jax.ShapeDtypeStruct((nt * tm, d // 2), jnp.uint32),
        compiler_params=pltpu.CompilerParams(
            dimension_semantics=("arbitrary",), vmem_limit_bytes=VMEM_LIMIT),
        name="experts",
    )(plan["tile_expert"], plan["n_tiles"], plan["first"], plan["slot"], plan["next_expert"],
      plan["next2_expert"], xs, w_gate, w_up, w_down)


def _combine_kernel(pos_ref, h_ref, rw_ref, ys_hbm, g_ref, o_ref, ybuf0, ybuf1, ybuf2, sem):
    i = pl.program_id(0)
    last = pl.num_programs(0) - 1
    tm = h_ref.shape[0]
    kh = ybuf0.shape[2]
    bufs = (ybuf0, ybuf1, ybuf2)
    nbuf = len(bufs)

    def start_gather(tile, buf, sm):
        for r in range(tm):
            for k in range(TOP_K_FINE):
                p = pos_ref[(tile * tm + r) * TOP_K_FINE + k]
                pltpu.make_async_copy(ys_hbm.at[pl.ds(p, 1), :], buf.at[k, pl.ds(r, 1), :],
                                      sm).start(priority=k % 2)

    def wait_gather(buf, sm):
        for k in range(TOP_K_FINE):
            pltpu.make_async_copy(ys_hbm.at[pl.ds(0, tm), :], buf.at[k], sm).wait()

    @pl.when(i == 0)
    def _():
        start_gather(0, ybuf0, sem.at[0])
        start_gather(1, ybuf1, sem.at[1])

    def step(par):
        cur = bufs[par]
        ahead = (par + nbuf - 1) % nbuf
        wait_gather(cur, sem.at[par])
        start_gather(jnp.minimum(i + nbuf - 1, last), bufs[ahead], sem.at[ahead])
        w = rw_ref[...]
        halves = []
        for part in range(2):
            y = None
            for k in range(TOP_K_FINE):
                words = cur[k]
                bits = (words & jnp.uint32(0xFFFF0000)) if part == 0 else (words << 16)
                term = w[:, k:k + 1] * lax.bitcast_convert_type(bits, F32)
                y = term if y is None else y + term
            halves.append(h_ref[:, part * kh:(part + 1) * kh] + y)
        ms = sum(jnp.sum(v * v, axis=-1, keepdims=True) for v in halves) / (2 * kh)
        inv = lax.rsqrt(ms + RMS_EPS)
        for part in range(2):
            o_ref[:, part * kh:(part + 1) * kh] = (
                halves[part] * inv * g_ref[:, part * kh:(part + 1) * kh])

        @pl.when(i == last)
        def _():
            for other in range(nbuf):
                if other != par:
                    wait_gather(bufs[other], sem.at[other])

    for par in range(nbuf):
        pl.when(i % nbuf == par)(functools.partial(step, par))


def _combine(pos, h, rw, ys, g, tm=256):
    t, d = h.shape
    grid_spec = pltpu.PrefetchScalarGridSpec(
        num_scalar_prefetch=1,
        grid=(t // tm,),
        in_specs=[
            pl.BlockSpec((tm, d), lambda i, p: (i, 0)),
            pl.BlockSpec((tm, LANES), lambda i, p: (i, 0)),
            pl.BlockSpec(memory_space=pl.ANY),
            pl.BlockSpec((1, d), lambda i, p: (0, 0)),
        ],
        out_specs=pl.BlockSpec((tm, d), lambda i, p: (i, 0)),
        scratch_shapes=[pltpu.VMEM((TOP_K_FINE, tm, d // 2), jnp.uint32),
                        pltpu.VMEM((TOP_K_FINE, tm, d // 2), jnp.uint32),
                        pltpu.VMEM((TOP_K_FINE, tm, d // 2), jnp.uint32),
                        pltpu.SemaphoreType.DMA((3,))],
    )
    return pl.pallas_call(
        _combine_kernel,
        grid_spec=grid_spec,
        out_shape=jax.ShapeDtypeStruct((t, d), F32),
        compiler_params=pltpu.CompilerParams(
            dimension_semantics=("arbitrary",), vmem_limit_bytes=VMEM_LIMIT),
        name="combine",
    )(pos, h, rw, ys, g.reshape(1, d))


def _rank_kernel(ri_ref, meta_ref, cnt_ref, carry_ref, tri_ref):
    tt = ri_ref.shape[0]
    ne = cnt_ref.shape[0]

    @pl.when(pl.program_id(0) == 0)
    def _():
        carry_ref[...] = jnp.zeros_like(carry_ref)
        earlier = (lax.broadcasted_iota(jnp.int32, (tt, tt), 0)
                   < lax.broadcasted_iota(jnp.int32, (tt, tt), 1))
        tri_ref[...] = jnp.where(earlier, 1.0, 0.0).astype(BF16)

    rit = ri_ref[...].astype(F32).T
    expert = lax.broadcasted_iota(jnp.int32, (ne, tt), 0).astype(F32)
    ranks = []
    hits = []
    for k in range(TOP_K_FINE):
        hits.append(expert == rit[k:k + 1, :])
    chosen = jnp.where(hits[0] | hits[1], 1.0, 0.0)
    before = (jnp.dot(chosen.astype(BF16), tri_ref[...], preferred_element_type=F32)
              + carry_ref[:, 0:1])
    for k in range(TOP_K_FINE):
        meta_ref[k:k + 1, :] = rit[k:k + 1, :].astype(jnp.int32)
        rank = jnp.sum(jnp.where(hits[k], before, 0.0), axis=0, keepdims=True)
        meta_ref[TOP_K_FINE + k:TOP_K_FINE + k + 1, :] = rank.astype(jnp.int32)
    meta_ref[2 * TOP_K_FINE:, :] = jnp.zeros((meta_ref.shape[0] - 2 * TOP_K_FINE, tt), jnp.int32)
    carry_ref[...] = carry_ref[...] + jnp.sum(chosen, axis=1, keepdims=True)
    cnt_ref[...] = carry_ref[...].astype(jnp.int32)


def _rank(ri, tt=1024):
    t = ri.shape[0]
    assert TOP_K_FINE == 2
    return pl.pallas_call(
        _rank_kernel,
        grid=(t // tt,),
        in_specs=[pl.BlockSpec((tt, LANES), lambda i: (i, 0))],
        out_specs=[pl.BlockSpec((None, 8, tt), lambda i: (i, 0, 0)),
                   pl.BlockSpec((N_EXPERTS, LANES), lambda i: (0, 0))],
        out_shape=[jax.ShapeDtypeStruct((t // tt, 8, tt), jnp.int32),
                   jax.ShapeDtypeStruct((N_EXPERTS, LANES), jnp.int32)],
        scratch_shapes=[pltpu.VMEM((N_EXPERTS, LANES), F32), pltpu.VMEM((tt, tt), BF16)],
        compiler_params=pltpu.CompilerParams(
            dimension_semantics=("arbitrary",), vmem_limit_bytes=VMEM_LIMIT),
        name="route_rank",
    )(ri)


def _scatter_rows_kernel(pos_ref, end_ref, nt_ref, src_hbm, dst_hbm, zbuf, st0, st1, st2,
                         sem, lsem, zsem, *, tm, row_tile):
    i = pl.program_id(0)
    stage = (st0, st1, st2)
    n_tiles_max = dst_hbm.shape[0] // row_tile

    def zero_copy(tile):
        row0 = pl.multiple_of(tile * row_tile, row_tile)
        return pltpu.make_async_copy(zbuf, dst_hbm.at[pl.ds(row0, row_tile), :], zsem)

    @pl.when(i == 0)
    def _():
        zbuf[...] = jnp.zeros_like(zbuf)
        n_exp = end_ref.shape[0]

        def has_rows(e):
            return end_ref[e] > (end_ref[e - 1] if e else 0)

        for e in range(n_exp):
            pl.when(has_rows(e))(lambda e=e: zero_copy(end_ref[e] - 1).start())

        def start_tail(tile, c):
            zero_copy(tile).start()
            return c

        def wait_tail(tile, c):
            zero_copy(tile).wait()
            return c

        lax.fori_loop(nt_ref[0], n_tiles_max, start_tail, 0)
        for e in range(n_exp):
            pl.when(has_rows(e))(lambda: zero_copy(0).wait())
        lax.fori_loop(nt_ref[0], n_tiles_max, wait_tail, 0)

    last = pl.num_programs(0) - 1
    nbuf = len(stage)

    def load(tile, b):
        row0 = pl.multiple_of(tile * tm, tm)
        return pltpu.make_async_copy(src_hbm.at[pl.ds(row0, tm), :], stage[b], lsem.at[b])

    def wait_rows(b):
        for _ in range(TOP_K_FINE):
            pltpu.make_async_copy(stage[b], dst_hbm.at[pl.ds(0, tm), :], sem.at[b]).wait()

    @pl.when(i == 0)
    def _():
        load(0, 0).start()

    def step(b):
        nb = (b + 1) % nbuf

        @pl.when(i >= nbuf - 1)
        def _():
            wait_rows(nb)

        @pl.when(i < last)
        def _():
            load(i + 1, nb).start()

        load(i, b).wait()
        for r in range(tm):
            for k in range(TOP_K_FINE):
                p = pos_ref[(i * tm + r) * TOP_K_FINE + k]
                pltpu.make_async_copy(stage[b].at[pl.ds(r, 1), :], dst_hbm.at[pl.ds(p, 1), :],
                                      sem.at[b]).start(priority=k % 2)

        @pl.when(i == last)
        def _():
            for d in range(nbuf - 1):
                @pl.when(i >= d)
                def _(d=d):
                    wait_rows((b - d) % nbuf)

    for b in range(nbuf):
        pl.when(i % nbuf == b)(functools.partial(step, b))


def _scatter_rows(plan, src, tm=256, row_tile=ROW_TILE):
    t, w = src.shape
    any_spec = pl.BlockSpec(memory_space=pl.ANY)
    grid_spec = pltpu.PrefetchScalarGridSpec(
        num_scalar_prefetch=3,
        grid=(t // tm,),
        in_specs=[any_spec],
        out_specs=any_spec,
        scratch_shapes=[pltpu.VMEM((row_tile, w), src.dtype),
                        pltpu.VMEM((tm, w), src.dtype), pltpu.VMEM((tm, w), src.dtype),
                        pltpu.VMEM((tm, w), src.dtype),
                        pltpu.SemaphoreType.DMA((3,)), pltpu.SemaphoreType.DMA((3,)),
                        pltpu.SemaphoreType.DMA(())],
    )
    return pl.pallas_call(
        functools.partial(_scatter_rows_kernel, tm=tm, row_tile=row_tile),
        grid_spec=grid_spec,
        out_shape=jax.ShapeDtypeStruct((plan["n_rows"], w), src.dtype),
        compiler_params=pltpu.CompilerParams(dimension_semantics=("arbitrary",)),
        name="scatter_rows",
    )(plan["pos"], plan["end_tile"], plan["n_tiles"], src)


def _sort_plan(ri, tm=ROW_TILE):
    t = ri.shape[0]
    k = TOP_K_FINE
    a = t * k
    nt = a // tm + N_EXPERTS
    meta, cnt = _rank(ri)
    counts = cnt[:, 0]
    padded = ((counts + tm - 1) // tm) * tm
    off_end = jnp.cumsum(padded)
    off = off_end - padded
    ef = meta[:, 0:k, :]
    eid = jnp.arange(N_EXPERTS, dtype=jnp.int32)
    pos = jnp.sum(jnp.where(ef[..., None] == eid, off, 0), axis=-1) + meta[:, k:2 * k, :]
    pos = jnp.transpose(pos, (0, 2, 1)).reshape(a)
    n_tiles = off_end[-1] // tm
    tile_idx = jnp.arange(nt, dtype=jnp.int32)
    te = jnp.sum((off_end[None, :] <= (tile_idx * tm)[:, None]).astype(jnp.int32), axis=1)
    te = jnp.minimum(te, N_EXPERTS - 1)
    te = jnp.where(tile_idx < n_tiles, te, te[n_tiles - 1]).astype(jnp.int32)
    first = ((tile_idx == 0) | (te != jnp.roll(te, 1))) & (tile_idx < n_tiles)
    slot = (jnp.cumsum(first.astype(jnp.int32)) - 1) % 2
    eid = jnp.arange(N_EXPERTS, dtype=jnp.int32)
    later = (eid[None, :] > eid[:, None]) & (counts[None, :] > 0)
    next_e = jnp.min(jnp.where(later, eid[None, :], N_EXPERTS), axis=1)
    next_e = jnp.where(next_e < N_EXPERTS, next_e, -1)
    next2_e = jnp.where(next_e >= 0, next_e[jnp.maximum(next_e, 0)], -1)
    return dict(
        next2_expert=next2_e[te].astype(jnp.int32),
        end_tile=(off_end // tm).astype(jnp.int32),
        pos=pos.astype(jnp.int32), n_rows=nt * tm, tile_expert=te,
        n_tiles=n_tiles.reshape(1).astype(jnp.int32), first=first.astype(jnp.int32),
        slot=slot.astype(jnp.int32), next_expert=next_e[te].astype(jnp.int32))


def kernel(x, norm1_g, w_in, lambda_q1, lambda_k1, lambda_q2, lambda_k2, subln_g, pool_w, pool_scale, w_out, norm2_g, w_coarse, b_coarse, w_fine, b_fine, w_gate, w_up, w_down, final_norm_g):
    b, s, d = x.shape
    t = b * s
    assert norm1_g.shape[0] == 1
    x2 = x.reshape(t, d)
    proj = _inproj(x2, norm1_g[0], w_in[0].astype(BF16))
    proj3 = proj.reshape(b, s, proj.shape[1])
    attn = _attention(proj3, lambda_q1, lambda_k1, lambda_q2, lambda_k2, subln_g[0])
    pool = _pool(proj3, pool_w[0], pool_scale[0])

    ng, epg = N_EXPERT_GROUPS, EXPERTS_PER_GROUP
    wr = jnp.concatenate(
        [w_coarse[0], jnp.transpose(w_fine[0], (1, 0, 2)).reshape(d, ng * epg)], axis=1)
    wr = jnp.pad(wr, ((0, 0), (0, LANES - wr.shape[1])))
    br = jnp.concatenate([b_coarse[0], b_fine[0].reshape(ng * epg)])
    br = jnp.pad(br, (0, LANES - br.shape[0])).reshape(1, LANES)
    wr_hi = wr.astype(BF16)
    wr_lo = (wr - wr_hi.astype(F32)).astype(BF16)
    h, hn, ri, rw = _outproj(attn.reshape(t, ATTN_WIDTH), pool.reshape(t, POOL_WIDTH), x2,
                             w_out[0].astype(BF16), norm2_g[0], jnp.stack([wr_hi, wr_lo]), br)

    plan = _sort_plan(ri)
    xs = _scatter_rows(plan, hn)
    ys = _experts(plan, xs, w_gate[0], w_up[0], w_down[0])
    out = _combine(plan["pos"], h, rw, ys, final_norm_g)
    return out.reshape(b, s, d)
```

```python
import functools
import math

import jax
import jax.numpy as jnp
from jax import lax
from jax.experimental import pallas as pl
from jax.experimental.pallas import tpu as pltpu

N_DIFF_HEADS = 8
DIFF_HEAD_DIM = 64
DIFF_V_DIM = 2 * DIFF_HEAD_DIM
ATTN_WIDTH = N_DIFF_HEADS * DIFF_V_DIM
POOL_WINDOWS = (2, 4, 8, 16)
POOL_GROUP_DIM = 256
POOL_WIDTH = len(POOL_WINDOWS) * POOL_GROUP_DIM
N_EXPERT_GROUPS = 4
EXPERTS_PER_GROUP = 8
N_EXPERTS = N_EXPERT_GROUPS * EXPERTS_PER_GROUP
TOP_K_FINE = 2
RMS_EPS = 1e-6
NEG_INF = -1e30
LAM_INIT = 0.8 - 0.6 * math.exp(-0.3 * 0)
LOG2E = math.log2(math.e)
Q_SCALE = DIFF_HEAD_DIM ** -0.5 * LOG2E

LANES = 128
VMEM_LIMIT = 56 * 1024 * 1024
ROW_TILE = 256

F32 = jnp.float32
BF16 = jnp.bfloat16


def _rms(x, g):
    return x * lax.rsqrt(jnp.mean(x * x, axis=-1, keepdims=True) + RMS_EPS) * g


def _inproj_kernel(x_ref, g_ref, w_ref, o_ref, *, tn):
    xn = _rms(x_ref[...], g_ref[...]).astype(BF16)
    for c in range(w_ref.shape[1] // tn):
        cols = slice(c * tn, (c + 1) * tn)
        acc = jnp.dot(xn, w_ref[:, cols], preferred_element_type=F32)
        if c == 0:
            acc = acc * Q_SCALE
        o_ref[:, cols] = acc.astype(o_ref.dtype)


def _inproj(x2, g, w_bf16, tm=512, tn=ATTN_WIDTH):
    t, d = x2.shape
    n = w_bf16.shape[1]
    return pl.pallas_call(
        functools.partial(_inproj_kernel, tn=tn),
        grid=(t // tm,),
        in_specs=[
            pl.BlockSpec((tm, d), lambda i: (i, 0)),
            pl.BlockSpec((1, d), lambda i: (0, 0)),
            pl.BlockSpec((d, n), lambda i: (0, 0), pipeline_mode=pl.Buffered(1)),
        ],
        out_specs=pl.BlockSpec((tm, n), lambda i: (i, 0)),
        out_shape=jax.ShapeDtypeStruct((t, n), BF16),
        compiler_params=pltpu.CompilerParams(
            dimension_semantics=("arbitrary",), vmem_limit_bytes=VMEM_LIMIT),
        name="inproj",
    )(x2, g.reshape(1, d), w_bf16)


def _attn_kernel(lq1_ref, lk1_ref, lq2_ref, lk2_ref, q_ref, k_ref, v_ref, g_ref, o_ref,
                 tab_ref, vt_ref, *, tq):
    h = pl.program_id(0)
    s_len = q_ref.shape[0]
    nq = s_len // tq
    d, dv = DIFF_HEAD_DIM, DIFF_V_DIM

    @pl.when(pl.program_id(1) == 0)
    def _():
        slope = jnp.exp2(jnp.full((1, 1), -8.0 / N_DIFF_HEADS, F32) * (h + 1).astype(F32))
        rel = ((nq - 1) * tq + lax.broadcasted_iota(jnp.int32, (s_len, tq), 1)
               - lax.broadcasted_iota(jnp.int32, (s_len, tq), 0))
        tab_ref[...] = jnp.where(rel >= 0, (-LOG2E * slope) * rel.astype(F32), NEG_INF)

    vt_ref[:dv, :] = v_ref[...].astype(F32).T.astype(BF16)
    vt_ref[dv:, :] = jnp.ones((vt_ref.shape[0] - dv, s_len), BF16)

    lam = (jnp.exp(jnp.sum(lq1_ref[...] * lk1_ref[...], axis=1, keepdims=True))
           - jnp.exp(jnp.sum(lq2_ref[...] * lk2_ref[...], axis=1, keepdims=True)) + LAM_INIT)
    dn = (((1,), (1,)), ((), ()))
    lane = lax.broadcasted_iota(jnp.int32, (tq, 2 * d), 1)

    def scores(qi):
        n = (qi + 1) * tq
        q = q_ref[qi * tq:(qi + 1) * tq, :]
        zero = jnp.zeros_like(q)
        qh = (jnp.where(lane < d, q, zero), jnp.where(lane >= d, q, zero))
        kk = k_ref[:n, :]
        bias = tab_ref[(nq - 1 - qi) * tq:(nq - 1 - qi) * tq + n, :]
        return [lax.dot_general(kk, qh[a], dn, preferred_element_type=F32) + bias
                for a in range(2)]

    s_next = scores(0)
    for qi in range(nq):
        n = (qi + 1) * tq
        s_cur = s_next
        if qi + 1 < nq:
            s_next = scores(qi + 1)
        outs = []
        for a in range(2):
            s = s_cur[a]
            m = jnp.max(s, axis=0, keepdims=True)
            p = jnp.exp2(s - m).astype(BF16)
            acc = jnp.dot(vt_ref[:, :n], p, preferred_element_type=F32)
            outs.append(acc[:dv] / acc[dv:dv + 1])
        ot = outs[0] - lam * outs[1]
        yt = ot * lax.rsqrt(jnp.mean(ot * ot, axis=0, keepdims=True) + RMS_EPS) * g_ref[...]
        o_ref[qi * tq:(qi + 1) * tq, :] = (yt * (1.0 - LAM_INIT)).T.astype(o_ref.dtype)


def _attention(proj3, lq1, lk1, lq2, lk2, subln_g, tq=256):
    b, s, _ = proj3.shape
    dv = DIFF_V_DIM
    nh = N_DIFF_HEADS
    ones_rows = 16
    lam_spec = pl.BlockSpec((1, DIFF_HEAD_DIM), lambda hi, bi: (0, 0))
    return pl.pallas_call(
        functools.partial(_attn_kernel, tq=tq),
        grid=(nh, b),
        in_specs=[
            lam_spec, lam_spec, lam_spec, lam_spec,
            pl.BlockSpec((None, s, dv), lambda hi, bi: (bi, 0, hi)),
            pl.BlockSpec((None, s, dv), lambda hi, bi: (bi, 0, nh + hi)),
            pl.BlockSpec((None, s, dv), lambda hi, bi: (bi, 0, 2 * nh + hi)),
            pl.BlockSpec((dv, 1), lambda hi, bi: (0, 0)),
        ],
        out_specs=pl.BlockSpec((None, s, dv), lambda hi, bi: (bi, 0, hi)),
        out_shape=jax.ShapeDtypeStruct((b, s, ATTN_WIDTH), BF16),
        scratch_shapes=[
            pltpu.VMEM((s, tq), F32),
            pltpu.VMEM((dv + ones_rows, s), BF16),
        ],
        compiler_params=pltpu.CompilerParams(
            dimension_semantics=("arbitrary", "arbitrary"),
            vmem_limit_bytes=VMEM_LIMIT),
        name="diff_attn",
    )(lq1, lk1, lq2, lk2, proj3, proj3, proj3, subln_g.reshape(dv, 1))


def _pool_kernel(u_ref, w_ref, sc_ref, o_ref):
    s = u_ref.shape[0]
    c = POOL_GROUP_DIM
    t = lax.broadcasted_iota(jnp.int32, (s, 1), 0)
    for gi, win in enumerate(POOL_WINDOWS):
        ch = u_ref[:, gi * c:(gi + 1) * c].astype(F32)
        acc = ch
        span = 1
        while span < win:
            acc = acc + jnp.where(t >= span, pltpu.roll(acc, span, axis=0), 0.0)
            span *= 2
        count = jnp.minimum(t + 1, win).astype(F32)
        pooled = acc / count - ch
        mixed = jnp.dot(pooled.astype(BF16), w_ref[gi].astype(BF16), preferred_element_type=F32)
        o_ref[:, gi * c:(gi + 1) * c] = (mixed * sc_ref[:, gi * c:(gi + 1) * c]).astype(o_ref.dtype)


def _pool(proj3, pool_w, pool_scale):
    b, s, n = proj3.shape
    assert all(w & (w - 1) == 0 for w in POOL_WINDOWS)
    return pl.pallas_call(
        _pool_kernel,
        grid=(b,),
        in_specs=[
            pl.BlockSpec((None, s, POOL_WIDTH), lambda bi: (bi, 0, n // POOL_WIDTH - 1)),
            pl.BlockSpec(pool_w.shape, lambda bi: (0, 0, 0)),
            pl.BlockSpec((1, POOL_WIDTH), lambda bi: (0, 0)),
        ],
        out_specs=pl.BlockSpec((None, s, POOL_WIDTH), lambda bi: (bi, 0, 0)),
        out_shape=jax.ShapeDtypeStruct((b, s, POOL_WIDTH), BF16),
        compiler_params=pltpu.CompilerParams(
            dimension_semantics=("arbitrary",), vmem_limit_bytes=VMEM_LIMIT),
        name="pool_mixer",
    )(proj3, pool_w, pool_scale.reshape(1, POOL_WIDTH))


def _pack_bf16_pair(x):
    bits = lax.bitcast_convert_type(x.astype(BF16).astype(F32), jnp.uint32)
    half = bits.shape[1] // 2
    return bits[:, :half] | (bits[:, half:] >> 16)


def _outproj_kernel(a_ref, p_ref, x_ref, w_ref, g_ref, wr_ref, br_ref,
                    h_ref, hn_ref, ri_ref, rw_ref, *, n_sub):
    ka = a_ref.shape[1]
    hs = x_ref.shape[0] // n_sub
    normed = []
    for r in range(n_sub):
        rows = pl.ds(r * hs, hs)
        mixed = (jnp.dot(a_ref[rows, :], w_ref[:ka, :], preferred_element_type=F32)
                 + jnp.dot(p_ref[rows, :], w_ref[ka:, :], preferred_element_type=F32))
        h = x_ref[rows, :] + mixed
        h_ref[rows, :] = h
        hn = _rms(h, g_ref[...])
        hn_ref[rows, :] = _pack_bf16_pair(hn)
        hn_hi = hn.astype(BF16)
        normed.append((hn_hi, (hn - hn_hi.astype(F32)).astype(BF16)))
    for r in range(n_sub):
        rows = pl.ds(r * hs, hs)
        ri, rw = _route(normed[r][0], normed[r][1], wr_ref, br_ref)
        ri_ref[rows, :] = ri
        rw_ref[rows, :] = rw


def _route(hn_hi, hn_lo, wr_ref, br_ref):
    wr_hi = wr_ref[0]
    logits = (jnp.dot(hn_hi, wr_hi, preferred_element_type=F32)
              + jnp.dot(hn_lo, wr_hi, preferred_element_type=F32)
              + jnp.dot(hn_hi, wr_ref[1], preferred_element_type=F32)) + br_ref[...]
    ng, epg = N_EXPERT_GROUPS, EXPERTS_PER_GROUP
    lane = lax.broadcasted_iota(jnp.int32, logits.shape, 1)
    big = jnp.int32(LANES)
    low = jnp.float32(-3.0e38)
    cm = lane < ng
    c = jnp.where(cm, logits, low)
    cmax = jnp.max(c, axis=1, keepdims=True)
    gsel = jnp.min(jnp.where(c == cmax, lane, big), axis=1, keepdims=True)
    p_group = 1.0 / jnp.sum(jnp.where(cm, jnp.exp(c - cmax), 0.0), axis=1, keepdims=True)
    f_lo = ng + epg * gsel
    fm = (lane >= f_lo) & (lane < f_lo + epg)
    f = jnp.where(fm, logits, low)
    v1 = jnp.max(f, axis=1, keepdims=True)
    i1 = jnp.min(jnp.where(fm & (f == v1), lane, big), axis=1, keepdims=True)
    fm2 = fm & (lane != i1)
    f2 = jnp.where(fm2, logits, low)
    v2 = jnp.max(f2, axis=1, keepdims=True)
    i2 = jnp.min(jnp.where(fm2 & (f2 == v2), lane, big), axis=1, keepdims=True)
    e21 = jnp.exp(v2 - v1)
    w1 = p_group / (1.0 + e21)
    w2 = p_group * e21 / (1.0 + e21)
    return (jnp.where(lane == 0, i1 - ng, jnp.where(lane == 1, i2 - ng, 0)),
            jnp.where(lane == 0, w1, jnp.where(lane == 1, w2, 0.0)))


def _outproj(attn2, pool2, x2, w_out_bf16, g2, wr, br, tm=512, n_sub=2):
    t, d = x2.shape
    ka, kp = attn2.shape[1], pool2.shape[1]
    row = lambda i: (i, 0)
    const = lambda i: (0, 0)
    return pl.pallas_call(
        functools.partial(_outproj_kernel, n_sub=n_sub),
        grid=(t // tm,),
        in_specs=[
            pl.BlockSpec((tm, ka), row),
            pl.BlockSpec((tm, kp), row),
            pl.BlockSpec((tm, d), row),
            pl.BlockSpec((ka + kp, d), const, pipeline_mode=pl.Buffered(1)),
            pl.BlockSpec((1, d), const),
            pl.BlockSpec((2, d, LANES), lambda i: (0, 0, 0)),
            pl.BlockSpec((1, LANES), const),
        ],
        out_specs=[
            pl.BlockSpec((tm, d), row),
            pl.BlockSpec((tm, d // 2), row),
            pl.BlockSpec((tm, LANES), row),
            pl.BlockSpec((tm, LANES), row),
        ],
        out_shape=[
            jax.ShapeDtypeStruct((t, d), F32),
            jax.ShapeDtypeStruct((t, d // 2), jnp.uint32),
            jax.ShapeDtypeStruct((t, LANES), jnp.int32),
            jax.ShapeDtypeStruct((t, LANES), F32),
        ],
        compiler_params=pltpu.CompilerParams(
            dimension_semantics=("arbitrary",), vmem_limit_bytes=VMEM_LIMIT),
        name="outproj_router",
    )(attn2, pool2, x2, w_out_bf16, g2.reshape(1, d), wr, br)


def _unpack_bf16_pair(words):
    hi = lax.bitcast_convert_type(words & jnp.uint32(0xFFFF0000), F32).astype(BF16)
    lo = lax.bitcast_convert_type(words << 16, F32).astype(BF16)
    return hi, lo


def _expert_kernel(te_ref, nt_ref, first_ref, slot_ref, nxt_ref,
                   x_ref, wg_hbm, wu_hbm, wd_hbm, y_ref,
                   wgb, wub, wdb, wsem):
    j = pl.program_id(0)
    nt = nt_ref[0]
    kh = x_ref.shape[1]
    fh = wdb.shape[1] // 2

    def weight_copies(e, s):
        return ((pltpu.make_async_copy(wg_hbm.at[e], wgb.at[s], wsem.at[s]), 0),
                (pltpu.make_async_copy(wu_hbm.at[e], wub.at[s], wsem.at[s]), 1),
                (pltpu.make_async_copy(wd_hbm.at[e, :fh, :], wdb.at[s, :fh, :], wsem.at[s]), 0),
                (pltpu.make_async_copy(wd_hbm.at[e, fh:, :], wdb.at[s, fh:, :], wsem.at[s]), 1))

    @pl.when(j == 0)
    def _():
        for c, prio in weight_copies(te_ref[0], 0):
            c.start(priority=prio)

    @pl.when(j < nt)
    def _():
        ws = slot_ref[j]

        @pl.when(first_ref[j] == 1)
        def _():
            for c, _ in weight_copies(te_ref[j], ws):
                c.wait()

            @pl.when(nxt_ref[j] >= 0)
            def _():
                for c, prio in weight_copies(nxt_ref[j], 1 - ws):
                    c.start(priority=prio)

        xa, xb = _unpack_bf16_pair(x_ref[...])
        hg = (jnp.dot(xa, wgb[ws, :kh, :].astype(BF16), preferred_element_type=F32)
              + jnp.dot(xb, wgb[ws, kh:, :].astype(BF16), preferred_element_type=F32))
        hu = (jnp.dot(xa, wub[ws, :kh, :].astype(BF16), preferred_element_type=F32)
              + jnp.dot(xb, wub[ws, kh:, :].astype(BF16), preferred_element_type=F32))
        act = (hg / (1.0 + jnp.exp(-hg))) * hu
        y_ref[...] = _pack_bf16_pair(
            jnp.dot(act.astype(BF16), wdb[ws].astype(BF16), preferred_element_type=F32))

    @pl.when(j >= nt)
    def _():
        y_ref[...] = jnp.zeros_like(y_ref)


def _experts(plan, xs, w_gate, w_up, w_down, tm=ROW_TILE):
    d, f = w_gate.shape[1], w_gate.shape[2]
    assert xs.shape[1] * 2 == d
    nt = plan["tile_expert"].shape[0]
    any_spec = pl.BlockSpec(memory_space=pl.ANY)
    grid_spec = pltpu.PrefetchScalarGridSpec(
        num_scalar_prefetch=5,
        grid=(nt,),
        in_specs=[
            pl.BlockSpec((tm, d // 2), lambda j, te, n, *_: (jnp.minimum(j, n[0] - 1), 0)),
            any_spec, any_spec, any_spec],
        out_specs=pl.BlockSpec((tm, d // 2), lambda j, *_: (j, 0)),
        scratch_shapes=[
            pltpu.VMEM((2, d, f), F32),
            pltpu.VMEM((2, d, f), F32),
            pltpu.VMEM((2, f, d), F32),
            pltpu.SemaphoreType.DMA((2,)),
        ],
    )
    return pl.pallas_call(
        _expert_kernel,
        grid_spec=grid_spec,
        out_shape=jax.ShapeDtypeStruct((nt * tm, d // 2), jnp.uint32),
        compiler_params=pltpu.CompilerParams(
            dimension_semantics=("arbitrary",), vmem_limit_bytes=VMEM_LIMIT),
        name="experts",
    )(plan["tile_expert"], plan["n_tiles"], plan["first"], plan["slot"], plan["next_expert"],
      xs, w_gate, w_up, w_down)


def _combine_kernel(pos_ref, h_ref, rw_ref, ys_hbm, g_ref, o_ref, ybuf0, ybuf1, ybuf2, sem):
    i = pl.program_id(0)
    last = pl.num_programs(0) - 1
    tm = h_ref.shape[0]
    kh = ybuf0.shape[2]
    bufs = (ybuf0, ybuf1, ybuf2)
    nbuf = len(bufs)

    def start_gather(tile, buf, sm):
        for r in range(tm):
            for k in range(TOP_K_FINE):
                p = pos_ref[(tile * tm + r) * TOP_K_FINE + k]
                pltpu.make_async_copy(ys_hbm.at[pl.ds(p, 1), :], buf.at[k, pl.ds(r, 1), :],
                                      sm).start(priority=k % 2)

    def wait_gather(buf, sm):
        for k in range(TOP_K_FINE):
            pltpu.make_async_copy(ys_hbm.at[pl.ds(0, tm), :], buf.at[k], sm).wait()

    @pl.when(i == 0)
    def _():
        start_gather(0, ybuf0, sem.at[0])
        start_gather(1, ybuf1, sem.at[1])

    def step(par):
        cur = bufs[par]
        ahead = (par + nbuf - 1) % nbuf
        wait_gather(cur, sem.at[par])
        start_gather(jnp.minimum(i + nbuf - 1, last), bufs[ahead], sem.at[ahead])
        w = rw_ref[...]
        halves = []
        for part in range(2):
            y = None
            for k in range(TOP_K_FINE):
                words = cur[k]
                bits = (words & jnp.uint32(0xFFFF0000)) if part == 0 else (words << 16)
                term = w[:, k:k + 1] * lax.bitcast_convert_type(bits, F32)
                y = term if y is None else y + term
            halves.append(h_ref[:, part * kh:(part + 1) * kh] + y)
        ms = sum(jnp.sum(v * v, axis=-1, keepdims=True) for v in halves) / (2 * kh)
        inv = lax.rsqrt(ms + RMS_EPS)
        for part in range(2):
            o_ref[:, part * kh:(part + 1) * kh] = (
                halves[part] * inv * g_ref[:, part * kh:(part + 1) * kh])

        @pl.when(i == last)
        def _():
            for other in range(nbuf):
                if other != par:
                    wait_gather(bufs[other], sem.at[other])

    for par in range(nbuf):
        pl.when(i % nbuf == par)(functools.partial(step, par))


def _combine(pos, h, rw, ys, g, tm=256):
    t, d = h.shape
    grid_spec = pltpu.PrefetchScalarGridSpec(
        num_scalar_prefetch=1,
        grid=(t // tm,),
        in_specs=[
            pl.BlockSpec((tm, d), lambda i, p: (i, 0)),
            pl.BlockSpec((tm, LANES), lambda i, p: (i, 0)),
            pl.BlockSpec(memory_space=pl.ANY),
            pl.BlockSpec((1, d), lambda i, p: (0, 0)),
        ],
        out_specs=pl.BlockSpec((tm, d), lambda i, p: (i, 0)),
        scratch_shapes=[pltpu.VMEM((TOP_K_FINE, tm, d // 2), jnp.uint32),
                        pltpu.VMEM((TOP_K_FINE, tm, d // 2), jnp.uint32),
                        pltpu.VMEM((TOP_K_FINE, tm, d // 2), jnp.uint32),
                        pltpu.SemaphoreType.DMA((3,))],
    )
    return pl.pallas_call(
        _combine_kernel,
        grid_spec=grid_spec,
        out_shape=jax.ShapeDtypeStruct((t, d), F32),
        compiler_params=pltpu.CompilerParams(
            dimension_semantics=("arbitrary",), vmem_limit_bytes=VMEM_LIMIT),
        name="combine",
    )(pos, h, rw, ys, g.reshape(1, d))


def _rank_kernel(ri_ref, meta_ref, cnt_ref, carry_ref, tri_ref):
    tt = ri_ref.shape[0]
    ne = cnt_ref.shape[0]

    @pl.when(pl.program_id(0) == 0)
    def _():
        carry_ref[...] = jnp.zeros_like(carry_ref)
        earlier = (lax.broadcasted_iota(jnp.int32, (tt, tt), 0)
                   < lax.broadcasted_iota(jnp.int32, (tt, tt), 1))
        tri_ref[...] = jnp.where(earlier, 1.0, 0.0).astype(BF16)

    rit = ri_ref[...].astype(F32).T
    expert = lax.broadcasted_iota(jnp.int32, (ne, tt), 0).astype(F32)
    ranks = []
    hits = []
    for k in range(TOP_K_FINE):
        hits.append(expert == rit[k:k + 1, :])
    chosen = jnp.where(hits[0] | hits[1], 1.0, 0.0)
    before = (jnp.dot(chosen.astype(BF16), tri_ref[...], preferred_element_type=F32)
              + carry_ref[:, 0:1])
    for k in range(TOP_K_FINE):
        meta_ref[k:k + 1, :] = rit[k:k + 1, :].astype(jnp.int32)
        rank = jnp.sum(jnp.where(hits[k], before, 0.0), axis=0, keepdims=True)
        meta_ref[TOP_K_FINE + k:TOP_K_FINE + k + 1, :] = rank.astype(jnp.int32)
    meta_ref[2 * TOP_K_FINE:, :] = jnp.zeros((meta_ref.shape[0] - 2 * TOP_K_FINE, tt), jnp.int32)
    carry_ref[...] = carry_ref[...] + jnp.sum(chosen, axis=1, keepdims=True)
    cnt_ref[...] = carry_ref[...].astype(jnp.int32)


def _rank(ri, tt=1024):
    t = ri.shape[0]
    assert TOP_K_FINE == 2
    return pl.pallas_call(
        _rank_kernel,
        grid=(t // tt,),
        in_specs=[pl.BlockSpec((tt, LANES), lambda i: (i, 0))],
        out_specs=[pl.BlockSpec((None, 8, tt), lambda i: (i, 0, 0)),
                   pl.BlockSpec((N_EXPERTS, LANES), lambda i: (0, 0))],
        out_shape=[jax.ShapeDtypeStruct((t // tt, 8, tt), jnp.int32),
                   jax.ShapeDtypeStruct((N_EXPERTS, LANES), jnp.int32)],
        scratch_shapes=[pltpu.VMEM((N_EXPERTS, LANES), F32), pltpu.VMEM((tt, tt), BF16)],
        compiler_params=pltpu.CompilerParams(
            dimension_semantics=("arbitrary",), vmem_limit_bytes=VMEM_LIMIT),
        name="route_rank",
    )(ri)


def _scatter_rows_kernel(pos_ref, end_ref, nt_ref, src_hbm, dst_hbm, zbuf, st0, st1, st2,
                         sem, lsem, zsem, *, tm, row_tile):
    i = pl.program_id(0)
    stage = (st0, st1, st2)
    n_tiles_max = dst_hbm.shape[0] // row_tile

    def zero_copy(tile):
        row0 = pl.multiple_of(tile * row_tile, row_tile)
        return pltpu.make_async_copy(zbuf, dst_hbm.at[pl.ds(row0, row_tile), :], zsem)

    @pl.when(i == 0)
    def _():
        zbuf[...] = jnp.zeros_like(zbuf)
        n_exp = end_ref.shape[0]

        def has_rows(e):
            return end_ref[e] > (end_ref[e - 1] if e else 0)

        for e in range(n_exp):
            pl.when(has_rows(e))(lambda e=e: zero_copy(end_ref[e] - 1).start())

        def start_tail(tile, c):
            zero_copy(tile).start()
            return c

        def wait_tail(tile, c):
            zero_copy(tile).wait()
            return c

        lax.fori_loop(nt_ref[0], n_tiles_max, start_tail, 0)
        for e in range(n_exp):
            pl.when(has_rows(e))(lambda: zero_copy(0).wait())
        lax.fori_loop(nt_ref[0], n_tiles_max, wait_tail, 0)

    last = pl.num_programs(0) - 1
    nbuf = len(stage)

    def load(tile, b):
        row0 = pl.multiple_of(tile * tm, tm)
        return pltpu.make_async_copy(src_hbm.at[pl.ds(row0, tm), :], stage[b], lsem.at[b])

    def wait_rows(b):
        for _ in range(TOP_K_FINE):
            pltpu.make_async_copy(stage[b], dst_hbm.at[pl.ds(0, tm), :], sem.at[b]).wait()

    @pl.when(i == 0)
    def _():
        load(0, 0).start()

    def step(b):
        nb = (b + 1) % nbuf

        @pl.when(i >= nbuf - 1)
        def _():
            wait_rows(nb)

        @pl.when(i < last)
        def _():
            load(i + 1, nb).start()

        load(i, b).wait()
        for r in range(tm):
            for k in range(TOP_K_FINE):
                p = pos_ref[(i * tm + r) * TOP_K_FINE + k]
                pltpu.make_async_copy(stage[b].at[pl.ds(r, 1), :], dst_hbm.at[pl.ds(p, 1), :],
                                      sem.at[b]).start(priority=k % 2)

        @pl.when(i == last)
        def _():
            for d in range(nbuf - 1):
                @pl.when(i >= d)
                def _(d=d):
                    wait_rows((b - d) % nbuf)

    for b in range(nbuf):
        pl.when(i % nbuf == b)(functools.partial(step, b))


def _scatter_rows(plan, src, tm=256, row_tile=ROW_TILE):
    t, w = src.shape
    any_spec = pl.BlockSpec(memory_space=pl.ANY)
    grid_spec = pltpu.PrefetchScalarGridSpec(
        num_scalar_prefetch=3,
        grid=(t // tm,),
        in_specs=[any_spec],
        out_specs=any_spec,
        scratch_shapes=[pltpu.VMEM((row_tile, w), src.dtype),
                        pltpu.VMEM((tm, w), src.dtype), pltpu.VMEM((tm, w), src.dtype),
                        pltpu.VMEM((tm, w), src.dtype),
                        pltpu.SemaphoreType.DMA((3,)), pltpu.SemaphoreType.DMA((3,)),
                        pltpu.SemaphoreType.DMA(())],
    )
    return pl.pallas_call(
        functools.partial(_scatter_rows_kernel, tm=tm, row_tile=row_tile),
        grid_spec=grid_spec,
        out_shape=jax.ShapeDtypeStruct((plan["n_rows"], w), src.dtype),
        compiler_params=pltpu.CompilerParams(dimension_semantics=("arbitrary",)),
        name="scatter_rows",
    )(plan["pos"], plan["end_tile"], plan["n_tiles"], src)


def _sort_plan(ri, tm=ROW_TILE):
    t = ri.shape[0]
    k = TOP_K_FINE
    a = t * k
    nt = a // tm + N_EXPERTS
    meta, cnt = _rank(ri)
    counts = cnt[:, 0]
    padded = ((counts + tm - 1) // tm) * tm
    off_end = jnp.cumsum(padded)
    off = off_end - padded
    ef = meta[:, 0:k, :]
    eid = jnp.arange(N_EXPERTS, dtype=jnp.int32)
    pos = jnp.sum(jnp.where(ef[..., None] == eid, off, 0), axis=-1) + meta[:, k:2 * k, :]
    pos = jnp.transpose(pos, (0, 2, 1)).reshape(a)
    n_tiles = off_end[-1] // tm
    tile_idx = jnp.arange(nt, dtype=jnp.int32)
    te = jnp.sum((off_end[None, :] <= (tile_idx * tm)[:, None]).astype(jnp.int32), axis=1)
    te = jnp.minimum(te, N_EXPERTS - 1)
    te = jnp.where(tile_idx < n_tiles, te, te[n_tiles - 1]).astype(jnp.int32)
    first = ((tile_idx == 0) | (te != jnp.roll(te, 1))) & (tile_idx < n_tiles)
    slot = (jnp.cumsum(first.astype(jnp.int32)) - 1) % 2
    eid = jnp.arange(N_EXPERTS, dtype=jnp.int32)
    later = (eid[None, :] > eid[:, None]) & (counts[None, :] > 0)
    next_e = jnp.min(jnp.where(later, eid[None, :], N_EXPERTS), axis=1)
    next_e = jnp.where(next_e < N_EXPERTS, next_e, -1)
    return dict(
        end_tile=(off_end // tm).astype(jnp.int32),
        pos=pos.astype(jnp.int32), n_rows=nt * tm, tile_expert=te,
        n_tiles=n_tiles.reshape(1).astype(jnp.int32), first=first.astype(jnp.int32),
        slot=slot.astype(jnp.int32), next_expert=next_e[te].astype(jnp.int32))


def kernel(x, norm1_g, w_in, lambda_q1, lambda_k1, lambda_q2, lambda_k2, subln_g, pool_w, pool_scale, w_out, norm2_g, w_coarse, b_coarse, w_fine, b_fine, w_gate, w_up, w_down, final_norm_g):
    b, s, d = x.shape
    t = b * s
    assert norm1_g.shape[0] == 1
    x2 = x.reshape(t, d)
    proj = _inproj(x2, norm1_g[0], w_in[0].astype(BF16))
    proj3 = proj.reshape(b, s, proj.shape[1])
    attn = _attention(proj3, lambda_q1, lambda_k1, lambda_q2, lambda_k2, subln_g[0])
    pool = _pool(proj3, pool_w[0], pool_scale[0])

    ng, epg = N_EXPERT_GROUPS, EXPERTS_PER_GROUP
    wr = jnp.concatenate(
        [w_coarse[0], jnp.transpose(w_fine[0], (1, 0, 2)).reshape(d, ng * epg)], axis=1)
    wr = jnp.pad(wr, ((0, 0), (0, LANES - wr.shape[1])))
    br = jnp.concatenate([b_coarse[0], b_fine[0].reshape(ng * epg)])
    br = jnp.pad(br, (0, LANES - br.shape[0])).reshape(1, LANES)
    wr_hi = wr.astype(BF16)
    wr_lo = (wr - wr_hi.astype(F32)).astype(BF16)
    h, hn, ri, rw = _outproj(attn.reshape(t, ATTN_WIDTH), pool.reshape(t, POOL_WIDTH), x2,
                             w_out[0].astype(BF16), norm2_g[0], jnp.stack([wr_hi, wr_lo]), br)

    plan = _sort_plan(ri)
    xs = _scatter_rows(plan, hn)
    ys = _experts(plan, xs, w_gate[0], w_up[0], w_down[0])
    out = _combine(plan["pos"], h, rw, ys, final_norm_g)
    return out.reshape(b, s, d)
```

```python
import functools
import math

import jax
import jax.numpy as jnp
from jax import lax
from jax.experimental import pallas as pl
from jax.experimental.pallas import tpu as pltpu

N_DIFF_HEADS = 8
DIFF_HEAD_DIM = 64
DIFF_V_DIM = 2 * DIFF_HEAD_DIM
ATTN_WIDTH = N_DIFF_HEADS * DIFF_V_DIM
POOL_WINDOWS = (2, 4, 8, 16)
POOL_GROUP_DIM = 256
POOL_WIDTH = len(POOL_WINDOWS) * POOL_GROUP_DIM
N_EXPERT_GROUPS = 4
EXPERTS_PER_GROUP = 8
N_EXPERTS = N_EXPERT_GROUPS * EXPERTS_PER_GROUP
TOP_K_FINE = 2
RMS_EPS = 1e-6
NEG_INF = -1e30
LAM_INIT = 0.8 - 0.6 * math.exp(-0.3 * 0)
LOG2E = math.log2(math.e)
Q_SCALE = DIFF_HEAD_DIM ** -0.5 * LOG2E

LANES = 128
VMEM_LIMIT = 56 * 1024 * 1024
ROW_TILE = 256

F32 = jnp.float32
BF16 = jnp.bfloat16


def _rms(x, g):
    return x * lax.rsqrt(jnp.mean(x * x, axis=-1, keepdims=True) + RMS_EPS) * g


def _load_weight_bf16(w_hbm, wb_ref, stage, sem):
    ck = stage.shape[1]
    nchunk = w_hbm.shape[0] // ck

    def copy(c):
        return pltpu.make_async_copy(w_hbm.at[c * ck:(c + 1) * ck, :], stage.at[c % 2], sem.at[c % 2])

    copy(0).start()
    for c in range(nchunk):
        if c + 1 < nchunk:
            copy(c + 1).start()
        copy(c).wait()
        wb_ref[c * ck:(c + 1) * ck, :] = stage[c % 2].astype(BF16)


def _inproj_kernel(x_ref, g_ref, w_hbm, o_ref, vt_ref, wb_ref, stage, sem, *, tn, v_chunk):
    @pl.when(pl.program_id(0) == 0)
    def _():
        _load_weight_bf16(w_hbm, wb_ref, stage, sem)

    xn = _rms(x_ref[...], g_ref[...]).astype(BF16)
    for c in range(wb_ref.shape[1] // tn):
        cols = slice(c * tn, (c + 1) * tn)
        acc = jnp.dot(xn, wb_ref[:, cols], preferred_element_type=F32)
        if c == 0:
            acc = acc * Q_SCALE
        if c == v_chunk:
            vt_ref[...] = acc.T.astype(vt_ref.dtype)
        else:
            oc = c if c < v_chunk else c - 1
            o_ref[:, oc * tn:(oc + 1) * tn] = acc.astype(o_ref.dtype)


def _inproj(x2, g, w, tm=512, tn=ATTN_WIDTH, chunk_rows=256):
    t, d = x2.shape
    n = w.shape[1]
    v_chunk = 2
    return pl.pallas_call(
        functools.partial(_inproj_kernel, tn=tn, v_chunk=v_chunk),
        grid=(t // tm,),
        in_specs=[
            pl.BlockSpec((tm, d), lambda i: (i, 0)),
            pl.BlockSpec((1, d), lambda i: (0, 0)),
            pl.BlockSpec(memory_space=pl.ANY),
        ],
        out_specs=[pl.BlockSpec((tm, n - tn), lambda i: (i, 0)),
                   pl.BlockSpec((tn, tm), lambda i: (0, i))],
        out_shape=[jax.ShapeDtypeStruct((t, n - tn), BF16), jax.ShapeDtypeStruct((tn, t), BF16)],
        scratch_shapes=[pltpu.VMEM((d, n), BF16), pltpu.VMEM((2, chunk_rows, n), F32),
                        pltpu.SemaphoreType.DMA((2,))],
        compiler_params=pltpu.CompilerParams(
            dimension_semantics=("arbitrary",), vmem_limit_bytes=VMEM_LIMIT),
        name="inproj",
    )(x2, g.reshape(1, d), w)


def _attn_kernel(lq1_ref, lk1_ref, lq2_ref, lk2_ref, q_ref, k_ref, v_ref, g_ref, o_ref,
                 tab_ref, vt_ref, *, tq):
    h = pl.program_id(0)
    s_len = q_ref.shape[0]
    nq = s_len // tq
    d, dv = DIFF_HEAD_DIM, DIFF_V_DIM

    @pl.when(pl.program_id(1) == 0)
    def _():
        slope = jnp.exp2(jnp.full((1, 1), -8.0 / N_DIFF_HEADS, F32) * (h + 1).astype(F32))
        rel = ((nq - 1) * tq + lax.broadcasted_iota(jnp.int32, (s_len, tq), 1)
               - lax.broadcasted_iota(jnp.int32, (s_len, tq), 0))
        tab_ref[...] = jnp.where(rel >= 0, (-LOG2E * slope) * rel.astype(F32), NEG_INF)

    vt_ref[:dv, :] = v_ref[...]
    vt_ref[dv:, :] = jnp.ones((vt_ref.shape[0] - dv, s_len), BF16)

    lam = (jnp.exp(jnp.sum(lq1_ref[...] * lk1_ref[...], axis=1, keepdims=True))
           - jnp.exp(jnp.sum(lq2_ref[...] * lk2_ref[...], axis=1, keepdims=True)) + LAM_INIT)
    dn = (((1,), (1,)), ((), ()))
    lane = lax.broadcasted_iota(jnp.int32, (tq, 2 * d), 1)

    def scores(qi):
        n = (qi + 1) * tq
        q = q_ref[qi * tq:(qi + 1) * tq, :]
        zero = jnp.zeros_like(q)
        qh = (jnp.where(lane < d, q, zero), jnp.where(lane >= d, q, zero))
        kk = k_ref[:n, :]
        bias = tab_ref[(nq - 1 - qi) * tq:(nq - 1 - qi) * tq + n, :]
        return [lax.dot_general(kk, qh[a], dn, preferred_element_type=F32) + bias
                for a in range(2)]

    ahead = 1
    pending = [scores(t) for t in range(min(ahead, nq))]
    for qi in range(nq):
        n = (qi + 1) * tq
        s_cur = pending.pop(0)
        if qi + ahead < nq:
            pending.append(scores(qi + ahead))
        outs = []
        for a in range(2):
            s = s_cur[a]
            m = jnp.max(s, axis=0, keepdims=True)
            p = jnp.exp2(s - m).astype(BF16)
            acc = jnp.dot(vt_ref[:, :n], p, preferred_element_type=F32)
            outs.append(acc[:dv] / acc[dv:dv + 1])
        ot = outs[0] - lam * outs[1]
        yt = ot * lax.rsqrt(jnp.mean(ot * ot, axis=0, keepdims=True) + RMS_EPS) * g_ref[...]
        o_ref[qi * tq:(qi + 1) * tq, :] = (yt * (1.0 - LAM_INIT)).T.astype(o_ref.dtype)


def _attention(proj3, v_t, lq1, lk1, lq2, lk2, subln_g, tq=256):
    b, s, _ = proj3.shape
    dv = DIFF_V_DIM
    nh = N_DIFF_HEADS
    ones_rows = 16
    lam_spec = pl.BlockSpec((1, DIFF_HEAD_DIM), lambda hi, bi: (0, 0))
    return pl.pallas_call(
        functools.partial(_attn_kernel, tq=tq),
        grid=(nh, b),
        in_specs=[
            lam_spec, lam_spec, lam_spec, lam_spec,
            pl.BlockSpec((None, s, dv), lambda hi, bi: (bi, 0, hi)),
            pl.BlockSpec((None, s, dv), lambda hi, bi: (bi, 0, nh + hi)),
            pl.BlockSpec((dv, s), lambda hi, bi: (hi, bi)),
            pl.BlockSpec((dv, 1), lambda hi, bi: (0, 0)),
        ],
        out_specs=pl.BlockSpec((None, s, dv), lambda hi, bi: (bi, 0, hi)),
        out_shape=jax.ShapeDtypeStruct((b, s, ATTN_WIDTH), BF16),
        scratch_shapes=[
            pltpu.VMEM((s, tq), F32),
            pltpu.VMEM((dv + ones_rows, s), BF16),
        ],
        compiler_params=pltpu.CompilerParams(
            dimension_semantics=("arbitrary", "arbitrary"),
            vmem_limit_bytes=VMEM_LIMIT),
        name="diff_attn",
    )(lq1, lk1, lq2, lk2, proj3, proj3, v_t, subln_g.reshape(dv, 1))


def _pool_kernel(u_ref, w_ref, sc_ref, o_ref):
    s = u_ref.shape[0]
    c = POOL_GROUP_DIM
    t = lax.broadcasted_iota(jnp.int32, (s, 1), 0)
    for gi, win in enumerate(POOL_WINDOWS):
        ch = u_ref[:, gi * c:(gi + 1) * c].astype(F32)
        acc = ch
        span = 1
        while span < win:
            acc = acc + jnp.where(t >= span, pltpu.roll(acc, span, axis=0), 0.0)
            span *= 2
        count = jnp.minimum(t + 1, win).astype(F32)
        pooled = acc / count - ch
        mixed = jnp.dot(pooled.astype(BF16), w_ref[gi].astype(BF16), preferred_element_type=F32)
        o_ref[:, gi * c:(gi + 1) * c] = (mixed * sc_ref[:, gi * c:(gi + 1) * c]).astype(o_ref.dtype)


def _pool(proj3, pool_w, pool_scale):
    b, s, n = proj3.shape
    assert all(w & (w - 1) == 0 for w in POOL_WINDOWS)
    return pl.pallas_call(
        _pool_kernel,
        grid=(b,),
        in_specs=[
            pl.BlockSpec((None, s, POOL_WIDTH), lambda bi: (bi, 0, n // POOL_WIDTH - 1)),
            pl.BlockSpec(pool_w.shape, lambda bi: (0, 0, 0)),
            pl.BlockSpec((1, POOL_WIDTH), lambda bi: (0, 0)),
        ],
        out_specs=pl.BlockSpec((None, s, POOL_WIDTH), lambda bi: (bi, 0, 0)),
        out_shape=jax.ShapeDtypeStruct((b, s, POOL_WIDTH), BF16),
        compiler_params=pltpu.CompilerParams(
            dimension_semantics=("arbitrary",), vmem_limit_bytes=VMEM_LIMIT),
        name="pool_mixer",
    )(proj3, pool_w, pool_scale.reshape(1, POOL_WIDTH))


def _pack_bf16_pair(x):
    bits = lax.bitcast_convert_type(x.astype(BF16).astype(F32), jnp.uint32)
    half = bits.shape[1] // 2
    return bits[:, :half] | (bits[:, half:] >> 16)


def _outproj_kernel(a_ref, p_ref, x_ref, w_hbm, g_ref, wr_ref, br_ref,
                    h_ref, hn_ref, ri_ref, rw_ref, w_ref, stage, sem, *, n_sub):
    @pl.when(pl.program_id(0) == 0)
    def _():
        _load_weight_bf16(w_hbm, w_ref, stage, sem)

    ka = a_ref.shape[1]
    hs = x_ref.shape[0] // n_sub
    normed = []
    for r in range(n_sub):
        rows = pl.ds(r * hs, hs)
        mixed = (jnp.dot(a_ref[rows, :], w_ref[:ka, :], preferred_element_type=F32)
                 + jnp.dot(p_ref[rows, :], w_ref[ka:, :], preferred_element_type=F32))
        h = x_ref[rows, :] + mixed
        h_ref[rows, :] = h
        hn = _rms(h, g_ref[...])
        hn_ref[rows, :] = _pack_bf16_pair(hn)
        hn_hi = hn.astype(BF16)
        normed.append((hn_hi, (hn - hn_hi.astype(F32)).astype(BF16)))
    for r in range(n_sub):
        rows = pl.ds(r * hs, hs)
        ri, rw = _route(normed[r][0], normed[r][1], wr_ref, br_ref)
        ri_ref[rows, :] = ri
        rw_ref[rows, :] = rw


def _route(hn_hi, hn_lo, wr_ref, br_ref):
    wr_hi = wr_ref[0]
    logits = (jnp.dot(hn_hi, wr_hi, preferred_element_type=F32)
              + jnp.dot(hn_lo, wr_hi, preferred_element_type=F32)
              + jnp.dot(hn_hi, wr_ref[1], preferred_element_type=F32)) + br_ref[...]
    ng, epg = N_EXPERT_GROUPS, EXPERTS_PER_GROUP
    lane = lax.broadcasted_iota(jnp.int32, logits.shape, 1)
    big = jnp.int32(LANES)
    low = jnp.float32(-3.0e38)
    cm = lane < ng
    c = jnp.where(cm, logits, low)
    cmax = jnp.max(c, axis=1, keepdims=True)
    gsel = jnp.min(jnp.where(c == cmax, lane, big), axis=1, keepdims=True)
    p_group = 1.0 / jnp.sum(jnp.where(cm, jnp.exp(c - cmax), 0.0), axis=1, keepdims=True)
    f_lo = ng + epg * gsel
    fm = (lane >= f_lo) & (lane < f_lo + epg)
    f = jnp.where(fm, logits, low)
    v1 = jnp.max(f, axis=1, keepdims=True)
    i1 = jnp.min(jnp.where(fm & (f == v1), lane, big), axis=1, keepdims=True)
    fm2 = fm & (lane != i1)
    f2 = jnp.where(fm2, logits, low)
    v2 = jnp.max(f2, axis=1, keepdims=True)
    i2 = jnp.min(jnp.where(fm2 & (f2 == v2), lane, big), axis=1, keepdims=True)
    e21 = jnp.exp(v2 - v1)
    w1 = p_group / (1.0 + e21)
    w2 = p_group * e21 / (1.0 + e21)
    return (jnp.where(lane == 0, i1 - ng, jnp.where(lane == 1, i2 - ng, 0)),
            jnp.where(lane == 0, w1, jnp.where(lane == 1, w2, 0.0)))


def _outproj(attn2, pool2, x2, w_out, g2, wr, br, tm=512, n_sub=2, chunk_rows=512):
    t, d = x2.shape
    ka, kp = attn2.shape[1], pool2.shape[1]
    row = lambda i: (i, 0)
    const = lambda i: (0, 0)
    return pl.pallas_call(
        functools.partial(_outproj_kernel, n_sub=n_sub),
        grid=(t // tm,),
        in_specs=[
            pl.BlockSpec((tm, ka), row),
            pl.BlockSpec((tm, kp), row),
            pl.BlockSpec((tm, d), row),
            pl.BlockSpec(memory_space=pl.ANY),
            pl.BlockSpec((1, d), const),
            pl.BlockSpec((2, d, LANES), lambda i: (0, 0, 0)),
            pl.BlockSpec((1, LANES), const),
        ],
        out_specs=[
            pl.BlockSpec((tm, d), row),
            pl.BlockSpec((tm, d // 2), row),
            pl.BlockSpec((tm, LANES), row),
            pl.BlockSpec((tm, LANES), row),
        ],
        out_shape=[
            jax.ShapeDtypeStruct((t, d), F32),
            jax.ShapeDtypeStruct((t, d // 2), jnp.uint32),
            jax.ShapeDtypeStruct((t, LANES), jnp.int32),
            jax.ShapeDtypeStruct((t, LANES), F32),
        ],
        scratch_shapes=[pltpu.VMEM((ka + kp, d), BF16), pltpu.VMEM((2, chunk_rows, d), F32),
                        pltpu.SemaphoreType.DMA((2,))],
        compiler_params=pltpu.CompilerParams(
            dimension_semantics=("arbitrary",), vmem_limit_bytes=VMEM_LIMIT),
        name="outproj_router",
    )(attn2, pool2, x2, w_out, g2.reshape(1, d), wr, br)


def _unpack_bf16_pair(words):
    hi = lax.bitcast_convert_type(words & jnp.uint32(0xFFFF0000), F32).astype(BF16)
    lo = lax.bitcast_convert_type(words << 16, F32).astype(BF16)
    return hi, lo


def _expert_kernel(te_ref, nt_ref, first_ref, slot_ref, nxt_ref,
                   x_ref, wg_hbm, wu_hbm, wd_hbm, y_ref,
                   wgb, wub, wdb, wsem):
    j = pl.program_id(0)
    nt = nt_ref[0]
    kh = x_ref.shape[1]
    fh = wdb.shape[1] // 2

    def weight_copies(e, s):
        return ((pltpu.make_async_copy(wg_hbm.at[e], wgb.at[s], wsem.at[s]), 0),
                (pltpu.make_async_copy(wu_hbm.at[e], wub.at[s], wsem.at[s]), 1),
                (pltpu.make_async_copy(wd_hbm.at[e, :fh, :], wdb.at[s, :fh, :], wsem.at[s]), 0),
                (pltpu.make_async_copy(wd_hbm.at[e, fh:, :], wdb.at[s, fh:, :], wsem.at[s]), 1))

    @pl.when(j == 0)
    def _():
        for c, prio in weight_copies(te_ref[0], 0):
            c.start(priority=prio)

    @pl.when(j < nt)
    def _():
        ws = slot_ref[j]

        @pl.when(first_ref[j] == 1)
        def _():
            for c, _ in weight_copies(te_ref[j], ws):
                c.wait()

            @pl.when(nxt_ref[j] >= 0)
            def _():
                for c, prio in weight_copies(nxt_ref[j], 1 - ws):
                    c.start(priority=prio)

        xa, xb = _unpack_bf16_pair(x_ref[...])
        hg = (jnp.dot(xa, wgb[ws, :kh, :].astype(BF16), preferred_element_type=F32)
              + jnp.dot(xb, wgb[ws, kh:, :].astype(BF16), preferred_element_type=F32))
        hu = (jnp.dot(xa, wub[ws, :kh, :].astype(BF16), preferred_element_type=F32)
              + jnp.dot(xb, wub[ws, kh:, :].astype(BF16), preferred_element_type=F32))
        act = (hg / (1.0 + jnp.exp(-hg))) * hu
        y_ref[...] = _pack_bf16_pair(
            jnp.dot(act.astype(BF16), wdb[ws].astype(BF16), preferred_element_type=F32))

    @pl.when(j >= nt)
    def _():
        y_ref[...] = jnp.zeros_like(y_ref)


def _experts(plan, xs, w_gate, w_up, w_down, tm=ROW_TILE):
    d, f = w_gate.shape[1], w_gate.shape[2]
    assert xs.shape[1] * 2 == d
    nt = plan["tile_expert"].shape[0]
    any_spec = pl.BlockSpec(memory_space=pl.ANY)
    grid_spec = pltpu.PrefetchScalarGridSpec(
        num_scalar_prefetch=5,
        grid=(nt,),
        in_specs=[
            pl.BlockSpec((tm, d // 2), lambda j, te, n, *_: (jnp.minimum(j, n[0] - 1), 0)),
            any_spec, any_spec, any_spec],
        out_specs=pl.BlockSpec((tm, d // 2), lambda j, *_: (j, 0)),
        scratch_shapes=[
            pltpu.VMEM((2, d, f), F32),
            pltpu.VMEM((2, d, f), F32),
            pltpu.VMEM((2, f, d), F32),
            pltpu.SemaphoreType.DMA((2,)),
        ],
    )
    return pl.pallas_call(
        _expert_kernel,
        grid_spec=grid_spec,
        out_shape=jax.ShapeDtypeStruct((nt * tm, d // 2), jnp.uint32),
        compiler_params=pltpu.CompilerParams(
            dimension_semantics=("arbitrary",), vmem_limit_bytes=VMEM_LIMIT),
        name="experts",
    )(plan["tile_expert"], plan["n_tiles"], plan["first"], plan["slot"], plan["next_expert"],
      xs, w_gate, w_up, w_down)


def _combine_kernel(pos_ref, h_ref, rw_ref, ys_hbm, g_ref, o_ref, ybuf0, ybuf1, ybuf2, sem):
    i = pl.program_id(0)
    last = pl.num_programs(0) - 1
    tm = h_ref.shape[0]
    kh = ybuf0.shape[2]
    bufs = (ybuf0, ybuf1, ybuf2)
    nbuf = len(bufs)

    def start_gather(tile, buf, sm):
        for r in range(tm):
            for k in range(TOP_K_FINE):
                p = pos_ref[(tile * tm + r) * TOP_K_FINE + k]
                pltpu.make_async_copy(ys_hbm.at[pl.ds(p, 1), :], buf.at[k, pl.ds(r, 1), :],
                                      sm).start(priority=k % 2)

    def wait_gather(buf, sm):
        for k in range(TOP_K_FINE):
            pltpu.make_async_copy(ys_hbm.at[pl.ds(0, tm), :], buf.at[k], sm).wait()

    @pl.when(i == 0)
    def _():
        start_gather(0, ybuf0, sem.at[0])
        start_gather(1, ybuf1, sem.at[1])

    def step(par):
        cur = bufs[par]
        ahead = (par + nbuf - 1) % nbuf
        wait_gather(cur, sem.at[par])
        start_gather(jnp.minimum(i + nbuf - 1, last), bufs[ahead], sem.at[ahead])
        w = rw_ref[...]
        halves = []
        for part in range(2):
            y = None
            for k in range(TOP_K_FINE):
                words = cur[k]
                bits = (words & jnp.uint32(0xFFFF0000)) if part == 0 else (words << 16)
                term = w[:, k:k + 1] * lax.bitcast_convert_type(bits, F32)
                y = term if y is None else y + term
            halves.append(h_ref[:, part * kh:(part + 1) * kh] + y)
        ms = sum(jnp.sum(v * v, axis=-1, keepdims=True) for v in halves) / (2 * kh)
        inv = lax.rsqrt(ms + RMS_EPS)
        for part in range(2):
            o_ref[:, part * kh:(part + 1) * kh] = (
                halves[part] * inv * g_ref[:, part * kh:(part + 1) * kh])

        @pl.when(i == last)
        def _():
            for other in range(nbuf):
                if other != par:
                    wait_gather(bufs[other], sem.at[other])

    for par in range(nbuf):
        pl.when(i % nbuf == par)(functools.partial(step, par))


def _combine(pos, h, rw, ys, g, tm=256):
    t, d = h.shape
    grid_spec = pltpu.PrefetchScalarGridSpec(
        num_scalar_prefetch=1,
        grid=(t // tm,),
        in_specs=[
            pl.BlockSpec((tm, d), lambda i, p: (i, 0)),
            pl.BlockSpec((tm, LANES), lambda i, p: (i, 0)),
            pl.BlockSpec(memory_space=pl.ANY),
            pl.BlockSpec((1, d), lambda i, p: (0, 0)),
        ],
        out_specs=pl.BlockSpec((tm, d), lambda i, p: (i, 0)),
        scratch_shapes=[pltpu.VMEM((TOP_K_FINE, tm, d // 2), jnp.uint32),
                        pltpu.VMEM((TOP_K_FINE, tm, d // 2), jnp.uint32),
                        pltpu.VMEM((TOP_K_FINE, tm, d // 2), jnp.uint32),
                        pltpu.SemaphoreType.DMA((3,))],
    )
    return pl.pallas_call(
        _combine_kernel,
        grid_spec=grid_spec,
        out_shape=jax.ShapeDtypeStruct((t, d), F32),
        compiler_params=pltpu.CompilerParams(
            dimension_semantics=("arbitrary",), vmem_limit_bytes=VMEM_LIMIT),
        name="combine",
    )(pos, h, rw, ys, g.reshape(1, d))


def _rank_kernel(ri_ref, meta_ref, cnt_ref, carry_ref, tri_ref):
    tt = ri_ref.shape[0]
    ne = cnt_ref.shape[0]

    @pl.when(pl.program_id(0) == 0)
    def _():
        carry_ref[...] = jnp.zeros_like(carry_ref)
        earlier = (lax.broadcasted_iota(jnp.int32, (tt, tt), 0)
                   < lax.broadcasted_iota(jnp.int32, (tt, tt), 1))
        tri_ref[...] = jnp.where(earlier, 1.0, 0.0).astype(BF16)

    rit = ri_ref[...].astype(F32).T
    expert = lax.broadcasted_iota(jnp.int32, (ne, tt), 0).astype(F32)
    ranks = []
    hits = []
    for k in range(TOP_K_FINE):
        hits.append(expert == rit[k:k + 1, :])
    chosen = jnp.where(hits[0] | hits[1], 1.0, 0.0)
    before = (jnp.dot(chosen.astype(BF16), tri_ref[...], preferred_element_type=F32)
              + carry_ref[:, 0:1])
    for k in range(TOP_K_FINE):
        meta_ref[k:k + 1, :] = rit[k:k + 1, :].astype(jnp.int32)
        rank = jnp.sum(jnp.where(hits[k], before, 0.0), axis=0, keepdims=True)
        meta_ref[TOP_K_FINE + k:TOP_K_FINE + k + 1, :] = rank.astype(jnp.int32)
    meta_ref[2 * TOP_K_FINE:, :] = jnp.zeros((meta_ref.shape[0] - 2 * TOP_K_FINE, tt), jnp.int32)
    carry_ref[...] = carry_ref[...] + jnp.sum(chosen, axis=1, keepdims=True)
    cnt_ref[...] = carry_ref[...].astype(jnp.int32)


def _rank(ri, tt=1024):
    t = ri.shape[0]
    assert TOP_K_FINE == 2
    return pl.pallas_call(
        _rank_kernel,
        grid=(t // tt,),
        in_specs=[pl.BlockSpec((tt, LANES), lambda i: (i, 0))],
        out_specs=[pl.BlockSpec((None, 8, tt), lambda i: (i, 0, 0)),
                   pl.BlockSpec((N_EXPERTS, LANES), lambda i: (0, 0))],
        out_shape=[jax.ShapeDtypeStruct((t // tt, 8, tt), jnp.int32),
                   jax.ShapeDtypeStruct((N_EXPERTS, LANES), jnp.int32)],
        scratch_shapes=[pltpu.VMEM((N_EXPERTS, LANES), F32), pltpu.VMEM((tt, tt), BF16)],
        compiler_params=pltpu.CompilerParams(
            dimension_semantics=("arbitrary",), vmem_limit_bytes=VMEM_LIMIT),
        name="route_rank",
    )(ri)


def _scatter_rows_kernel(pos_ref, end_ref, nt_ref, src_hbm, dst_hbm, zbuf, st0, st1, st2,
                         sem, lsem, zsem, *, tm, row_tile):
    i = pl.program_id(0)
    stage = (st0, st1, st2)
    n_tiles_max = dst_hbm.shape[0] // row_tile

    def zero_copy(tile):
        row0 = pl.multiple_of(tile * row_tile, row_tile)
        return pltpu.make_async_copy(zbuf, dst_hbm.at[pl.ds(row0, row_tile), :], zsem)

    @pl.when(i == 0)
    def _():
        zbuf[...] = jnp.zeros_like(zbuf)
        n_exp = end_ref.shape[0]

        def has_rows(e):
            return end_ref[e] > (end_ref[e - 1] if e else 0)

        for e in range(n_exp):
            pl.when(has_rows(e))(lambda e=e: zero_copy(end_ref[e] - 1).start())

        def start_tail(tile, c):
            zero_copy(tile).start()
            return c

        def wait_tail(tile, c):
            zero_copy(tile).wait()
            return c

        lax.fori_loop(nt_ref[0], n_tiles_max, start_tail, 0)
        for e in range(n_exp):
            pl.when(has_rows(e))(lambda: zero_copy(0).wait())
        lax.fori_loop(nt_ref[0], n_tiles_max, wait_tail, 0)

    last = pl.num_programs(0) - 1
    nbuf = len(stage)

    def load(tile, b):
        row0 = pl.multiple_of(tile * tm, tm)
        return pltpu.make_async_copy(src_hbm.at[pl.ds(row0, tm), :], stage[b], lsem.at[b])

    def wait_rows(b):
        for _ in range(TOP_K_FINE):
            pltpu.make_async_copy(stage[b], dst_hbm.at[pl.ds(0, tm), :], sem.at[b]).wait()

    @pl.when(i == 0)
    def _():
        load(0, 0).start()

    def step(b):
        nb = (b + 1) % nbuf

        @pl.when(i >= nbuf - 1)
        def _():
            wait_rows(nb)

        @pl.when(i < last)
        def _():
            load(i + 1, nb).start()

        load(i, b).wait()
        for r in range(tm):
            for k in range(TOP_K_FINE):
                p = pos_ref[(i * tm + r) * TOP_K_FINE + k]
                pltpu.make_async_copy(stage[b].at[pl.ds(r, 1), :], dst_hbm.at[pl.ds(p, 1), :],
                                      sem.at[b]).start(priority=k % 2)

        @pl.when(i == last)
        def _():
            for d in range(nbuf - 1):
                @pl.when(i >= d)
                def _(d=d):
                    wait_rows((b - d) % nbuf)

    for b in range(nbuf):
        pl.when(i % nbuf == b)(functools.partial(step, b))


def _scatter_rows(plan, src, tm=256, row_tile=ROW_TILE):
    t, w = src.shape
    any_spec = pl.BlockSpec(memory_space=pl.ANY)
    grid_spec = pltpu.PrefetchScalarGridSpec(
        num_scalar_prefetch=3,
        grid=(t // tm,),
        in_specs=[any_spec],
        out_specs=any_spec,
        scratch_shapes=[pltpu.VMEM((row_tile, w), src.dtype),
                        pltpu.VMEM((tm, w), src.dtype), pltpu.VMEM((tm, w), src.dtype),
                        pltpu.VMEM((tm, w), src.dtype),
                        pltpu.SemaphoreType.DMA((3,)), pltpu.SemaphoreType.DMA((3,)),
                        pltpu.SemaphoreType.DMA(())],
    )
    return pl.pallas_call(
        functools.partial(_scatter_rows_kernel, tm=tm, row_tile=row_tile),
        grid_spec=grid_spec,
        out_shape=jax.ShapeDtypeStruct((plan["n_rows"], w), src.dtype),
        compiler_params=pltpu.CompilerParams(dimension_semantics=("arbitrary",)),
        name="scatter_rows",
    )(plan["pos"], plan["end_tile"], plan["n_tiles"], src)


def _sort_plan(ri, tm=ROW_TILE):
    t = ri.shape[0]
    k = TOP_K_FINE
    a = t * k
    nt = a // tm + N_EXPERTS
    meta, cnt = _rank(ri)
    counts = cnt[:, 0]
    padded = ((counts + tm - 1) // tm) * tm
    off_end = jnp.cumsum(padded)
    off = off_end - padded
    ef = meta[:, 0:k, :]
    eid = jnp.arange(N_EXPERTS, dtype=jnp.int32)
    pos = jnp.sum(jnp.where(ef[..., None] == eid, off, 0), axis=-1) + meta[:, k:2 * k, :]
    pos = jnp.transpose(pos, (0, 2, 1)).reshape(a)
    n_tiles = off_end[-1] // tm
    tile_idx = jnp.arange(nt, dtype=jnp.int32)
    te = jnp.sum((off_end[None, :] <= (tile_idx * tm)[:, None]).astype(jnp.int32), axis=1)
    te = jnp.minimum(te, N_EXPERTS - 1)
    te = jnp.where(tile_idx < n_tiles, te, te[n_tiles - 1]).astype(jnp.int32)
    first = ((tile_idx == 0) | (te != jnp.roll(te, 1))) & (tile_idx < n_tiles)
    slot = (jnp.cumsum(first.astype(jnp.int32)) - 1) % 2
    eid = jnp.arange(N_EXPERTS, dtype=jnp.int32)
    later = (eid[None, :] > eid[:, None]) & (counts[None, :] > 0)
    next_e = jnp.min(jnp.where(later, eid[None, :], N_EXPERTS), axis=1)
    next_e = jnp.where(next_e < N_EXPERTS, next_e, -1)
    return dict(
        end_tile=(off_end // tm).astype(jnp.int32),
        pos=pos.astype(jnp.int32), n_rows=nt * tm, tile_expert=te,
        n_tiles=n_tiles.reshape(1).astype(jnp.int32), first=first.astype(jnp.int32),
        slot=slot.astype(jnp.int32), next_expert=next_e[te].astype(jnp.int32))


def kernel(x, norm1_g, w_in, lambda_q1, lambda_k1, lambda_q2, lambda_k2, subln_g, pool_w, pool_scale, w_out, norm2_g, w_coarse, b_coarse, w_fine, b_fine, w_gate, w_up, w_down, final_norm_g):
    b, s, d = x.shape
    t = b * s
    assert norm1_g.shape[0] == 1
    x2 = x.reshape(t, d)
    proj, v_t = _inproj(x2, norm1_g[0], w_in[0])
    proj3 = proj.reshape(b, s, proj.shape[1])
    attn = _attention(proj3, v_t, lambda_q1, lambda_k1, lambda_q2, lambda_k2, subln_g[0])
    pool = _pool(proj3, pool_w[0], pool_scale[0])

    ng, epg = N_EXPERT_GROUPS, EXPERTS_PER_GROUP
    wr = jnp.concatenate(
        [w_coarse[0], jnp.transpose(w_fine[0], (1, 0, 2)).reshape(d, ng * epg)], axis=1)
    wr = jnp.pad(wr, ((0, 0), (0, LANES - wr.shape[1])))
    br = jnp.concatenate([b_coarse[0], b_fine[0].reshape(ng * epg)])
    br = jnp.pad(br, (0, LANES - br.shape[0])).reshape(1, LANES)
    wr_hi = wr.astype(BF16)
    wr_lo = (wr - wr_hi.astype(F32)).astype(BF16)
    h, hn, ri, rw = _outproj(attn.reshape(t, ATTN_WIDTH), pool.reshape(t, POOL_WIDTH), x2,
                             w_out[0], norm2_g[0], jnp.stack([wr_hi, wr_lo]), br)

    plan = _sort_plan(ri)
    xs = _scatter_rows(plan, hn)
    ys = _experts(plan, xs, w_gate[0], w_up[0], w_down[0])
    out = _combine(plan["pos"], h, rw, ys, final_norm_g)
    return out.reshape(b, s, d)
```

```python
import functools
import math

import jax
import jax.numpy as jnp
from jax import lax
from jax.experimental import pallas as pl
from jax.experimental.pallas import tpu as pltpu

N_DIFF_HEADS = 8
DIFF_HEAD_DIM = 64
DIFF_V_DIM = 2 * DIFF_HEAD_DIM
ATTN_WIDTH = N_DIFF_HEADS * DIFF_V_DIM
POOL_WINDOWS = (2, 4, 8, 16)
POOL_GROUP_DIM = 256
POOL_WIDTH = len(POOL_WINDOWS) * POOL_GROUP_DIM
N_EXPERT_GROUPS = 4
EXPERTS_PER_GROUP = 8
N_EXPERTS = N_EXPERT_GROUPS * EXPERTS_PER_GROUP
TOP_K_FINE = 2
RMS_EPS = 1e-6
NEG_INF = -1e30
LAM_INIT = 0.8 - 0.6 * math.exp(-0.3 * 0)
LOG2E = math.log2(math.e)
Q_SCALE = DIFF_HEAD_DIM ** -0.5 * LOG2E

LANES = 128
SUBLANES = 8
VMEM_LIMIT = 56 * 1024 * 1024
ROW_TILE = 256

F32 = jnp.float32
BF16 = jnp.bfloat16


def _rms(x, g):
    return x * lax.rsqrt(jnp.mean(x * x, axis=-1, keepdims=True) + RMS_EPS) * g


def _load_weight_bf16(w_hbm, wb_ref, stage, sem):
    ck = stage.shape[1]
    nchunk = w_hbm.shape[0] // ck

    def copy(c):
        return pltpu.make_async_copy(w_hbm.at[c * ck:(c + 1) * ck, :], stage.at[c % 2], sem.at[c % 2])

    copy(0).start()
    for c in range(nchunk):
        if c + 1 < nchunk:
            copy(c + 1).start()
        copy(c).wait()
        wb_ref[c * ck:(c + 1) * ck, :] = stage[c % 2].astype(BF16)


def _pool_mix(u, prev, seq_pos, pw_ref, ps_ref):
    halo = prev.shape[0]
    c = POOL_GROUP_DIM
    ext = jnp.concatenate([prev, u], axis=0)
    outs = []
    for gi, win in enumerate(POOL_WINDOWS):
        assert win & (win - 1) == 0 and win <= halo
        run = ext[:, gi * c:(gi + 1) * c]
        span = 1
        while span < win:
            run = run + pltpu.roll(run, span, axis=0)
            span *= 2
        count = jnp.minimum(seq_pos + 1, win).astype(F32)
        pooled = run[halo:, :] / count - u[:, gi * c:(gi + 1) * c]
        mixed = jnp.dot(pooled.astype(BF16), pw_ref[gi].astype(BF16), preferred_element_type=F32)
        outs.append(mixed * ps_ref[:, gi * c:(gi + 1) * c])
    return outs


def _inproj_kernel(x_ref, g_ref, w_hbm, pw_ref, ps_ref, o_ref, pool_ref, vt_ref,
                   wb_ref, stage, sem, halo_ref, *, tn, seq_tiles):
    i = pl.program_id(0)
    tm = x_ref.shape[0]

    @pl.when(i == 0)
    def _():
        halo_ref[...] = jnp.zeros_like(halo_ref)
        _load_weight_bf16(w_hbm, wb_ref, stage, sem)

    xn = _rms(x_ref[...], g_ref[...]).astype(BF16)
    q = jnp.dot(xn, wb_ref[:, 0 * tn:1 * tn], preferred_element_type=F32) * Q_SCALE
    o_ref[:, 0 * tn:1 * tn] = q.astype(o_ref.dtype)
    k = jnp.dot(xn, wb_ref[:, 1 * tn:2 * tn], preferred_element_type=F32)
    o_ref[:, 1 * tn:2 * tn] = k.astype(o_ref.dtype)
    v = jnp.dot(xn, wb_ref[:, 2 * tn:3 * tn], preferred_element_type=F32)
    vt_ref[...] = v.T.astype(vt_ref.dtype)
    u = jnp.dot(xn, wb_ref[:, 3 * tn:4 * tn], preferred_element_type=F32)

    tile_in_seq = i % seq_tiles
    prev = jnp.where(tile_in_seq == 0, 0.0, halo_ref[...])
    halo_ref[...] = u[tm - halo_ref.shape[0]:, :]
    seq_pos = tile_in_seq * tm + lax.broadcasted_iota(jnp.int32, (tm, 1), 0)
    c = POOL_GROUP_DIM
    for gi, mixed in enumerate(_pool_mix(u, prev, seq_pos, pw_ref, ps_ref)):
        pool_ref[:, gi * c:(gi + 1) * c] = mixed.astype(pool_ref.dtype)


def _inproj(x2, g, w, pool_w, pool_scale, seq_len, tm=512, tn=ATTN_WIDTH, chunk_rows=256):
    t, d = x2.shape
    n = w.shape[1]
    assert n == 4 * tn and tn == POOL_WIDTH and seq_len % tm == 0
    halo = 2 * SUBLANES
    assert max(POOL_WINDOWS) <= halo
    return pl.pallas_call(
        functools.partial(_inproj_kernel, tn=tn, seq_tiles=seq_len // tm),
        grid=(t // tm,),
        in_specs=[
            pl.BlockSpec((tm, d), lambda i: (i, 0)),
            pl.BlockSpec((1, d), lambda i: (0, 0)),
            pl.BlockSpec(memory_space=pl.ANY),
            pl.BlockSpec(pool_w.shape, lambda i: (0, 0, 0)),
            pl.BlockSpec((1, tn), lambda i: (0, 0)),
        ],
        out_specs=[pl.BlockSpec((tm, 2 * tn), lambda i: (i, 0)),
                   pl.BlockSpec((tm, tn), lambda i: (i, 0)),
                   pl.BlockSpec((tn, tm), lambda i: (0, i))],
        out_shape=[jax.ShapeDtypeStruct((t, 2 * tn), BF16), jax.ShapeDtypeStruct((t, tn), BF16),
                   jax.ShapeDtypeStruct((tn, t), BF16)],
        scratch_shapes=[pltpu.VMEM((d, n), BF16), pltpu.VMEM((2, chunk_rows, n), F32),
                        pltpu.SemaphoreType.DMA((2,)), pltpu.VMEM((halo, tn), F32)],
        compiler_params=pltpu.CompilerParams(
            dimension_semantics=("arbitrary",), vmem_limit_bytes=VMEM_LIMIT),
        name="inproj",
    )(x2, g.reshape(1, d), w, pool_w, pool_scale.reshape(1, tn))


def _attn_kernel(lq1_ref, lk1_ref, lq2_ref, lk2_ref, q_ref, k_ref, v_ref, g_ref, o_ref,
                 tab_ref, vt_ref, *, tq):
    h = pl.program_id(0)
    s_len = q_ref.shape[0]
    nq = s_len // tq
    d, dv = DIFF_HEAD_DIM, DIFF_V_DIM

    @pl.when(pl.program_id(1) == 0)
    def _():
        slope = jnp.exp2(jnp.full((1, 1), -8.0 / N_DIFF_HEADS, F32) * (h + 1).astype(F32))
        rel = ((nq - 1) * tq + lax.broadcasted_iota(jnp.int32, (s_len, tq), 1)
               - lax.broadcasted_iota(jnp.int32, (s_len, tq), 0))
        tab_ref[...] = jnp.where(rel >= 0, (-LOG2E * slope) * rel.astype(F32), NEG_INF)

    vt_ref[:dv, :] = v_ref[...]
    vt_ref[dv:, :] = jnp.ones((vt_ref.shape[0] - dv, s_len), BF16)

    lam = (jnp.exp(jnp.sum(lq1_ref[...] * lk1_ref[...], axis=1, keepdims=True))
           - jnp.exp(jnp.sum(lq2_ref[...] * lk2_ref[...], axis=1, keepdims=True)) + LAM_INIT)
    dn = (((1,), (1,)), ((), ()))
    lane = lax.broadcasted_iota(jnp.int32, (tq, 2 * d), 1)

    def scores(qi):
        n = (qi + 1) * tq
        q = q_ref[qi * tq:(qi + 1) * tq, :]
        zero = jnp.zeros_like(q)
        qh = (jnp.where(lane < d, q, zero), jnp.where(lane >= d, q, zero))
        kk = k_ref[:n, :]
        bias = tab_ref[(nq - 1 - qi) * tq:(nq - 1 - qi) * tq + n, :]
        return [lax.dot_general(kk, qh[a], dn, preferred_element_type=F32) + bias
                for a in range(2)]

    ahead = 1
    pending = [scores(t) for t in range(min(ahead, nq))]
    for qi in range(nq):
        n = (qi + 1) * tq
        s_cur = pending.pop(0)
        if qi + ahead < nq:
            pending.append(scores(qi + ahead))
        outs = []
        for a in range(2):
            s = s_cur[a]
            m = jnp.max(s, axis=0, keepdims=True)
            p = jnp.exp2(s - m).astype(BF16)
            acc = jnp.dot(vt_ref[:, :n], p, preferred_element_type=F32)
            outs.append(acc[:dv] / acc[dv:dv + 1])
        ot = outs[0] - lam * outs[1]
        yt = ot * lax.rsqrt(jnp.mean(ot * ot, axis=0, keepdims=True) + RMS_EPS) * g_ref[...]
        o_ref[qi * tq:(qi + 1) * tq, :] = (yt * (1.0 - LAM_INIT)).T.astype(o_ref.dtype)


def _attention(proj3, v_t, lq1, lk1, lq2, lk2, subln_g, tq=256):
    b, s, _ = proj3.shape
    dv = DIFF_V_DIM
    nh = N_DIFF_HEADS
    ones_rows = 16
    lam_spec = pl.BlockSpec((1, DIFF_HEAD_DIM), lambda hi, bi: (0, 0))
    return pl.pallas_call(
        functools.partial(_attn_kernel, tq=tq),
        grid=(nh, b),
        in_specs=[
            lam_spec, lam_spec, lam_spec, lam_spec,
            pl.BlockSpec((None, s, dv), lambda hi, bi: (bi, 0, hi)),
            pl.BlockSpec((None, s, dv), lambda hi, bi: (bi, 0, nh + hi)),
            pl.BlockSpec((dv, s), lambda hi, bi: (hi, bi)),
            pl.BlockSpec((dv, 1), lambda hi, bi: (0, 0)),
        ],
        out_specs=pl.BlockSpec((None, s, dv), lambda hi, bi: (bi, 0, hi)),
        out_shape=jax.ShapeDtypeStruct((b, s, ATTN_WIDTH), BF16),
        scratch_shapes=[
            pltpu.VMEM((s, tq), F32),
            pltpu.VMEM((dv + ones_rows, s), BF16),
        ],
        compiler_params=pltpu.CompilerParams(
            dimension_semantics=("arbitrary", "arbitrary"),
            vmem_limit_bytes=VMEM_LIMIT),
        name="diff_attn",
    )(lq1, lk1, lq2, lk2, proj3, proj3, v_t, subln_g.reshape(dv, 1))


def _pack_bf16_pair(x):
    bits = lax.bitcast_convert_type(x.astype(BF16).astype(F32), jnp.uint32)
    half = bits.shape[1] // 2
    return bits[:, :half] | (bits[:, half:] >> 16)


def _outproj_kernel(a_ref, p_ref, x_ref, w_hbm, g_ref, wr_ref, br_ref,
                    h_ref, hn_ref, ri_ref, rw_ref, w_ref, stage, sem, *, n_sub):
    @pl.when(pl.program_id(0) == 0)
    def _():
        _load_weight_bf16(w_hbm, w_ref, stage, sem)

    ka = a_ref.shape[1]
    hs = x_ref.shape[0] // n_sub
    normed = []
    for r in range(n_sub):
        rows = pl.ds(r * hs, hs)
        mixed = (jnp.dot(a_ref[rows, :], w_ref[:ka, :], preferred_element_type=F32)
                 + jnp.dot(p_ref[rows, :], w_ref[ka:, :], preferred_element_type=F32))
        h = x_ref[rows, :] + mixed
        h_ref[rows, :] = h
        hn = _rms(h, g_ref[...])
        hn_ref[rows, :] = _pack_bf16_pair(hn)
        hn_hi = hn.astype(BF16)
        normed.append((hn_hi, (hn - hn_hi.astype(F32)).astype(BF16)))
    for r in range(n_sub):
        rows = pl.ds(r * hs, hs)
        ri, rw = _route(normed[r][0], normed[r][1], wr_ref, br_ref)
        ri_ref[rows, :] = ri
        rw_ref[rows, :] = rw


def _route(hn_hi, hn_lo, wr_ref, br_ref):
    wr_hi = wr_ref[0]
    logits = (jnp.dot(hn_hi, wr_hi, preferred_element_type=F32)
              + jnp.dot(hn_lo, wr_hi, preferred_element_type=F32)
              + jnp.dot(hn_hi, wr_ref[1], preferred_element_type=F32)) + br_ref[...]
    ng, epg = N_EXPERT_GROUPS, EXPERTS_PER_GROUP
    lane = lax.broadcasted_iota(jnp.int32, logits.shape, 1)
    big = jnp.int32(LANES)
    low = jnp.float32(-3.0e38)
    cm = lane < ng
    c = jnp.where(cm, logits, low)
    cmax = jnp.max(c, axis=1, keepdims=True)
    gsel = jnp.min(jnp.where(c == cmax, lane, big), axis=1, keepdims=True)
    p_group = 1.0 / jnp.sum(jnp.where(cm, jnp.exp(c - cmax), 0.0), axis=1, keepdims=True)
    f_lo = ng + epg * gsel
    fm = (lane >= f_lo) & (lane < f_lo + epg)
    f = jnp.where(fm, logits, low)
    v1 = jnp.max(f, axis=1, keepdims=True)
    i1 = jnp.min(jnp.where(fm & (f == v1), lane, big), axis=1, keepdims=True)
    fm2 = fm & (lane != i1)
    f2 = jnp.where(fm2, logits, low)
    v2 = jnp.max(f2, axis=1, keepdims=True)
    i2 = jnp.min(jnp.where(fm2 & (f2 == v2), lane, big), axis=1, keepdims=True)
    e21 = jnp.exp(v2 - v1)
    w1 = p_group / (1.0 + e21)
    w2 = p_group * e21 / (1.0 + e21)
    return (jnp.where(lane == 0, i1 - ng, jnp.where(lane == 1, i2 - ng, 0)),
            jnp.where(lane == 0, w1, jnp.where(lane == 1, w2, 0.0)))


def _outproj(attn2, pool2, x2, w_out, g2, wr, br, tm=512, n_sub=2, chunk_rows=512):
    t, d = x2.shape
    ka, kp = attn2.shape[1], pool2.shape[1]
    row = lambda i: (i, 0)
    const = lambda i: (0, 0)
    return pl.pallas_call(
        functools.partial(_outproj_kernel, n_sub=n_sub),
        grid=(t // tm,),
        in_specs=[
            pl.BlockSpec((tm, ka), row),
            pl.BlockSpec((tm, kp), row),
            pl.BlockSpec((tm, d), row),
            pl.BlockSpec(memory_space=pl.ANY),
            pl.BlockSpec((1, d), const),
            pl.BlockSpec((2, d, LANES), lambda i: (0, 0, 0)),
            pl.BlockSpec((1, LANES), const),
        ],
        out_specs=[
            pl.BlockSpec((tm, d), row),
            pl.BlockSpec((tm, d // 2), row),
            pl.BlockSpec((tm, LANES), row),
            pl.BlockSpec((tm, LANES), row),
        ],
        out_shape=[
            jax.ShapeDtypeStruct((t, d), F32),
            jax.ShapeDtypeStruct((t, d // 2), jnp.uint32),
            jax.ShapeDtypeStruct((t, LANES), jnp.int32),
            jax.ShapeDtypeStruct((t, LANES), F32),
        ],
        scratch_shapes=[pltpu.VMEM((ka + kp, d), BF16), pltpu.VMEM((2, chunk_rows, d), F32),
                        pltpu.SemaphoreType.DMA((2,))],
        compiler_params=pltpu.CompilerParams(
            dimension_semantics=("arbitrary",), vmem_limit_bytes=VMEM_LIMIT),
        name="outproj_router",
    )(attn2, pool2, x2, w_out, g2.reshape(1, d), wr, br)


def _unpack_bf16_pair(words):
    hi = lax.bitcast_convert_type(words & jnp.uint32(0xFFFF0000), F32).astype(BF16)
    lo = lax.bitcast_convert_type(words << 16, F32).astype(BF16)
    return hi, lo


def _expert_kernel(te_ref, nt_ref, first_ref, slot_ref, nxt_ref,
                   x_ref, wg_hbm, wu_hbm, wd_hbm, y_ref,
                   wgb, wub, wdb, wsem):
    j = pl.program_id(0)
    nt = nt_ref[0]
    kh = x_ref.shape[1]
    fh = wdb.shape[1] // 2

    def weight_copies(e, s):
        return ((pltpu.make_async_copy(wg_hbm.at[e], wgb.at[s], wsem.at[s]), 0),
                (pltpu.make_async_copy(wu_hbm.at[e], wub.at[s], wsem.at[s]), 1),
                (pltpu.make_async_copy(wd_hbm.at[e, :fh, :], wdb.at[s, :fh, :], wsem.at[s]), 0),
                (pltpu.make_async_copy(wd_hbm.at[e, fh:, :], wdb.at[s, fh:, :], wsem.at[s]), 1))

    @pl.when(j == 0)
    def _():
        for c, prio in weight_copies(te_ref[0], 0):
            c.start(priority=prio)

    @pl.when(j < nt)
    def _():
        ws = slot_ref[j]

        @pl.when(first_ref[j] == 1)
        def _():
            for c, _ in weight_copies(te_ref[j], ws):
                c.wait()

            @pl.when(nxt_ref[j] >= 0)
            def _():
                for c, prio in weight_copies(nxt_ref[j], 1 - ws):
                    c.start(priority=prio)

        xa, xb = _unpack_bf16_pair(x_ref[...])
        hg = (jnp.dot(xa, wgb[ws, :kh, :].astype(BF16), preferred_element_type=F32)
              + jnp.dot(xb, wgb[ws, kh:, :].astype(BF16), preferred_element_type=F32))
        hu = (jnp.dot(xa, wub[ws, :kh, :].astype(BF16), preferred_element_type=F32)
              + jnp.dot(xb, wub[ws, kh:, :].astype(BF16), preferred_element_type=F32))
        act = (hg / (1.0 + jnp.exp(-hg))) * hu
        y_ref[...] = _pack_bf16_pair(
            jnp.dot(act.astype(BF16), wdb[ws].astype(BF16), preferred_element_type=F32))

    @pl.when(j >= nt)
    def _():
        y_ref[...] = jnp.zeros_like(y_ref)


def _experts(plan, xs, w_gate, w_up, w_down, tm=ROW_TILE):
    d, f = w_gate.shape[1], w_gate.shape[2]
    assert xs.shape[1] * 2 == d
    nt = plan["tile_expert"].shape[0]
    any_spec = pl.BlockSpec(memory_space=pl.ANY)
    grid_spec = pltpu.PrefetchScalarGridSpec(
        num_scalar_prefetch=5,
        grid=(nt,),
        in_specs=[
            pl.BlockSpec((tm, d // 2), lambda j, te, n, *_: (jnp.minimum(j, n[0] - 1), 0)),
            any_spec, any_spec, any_spec],
        out_specs=pl.BlockSpec((tm, d // 2), lambda j, *_: (j, 0)),
        scratch_shapes=[
            pltpu.VMEM((2, d, f), F32),
            pltpu.VMEM((2, d, f), F32),
            pltpu.VMEM((2, f, d), F32),
            pltpu.SemaphoreType.DMA((2,)),
        ],
    )
    return pl.pallas_call(
        _expert_kernel,
        grid_spec=grid_spec,
        out_shape=jax.ShapeDtypeStruct((nt * tm, d // 2), jnp.uint32),
        compiler_params=pltpu.CompilerParams(
            dimension_semantics=("arbitrary",), vmem_limit_bytes=VMEM_LIMIT),
        name="experts",
    )(plan["tile_expert"], plan["n_tiles"], plan["first"], plan["slot"], plan["next_expert"],
      xs, w_gate, w_up, w_down)


def _combine_kernel(pos_ref, h_ref, rw_ref, ys_hbm, g_ref, o_ref, ybuf0, ybuf1, ybuf2, sem):
    i = pl.program_id(0)
    last = pl.num_programs(0) - 1
    tm = h_ref.shape[0]
    kh = ybuf0.shape[2]
    bufs = (ybuf0, ybuf1, ybuf2)
    nbuf = len(bufs)

    def start_gather(tile, buf, sm):
        for r in range(tm):
            for k in range(TOP_K_FINE):
                p = pos_ref[(tile * tm + r) * TOP_K_FINE + k]
                pltpu.make_async_copy(ys_hbm.at[pl.ds(p, 1), :], buf.at[k, pl.ds(r, 1), :],
                                      sm).start(priority=k % 2)

    def wait_gather(buf, sm):
        for k in range(TOP_K_FINE):
            pltpu.make_async_copy(ys_hbm.at[pl.ds(0, tm), :], buf.at[k], sm).wait()

    @pl.when(i == 0)
    def _():
        start_gather(0, ybuf0, sem.at[0])
        start_gather(1, ybuf1, sem.at[1])

    def step(par):
        cur = bufs[par]
        ahead = (par + nbuf - 1) % nbuf
        wait_gather(cur, sem.at[par])
        start_gather(jnp.minimum(i + nbuf - 1, last), bufs[ahead], sem.at[ahead])
        w = rw_ref[...]
        halves = []
        for part in range(2):
            y = None
            for k in range(TOP_K_FINE):
                words = cur[k]
                bits = (words & jnp.uint32(0xFFFF0000)) if part == 0 else (words << 16)
                term = w[:, k:k + 1] * lax.bitcast_convert_type(bits, F32)
                y = term if y is None else y + term
            halves.append(h_ref[:, part * kh:(part + 1) * kh] + y)
        ms = sum(jnp.sum(v * v, axis=-1, keepdims=True) for v in halves) / (2 * kh)
        inv = lax.rsqrt(ms + RMS_EPS)
        for part in range(2):
            o_ref[:, part * kh:(part + 1) * kh] = (
                halves[part] * inv * g_ref[:, part * kh:(part + 1) * kh])

        @pl.when(i == last)
        def _():
            for other in range(nbuf):
                if other != par:
                    wait_gather(bufs[other], sem.at[other])

    for par in range(nbuf):
        pl.when(i % nbuf == par)(functools.partial(step, par))


def _combine(pos, h, rw, ys, g, tm=256):
    t, d = h.shape
    grid_spec = pltpu.PrefetchScalarGridSpec(
        num_scalar_prefetch=1,
        grid=(t // tm,),
        in_specs=[
            pl.BlockSpec((tm, d), lambda i, p: (i, 0)),
            pl.BlockSpec((tm, LANES), lambda i, p: (i, 0)),
            pl.BlockSpec(memory_space=pl.ANY),
            pl.BlockSpec((1, d), lambda i, p: (0, 0)),
        ],
        out_specs=pl.BlockSpec((tm, d), lambda i, p: (i, 0)),
        scratch_shapes=[pltpu.VMEM((TOP_K_FINE, tm, d // 2), jnp.uint32),
                        pltpu.VMEM((TOP_K_FINE, tm, d // 2), jnp.uint32),
                        pltpu.VMEM((TOP_K_FINE, tm, d // 2), jnp.uint32),
                        pltpu.SemaphoreType.DMA((3,))],
    )
    return pl.pallas_call(
        _combine_kernel,
        grid_spec=grid_spec,
        out_shape=jax.ShapeDtypeStruct((t, d), F32),
        compiler_params=pltpu.CompilerParams(
            dimension_semantics=("arbitrary",), vmem_limit_bytes=VMEM_LIMIT),
        name="combine",
    )(pos, h, rw, ys, g.reshape(1, d))


def _rank_kernel(ri_ref, meta_ref, cnt_ref, carry_ref, tri_ref):
    tt = ri_ref.shape[0]
    ne = cnt_ref.shape[0]

    @pl.when(pl.program_id(0) == 0)
    def _():
        carry_ref[...] = jnp.zeros_like(carry_ref)
        earlier = (lax.broadcasted_iota(jnp.int32, (tt, tt), 0)
                   < lax.broadcasted_iota(jnp.int32, (tt, tt), 1))
        tri_ref[...] = jnp.where(earlier, 1.0, 0.0).astype(BF16)

    rit = ri_ref[...].astype(F32).T
    expert = lax.broadcasted_iota(jnp.int32, (ne, tt), 0).astype(F32)
    ranks = []
    hits = []
    for k in range(TOP_K_FINE):
        hits.append(expert == rit[k:k + 1, :])
    chosen = jnp.where(hits[0] | hits[1], 1.0, 0.0)
    before = (jnp.dot(chosen.astype(BF16), tri_ref[...], preferred_element_type=F32)
              + carry_ref[:, 0:1])
    for k in range(TOP_K_FINE):
        meta_ref[k:k + 1, :] = rit[k:k + 1, :].astype(jnp.int32)
        rank = jnp.sum(jnp.where(hits[k], before, 0.0), axis=0, keepdims=True)
        meta_ref[TOP_K_FINE + k:TOP_K_FINE + k + 1, :] = rank.astype(jnp.int32)
    meta_ref[2 * TOP_K_FINE:, :] = jnp.zeros((meta_ref.shape[0] - 2 * TOP_K_FINE, tt), jnp.int32)
    carry_ref[...] = carry_ref[...] + jnp.sum(chosen, axis=1, keepdims=True)
    cnt_ref[...] = carry_ref[...].astype(jnp.int32)


def _rank(ri, tt=1024):
    t = ri.shape[0]
    assert TOP_K_FINE == 2
    return pl.pallas_call(
        _rank_kernel,
        grid=(t // tt,),
        in_specs=[pl.BlockSpec((tt, LANES), lambda i: (i, 0))],
        out_specs=[pl.BlockSpec((None, 8, tt), lambda i: (i, 0, 0)),
                   pl.BlockSpec((N_EXPERTS, LANES), lambda i: (0, 0))],
        out_shape=[jax.ShapeDtypeStruct((t // tt, 8, tt), jnp.int32),
                   jax.ShapeDtypeStruct((N_EXPERTS, LANES), jnp.int32)],
        scratch_shapes=[pltpu.VMEM((N_EXPERTS, LANES), F32), pltpu.VMEM((tt, tt), BF16)],
        compiler_params=pltpu.CompilerParams(
            dimension_semantics=("arbitrary",), vmem_limit_bytes=VMEM_LIMIT),
        name="route_rank",
    )(ri)


def _scatter_rows_kernel(pos_ref, end_ref, nt_ref, src_hbm, dst_hbm, zbuf, st0, st1, st2,
                         sem, lsem, zsem, *, tm, row_tile):
    i = pl.program_id(0)
    stage = (st0, st1, st2)
    n_tiles_max = dst_hbm.shape[0] // row_tile

    def zero_copy(tile):
        row0 = pl.multiple_of(tile * row_tile, row_tile)
        return pltpu.make_async_copy(zbuf, dst_hbm.at[pl.ds(row0, row_tile), :], zsem)

    @pl.when(i == 0)
    def _():
        zbuf[...] = jnp.zeros_like(zbuf)
        n_exp = end_ref.shape[0]

        def has_rows(e):
            return end_ref[e] > (end_ref[e - 1] if e else 0)

        for e in range(n_exp):
            pl.when(has_rows(e))(lambda e=e: zero_copy(end_ref[e] - 1).start())

        def start_tail(tile, c):
            zero_copy(tile).start()
            return c

        def wait_tail(tile, c):
            zero_copy(tile).wait()
            return c

        lax.fori_loop(nt_ref[0], n_tiles_max, start_tail, 0)
        for e in range(n_exp):
            pl.when(has_rows(e))(lambda: zero_copy(0).wait())
        lax.fori_loop(nt_ref[0], n_tiles_max, wait_tail, 0)

    last = pl.num_programs(0) - 1
    nbuf = len(stage)

    def load(tile, b):
        row0 = pl.multiple_of(tile * tm, tm)
        return pltpu.make_async_copy(src_hbm.at[pl.ds(row0, tm), :], stage[b], lsem.at[b])

    def wait_rows(b):
        for _ in range(TOP_K_FINE):
            pltpu.make_async_copy(stage[b], dst_hbm.at[pl.ds(0, tm), :], sem.at[b]).wait()

    @pl.when(i == 0)
    def _():
        load(0, 0).start()

    def step(b):
        nb = (b + 1) % nbuf

        @pl.when(i >= nbuf - 1)
        def _():
            wait_rows(nb)

        @pl.when(i < last)
        def _():
            load(i + 1, nb).start()

        load(i, b).wait()
        for r in range(tm):
            for k in range(TOP_K_FINE):
                p = pos_ref[(i * tm + r) * TOP_K_FINE + k]
                pltpu.make_async_copy(stage[b].at[pl.ds(r, 1), :], dst_hbm.at[pl.ds(p, 1), :],
                                      sem.at[b]).start(priority=k % 2)

        @pl.when(i == last)
        def _():
            for d in range(nbuf - 1):
                @pl.when(i >= d)
                def _(d=d):
                    wait_rows((b - d) % nbuf)

    for b in range(nbuf):
        pl.when(i % nbuf == b)(functools.partial(step, b))


def _scatter_rows(plan, src, tm=256, row_tile=ROW_TILE):
    t, w = src.shape
    any_spec = pl.BlockSpec(memory_space=pl.ANY)
    grid_spec = pltpu.PrefetchScalarGridSpec(
        num_scalar_prefetch=3,
        grid=(t // tm,),
        in_specs=[any_spec],
        out_specs=any_spec,
        scratch_shapes=[pltpu.VMEM((row_tile, w), src.dtype),
                        pltpu.VMEM((tm, w), src.dtype), pltpu.VMEM((tm, w), src.dtype),
                        pltpu.VMEM((tm, w), src.dtype),
                        pltpu.SemaphoreType.DMA((3,)), pltpu.SemaphoreType.DMA((3,)),
                        pltpu.SemaphoreType.DMA(())],
    )
    return pl.pallas_call(
        functools.partial(_scatter_rows_kernel, tm=tm, row_tile=row_tile),
        grid_spec=grid_spec,
        out_shape=jax.ShapeDtypeStruct((plan["n_rows"], w), src.dtype),
        compiler_params=pltpu.CompilerParams(dimension_semantics=("arbitrary",)),
        name="scatter_rows",
    )(plan["pos"], plan["end_tile"], plan["n_tiles"], src)


def _sort_plan(ri, tm=ROW_TILE):
    t = ri.shape[0]
    k = TOP_K_FINE
    a = t * k
    nt = a // tm + N_EXPERTS
    meta, cnt = _rank(ri)
    counts = cnt[:, 0]
    padded = ((counts + tm - 1) // tm) * tm
    off_end = jnp.cumsum(padded)
    off = off_end - padded
    ef = meta[:, 0:k, :]
    eid = jnp.arange(N_EXPERTS, dtype=jnp.int32)
    pos = jnp.sum(jnp.where(ef[..., None] == eid, off, 0), axis=-1) + meta[:, k:2 * k, :]
    pos = jnp.transpose(pos, (0, 2, 1)).reshape(a)
    n_tiles = off_end[-1] // tm
    tile_idx = jnp.arange(nt, dtype=jnp.int32)
    te = jnp.sum((off_end[None, :] <= (tile_idx * tm)[:, None]).astype(jnp.int32), axis=1)
    te = jnp.minimum(te, N_EXPERTS - 1)
    te = jnp.where(tile_idx < n_tiles, te, te[n_tiles - 1]).astype(jnp.int32)
    first = ((tile_idx == 0) | (te != jnp.roll(te, 1))) & (tile_idx < n_tiles)
    slot = (jnp.cumsum(first.astype(jnp.int32)) - 1) % 2
    eid = jnp.arange(N_EXPERTS, dtype=jnp.int32)
    later = (eid[None, :] > eid[:, None]) & (counts[None, :] > 0)
    next_e = jnp.min(jnp.where(later, eid[None, :], N_EXPERTS), axis=1)
    next_e = jnp.where(next_e < N_EXPERTS, next_e, -1)
    return dict(
        end_tile=(off_end // tm).astype(jnp.int32),
        pos=pos.astype(jnp.int32), n_rows=nt * tm, tile_expert=te,
        n_tiles=n_tiles.reshape(1).astype(jnp.int32), first=first.astype(jnp.int32),
        slot=slot.astype(jnp.int32), next_expert=next_e[te].astype(jnp.int32))


def kernel(x, norm1_g, w_in, lambda_q1, lambda_k1, lambda_q2, lambda_k2, subln_g, pool_w, pool_scale, w_out, norm2_g, w_coarse, b_coarse, w_fine, b_fine, w_gate, w_up, w_down, final_norm_g):
    b, s, d = x.shape
    t = b * s
    assert norm1_g.shape[0] == 1
    x2 = x.reshape(t, d)
    proj, pool, v_t = _inproj(x2, norm1_g[0], w_in[0], pool_w[0], pool_scale[0], s)
    proj3 = proj.reshape(b, s, proj.shape[1])
    attn = _attention(proj3, v_t, lambda_q1, lambda_k1, lambda_q2, lambda_k2, subln_g[0])

    ng, epg = N_EXPERT_GROUPS, EXPERTS_PER_GROUP
    wr = jnp.concatenate(
        [w_coarse[0], jnp.transpose(w_fine[0], (1, 0, 2)).reshape(d, ng * epg)], axis=1)
    wr = jnp.pad(wr, ((0, 0), (0, LANES - wr.shape[1])))
    br = jnp.concatenate([b_coarse[0], b_fine[0].reshape(ng * epg)])
    br = jnp.pad(br, (0, LANES - br.shape[0])).reshape(1, LANES)
    wr_hi = wr.astype(BF16)
    wr_lo = (wr - wr_hi.astype(F32)).astype(BF16)
    h, hn, ri, rw = _outproj(attn.reshape(t, ATTN_WIDTH), pool, x2,
                             w_out[0], norm2_g[0], jnp.stack([wr_hi, wr_lo]), br)

    plan = _sort_plan(ri)
    xs = _scatter_rows(plan, hn)
    ys = _experts(plan, xs, w_gate[0], w_up[0], w_down[0])
    out = _combine(plan["pos"], h, rw, ys, final_norm_g)
    return out.reshape(b, s, d)
```

```python
import functools
import math

import jax
import jax.numpy as jnp
from jax import lax
from jax.experimental import pallas as pl
from jax.experimental.pallas import tpu as pltpu

N_DIFF_HEADS = 8
DIFF_HEAD_DIM = 64
DIFF_V_DIM = 2 * DIFF_HEAD_DIM
ATTN_WIDTH = N_DIFF_HEADS * DIFF_V_DIM
POOL_WINDOWS = (2, 4, 8, 16)
POOL_GROUP_DIM = 256
POOL_WIDTH = len(POOL_WINDOWS) * POOL_GROUP_DIM
N_EXPERT_GROUPS = 4
EXPERTS_PER_GROUP = 8
N_EXPERTS = N_EXPERT_GROUPS * EXPERTS_PER_GROUP
TOP_K_FINE = 2
RMS_EPS = 1e-6
NEG_INF = -1e30
LAM_INIT = 0.8 - 0.6 * math.exp(-0.3 * 0)
LOG2E = math.log2(math.e)
Q_SCALE = DIFF_HEAD_DIM ** -0.5 * LOG2E

LANES = 128
SUBLANES = 8
VMEM_LIMIT = 56 * 1024 * 1024
ROW_TILE = 256

F32 = jnp.float32
BF16 = jnp.bfloat16


def _rms(x, g):
    return x * lax.rsqrt(jnp.mean(x * x, axis=-1, keepdims=True) + RMS_EPS) * g


def _load_weight_bf16(w_hbm, wb_ref, stage, sem):
    ck = stage.shape[1]
    nchunk = w_hbm.shape[0] // ck

    def copy(c):
        return pltpu.make_async_copy(w_hbm.at[c * ck:(c + 1) * ck, :], stage.at[c % 2], sem.at[c % 2])

    copy(0).start()
    for c in range(nchunk):
        if c + 1 < nchunk:
            copy(c + 1).start()
        copy(c).wait()
        wb_ref[c * ck:(c + 1) * ck, :] = stage[c % 2].astype(BF16)


def _pool_mix(u, prev, seq_pos, pw_ref, ps_ref):
    halo = prev.shape[0]
    c = POOL_GROUP_DIM
    ext = jnp.concatenate([prev, u], axis=0)
    outs = []
    for gi, win in enumerate(POOL_WINDOWS):
        assert win & (win - 1) == 0 and win <= halo
        run = ext[:, gi * c:(gi + 1) * c]
        span = 1
        while span < win:
            run = run + pltpu.roll(run, span, axis=0)
            span *= 2
        count = jnp.minimum(seq_pos + 1, win).astype(F32)
        pooled = run[halo:, :] / count - u[:, gi * c:(gi + 1) * c]
        mixed = jnp.dot(pooled.astype(BF16), pw_ref[gi].astype(BF16), preferred_element_type=F32)
        outs.append(mixed * ps_ref[:, gi * c:(gi + 1) * c])
    return outs


def _inproj_kernel(x_ref, g_ref, w_hbm, pw_ref, ps_ref, o_ref, pool_ref, vt_ref,
                   wb_ref, stage, sem, halo_ref, *, tn, seq_tiles):
    i = pl.program_id(0)
    tm = x_ref.shape[0]

    @pl.when(i == 0)
    def _():
        halo_ref[...] = jnp.zeros_like(halo_ref)
        _load_weight_bf16(w_hbm, wb_ref, stage, sem)

    xn = _rms(x_ref[...], g_ref[...]).astype(BF16)
    q = jnp.dot(xn, wb_ref[:, 0 * tn:1 * tn], preferred_element_type=F32) * Q_SCALE
    o_ref[:, 0 * tn:1 * tn] = q.astype(o_ref.dtype)
    k = jnp.dot(xn, wb_ref[:, 1 * tn:2 * tn], preferred_element_type=F32)
    o_ref[:, 1 * tn:2 * tn] = k.astype(o_ref.dtype)
    v = jnp.dot(xn, wb_ref[:, 2 * tn:3 * tn], preferred_element_type=F32)
    vt_ref[...] = v.T.astype(vt_ref.dtype)
    u = jnp.dot(xn, wb_ref[:, 3 * tn:4 * tn], preferred_element_type=F32)

    tile_in_seq = i % seq_tiles
    prev = jnp.where(tile_in_seq == 0, 0.0, halo_ref[...])
    halo_ref[...] = u[tm - halo_ref.shape[0]:, :]
    seq_pos = tile_in_seq * tm + lax.broadcasted_iota(jnp.int32, (tm, 1), 0)
    c = POOL_GROUP_DIM
    for gi, mixed in enumerate(_pool_mix(u, prev, seq_pos, pw_ref, ps_ref)):
        pool_ref[:, gi * c:(gi + 1) * c] = mixed.astype(pool_ref.dtype)


def _inproj(x2, g, w, pool_w, pool_scale, seq_len, tm=512, tn=ATTN_WIDTH, chunk_rows=256):
    t, d = x2.shape
    n = w.shape[1]
    assert n == 4 * tn and tn == POOL_WIDTH and seq_len % tm == 0
    halo = 2 * SUBLANES
    assert max(POOL_WINDOWS) <= halo
    return pl.pallas_call(
        functools.partial(_inproj_kernel, tn=tn, seq_tiles=seq_len // tm),
        grid=(t // tm,),
        in_specs=[
            pl.BlockSpec((tm, d), lambda i: (i, 0)),
            pl.BlockSpec((1, d), lambda i: (0, 0)),
            pl.BlockSpec(memory_space=pl.ANY),
            pl.BlockSpec(pool_w.shape, lambda i: (0, 0, 0)),
            pl.BlockSpec((1, tn), lambda i: (0, 0)),
        ],
        out_specs=[pl.BlockSpec((tm, 2 * tn), lambda i: (i, 0)),
                   pl.BlockSpec((tm, tn), lambda i: (i, 0)),
                   pl.BlockSpec((tn, tm), lambda i: (0, i))],
        out_shape=[jax.ShapeDtypeStruct((t, 2 * tn), BF16), jax.ShapeDtypeStruct((t, tn), BF16),
                   jax.ShapeDtypeStruct((tn, t), BF16)],
        scratch_shapes=[pltpu.VMEM((d, n), BF16), pltpu.VMEM((2, chunk_rows, n), F32),
                        pltpu.SemaphoreType.DMA((2,)), pltpu.VMEM((halo, tn), F32)],
        compiler_params=pltpu.CompilerParams(
            dimension_semantics=("arbitrary",), vmem_limit_bytes=VMEM_LIMIT),
        name="inproj",
    )(x2, g.reshape(1, d), w, pool_w, pool_scale.reshape(1, tn))


def _attn_kernel(lq1_ref, lk1_ref, lq2_ref, lk2_ref, q_ref, k_ref, v_ref, g_ref, o_ref,
                 tab_ref, vt_ref, *, tq):
    h = pl.program_id(0)
    s_len = q_ref.shape[0]
    nq = s_len // tq
    d, dv = DIFF_HEAD_DIM, DIFF_V_DIM

    @pl.when(pl.program_id(1) == 0)
    def _():
        slope = jnp.exp2(jnp.full((1, 1), -8.0 / N_DIFF_HEADS, F32) * (h + 1).astype(F32))
        rel = ((nq - 1) * tq + lax.broadcasted_iota(jnp.int32, (s_len, tq), 1)
               - lax.broadcasted_iota(jnp.int32, (s_len, tq), 0))
        tab_ref[...] = jnp.where(rel >= 0, (-LOG2E * slope) * rel.astype(F32), NEG_INF)

    vt_ref[:dv, :] = v_ref[...]
    vt_ref[dv:, :] = jnp.ones((vt_ref.shape[0] - dv, s_len), BF16)

    lam = (jnp.exp(jnp.sum(lq1_ref[...] * lk1_ref[...], axis=1, keepdims=True))
           - jnp.exp(jnp.sum(lq2_ref[...] * lk2_ref[...], axis=1, keepdims=True)) + LAM_INIT)
    dn = (((1,), (1,)), ((), ()))
    lane = lax.broadcasted_iota(jnp.int32, (tq, 2 * d), 1)

    def scores(qi):
        n = (qi + 1) * tq
        q = q_ref[qi * tq:(qi + 1) * tq, :]
        zero = jnp.zeros_like(q)
        qh = (jnp.where(lane < d, q, zero), jnp.where(lane >= d, q, zero))
        kk = k_ref[:n, :]
        bias = tab_ref[(nq - 1 - qi) * tq:(nq - 1 - qi) * tq + n, :]
        return [lax.dot_general(kk, qh[a], dn, preferred_element_type=F32) + bias
                for a in range(2)]

    ahead = 1
    pending = [scores(t) for t in range(min(ahead, nq))]
    for qi in range(nq):
        n = (qi + 1) * tq
        s_cur = pending.pop(0)
        if qi + ahead < nq:
            pending.append(scores(qi + ahead))
        outs = []
        for a in range(2):
            s = s_cur[a]
            m = jnp.max(s, axis=0, keepdims=True)
            p = jnp.exp2(s - m).astype(BF16)
            acc = jnp.dot(vt_ref[:, :n], p, preferred_element_type=F32)
            outs.append(acc[:dv] / acc[dv:dv + 1])
        ot = outs[0] - lam * outs[1]
        yt = ot * lax.rsqrt(jnp.mean(ot * ot, axis=0, keepdims=True) + RMS_EPS) * g_ref[...]
        o_ref[qi * tq:(qi + 1) * tq, :] = (yt * (1.0 - LAM_INIT)).T.astype(o_ref.dtype)


def _attention(proj3, v_t, lq1, lk1, lq2, lk2, subln_g, tq=256):
    b, s, _ = proj3.shape
    dv = DIFF_V_DIM
    nh = N_DIFF_HEADS
    ones_rows = 16
    lam_spec = pl.BlockSpec((1, DIFF_HEAD_DIM), lambda hi, bi: (0, 0))
    return pl.pallas_call(
        functools.partial(_attn_kernel, tq=tq),
        grid=(nh, b),
        in_specs=[
            lam_spec, lam_spec, lam_spec, lam_spec,
            pl.BlockSpec((None, s, dv), lambda hi, bi: (bi, 0, hi)),
            pl.BlockSpec((None, s, dv), lambda hi, bi: (bi, 0, nh + hi)),
            pl.BlockSpec((dv, s), lambda hi, bi: (hi, bi)),
            pl.BlockSpec((dv, 1), lambda hi, bi: (0, 0)),
        ],
        out_specs=pl.BlockSpec((None, s, dv), lambda hi, bi: (bi, 0, hi)),
        out_shape=jax.ShapeDtypeStruct((b, s, ATTN_WIDTH), BF16),
        scratch_shapes=[
            pltpu.VMEM((s, tq), F32),
            pltpu.VMEM((dv + ones_rows, s), BF16),
        ],
        compiler_params=pltpu.CompilerParams(
            dimension_semantics=("arbitrary", "arbitrary"),
            vmem_limit_bytes=VMEM_LIMIT),
        name="diff_attn",
    )(lq1, lk1, lq2, lk2, proj3, proj3, v_t, subln_g.reshape(dv, 1))


def _pack_bf16_pair(x):
    bits = lax.bitcast_convert_type(x.astype(BF16).astype(F32), jnp.uint32)
    half = bits.shape[1] // 2
    return bits[:, :half] | (bits[:, half:] >> 16)


def _outproj_kernel(a_ref, p_ref, x_ref, w_hbm, g_ref, wr_ref, br_ref,
                    h_ref, hn_ref, ri_ref, rw_ref, w_ref, stage, sem, *, n_sub):
    @pl.when(pl.program_id(0) == 0)
    def _():
        _load_weight_bf16(w_hbm, w_ref, stage, sem)

    ka = a_ref.shape[1]
    hs = x_ref.shape[0] // n_sub
    normed = []
    for r in range(n_sub):
        rows = pl.ds(r * hs, hs)
        mixed = (jnp.dot(a_ref[rows, :], w_ref[:ka, :], preferred_element_type=F32)
                 + jnp.dot(p_ref[rows, :], w_ref[ka:, :], preferred_element_type=F32))
        h = x_ref[rows, :] + mixed
        h_ref[rows, :] = h
        hn = _rms(h, g_ref[...])
        hn_ref[rows, :] = _pack_bf16_pair(hn)
        hn_hi = hn.astype(BF16)
        normed.append((hn_hi, (hn - hn_hi.astype(F32)).astype(BF16)))
    for r in range(n_sub):
        rows = pl.ds(r * hs, hs)
        ri, rw = _route(normed[r][0], normed[r][1], wr_ref, br_ref)
        ri_ref[rows, :] = ri
        rw_ref[rows, :] = rw


def _route(hn_hi, hn_lo, wr_ref, br_ref):
    both = jnp.dot(hn_hi, wr_ref[...], preferred_element_type=F32)
    logits = (both[:, :LANES] + both[:, LANES:]
              + jnp.dot(hn_lo, wr_ref[:, :LANES], preferred_element_type=F32)) + br_ref[...]
    ng, epg = N_EXPERT_GROUPS, EXPERTS_PER_GROUP
    lane = lax.broadcasted_iota(jnp.int32, logits.shape, 1)
    big = jnp.int32(LANES)
    low = jnp.float32(-3.0e38)
    cm = lane < ng
    c = jnp.where(cm, logits, low)
    cmax = jnp.max(c, axis=1, keepdims=True)
    gsel = jnp.min(jnp.where(c == cmax, lane, big), axis=1, keepdims=True)
    p_group = 1.0 / jnp.sum(jnp.where(cm, jnp.exp(c - cmax), 0.0), axis=1, keepdims=True)
    f_lo = ng + epg * gsel
    fm = (lane >= f_lo) & (lane < f_lo + epg)
    f = jnp.where(fm, logits, low)
    v1 = jnp.max(f, axis=1, keepdims=True)
    i1 = jnp.min(jnp.where(fm & (f == v1), lane, big), axis=1, keepdims=True)
    fm2 = fm & (lane != i1)
    f2 = jnp.where(fm2, logits, low)
    v2 = jnp.max(f2, axis=1, keepdims=True)
    i2 = jnp.min(jnp.where(fm2 & (f2 == v2), lane, big), axis=1, keepdims=True)
    e21 = jnp.exp(v2 - v1)
    w1 = p_group / (1.0 + e21)
    w2 = p_group * e21 / (1.0 + e21)
    return (jnp.where(lane == 0, i1 - ng, jnp.where(lane == 1, i2 - ng, 0)),
            jnp.where(lane == 0, w1, jnp.where(lane == 1, w2, 0.0)))


def _outproj(attn2, pool2, x2, w_out, g2, wr, br, tm=512, n_sub=2, chunk_rows=512):
    t, d = x2.shape
    ka, kp = attn2.shape[1], pool2.shape[1]
    row = lambda i: (i, 0)
    const = lambda i: (0, 0)
    return pl.pallas_call(
        functools.partial(_outproj_kernel, n_sub=n_sub),
        grid=(t // tm,),
        in_specs=[
            pl.BlockSpec((tm, ka), row),
            pl.BlockSpec((tm, kp), row),
            pl.BlockSpec((tm, d), row),
            pl.BlockSpec(memory_space=pl.ANY),
            pl.BlockSpec((1, d), const),
            pl.BlockSpec((d, 2 * LANES), const),
            pl.BlockSpec((1, LANES), const),
        ],
        out_specs=[
            pl.BlockSpec((tm, d), row),
            pl.BlockSpec((tm, d // 2), row),
            pl.BlockSpec((tm, LANES), row),
            pl.BlockSpec((tm, LANES), row),
        ],
        out_shape=[
            jax.ShapeDtypeStruct((t, d), F32),
            jax.ShapeDtypeStruct((t, d // 2), jnp.uint32),
            jax.ShapeDtypeStruct((t, LANES), jnp.int32),
            jax.ShapeDtypeStruct((t, LANES), F32),
        ],
        scratch_shapes=[pltpu.VMEM((ka + kp, d), BF16), pltpu.VMEM((2, chunk_rows, d), F32),
                        pltpu.SemaphoreType.DMA((2,))],
        compiler_params=pltpu.CompilerParams(
            dimension_semantics=("arbitrary",), vmem_limit_bytes=VMEM_LIMIT),
        name="outproj_router",
    )(attn2, pool2, x2, w_out, g2.reshape(1, d), wr, br)


def _unpack_bf16_pair(words):
    hi = lax.bitcast_convert_type(words & jnp.uint32(0xFFFF0000), F32).astype(BF16)
    lo = lax.bitcast_convert_type(words << 16, F32).astype(BF16)
    return hi, lo


def _expert_kernel(te_ref, nt_ref, first_ref, slot_ref, nxt_ref,
                   x_ref, wg_hbm, wu_hbm, wd_hbm, y_ref,
                   wgb, wub, wdb, wsem):
    j = pl.program_id(0)
    nt = nt_ref[0]
    kh = x_ref.shape[1]
    fh = wdb.shape[1] // 2

    def weight_copies(e, s):
        return ((pltpu.make_async_copy(wg_hbm.at[e], wgb.at[s], wsem.at[s]), 0),
                (pltpu.make_async_copy(wu_hbm.at[e], wub.at[s], wsem.at[s]), 1),
                (pltpu.make_async_copy(wd_hbm.at[e, :fh, :], wdb.at[s, :fh, :], wsem.at[s]), 0),
                (pltpu.make_async_copy(wd_hbm.at[e, fh:, :], wdb.at[s, fh:, :], wsem.at[s]), 1))

    @pl.when(j == 0)
    def _():
        for c, prio in weight_copies(te_ref[0], 0):
            c.start(priority=prio)

    @pl.when(j < nt)
    def _():
        ws = slot_ref[j]

        @pl.when(first_ref[j] == 1)
        def _():
            for c, _ in weight_copies(te_ref[j], ws):
                c.wait()

            @pl.when(nxt_ref[j] >= 0)
            def _():
                for c, prio in weight_copies(nxt_ref[j], 1 - ws):
                    c.start(priority=prio)

        xa, xb = _unpack_bf16_pair(x_ref[...])
        hg = (jnp.dot(xa, wgb[ws, :kh, :].astype(BF16), preferred_element_type=F32)
              + jnp.dot(xb, wgb[ws, kh:, :].astype(BF16), preferred_element_type=F32))
        hu = (jnp.dot(xa, wub[ws, :kh, :].astype(BF16), preferred_element_type=F32)
              + jnp.dot(xb, wub[ws, kh:, :].astype(BF16), preferred_element_type=F32))
        act = (hg / (1.0 + jnp.exp(-hg))) * hu
        y_ref[...] = _pack_bf16_pair(
            jnp.dot(act.astype(BF16), wdb[ws].astype(BF16), preferred_element_type=F32))

    @pl.when(j >= nt)
    def _():
        y_ref[...] = jnp.zeros_like(y_ref)


def _experts(plan, xs, w_gate, w_up, w_down, tm=ROW_TILE):
    d, f = w_gate.shape[1], w_gate.shape[2]
    assert xs.shape[1] * 2 == d
    nt = plan["tile_expert"].shape[0]
    any_spec = pl.BlockSpec(memory_space=pl.ANY)
    grid_spec = pltpu.PrefetchScalarGridSpec(
        num_scalar_prefetch=5,
        grid=(nt,),
        in_specs=[
            pl.BlockSpec((tm, d // 2), lambda j, te, n, *_: (jnp.minimum(j, n[0] - 1), 0)),
            any_spec, any_spec, any_spec],
        out_specs=pl.BlockSpec((tm, d // 2), lambda j, *_: (j, 0)),
        scratch_shapes=[
            pltpu.VMEM((2, d, f), F32),
            pltpu.VMEM((2, d, f), F32),
            pltpu.VMEM((2, f, d), F32),
            pltpu.SemaphoreType.DMA((2,)),
        ],
    )
    return pl.pallas_call(
        _expert_kernel,
        grid_spec=grid_spec,
        out_shape=jax.ShapeDtypeStruct((nt * tm, d // 2), jnp.uint32),
        compiler_params=pltpu.CompilerParams(
            dimension_semantics=("arbitrary",), vmem_limit_bytes=VMEM_LIMIT),
        name="experts",
    )(plan["tile_expert"], plan["n_tiles"], plan["first"], plan["slot"], plan["next_expert"],
      xs, w_gate, w_up, w_down)


def _combine_kernel(pos_ref, h_ref, rw_ref, ys_hbm, g_ref, o_ref, ybuf0, ybuf1, ybuf2, sem):
    i = pl.program_id(0)
    last = pl.num_programs(0) - 1
    tm = h_ref.shape[0]
    kh = ybuf0.shape[2]
    bufs = (ybuf0, ybuf1, ybuf2)
    nbuf = len(bufs)

    def start_gather(tile, buf, sm):
        for r in range(tm):
            for k in range(TOP_K_FINE):
                p = pos_ref[(tile * tm + r) * TOP_K_FINE + k]
                pltpu.make_async_copy(ys_hbm.at[pl.ds(p, 1), :], buf.at[k, pl.ds(r, 1), :],
                                      sm).start(priority=k % 2)

    def wait_gather(buf, sm):
        for k in range(TOP_K_FINE):
            pltpu.make_async_copy(ys_hbm.at[pl.ds(0, tm), :], buf.at[k], sm).wait()

    @pl.when(i == 0)
    def _():
        start_gather(0, ybuf0, sem.at[0])
        start_gather(1, ybuf1, sem.at[1])

    def step(par):
        cur = bufs[par]
        ahead = (par + nbuf - 1) % nbuf
        wait_gather(cur, sem.at[par])
        start_gather(jnp.minimum(i + nbuf - 1, last), bufs[ahead], sem.at[ahead])
        w = rw_ref[...]
        halves = []
        for part in range(2):
            y = None
            for k in range(TOP_K_FINE):
                words = cur[k]
                bits = (words & jnp.uint32(0xFFFF0000)) if part == 0 else (words << 16)
                term = w[:, k:k + 1] * lax.bitcast_convert_type(bits, F32)
                y = term if y is None else y + term
            halves.append(h_ref[:, part * kh:(part + 1) * kh] + y)
        ms = sum(jnp.sum(v * v, axis=-1, keepdims=True) for v in halves) / (2 * kh)
        inv = lax.rsqrt(ms + RMS_EPS)
        for part in range(2):
            o_ref[:, part * kh:(part + 1) * kh] = (
                halves[part] * inv * g_ref[:, part * kh:(part + 1) * kh])

        @pl.when(i == last)
        def _():
            for other in range(nbuf):
                if other != par:
                    wait_gather(bufs[other], sem.at[other])

    for par in range(nbuf):
        pl.when(i % nbuf == par)(functools.partial(step, par))


def _combine(pos, h, rw, ys, g, tm=256):
    t, d = h.shape
    grid_spec = pltpu.PrefetchScalarGridSpec(
        num_scalar_prefetch=1,
        grid=(t // tm,),
        in_specs=[
            pl.BlockSpec((tm, d), lambda i, p: (i, 0)),
            pl.BlockSpec((tm, LANES), lambda i, p: (i, 0)),
            pl.BlockSpec(memory_space=pl.ANY),
            pl.BlockSpec((1, d), lambda i, p: (0, 0)),
        ],
        out_specs=pl.BlockSpec((tm, d), lambda i, p: (i, 0)),
        scratch_shapes=[pltpu.VMEM((TOP_K_FINE, tm, d // 2), jnp.uint32),
                        pltpu.VMEM((TOP_K_FINE, tm, d // 2), jnp.uint32),
                        pltpu.VMEM((TOP_K_FINE, tm, d // 2), jnp.uint32),
                        pltpu.SemaphoreType.DMA((3,))],
    )
    return pl.pallas_call(
        _combine_kernel,
        grid_spec=grid_spec,
        out_shape=jax.ShapeDtypeStruct((t, d), F32),
        compiler_params=pltpu.CompilerParams(
            dimension_semantics=("arbitrary",), vmem_limit_bytes=VMEM_LIMIT),
        name="combine",
    )(pos, h, rw, ys, g.reshape(1, d))


def _rank_kernel(ri_ref, meta_ref, cnt_ref, carry_ref, tri_ref):
    tt = ri_ref.shape[0]
    ne = cnt_ref.shape[0]

    @pl.when(pl.program_id(0) == 0)
    def _():
        carry_ref[...] = jnp.zeros_like(carry_ref)
        earlier = (lax.broadcasted_iota(jnp.int32, (tt, tt), 0)
                   < lax.broadcasted_iota(jnp.int32, (tt, tt), 1))
        tri_ref[...] = jnp.where(earlier, 1.0, 0.0).astype(BF16)

    rit = ri_ref[...].astype(F32).T
    expert = lax.broadcasted_iota(jnp.int32, (ne, tt), 0).astype(F32)
    ranks = []
    hits = []
    for k in range(TOP_K_FINE):
        hits.append(expert == rit[k:k + 1, :])
    chosen = jnp.where(hits[0] | hits[1], 1.0, 0.0)
    before = (jnp.dot(chosen.astype(BF16), tri_ref[...], preferred_element_type=F32)
              + carry_ref[:, 0:1])
    for k in range(TOP_K_FINE):
        meta_ref[k:k + 1, :] = rit[k:k + 1, :].astype(jnp.int32)
        rank = jnp.sum(jnp.where(hits[k], before, 0.0), axis=0, keepdims=True)
        meta_ref[TOP_K_FINE + k:TOP_K_FINE + k + 1, :] = rank.astype(jnp.int32)
    meta_ref[2 * TOP_K_FINE:, :] = jnp.zeros((meta_ref.shape[0] - 2 * TOP_K_FINE, tt), jnp.int32)
    carry_ref[...] = carry_ref[...] + jnp.sum(chosen, axis=1, keepdims=True)
    cnt_ref[...] = carry_ref[...].astype(jnp.int32)


def _rank(ri, tt=1024):
    t = ri.shape[0]
    assert TOP_K_FINE == 2
    return pl.pallas_call(
        _rank_kernel,
        grid=(t // tt,),
        in_specs=[pl.BlockSpec((tt, LANES), lambda i: (i, 0))],
        out_specs=[pl.BlockSpec((None, 8, tt), lambda i: (i, 0, 0)),
                   pl.BlockSpec((N_EXPERTS, LANES), lambda i: (0, 0))],
        out_shape=[jax.ShapeDtypeStruct((t // tt, 8, tt), jnp.int32),
                   jax.ShapeDtypeStruct((N_EXPERTS, LANES), jnp.int32)],
        scratch_shapes=[pltpu.VMEM((N_EXPERTS, LANES), F32), pltpu.VMEM((tt, tt), BF16)],
        compiler_params=pltpu.CompilerParams(
            dimension_semantics=("arbitrary",), vmem_limit_bytes=VMEM_LIMIT),
        name="route_rank",
    )(ri)


def _scatter_rows_kernel(pos_ref, end_ref, nt_ref, src_hbm, dst_hbm, zbuf, st0, st1, st2,
                         sem, lsem, zsem, *, tm, row_tile):
    i = pl.program_id(0)
    stage = (st0, st1, st2)
    n_tiles_max = dst_hbm.shape[0] // row_tile

    def zero_copy(tile):
        row0 = pl.multiple_of(tile * row_tile, row_tile)
        return pltpu.make_async_copy(zbuf, dst_hbm.at[pl.ds(row0, row_tile), :], zsem)

    @pl.when(i == 0)
    def _():
        zbuf[...] = jnp.zeros_like(zbuf)
        n_exp = end_ref.shape[0]

        def has_rows(e):
            return end_ref[e] > (end_ref[e - 1] if e else 0)

        for e in range(n_exp):
            pl.when(has_rows(e))(lambda e=e: zero_copy(end_ref[e] - 1).start())

        def start_tail(tile, c):
            zero_copy(tile).start()
            return c

        def wait_tail(tile, c):
            zero_copy(tile).wait()
            return c

        lax.fori_loop(nt_ref[0], n_tiles_max, start_tail, 0)
        for e in range(n_exp):
            pl.when(has_rows(e))(lambda: zero_copy(0).wait())
        lax.fori_loop(nt_ref[0], n_tiles_max, wait_tail, 0)

    last = pl.num_programs(0) - 1
    nbuf = len(stage)

    def load(tile, b):
        row0 = pl.multiple_of(tile * tm, tm)
        return pltpu.make_async_copy(src_hbm.at[pl.ds(row0, tm), :], stage[b], lsem.at[b])

    def wait_rows(b):
        for _ in range(TOP_K_FINE):
            pltpu.make_async_copy(stage[b], dst_hbm.at[pl.ds(0, tm), :], sem.at[b]).wait()

    @pl.when(i == 0)
    def _():
        load(0, 0).start()

    def step(b):
        nb = (b + 1) % nbuf

        @pl.when(i >= nbuf - 1)
        def _():
            wait_rows(nb)

        @pl.when(i < last)
        def _():
            load(i + 1, nb).start()

        load(i, b).wait()
        for r in range(tm):
            for k in range(TOP_K_FINE):
                p = pos_ref[(i * tm + r) * TOP_K_FINE + k]
                pltpu.make_async_copy(stage[b].at[pl.ds(r, 1), :], dst_hbm.at[pl.ds(p, 1), :],
                                      sem.at[b]).start(priority=k % 2)

        @pl.when(i == last)
        def _():
            for d in range(nbuf - 1):
                @pl.when(i >= d)
                def _(d=d):
                    wait_rows((b - d) % nbuf)

    for b in range(nbuf):
        pl.when(i % nbuf == b)(functools.partial(step, b))


def _scatter_rows(plan, src, tm=256, row_tile=ROW_TILE):
    t, w = src.shape
    any_spec = pl.BlockSpec(memory_space=pl.ANY)
    grid_spec = pltpu.PrefetchScalarGridSpec(
        num_scalar_prefetch=3,
        grid=(t // tm,),
        in_specs=[any_spec],
        out_specs=any_spec,
        scratch_shapes=[pltpu.VMEM((row_tile, w), src.dtype),
                        pltpu.VMEM((tm, w), src.dtype), pltpu.VMEM((tm, w), src.dtype),
                        pltpu.VMEM((tm, w), src.dtype),
                        pltpu.SemaphoreType.DMA((3,)), pltpu.SemaphoreType.DMA((3,)),
                        pltpu.SemaphoreType.DMA(())],
    )
    return pl.pallas_call(
        functools.partial(_scatter_rows_kernel, tm=tm, row_tile=row_tile),
        grid_spec=grid_spec,
        out_shape=jax.ShapeDtypeStruct((plan["n_rows"], w), src.dtype),
        compiler_params=pltpu.CompilerParams(dimension_semantics=("arbitrary",)),
        name="scatter_rows",
    )(plan["pos"], plan["end_tile"], plan["n_tiles"], src)


def _sort_plan(ri, tm=ROW_TILE):
    t = ri.shape[0]
    k = TOP_K_FINE
    a = t * k
    nt = a // tm + N_EXPERTS
    meta, cnt = _rank(ri)
    counts = cnt[:, 0]
    padded = ((counts + tm - 1) // tm) * tm
    off_end = jnp.cumsum(padded)
    off = off_end - padded
    ef = meta[:, 0:k, :]
    eid = jnp.arange(N_EXPERTS, dtype=jnp.int32)
    pos = jnp.sum(jnp.where(ef[..., None] == eid, off, 0), axis=-1) + meta[:, k:2 * k, :]
    pos = jnp.transpose(pos, (0, 2, 1)).reshape(a)
    n_tiles = off_end[-1] // tm
    tile_idx = jnp.arange(nt, dtype=jnp.int32)
    te = jnp.sum((off_end[None, :] <= (tile_idx * tm)[:, None]).astype(jnp.int32), axis=1)
    te = jnp.minimum(te, N_EXPERTS - 1)
    te = jnp.where(tile_idx < n_tiles, te, te[n_tiles - 1]).astype(jnp.int32)
    first = ((tile_idx == 0) | (te != jnp.roll(te, 1))) & (tile_idx < n_tiles)
    slot = (jnp.cumsum(first.astype(jnp.int32)) - 1) % 2
    eid = jnp.arange(N_EXPERTS, dtype=jnp.int32)
    later = (eid[None, :] > eid[:, None]) & (counts[None, :] > 0)
    next_e = jnp.min(jnp.where(later, eid[None, :], N_EXPERTS), axis=1)
    next_e = jnp.where(next_e < N_EXPERTS, next_e, -1)
    return dict(
        end_tile=(off_end // tm).astype(jnp.int32),
        pos=pos.astype(jnp.int32), n_rows=nt * tm, tile_expert=te,
        n_tiles=n_tiles.reshape(1).astype(jnp.int32), first=first.astype(jnp.int32),
        slot=slot.astype(jnp.int32), next_expert=next_e[te].astype(jnp.int32))


def kernel(x, norm1_g, w_in, lambda_q1, lambda_k1, lambda_q2, lambda_k2, subln_g, pool_w, pool_scale, w_out, norm2_g, w_coarse, b_coarse, w_fine, b_fine, w_gate, w_up, w_down, final_norm_g):
    b, s, d = x.shape
    t = b * s
    assert norm1_g.shape[0] == 1
    x2 = x.reshape(t, d)
    proj, pool, v_t = _inproj(x2, norm1_g[0], w_in[0], pool_w[0], pool_scale[0], s)
    proj3 = proj.reshape(b, s, proj.shape[1])
    attn = _attention(proj3, v_t, lambda_q1, lambda_k1, lambda_q2, lambda_k2, subln_g[0])

    ng, epg = N_EXPERT_GROUPS, EXPERTS_PER_GROUP
    wr = jnp.concatenate(
        [w_coarse[0], jnp.transpose(w_fine[0], (1, 0, 2)).reshape(d, ng * epg)], axis=1)
    wr = jnp.pad(wr, ((0, 0), (0, LANES - wr.shape[1])))
    br = jnp.concatenate([b_coarse[0], b_fine[0].reshape(ng * epg)])
    br = jnp.pad(br, (0, LANES - br.shape[0])).reshape(1, LANES)
    wr_hi = wr.astype(BF16)
    wr_lo = (wr - wr_hi.astype(F32)).astype(BF16)
    h, hn, ri, rw = _outproj(attn.reshape(t, ATTN_WIDTH), pool, x2,
                             w_out[0], norm2_g[0], jnp.concatenate([wr_hi, wr_lo], axis=1), br)

    plan = _sort_plan(ri)
    xs = _scatter_rows(plan, hn)
    ys = _experts(plan, xs, w_gate[0], w_up[0], w_down[0])
    out = _combine(plan["pos"], h, rw, ys, final_norm_g)
    return out.reshape(b, s, d)
```

```python
import functools
import math

import jax
import jax.numpy as jnp
from jax import lax
from jax.experimental import pallas as pl
from jax.experimental.pallas import tpu as pltpu

N_DIFF_HEADS = 8
DIFF_HEAD_DIM = 64
DIFF_V_DIM = 2 * DIFF_HEAD_DIM
ATTN_WIDTH = N_DIFF_HEADS * DIFF_V_DIM
POOL_WINDOWS = (2, 4, 8, 16)
POOL_GROUP_DIM = 256
POOL_WIDTH = len(POOL_WINDOWS) * POOL_GROUP_DIM
N_EXPERT_GROUPS = 4
EXPERTS_PER_GROUP = 8
N_EXPERTS = N_EXPERT_GROUPS * EXPERTS_PER_GROUP
TOP_K_FINE = 2
RMS_EPS = 1e-6
NEG_INF = -1e30
LAM_INIT = 0.8 - 0.6 * math.exp(-0.3 * 0)
LOG2E = math.log2(math.e)
Q_SCALE = DIFF_HEAD_DIM ** -0.5 * LOG2E

LANES = 128
SUBLANES = 8
VMEM_LIMIT = 56 * 1024 * 1024
ROW_TILE = 256

F32 = jnp.float32
BF16 = jnp.bfloat16


def _rms(x, g):
    return x * lax.rsqrt(jnp.mean(x * x, axis=-1, keepdims=True) + RMS_EPS) * g


def _load_weight_bf16(w_hbm, wb_ref, stage, sem):
    ck = stage.shape[1]
    nchunk = w_hbm.shape[0] // ck

    def copy(c):
        return pltpu.make_async_copy(w_hbm.at[c * ck:(c + 1) * ck, :], stage.at[c % 2], sem.at[c % 2])

    copy(0).start()
    for c in range(nchunk):
        if c + 1 < nchunk:
            copy(c + 1).start()
        copy(c).wait()
        wb_ref[c * ck:(c + 1) * ck, :] = stage[c % 2].astype(BF16)


def _pool_mix(u, prev, seq_pos, pw_ref, ps_ref):
    halo = prev.shape[0]
    c = POOL_GROUP_DIM
    ext = jnp.concatenate([prev, u], axis=0)
    outs = []
    for gi, win in enumerate(POOL_WINDOWS):
        assert win & (win - 1) == 0 and win <= halo
        run = ext[:, gi * c:(gi + 1) * c]
        span = 1
        while span < win:
            run = run + pltpu.roll(run, span, axis=0)
            span *= 2
        count = jnp.minimum(seq_pos + 1, win).astype(F32)
        pooled = run[halo:, :] / count - u[:, gi * c:(gi + 1) * c]
        mixed = jnp.dot(pooled.astype(BF16), pw_ref[gi].astype(BF16), preferred_element_type=F32)
        outs.append(mixed * ps_ref[:, gi * c:(gi + 1) * c])
    return outs


def _inproj_kernel(x_ref, g_ref, w_hbm, pw_ref, ps_ref, o_ref, pool_ref, vt_ref,
                   wb_ref, stage, sem, halo_ref, *, tn, seq_tiles):
    i = pl.program_id(0)
    tm = x_ref.shape[0]

    @pl.when(i == 0)
    def _():
        halo_ref[...] = jnp.zeros_like(halo_ref)
        _load_weight_bf16(w_hbm, wb_ref, stage, sem)

    xn = _rms(x_ref[...], g_ref[...]).astype(BF16)
    q = jnp.dot(xn, wb_ref[:, 0 * tn:1 * tn], preferred_element_type=F32) * Q_SCALE
    o_ref[:, 0 * tn:1 * tn] = q.astype(o_ref.dtype)
    k = jnp.dot(xn, wb_ref[:, 1 * tn:2 * tn], preferred_element_type=F32)
    o_ref[:, 1 * tn:2 * tn] = k.astype(o_ref.dtype)
    v = jnp.dot(xn, wb_ref[:, 2 * tn:3 * tn], preferred_element_type=F32)
    vt_ref[...] = v.T.astype(vt_ref.dtype)
    u = jnp.dot(xn, wb_ref[:, 3 * tn:4 * tn], preferred_element_type=F32)

    tile_in_seq = i % seq_tiles
    prev = jnp.where(tile_in_seq == 0, 0.0, halo_ref[...])
    halo_ref[...] = u[tm - halo_ref.shape[0]:, :]
    seq_pos = tile_in_seq * tm + lax.broadcasted_iota(jnp.int32, (tm, 1), 0)
    c = POOL_GROUP_DIM
    for gi, mixed in enumerate(_pool_mix(u, prev, seq_pos, pw_ref, ps_ref)):
        pool_ref[:, gi * c:(gi + 1) * c] = mixed.astype(pool_ref.dtype)


def _inproj(x2, g, w, pool_w, pool_scale, seq_len, tm=512, tn=ATTN_WIDTH, chunk_rows=256):
    t, d = x2.shape
    n = w.shape[1]
    assert n == 4 * tn and tn == POOL_WIDTH and seq_len % tm == 0
    halo = 2 * SUBLANES
    assert max(POOL_WINDOWS) <= halo
    return pl.pallas_call(
        functools.partial(_inproj_kernel, tn=tn, seq_tiles=seq_len // tm),
        grid=(t // tm,),
        in_specs=[
            pl.BlockSpec((tm, d), lambda i: (i, 0)),
            pl.BlockSpec((1, d), lambda i: (0, 0)),
            pl.BlockSpec(memory_space=pl.ANY),
            pl.BlockSpec(pool_w.shape, lambda i: (0, 0, 0)),
            pl.BlockSpec((1, tn), lambda i: (0, 0)),
        ],
        out_specs=[pl.BlockSpec((tm, 2 * tn), lambda i: (i, 0)),
                   pl.BlockSpec((tm, tn), lambda i: (i, 0)),
                   pl.BlockSpec((tn, tm), lambda i: (0, i))],
        out_shape=[jax.ShapeDtypeStruct((t, 2 * tn), BF16), jax.ShapeDtypeStruct((t, tn), BF16),
                   jax.ShapeDtypeStruct((tn, t), BF16)],
        scratch_shapes=[pltpu.VMEM((d, n), BF16), pltpu.VMEM((2, chunk_rows, n), F32),
                        pltpu.SemaphoreType.DMA((2,)), pltpu.VMEM((halo, tn), F32)],
        compiler_params=pltpu.CompilerParams(
            dimension_semantics=("arbitrary",), vmem_limit_bytes=VMEM_LIMIT),
        name="inproj",
    )(x2, g.reshape(1, d), w, pool_w, pool_scale.reshape(1, tn))


def _attn_kernel(lq1_ref, lk1_ref, lq2_ref, lk2_ref, q_ref, k_ref, v_ref, g_ref, o_ref,
                 tab_ref, vt_ref, *, tq):
    h = pl.program_id(0)
    s_len = q_ref.shape[0]
    nq = s_len // tq
    d, dv = DIFF_HEAD_DIM, DIFF_V_DIM

    @pl.when(pl.program_id(1) == 0)
    def _():
        slope = jnp.exp2(jnp.full((1, 1), -8.0 / N_DIFF_HEADS, F32) * (h + 1).astype(F32))
        rel = ((nq - 1) * tq + lax.broadcasted_iota(jnp.int32, (s_len, tq), 1)
               - lax.broadcasted_iota(jnp.int32, (s_len, tq), 0))
        tab_ref[...] = jnp.where(rel >= 0, (-LOG2E * slope) * rel.astype(F32), NEG_INF)

    vt_ref[:dv, :] = v_ref[...]
    vt_ref[dv:, :] = jnp.ones((vt_ref.shape[0] - dv, s_len), BF16)

    lam = (jnp.exp(jnp.sum(lq1_ref[...] * lk1_ref[...], axis=1, keepdims=True))
           - jnp.exp(jnp.sum(lq2_ref[...] * lk2_ref[...], axis=1, keepdims=True)) + LAM_INIT)
    dn = (((1,), (1,)), ((), ()))
    lane = lax.broadcasted_iota(jnp.int32, (tq, 2 * d), 1)

    def scores(qi):
        n = (qi + 1) * tq
        q = q_ref[qi * tq:(qi + 1) * tq, :]
        zero = jnp.zeros_like(q)
        qh = (jnp.where(lane < d, q, zero), jnp.where(lane >= d, q, zero))
        kk = k_ref[:n, :]
        bias = tab_ref[(nq - 1 - qi) * tq:(nq - 1 - qi) * tq + n, :]
        return [lax.dot_general(kk, qh[a], dn, preferred_element_type=F32) + bias
                for a in range(2)]

    ahead = 1
    pending = [scores(t) for t in range(min(ahead, nq))]
    for qi in range(nq):
        n = (qi + 1) * tq
        s_cur = pending.pop(0)
        if qi + ahead < nq:
            pending.append(scores(qi + ahead))
        outs = []
        for a in range(2):
            s = s_cur[a]
            m = jnp.max(s, axis=0, keepdims=True)
            p = jnp.exp2(s - m).astype(BF16)
            acc = jnp.dot(vt_ref[:, :n], p, preferred_element_type=F32)
            outs.append(acc[:dv] / acc[dv:dv + 1])
        ot = outs[0] - lam * outs[1]
        yt = ot * lax.rsqrt(jnp.mean(ot * ot, axis=0, keepdims=True) + RMS_EPS) * g_ref[...]
        o_ref[qi * tq:(qi + 1) * tq, :] = (yt * (1.0 - LAM_INIT)).T.astype(o_ref.dtype)


def _attention(proj3, v_t, lq1, lk1, lq2, lk2, subln_g, tq=256):
    b, s, _ = proj3.shape
    dv = DIFF_V_DIM
    nh = N_DIFF_HEADS
    ones_rows = 16
    lam_spec = pl.BlockSpec((1, DIFF_HEAD_DIM), lambda hi, bi: (0, 0))
    return pl.pallas_call(
        functools.partial(_attn_kernel, tq=tq),
        grid=(nh, b),
        in_specs=[
            lam_spec, lam_spec, lam_spec, lam_spec,
            pl.BlockSpec((None, s, dv), lambda hi, bi: (bi, 0, hi)),
            pl.BlockSpec((None, s, dv), lambda hi, bi: (bi, 0, nh + hi)),
            pl.BlockSpec((dv, s), lambda hi, bi: (hi, bi)),
            pl.BlockSpec((dv, 1), lambda hi, bi: (0, 0)),
        ],
        out_specs=pl.BlockSpec((None, s, dv), lambda hi, bi: (bi, 0, hi)),
        out_shape=jax.ShapeDtypeStruct((b, s, ATTN_WIDTH), BF16),
        scratch_shapes=[
            pltpu.VMEM((s, tq), F32),
            pltpu.VMEM((dv + ones_rows, s), BF16),
        ],
        compiler_params=pltpu.CompilerParams(
            dimension_semantics=("arbitrary", "arbitrary"),
            vmem_limit_bytes=VMEM_LIMIT),
        name="diff_attn",
    )(lq1, lk1, lq2, lk2, proj3, proj3, v_t, subln_g.reshape(dv, 1))


def _pack_bf16_pair(x):
    bits = lax.bitcast_convert_type(x.astype(BF16).astype(F32), jnp.uint32)
    half = bits.shape[1] // 2
    return bits[:, :half] | (bits[:, half:] >> 16)


def _outproj_kernel(a_ref, p_ref, x_ref, w_hbm, g_ref, wr_ref, br_ref,
                    h_ref, hn_ref, ri_ref, rw_ref, w_ref, stage, sem, *, n_sub):
    @pl.when(pl.program_id(0) == 0)
    def _():
        _load_weight_bf16(w_hbm, w_ref, stage, sem)

    ka = a_ref.shape[1]
    hs = x_ref.shape[0] // n_sub
    normed = []
    for r in range(n_sub):
        rows = pl.ds(r * hs, hs)
        mixed = (jnp.dot(a_ref[rows, :], w_ref[:ka, :], preferred_element_type=F32)
                 + jnp.dot(p_ref[rows, :], w_ref[ka:, :], preferred_element_type=F32))
        h = x_ref[rows, :] + mixed
        h_ref[rows, :] = h
        hn = _rms(h, g_ref[...])
        hn_ref[rows, :] = _pack_bf16_pair(hn)
        hn_hi = hn.astype(BF16)
        normed.append((hn_hi, (hn - hn_hi.astype(F32)).astype(BF16)))
    for r in range(n_sub):
        rows = pl.ds(r * hs, hs)
        ri, rw = _route(normed[r][0], normed[r][1], wr_ref, br_ref)
        ri_ref[rows, :] = ri
        rw_ref[rows, :] = rw


def _route(hn_hi, hn_lo, wr_ref, br_ref):
    both = jnp.dot(hn_hi, wr_ref[...], preferred_element_type=F32)
    logits = (both[:, :LANES] + both[:, LANES:]
              + jnp.dot(hn_lo, wr_ref[:, :LANES], preferred_element_type=F32)) + br_ref[...]
    ng, epg = N_EXPERT_GROUPS, EXPERTS_PER_GROUP
    lane = lax.broadcasted_iota(jnp.int32, logits.shape, 1)
    big = jnp.int32(LANES)
    low = jnp.float32(-3.0e38)
    cm = lane < ng
    c = jnp.where(cm, logits, low)
    cmax = jnp.max(c, axis=1, keepdims=True)
    gsel = jnp.min(jnp.where(c == cmax, lane, big), axis=1, keepdims=True)
    p_group = 1.0 / jnp.sum(jnp.where(cm, jnp.exp(c - cmax), 0.0), axis=1, keepdims=True)
    f_lo = ng + epg * gsel
    fm = (lane >= f_lo) & (lane < f_lo + epg)
    f = jnp.where(fm, logits, low)
    v1 = jnp.max(f, axis=1, keepdims=True)
    i1 = jnp.min(jnp.where(fm & (f == v1), lane, big), axis=1, keepdims=True)
    fm2 = fm & (lane != i1)
    f2 = jnp.where(fm2, logits, low)
    v2 = jnp.max(f2, axis=1, keepdims=True)
    i2 = jnp.min(jnp.where(fm2 & (f2 == v2), lane, big), axis=1, keepdims=True)
    e21 = jnp.exp(v2 - v1)
    w1 = p_group / (1.0 + e21)
    w2 = p_group * e21 / (1.0 + e21)
    return (jnp.where(lane == 0, i1 - ng, jnp.where(lane == 1, i2 - ng, 0)),
            jnp.where(lane == 0, w1, jnp.where(lane == 1, w2, 0.0)))


def _outproj(attn2, pool2, x2, w_out, g2, wr, br, tm=512, n_sub=2, chunk_rows=512):
    t, d = x2.shape
    ka, kp = attn2.shape[1], pool2.shape[1]
    row = lambda i: (i, 0)
    const = lambda i: (0, 0)
    return pl.pallas_call(
        functools.partial(_outproj_kernel, n_sub=n_sub),
        grid=(t // tm,),
        in_specs=[
            pl.BlockSpec((tm, ka), row),
            pl.BlockSpec((tm, kp), row),
            pl.BlockSpec((tm, d), row),
            pl.BlockSpec(memory_space=pl.ANY),
            pl.BlockSpec((1, d), const),
            pl.BlockSpec((d, 2 * LANES), const),
            pl.BlockSpec((1, LANES), const),
        ],
        out_specs=[
            pl.BlockSpec((tm, d), row),
            pl.BlockSpec((tm, d // 2), row),
            pl.BlockSpec((tm, LANES), row),
            pl.BlockSpec((tm, LANES), row),
        ],
        out_shape=[
            jax.ShapeDtypeStruct((t, d), F32),
            jax.ShapeDtypeStruct((t, d // 2), jnp.uint32),
            jax.ShapeDtypeStruct((t, LANES), jnp.int32),
            jax.ShapeDtypeStruct((t, LANES), F32),
        ],
        scratch_shapes=[pltpu.VMEM((ka + kp, d), BF16), pltpu.VMEM((2, chunk_rows, d), F32),
                        pltpu.SemaphoreType.DMA((2,))],
        compiler_params=pltpu.CompilerParams(
            dimension_semantics=("arbitrary",), vmem_limit_bytes=VMEM_LIMIT),
        name="outproj_router",
    )(attn2, pool2, x2, w_out, g2.reshape(1, d), wr, br)


def _unpack_bf16_pair(words):
    hi = lax.bitcast_convert_type(words & jnp.uint32(0xFFFF0000), F32).astype(BF16)
    lo = lax.bitcast_convert_type(words << 16, F32).astype(BF16)
    return hi, lo


def _expert_kernel(te_ref, nt_ref, first_ref, slot_ref, nxt_ref,
                   x_ref, wg_hbm, wu_hbm, wd_hbm, y_ref,
                   wgb, wub, wdb, wsem, *, tm, tiles_per_step):
    nt = nt_ref[0]
    kh = x_ref.shape[1]
    fh = wdb.shape[1] // 2

    def weight_copies(e, s):
        return ((pltpu.make_async_copy(wg_hbm.at[e], wgb.at[s], wsem.at[s]), 0),
                (pltpu.make_async_copy(wu_hbm.at[e], wub.at[s], wsem.at[s]), 1),
                (pltpu.make_async_copy(wd_hbm.at[e, :fh, :], wdb.at[s, :fh, :], wsem.at[s]), 0),
                (pltpu.make_async_copy(wd_hbm.at[e, fh:, :], wdb.at[s, fh:, :], wsem.at[s]), 1))

    @pl.when(pl.program_id(0) == 0)
    def _():
        for c, prio in weight_copies(te_ref[0], 0):
            c.start(priority=prio)

    def tile(j, rows):
        @pl.when(j < nt)
        def _():
            ws = slot_ref[j]

            @pl.when(first_ref[j] == 1)
            def _():
                for c, _ in weight_copies(te_ref[j], ws):
                    c.wait()

                @pl.when(nxt_ref[j] >= 0)
                def _():
                    for c, prio in weight_copies(nxt_ref[j], 1 - ws):
                        c.start(priority=prio)

            xa, xb = _unpack_bf16_pair(x_ref[rows, :])
            hg = (jnp.dot(xa, wgb[ws, :kh, :].astype(BF16), preferred_element_type=F32)
                  + jnp.dot(xb, wgb[ws, kh:, :].astype(BF16), preferred_element_type=F32))
            hu = (jnp.dot(xa, wub[ws, :kh, :].astype(BF16), preferred_element_type=F32)
                  + jnp.dot(xb, wub[ws, kh:, :].astype(BF16), preferred_element_type=F32))
            act = (hg / (1.0 + jnp.exp(-hg))) * hu
            y_ref[rows, :] = _pack_bf16_pair(
                jnp.dot(act.astype(BF16), wdb[ws].astype(BF16), preferred_element_type=F32))

        @pl.when(j >= nt)
        def _():
            y_ref[rows, :] = jnp.zeros((tm, kh), y_ref.dtype)

    for sub in range(tiles_per_step):
        tile(pl.program_id(0) * tiles_per_step + sub, pl.ds(sub * tm, tm))


def _experts(plan, xs, w_gate, w_up, w_down, tm=ROW_TILE, tiles_per_step=2):
    d, f = w_gate.shape[1], w_gate.shape[2]
    assert xs.shape[1] * 2 == d
    nt = plan["tile_expert"].shape[0]
    assert nt % tiles_per_step == 0
    rows = tm * tiles_per_step
    any_spec = pl.BlockSpec(memory_space=pl.ANY)
    grid_spec = pltpu.PrefetchScalarGridSpec(
        num_scalar_prefetch=5,
        grid=(nt // tiles_per_step,),
        in_specs=[
            pl.BlockSpec((rows, d // 2),
                         lambda j, te, n, *_: (jnp.minimum(j, (n[0] - 1) // tiles_per_step), 0)),
            any_spec, any_spec, any_spec],
        out_specs=pl.BlockSpec((rows, d // 2), lambda j, *_: (j, 0)),
        scratch_shapes=[
            pltpu.VMEM((2, d, f), F32),
            pltpu.VMEM((2, d, f), F32),
            pltpu.VMEM((2, f, d), F32),
            pltpu.SemaphoreType.DMA((2,)),
        ],
    )
    return pl.pallas_call(
        functools.partial(_expert_kernel, tm=tm, tiles_per_step=tiles_per_step),
        grid_spec=grid_spec,
        out_shape=jax.ShapeDtypeStruct((nt * tm, d // 2), jnp.uint32),
        compiler_params=pltpu.CompilerParams(
            dimension_semantics=("arbitrary",), vmem_limit_bytes=VMEM_LIMIT),
        name="experts",
    )(plan["tile_expert"], plan["n_tiles"], plan["first"], plan["slot"], plan["next_expert"],
      xs, w_gate, w_up, w_down)


def _combine_kernel(pos_ref, h_ref, rw_ref, ys_hbm, g_ref, o_ref, ybuf0, ybuf1, ybuf2, sem):
    i = pl.program_id(0)
    last = pl.num_programs(0) - 1
    tm = h_ref.shape[0]
    kh = ybuf0.shape[2]
    bufs = (ybuf0, ybuf1, ybuf2)
    nbuf = len(bufs)

    def start_gather(tile, buf, sm):
        for r in range(tm):
            for k in range(TOP_K_FINE):
                p = pos_ref[(tile * tm + r) * TOP_K_FINE + k]
                pltpu.make_async_copy(ys_hbm.at[pl.ds(p, 1), :], buf.at[k, pl.ds(r, 1), :],
                                      sm).start(priority=k % 2)

    def wait_gather(buf, sm):
        for k in range(TOP_K_FINE):
            pltpu.make_async_copy(ys_hbm.at[pl.ds(0, tm), :], buf.at[k], sm).wait()

    @pl.when(i == 0)
    def _():
        start_gather(0, ybuf0, sem.at[0])
        start_gather(1, ybuf1, sem.at[1])

    def step(par):
        cur = bufs[par]
        ahead = (par + nbuf - 1) % nbuf
        wait_gather(cur, sem.at[par])
        start_gather(jnp.minimum(i + nbuf - 1, last), bufs[ahead], sem.at[ahead])
        w = rw_ref[...]
        halves = []
        for part in range(2):
            y = None
            for k in range(TOP_K_FINE):
                words = cur[k]
                bits = (words & jnp.uint32(0xFFFF0000)) if part == 0 else (words << 16)
                term = w[:, k:k + 1] * lax.bitcast_convert_type(bits, F32)
                y = term if y is None else y + term
            halves.append(h_ref[:, part * kh:(part + 1) * kh] + y)
        ms = sum(jnp.sum(v * v, axis=-1, keepdims=True) for v in halves) / (2 * kh)
        inv = lax.rsqrt(ms + RMS_EPS)
        for part in range(2):
            o_ref[:, part * kh:(part + 1) * kh] = (
                halves[part] * inv * g_ref[:, part * kh:(part + 1) * kh])

        @pl.when(i == last)
        def _():
            for other in range(nbuf):
                if other != par:
                    wait_gather(bufs[other], sem.at[other])

    for par in range(nbuf):
        pl.when(i % nbuf == par)(functools.partial(step, par))


def _combine(pos, h, rw, ys, g, tm=256):
    t, d = h.shape
    grid_spec = pltpu.PrefetchScalarGridSpec(
        num_scalar_prefetch=1,
        grid=(t // tm,),
        in_specs=[
            pl.BlockSpec((tm, d), lambda i, p: (i, 0)),
            pl.BlockSpec((tm, LANES), lambda i, p: (i, 0)),
            pl.BlockSpec(memory_space=pl.ANY),
            pl.BlockSpec((1, d), lambda i, p: (0, 0)),
        ],
        out_specs=pl.BlockSpec((tm, d), lambda i, p: (i, 0)),
        scratch_shapes=[pltpu.VMEM((TOP_K_FINE, tm, d // 2), jnp.uint32),
                        pltpu.VMEM((TOP_K_FINE, tm, d // 2), jnp.uint32),
                        pltpu.VMEM((TOP_K_FINE, tm, d // 2), jnp.uint32),
                        pltpu.SemaphoreType.DMA((3,))],
    )
    return pl.pallas_call(
        _combine_kernel,
        grid_spec=grid_spec,
        out_shape=jax.ShapeDtypeStruct((t, d), F32),
        compiler_params=pltpu.CompilerParams(
            dimension_semantics=("arbitrary",), vmem_limit_bytes=VMEM_LIMIT),
        name="combine",
    )(pos, h, rw, ys, g.reshape(1, d))


def _rank_kernel(ri_ref, meta_ref, cnt_ref, carry_ref, tri_ref):
    tt = ri_ref.shape[0]
    ne = cnt_ref.shape[0]

    @pl.when(pl.program_id(0) == 0)
    def _():
        carry_ref[...] = jnp.zeros_like(carry_ref)
        earlier = (lax.broadcasted_iota(jnp.int32, (tt, tt), 0)
                   < lax.broadcasted_iota(jnp.int32, (tt, tt), 1))
        tri_ref[...] = jnp.where(earlier, 1.0, 0.0).astype(BF16)

    rit = ri_ref[...].astype(F32).T
    expert = lax.broadcasted_iota(jnp.int32, (ne, tt), 0).astype(F32)
    hits = []
    for k in range(TOP_K_FINE):
        hits.append(expert == rit[k:k + 1, :])
    chosen = jnp.where(hits[0] | hits[1], 1.0, 0.0)
    before = (jnp.dot(chosen.astype(BF16), tri_ref[...], preferred_element_type=F32)
              + carry_ref[:, 0:1])
    for k in range(TOP_K_FINE):
        meta_ref[k:k + 1, :] = rit[k:k + 1, :].astype(jnp.int32)
        rank = jnp.sum(jnp.where(hits[k], before, 0.0), axis=0, keepdims=True)
        meta_ref[TOP_K_FINE + k:TOP_K_FINE + k + 1, :] = rank.astype(jnp.int32)
    meta_ref[2 * TOP_K_FINE:, :] = jnp.zeros((meta_ref.shape[0] - 2 * TOP_K_FINE, tt), jnp.int32)
    carry_ref[...] = carry_ref[...] + jnp.sum(chosen, axis=1, keepdims=True)
    cnt_ref[...] = carry_ref[...].astype(jnp.int32)


def _rank(ri, tt=1024):
    t = ri.shape[0]
    assert TOP_K_FINE == 2
    return pl.pallas_call(
        _rank_kernel,
        grid=(t // tt,),
        in_specs=[pl.BlockSpec((tt, LANES), lambda i: (i, 0))],
        out_specs=[pl.BlockSpec((None, 8, tt), lambda i: (i, 0, 0)),
                   pl.BlockSpec((N_EXPERTS, LANES), lambda i: (0, 0))],
        out_shape=[jax.ShapeDtypeStruct((t // tt, 8, tt), jnp.int32),
                   jax.ShapeDtypeStruct((N_EXPERTS, LANES), jnp.int32)],
        scratch_shapes=[pltpu.VMEM((N_EXPERTS, LANES), F32), pltpu.VMEM((tt, tt), BF16)],
        compiler_params=pltpu.CompilerParams(
            dimension_semantics=("arbitrary",), vmem_limit_bytes=VMEM_LIMIT),
        name="route_rank",
    )(ri)


def _scatter_rows_kernel(pos_ref, end_ref, nt_ref, src_hbm, dst_hbm, zbuf, st0, st1, st2,
                         sem, lsem, zsem, *, tm, row_tile):
    i = pl.program_id(0)
    stage = (st0, st1, st2)
    n_tiles_max = dst_hbm.shape[0] // row_tile

    def zero_copy(tile):
        row0 = pl.multiple_of(tile * row_tile, row_tile)
        return pltpu.make_async_copy(zbuf, dst_hbm.at[pl.ds(row0, row_tile), :], zsem)

    @pl.when(i == 0)
    def _():
        zbuf[...] = jnp.zeros_like(zbuf)
        n_exp = end_ref.shape[0]

        def has_rows(e):
            return end_ref[e] > (end_ref[e - 1] if e else 0)

        for e in range(n_exp):
            pl.when(has_rows(e))(lambda e=e: zero_copy(end_ref[e] - 1).start())

        def start_tail(tile, c):
            zero_copy(tile).start()
            return c

        def wait_tail(tile, c):
            zero_copy(tile).wait()
            return c

        lax.fori_loop(nt_ref[0], n_tiles_max, start_tail, 0)
        for e in range(n_exp):
            pl.when(has_rows(e))(lambda: zero_copy(0).wait())
        lax.fori_loop(nt_ref[0], n_tiles_max, wait_tail, 0)

    last = pl.num_programs(0) - 1
    nbuf = len(stage)

    def load(tile, b):
        row0 = pl.multiple_of(tile * tm, tm)
        return pltpu.make_async_copy(src_hbm.at[pl.ds(row0, tm), :], stage[b], lsem.at[b])

    def wait_rows(b):
        for _ in range(TOP_K_FINE):
            pltpu.make_async_copy(stage[b], dst_hbm.at[pl.ds(0, tm), :], sem.at[b]).wait()

    @pl.when(i == 0)
    def _():
        load(0, 0).start()

    def step(b):
        nb = (b + 1) % nbuf

        @pl.when(i >= nbuf - 1)
        def _():
            wait_rows(nb)

        @pl.when(i < last)
        def _():
            load(i + 1, nb).start()

        load(i, b).wait()
        for r in range(tm):
            for k in range(TOP_K_FINE):
                p = pos_ref[(i * tm + r) * TOP_K_FINE + k]
                pltpu.make_async_copy(stage[b].at[pl.ds(r, 1), :], dst_hbm.at[pl.ds(p, 1), :],
                                      sem.at[b]).start(priority=k % 2)

        @pl.when(i == last)
        def _():
            for d in range(nbuf - 1):
                @pl.when(i >= d)
                def _(d=d):
                    wait_rows((b - d) % nbuf)

    for b in range(nbuf):
        pl.when(i % nbuf == b)(functools.partial(step, b))


def _scatter_rows(plan, src, tm=256, row_tile=ROW_TILE):
    t, w = src.shape
    any_spec = pl.BlockSpec(memory_space=pl.ANY)
    grid_spec = pltpu.PrefetchScalarGridSpec(
        num_scalar_prefetch=3,
        grid=(t // tm,),
        in_specs=[any_spec],
        out_specs=any_spec,
        scratch_shapes=[pltpu.VMEM((row_tile, w), src.dtype),
                        pltpu.VMEM((tm, w), src.dtype), pltpu.VMEM((tm, w), src.dtype),
                        pltpu.VMEM((tm, w), src.dtype),
                        pltpu.SemaphoreType.DMA((3,)), pltpu.SemaphoreType.DMA((3,)),
                        pltpu.SemaphoreType.DMA(())],
    )
    return pl.pallas_call(
        functools.partial(_scatter_rows_kernel, tm=tm, row_tile=row_tile),
        grid_spec=grid_spec,
        out_shape=jax.ShapeDtypeStruct((plan["n_rows"], w), src.dtype),
        compiler_params=pltpu.CompilerParams(dimension_semantics=("arbitrary",)),
        name="scatter_rows",
    )(plan["pos"], plan["end_tile"], plan["n_tiles"], src)


def _sort_plan(ri, tm=ROW_TILE):
    t = ri.shape[0]
    k = TOP_K_FINE
    a = t * k
    nt = a // tm + N_EXPERTS
    meta, cnt = _rank(ri)
    counts = cnt[:, 0]
    padded = ((counts + tm - 1) // tm) * tm
    off_end = jnp.cumsum(padded)
    off = off_end - padded
    ef = meta[:, 0:k, :]
    eid = jnp.arange(N_EXPERTS, dtype=jnp.int32)
    pos = jnp.sum(jnp.where(ef[..., None] == eid, off, 0), axis=-1) + meta[:, k:2 * k, :]
    pos = jnp.transpose(pos, (0, 2, 1)).reshape(a)
    n_tiles = off_end[-1] // tm
    tile_idx = jnp.arange(nt, dtype=jnp.int32)
    te = jnp.sum((off_end[None, :] <= (tile_idx * tm)[:, None]).astype(jnp.int32), axis=1)
    te = jnp.minimum(te, N_EXPERTS - 1)
    te = jnp.where(tile_idx < n_tiles, te, te[n_tiles - 1]).astype(jnp.int32)
    first = ((tile_idx == 0) | (te != jnp.roll(te, 1))) & (tile_idx < n_tiles)
    slot = (jnp.cumsum(first.astype(jnp.int32)) - 1) % 2
    eid = jnp.arange(N_EXPERTS, dtype=jnp.int32)
    later = (eid[None, :] > eid[:, None]) & (counts[None, :] > 0)
    next_e = jnp.min(jnp.where(later, eid[None, :], N_EXPERTS), axis=1)
    next_e = jnp.where(next_e < N_EXPERTS, next_e, -1)
    return dict(
        end_tile=(off_end // tm).astype(jnp.int32),
        pos=pos.astype(jnp.int32), n_rows=nt * tm, tile_expert=te,
        n_tiles=n_tiles.reshape(1).astype(jnp.int32), first=first.astype(jnp.int32),
        slot=slot.astype(jnp.int32), next_expert=next_e[te].astype(jnp.int32))


def kernel(x, norm1_g, w_in, lambda_q1, lambda_k1, lambda_q2, lambda_k2, subln_g, pool_w, pool_scale, w_out, norm2_g, w_coarse, b_coarse, w_fine, b_fine, w_gate, w_up, w_down, final_norm_g):
    b, s, d = x.shape
    t = b * s
    assert norm1_g.shape[0] == 1
    x2 = x.reshape(t, d)
    proj, pool, v_t = _inproj(x2, norm1_g[0], w_in[0], pool_w[0], pool_scale[0], s)
    proj3 = proj.reshape(b, s, proj.shape[1])
    attn = _attention(proj3, v_t, lambda_q1, lambda_k1, lambda_q2, lambda_k2, subln_g[0])

    ng, epg = N_EXPERT_GROUPS, EXPERTS_PER_GROUP
    wr = jnp.concatenate(
        [w_coarse[0], jnp.transpose(w_fine[0], (1, 0, 2)).reshape(d, ng * epg)], axis=1)
    wr = jnp.pad(wr, ((0, 0), (0, LANES - wr.shape[1])))
    br = jnp.concatenate([b_coarse[0], b_fine[0].reshape(ng * epg)])
    br = jnp.pad(br, (0, LANES - br.shape[0])).reshape(1, LANES)
    wr_hi = wr.astype(BF16)
    wr_lo = (wr - wr_hi.astype(F32)).astype(BF16)
    h, hn, ri, rw = _outproj(attn.reshape(t, ATTN_WIDTH), pool, x2,
                             w_out[0], norm2_g[0], jnp.concatenate([wr_hi, wr_lo], axis=1), br)

    plan = _sort_plan(ri)
    xs = _scatter_rows(plan, hn)
    ys = _experts(plan, xs, w_gate[0], w_up[0], w_down[0])
    out = _combine(plan["pos"], h, rw, ys, final_norm_g)
    return out.reshape(b, s, d)
```

```python
import functools
import math

import jax
import jax.numpy as jnp
from jax import lax
from jax.experimental import pallas as pl
from jax.experimental.pallas import tpu as pltpu

N_DIFF_HEADS = 8
DIFF_HEAD_DIM = 64
DIFF_V_DIM = 2 * DIFF_HEAD_DIM
ATTN_WIDTH = N_DIFF_HEADS * DIFF_V_DIM
POOL_WINDOWS = (2, 4, 8, 16)
POOL_GROUP_DIM = 256
POOL_WIDTH = len(POOL_WINDOWS) * POOL_GROUP_DIM
N_EXPERT_GROUPS = 4
EXPERTS_PER_GROUP = 8
N_EXPERTS = N_EXPERT_GROUPS * EXPERTS_PER_GROUP
TOP_K_FINE = 2
RMS_EPS = 1e-6
NEG_INF = -1e30
LAM_INIT = 0.8 - 0.6 * math.exp(-0.3 * 0)
LOG2E = math.log2(math.e)
Q_SCALE = DIFF_HEAD_DIM ** -0.5 * LOG2E

LANES = 128
SUBLANES = 8
VMEM_LIMIT = 56 * 1024 * 1024
ROW_TILE = 256
WEIGHT_SLOTS = 3

F32 = jnp.float32
BF16 = jnp.bfloat16


def _rms(x, g):
    return x * lax.rsqrt(jnp.mean(x * x, axis=-1, keepdims=True) + RMS_EPS) * g


def _load_weight_bf16(w_hbm, wb_ref, stage, sem):
    ck = stage.shape[1]
    nchunk = w_hbm.shape[0] // ck

    def copy(c):
        return pltpu.make_async_copy(w_hbm.at[c * ck:(c + 1) * ck, :], stage.at[c % 2], sem.at[c % 2])

    copy(0).start()
    for c in range(nchunk):
        if c + 1 < nchunk:
            copy(c + 1).start()
        copy(c).wait()
        wb_ref[c * ck:(c + 1) * ck, :] = stage[c % 2].astype(BF16)


def _pool_mix(u, prev, seq_pos, pw_ref, ps_ref):
    halo = prev.shape[0]
    c = POOL_GROUP_DIM
    ext = jnp.concatenate([prev, u], axis=0)
    outs = []
    for gi, win in enumerate(POOL_WINDOWS):
        assert win & (win - 1) == 0 and win <= halo
        run = ext[:, gi * c:(gi + 1) * c]
        span = 1
        while span < win:
            run = run + pltpu.roll(run, span, axis=0)
            span *= 2
        count = jnp.minimum(seq_pos + 1, win).astype(F32)
        pooled = run[halo:, :] / count - u[:, gi * c:(gi + 1) * c]
        mixed = jnp.dot(pooled.astype(BF16), pw_ref[gi].astype(BF16), preferred_element_type=F32)
        outs.append(mixed * ps_ref[:, gi * c:(gi + 1) * c])
    return outs


def _inproj_kernel(x_ref, g_ref, w_hbm, pw_ref, ps_ref, o_ref, pool_ref, vt_ref,
                   wb_ref, stage, sem, halo_ref, *, tn, seq_tiles):
    i = pl.program_id(0)
    tm = x_ref.shape[0]

    @pl.when(i == 0)
    def _():
        halo_ref[...] = jnp.zeros_like(halo_ref)
        _load_weight_bf16(w_hbm, wb_ref, stage, sem)

    xn = _rms(x_ref[...], g_ref[...]).astype(BF16)
    q = jnp.dot(xn, wb_ref[:, 0 * tn:1 * tn], preferred_element_type=F32) * Q_SCALE
    o_ref[:, 0 * tn:1 * tn] = q.astype(o_ref.dtype)
    k = jnp.dot(xn, wb_ref[:, 1 * tn:2 * tn], preferred_element_type=F32)
    o_ref[:, 1 * tn:2 * tn] = k.astype(o_ref.dtype)
    v = jnp.dot(xn, wb_ref[:, 2 * tn:3 * tn], preferred_element_type=F32)
    vt_ref[...] = v.T.astype(vt_ref.dtype)
    u = jnp.dot(xn, wb_ref[:, 3 * tn:4 * tn], preferred_element_type=F32)

    tile_in_seq = i % seq_tiles
    prev = jnp.where(tile_in_seq == 0, 0.0, halo_ref[...])
    halo_ref[...] = u[tm - halo_ref.shape[0]:, :]
    seq_pos = tile_in_seq * tm + lax.broadcasted_iota(jnp.int32, (tm, 1), 0)
    c = POOL_GROUP_DIM
    for gi, mixed in enumerate(_pool_mix(u, prev, seq_pos, pw_ref, ps_ref)):
        pool_ref[:, gi * c:(gi + 1) * c] = mixed.astype(pool_ref.dtype)


def _inproj(x2, g, w, pool_w, pool_scale, seq_len, tm=512, tn=ATTN_WIDTH, chunk_rows=256):
    t, d = x2.shape
    n = w.shape[1]
    assert n == 4 * tn and tn == POOL_WIDTH and seq_len % tm == 0
    halo = 2 * SUBLANES
    assert max(POOL_WINDOWS) <= halo
    return pl.pallas_call(
        functools.partial(_inproj_kernel, tn=tn, seq_tiles=seq_len // tm),
        grid=(t // tm,),
        in_specs=[
            pl.BlockSpec((tm, d), lambda i: (i, 0)),
            pl.BlockSpec((1, d), lambda i: (0, 0)),
            pl.BlockSpec(memory_space=pl.ANY),
            pl.BlockSpec(pool_w.shape, lambda i: (0, 0, 0)),
            pl.BlockSpec((1, tn), lambda i: (0, 0)),
        ],
        out_specs=[pl.BlockSpec((tm, 2 * tn), lambda i: (i, 0)),
                   pl.BlockSpec((tm, tn), lambda i: (i, 0)),
                   pl.BlockSpec((tn, tm), lambda i: (0, i))],
        out_shape=[jax.ShapeDtypeStruct((t, 2 * tn), BF16), jax.ShapeDtypeStruct((t, tn), BF16),
                   jax.ShapeDtypeStruct((tn, t), BF16)],
        scratch_shapes=[pltpu.VMEM((d, n), BF16), pltpu.VMEM((2, chunk_rows, n), F32),
                        pltpu.SemaphoreType.DMA((2,)), pltpu.VMEM((halo, tn), F32)],
        compiler_params=pltpu.CompilerParams(
            dimension_semantics=("arbitrary",), vmem_limit_bytes=VMEM_LIMIT),
        name="inproj",
    )(x2, g.reshape(1, d), w, pool_w, pool_scale.reshape(1, tn))


def _attn_kernel(lq1_ref, lk1_ref, lq2_ref, lk2_ref, q_ref, k_ref, v_ref, g_ref, o_ref,
                 tab_ref, vt_ref, *, tq):
    h = pl.program_id(0)
    s_len = q_ref.shape[0]
    nq = s_len // tq
    d, dv = DIFF_HEAD_DIM, DIFF_V_DIM

    @pl.when(pl.program_id(1) == 0)
    def _():
        slope = jnp.exp2(jnp.full((1, 1), -8.0 / N_DIFF_HEADS, F32) * (h + 1).astype(F32))
        rel = ((nq - 1) * tq + lax.broadcasted_iota(jnp.int32, (s_len, tq), 1)
               - lax.broadcasted_iota(jnp.int32, (s_len, tq), 0))
        tab_ref[...] = jnp.where(rel >= 0, (-LOG2E * slope) * rel.astype(F32), NEG_INF)

    vt_ref[:dv, :] = v_ref[...]
    vt_ref[dv:, :] = jnp.ones((vt_ref.shape[0] - dv, s_len), BF16)

    lam = (jnp.exp(jnp.sum(lq1_ref[...] * lk1_ref[...], axis=1, keepdims=True))
           - jnp.exp(jnp.sum(lq2_ref[...] * lk2_ref[...], axis=1, keepdims=True)) + LAM_INIT)
    dn = (((1,), (1,)), ((), ()))
    lane = lax.broadcasted_iota(jnp.int32, (tq, 2 * d), 1)

    def scores(qi):
        n = (qi + 1) * tq
        q = q_ref[qi * tq:(qi + 1) * tq, :]
        zero = jnp.zeros_like(q)
        qh = (jnp.where(lane < d, q, zero), jnp.where(lane >= d, q, zero))
        kk = k_ref[:n, :]
        bias = tab_ref[(nq - 1 - qi) * tq:(nq - 1 - qi) * tq + n, :]
        return [lax.dot_general(kk, qh[a], dn, preferred_element_type=F32) + bias
                for a in range(2)]

    ahead = 1
    pending = [scores(t) for t in range(min(ahead, nq))]
    for qi in range(nq):
        n = (qi + 1) * tq
        s_cur = pending.pop(0)
        if qi + ahead < nq:
            pending.append(scores(qi + ahead))
        outs = []
        for a in range(2):
            s = s_cur[a]
            m = jnp.max(s, axis=0, keepdims=True)
            p = jnp.exp2(s - m).astype(BF16)
            acc = jnp.dot(vt_ref[:, :n], p, preferred_element_type=F32)
            outs.append(acc[:dv] / acc[dv:dv + 1])
        ot = outs[0] - lam * outs[1]
        yt = ot * lax.rsqrt(jnp.mean(ot * ot, axis=0, keepdims=True) + RMS_EPS) * g_ref[...]
        o_ref[qi * tq:(qi + 1) * tq, :] = (yt * (1.0 - LAM_INIT)).T.astype(o_ref.dtype)


def _attention(proj3, v_t, lq1, lk1, lq2, lk2, subln_g, tq=256):
    b, s, _ = proj3.shape
    dv = DIFF_V_DIM
    nh = N_DIFF_HEADS
    ones_rows = 16
    lam_spec = pl.BlockSpec((1, DIFF_HEAD_DIM), lambda hi, bi: (0, 0))
    return pl.pallas_call(
        functools.partial(_attn_kernel, tq=tq),
        grid=(nh, b),
        in_specs=[
            lam_spec, lam_spec, lam_spec, lam_spec,
            pl.BlockSpec((None, s, dv), lambda hi, bi: (bi, 0, hi)),
            pl.BlockSpec((None, s, dv), lambda hi, bi: (bi, 0, nh + hi)),
            pl.BlockSpec((dv, s), lambda hi, bi: (hi, bi)),
            pl.BlockSpec((dv, 1), lambda hi, bi: (0, 0)),
        ],
        out_specs=pl.BlockSpec((None, s, dv), lambda hi, bi: (bi, 0, hi)),
        out_shape=jax.ShapeDtypeStruct((b, s, ATTN_WIDTH), BF16),
        scratch_shapes=[
            pltpu.VMEM((s, tq), F32),
            pltpu.VMEM((dv + ones_rows, s), BF16),
        ],
        compiler_params=pltpu.CompilerParams(
            dimension_semantics=("arbitrary", "arbitrary"),
            vmem_limit_bytes=VMEM_LIMIT),
        name="diff_attn",
    )(lq1, lk1, lq2, lk2, proj3, proj3, v_t, subln_g.reshape(dv, 1))


def _pack_bf16_pair(x):
    bits = lax.bitcast_convert_type(x.astype(BF16).astype(F32), jnp.uint32)
    half = bits.shape[1] // 2
    return bits[:, :half] | (bits[:, half:] >> 16)


def _outproj_kernel(a_ref, p_ref, x_ref, w_hbm, g_ref, wr_ref, br_ref,
                    h_ref, hn_ref, ri_ref, rw_ref, w_ref, stage, sem, *, n_sub):
    @pl.when(pl.program_id(0) == 0)
    def _():
        _load_weight_bf16(w_hbm, w_ref, stage, sem)

    ka = a_ref.shape[1]
    hs = x_ref.shape[0] // n_sub
    normed = []
    for r in range(n_sub):
        rows = pl.ds(r * hs, hs)
        mixed = (jnp.dot(a_ref[rows, :], w_ref[:ka, :], preferred_element_type=F32)
                 + jnp.dot(p_ref[rows, :], w_ref[ka:, :], preferred_element_type=F32))
        h = x_ref[rows, :] + mixed
        h_ref[rows, :] = h
        hn = _rms(h, g_ref[...])
        hn_ref[rows, :] = _pack_bf16_pair(hn)
        hn_hi = hn.astype(BF16)
        normed.append((hn_hi, (hn - hn_hi.astype(F32)).astype(BF16)))
    for r in range(n_sub):
        rows = pl.ds(r * hs, hs)
        ri, rw = _route(normed[r][0], normed[r][1], wr_ref, br_ref)
        ri_ref[rows, :] = ri
        rw_ref[rows, :] = rw


def _route(hn_hi, hn_lo, wr_ref, br_ref):
    both = jnp.dot(hn_hi, wr_ref[...], preferred_element_type=F32)
    logits = (both[:, :LANES] + both[:, LANES:]
              + jnp.dot(hn_lo, wr_ref[:, :LANES], preferred_element_type=F32)) + br_ref[...]
    ng, epg = N_EXPERT_GROUPS, EXPERTS_PER_GROUP
    lane = lax.broadcasted_iota(jnp.int32, logits.shape, 1)
    big = jnp.int32(LANES)
    low = jnp.float32(-3.0e38)
    cm = lane < ng
    c = jnp.where(cm, logits, low)
    cmax = jnp.max(c, axis=1, keepdims=True)
    gsel = jnp.min(jnp.where(c == cmax, lane, big), axis=1, keepdims=True)
    p_group = 1.0 / jnp.sum(jnp.where(cm, jnp.exp(c - cmax), 0.0), axis=1, keepdims=True)
    f_lo = ng + epg * gsel
    fm = (lane >= f_lo) & (lane < f_lo + epg)
    f = jnp.where(fm, logits, low)
    v1 = jnp.max(f, axis=1, keepdims=True)
    i1 = jnp.min(jnp.where(fm & (f == v1), lane, big), axis=1, keepdims=True)
    fm2 = fm & (lane != i1)
    f2 = jnp.where(fm2, logits, low)
    v2 = jnp.max(f2, axis=1, keepdims=True)
    i2 = jnp.min(jnp.where(fm2 & (f2 == v2), lane, big), axis=1, keepdims=True)
    e21 = jnp.exp(v2 - v1)
    w1 = p_group / (1.0 + e21)
    w2 = p_group * e21 / (1.0 + e21)
    return (jnp.where(lane == 0, i1 - ng, jnp.where(lane == 1, i2 - ng, 0)),
            jnp.where(lane == 0, w1, jnp.where(lane == 1, w2, 0.0)))


def _outproj(attn2, pool2, x2, w_out, g2, wr, br, tm=512, n_sub=2, chunk_rows=512):
    t, d = x2.shape
    ka, kp = attn2.shape[1], pool2.shape[1]
    row = lambda i: (i, 0)
    const = lambda i: (0, 0)
    return pl.pallas_call(
        functools.partial(_outproj_kernel, n_sub=n_sub),
        grid=(t // tm,),
        in_specs=[
            pl.BlockSpec((tm, ka), row),
            pl.BlockSpec((tm, kp), row),
            pl.BlockSpec((tm, d), row),
            pl.BlockSpec(memory_space=pl.ANY),
            pl.BlockSpec((1, d), const),
            pl.BlockSpec((d, 2 * LANES), const),
            pl.BlockSpec((1, LANES), const),
        ],
        out_specs=[
            pl.BlockSpec((tm, d), row),
            pl.BlockSpec((tm, d // 2), row),
            pl.BlockSpec((tm, LANES), row),
            pl.BlockSpec((tm, LANES), row),
        ],
        out_shape=[
            jax.ShapeDtypeStruct((t, d), F32),
            jax.ShapeDtypeStruct((t, d // 2), jnp.uint32),
            jax.ShapeDtypeStruct((t, LANES), jnp.int32),
            jax.ShapeDtypeStruct((t, LANES), F32),
        ],
        scratch_shapes=[pltpu.VMEM((ka + kp, d), BF16), pltpu.VMEM((2, chunk_rows, d), F32),
                        pltpu.SemaphoreType.DMA((2,))],
        compiler_params=pltpu.CompilerParams(
            dimension_semantics=("arbitrary",), vmem_limit_bytes=VMEM_LIMIT),
        name="outproj_router",
    )(attn2, pool2, x2, w_out, g2.reshape(1, d), wr, br)


def _unpack_bf16_pair(words):
    hi = lax.bitcast_convert_type(words & jnp.uint32(0xFFFF0000), F32).astype(BF16)
    lo = lax.bitcast_convert_type(words << 16, F32).astype(BF16)
    return hi, lo


def _expert_kernel(te_ref, nt_ref, first_ref, slot_ref, nxt_ref, nxt2_ref,
                   x_ref, wg_hbm, wu_hbm, wd_hbm, y_ref,
                   wgb, wub, wdb, wsem, *, tm, tiles_per_step):
    nt = nt_ref[0]
    n_slots = wgb.shape[0]
    kh = x_ref.shape[1]
    fh = wdb.shape[1] // 2

    def weight_copies(e, s):
        return ((pltpu.make_async_copy(wg_hbm.at[e], wgb.at[s], wsem.at[s]), 0),
                (pltpu.make_async_copy(wu_hbm.at[e], wub.at[s], wsem.at[s]), 1),
                (pltpu.make_async_copy(wd_hbm.at[e, :fh, :], wdb.at[s, :fh, :], wsem.at[s]), 0),
                (pltpu.make_async_copy(wd_hbm.at[e, fh:, :], wdb.at[s, fh:, :], wsem.at[s]), 1))

    @pl.when(pl.program_id(0) == 0)
    def _():
        for c, prio in weight_copies(te_ref[0], 0):
            c.start(priority=prio)

        @pl.when(nxt_ref[0] >= 0)
        def _():
            for c, prio in weight_copies(nxt_ref[0], 1):
                c.start(priority=prio)

    def tile(j, rows):
        @pl.when(j < nt)
        def _():
            ws = slot_ref[j]

            @pl.when(first_ref[j] == 1)
            def _():
                for c, _ in weight_copies(te_ref[j], ws):
                    c.wait()

                @pl.when(nxt2_ref[j] >= 0)
                def _():
                    for c, prio in weight_copies(nxt2_ref[j], (ws + 2) % n_slots):
                        c.start(priority=prio)

            xa, xb = _unpack_bf16_pair(x_ref[rows, :])
            hg = (jnp.dot(xa, wgb[ws, :kh, :].astype(BF16), preferred_element_type=F32)
                  + jnp.dot(xb, wgb[ws, kh:, :].astype(BF16), preferred_element_type=F32))
            hu = (jnp.dot(xa, wub[ws, :kh, :].astype(BF16), preferred_element_type=F32)
                  + jnp.dot(xb, wub[ws, kh:, :].astype(BF16), preferred_element_type=F32))
            act = (hg / (1.0 + jnp.exp(-hg))) * hu
            y_ref[rows, :] = _pack_bf16_pair(
                jnp.dot(act.astype(BF16), wdb[ws].astype(BF16), preferred_element_type=F32))

        @pl.when(j >= nt)
        def _():
            y_ref[rows, :] = jnp.zeros((tm, kh), y_ref.dtype)

    for sub in range(tiles_per_step):
        tile(pl.program_id(0) * tiles_per_step + sub, pl.ds(sub * tm, tm))


def _experts(plan, xs, w_gate, w_up, w_down, tm=ROW_TILE, tiles_per_step=2):
    d, f = w_gate.shape[1], w_gate.shape[2]
    assert xs.shape[1] * 2 == d
    nt = plan["tile_expert"].shape[0]
    assert nt % tiles_per_step == 0
    rows = tm * tiles_per_step
    any_spec = pl.BlockSpec(memory_space=pl.ANY)
    grid_spec = pltpu.PrefetchScalarGridSpec(
        num_scalar_prefetch=6,
        grid=(nt // tiles_per_step,),
        in_specs=[
            pl.BlockSpec((rows, d // 2),
                         lambda j, te, n, *_: (jnp.minimum(j, (n[0] - 1) // tiles_per_step), 0)),
            any_spec, any_spec, any_spec],
        out_specs=pl.BlockSpec((rows, d // 2), lambda j, *_: (j, 0)),
        scratch_shapes=[
            pltpu.VMEM((WEIGHT_SLOTS, d, f), F32),
            pltpu.VMEM((WEIGHT_SLOTS, d, f), F32),
            pltpu.VMEM((WEIGHT_SLOTS, f, d), F32),
            pltpu.SemaphoreType.DMA((WEIGHT_SLOTS,)),
        ],
    )
    return pl.pallas_call(
        functools.partial(_expert_kernel, tm=tm, tiles_per_step=tiles_per_step),
        grid_spec=grid_spec,
        out_shape=jax.ShapeDtypeStruct((nt * tm, d // 2), jnp.uint32),
        compiler_params=pltpu.CompilerParams(
            dimension_semantics=("arbitrary",), vmem_limit_bytes=VMEM_LIMIT),
        name="experts",
    )(plan["tile_expert"], plan["n_tiles"], plan["first"], plan["slot"], plan["next_expert"],
      plan["next2_expert"], xs, w_gate, w_up, w_down)


def _combine_kernel(pos_ref, h_ref, rw_ref, ys_hbm, g_ref, o_ref, ybuf0, ybuf1, ybuf2, sem):
    i = pl.program_id(0)
    last = pl.num_programs(0) - 1
    tm = h_ref.shape[0]
    kh = ybuf0.shape[2]
    bufs = (ybuf0, ybuf1, ybuf2)
    nbuf = len(bufs)

    def start_gather(tile, buf, sm):
        for r in range(tm):
            for k in range(TOP_K_FINE):
                p = pos_ref[(tile * tm + r) * TOP_K_FINE + k]
                pltpu.make_async_copy(ys_hbm.at[pl.ds(p, 1), :], buf.at[k, pl.ds(r, 1), :],
                                      sm).start(priority=k % 2)

    def wait_gather(buf, sm):
        for k in range(TOP_K_FINE):
            pltpu.make_async_copy(ys_hbm.at[pl.ds(0, tm), :], buf.at[k], sm).wait()

    @pl.when(i == 0)
    def _():
        start_gather(0, ybuf0, sem.at[0])
        start_gather(1, ybuf1, sem.at[1])

    def step(par):
        cur = bufs[par]
        ahead = (par + nbuf - 1) % nbuf
        wait_gather(cur, sem.at[par])
        start_gather(jnp.minimum(i + nbuf - 1, last), bufs[ahead], sem.at[ahead])
        w = rw_ref[...]
        halves = []
        for part in range(2):
            y = None
            for k in range(TOP_K_FINE):
                words = cur[k]
                bits = (words & jnp.uint32(0xFFFF0000)) if part == 0 else (words << 16)
                term = w[:, k:k + 1] * lax.bitcast_convert_type(bits, F32)
                y = term if y is None else y + term
            halves.append(h_ref[:, part * kh:(part + 1) * kh] + y)
        ms = sum(jnp.sum(v * v, axis=-1, keepdims=True) for v in halves) / (2 * kh)
        inv = lax.rsqrt(ms + RMS_EPS)
        for part in range(2):
            o_ref[:, part * kh:(part + 1) * kh] = (
                halves[part] * inv * g_ref[:, part * kh:(part + 1) * kh])

        @pl.when(i == last)
        def _():
            for other in range(nbuf):
                if other != par:
                    wait_gather(bufs[other], sem.at[other])

    for par in range(nbuf):
        pl.when(i % nbuf == par)(functools.partial(step, par))


def _combine(pos, h, rw, ys, g, tm=256):
    t, d = h.shape
    grid_spec = pltpu.PrefetchScalarGridSpec(
        num_scalar_prefetch=1,
        grid=(t // tm,),
        in_specs=[
            pl.BlockSpec((tm, d), lambda i, p: (i, 0)),
            pl.BlockSpec((tm, LANES), lambda i, p: (i, 0)),
            pl.BlockSpec(memory_space=pl.ANY),
            pl.BlockSpec((1, d), lambda i, p: (0, 0)),
        ],
        out_specs=pl.BlockSpec((tm, d), lambda i, p: (i, 0)),
        scratch_shapes=[pltpu.VMEM((TOP_K_FINE, tm, d // 2), jnp.uint32),
                        pltpu.VMEM((TOP_K_FINE, tm, d // 2), jnp.uint32),
                        pltpu.VMEM((TOP_K_FINE, tm, d // 2), jnp.uint32),
                        pltpu.SemaphoreType.DMA((3,))],
    )
    return pl.pallas_call(
        _combine_kernel,
        grid_spec=grid_spec,
        out_shape=jax.ShapeDtypeStruct((t, d), F32),
        compiler_params=pltpu.CompilerParams(
            dimension_semantics=("arbitrary",), vmem_limit_bytes=VMEM_LIMIT),
        name="combine",
    )(pos, h, rw, ys, g.reshape(1, d))


def _rank_kernel(ri_ref, meta_ref, cnt_ref, carry_ref, tri_ref):
    tt = ri_ref.shape[0]
    ne = cnt_ref.shape[0]

    @pl.when(pl.program_id(0) == 0)
    def _():
        carry_ref[...] = jnp.zeros_like(carry_ref)
        earlier = (lax.broadcasted_iota(jnp.int32, (tt, tt), 0)
                   < lax.broadcasted_iota(jnp.int32, (tt, tt), 1))
        tri_ref[...] = jnp.where(earlier, 1.0, 0.0).astype(BF16)

    rit = ri_ref[...].astype(F32).T
    expert = lax.broadcasted_iota(jnp.int32, (ne, tt), 0).astype(F32)
    hits = []
    for k in range(TOP_K_FINE):
        hits.append(expert == rit[k:k + 1, :])
    chosen = jnp.where(hits[0] | hits[1], 1.0, 0.0)
    before = (jnp.dot(chosen.astype(BF16), tri_ref[...], preferred_element_type=F32)
              + carry_ref[:, 0:1])
    for k in range(TOP_K_FINE):
        meta_ref[k:k + 1, :] = rit[k:k + 1, :].astype(jnp.int32)
        rank = jnp.sum(jnp.where(hits[k], before, 0.0), axis=0, keepdims=True)
        meta_ref[TOP_K_FINE + k:TOP_K_FINE + k + 1, :] = rank.astype(jnp.int32)
    meta_ref[2 * TOP_K_FINE:, :] = jnp.zeros((meta_ref.shape[0] - 2 * TOP_K_FINE, tt), jnp.int32)
    carry_ref[...] = carry_ref[...] + jnp.sum(chosen, axis=1, keepdims=True)
    cnt_ref[...] = carry_ref[...].astype(jnp.int32)


def _rank(ri, tt=1024):
    t = ri.shape[0]
    assert TOP_K_FINE == 2
    return pl.pallas_call(
        _rank_kernel,
        grid=(t // tt,),
        in_specs=[pl.BlockSpec((tt, LANES), lambda i: (i, 0))],
        out_specs=[pl.BlockSpec((None, 8, tt), lambda i: (i, 0, 0)),
                   pl.BlockSpec((N_EXPERTS, LANES), lambda i: (0, 0))],
        out_shape=[jax.ShapeDtypeStruct((t // tt, 8, tt), jnp.int32),
                   jax.ShapeDtypeStruct((N_EXPERTS, LANES), jnp.int32)],
        scratch_shapes=[pltpu.VMEM((N_EXPERTS, LANES), F32), pltpu.VMEM((tt, tt), BF16)],
        compiler_params=pltpu.CompilerParams(
            dimension_semantics=("arbitrary",), vmem_limit_bytes=VMEM_LIMIT),
        name="route_rank",
    )(ri)


def _scatter_rows_kernel(pos_ref, end_ref, nt_ref, src_hbm, dst_hbm, zbuf, st0, st1, st2,
                         sem, lsem, zsem, *, tm, row_tile):
    i = pl.program_id(0)
    stage = (st0, st1, st2)
    n_tiles_max = dst_hbm.shape[0] // row_tile

    def zero_copy(tile):
        row0 = pl.multiple_of(tile * row_tile, row_tile)
        return pltpu.make_async_copy(zbuf, dst_hbm.at[pl.ds(row0, row_tile), :], zsem)

    @pl.when(i == 0)
    def _():
        zbuf[...] = jnp.zeros_like(zbuf)
        n_exp = end_ref.shape[0]

        def has_rows(e):
            return end_ref[e] > (end_ref[e - 1] if e else 0)

        for e in range(n_exp):
            pl.when(has_rows(e))(lambda e=e: zero_copy(end_ref[e] - 1).start())

        def start_tail(tile, c):
            zero_copy(tile).start()
            return c

        def wait_tail(tile, c):
            zero_copy(tile).wait()
            return c

        lax.fori_loop(nt_ref[0], n_tiles_max, start_tail, 0)
        for e in range(n_exp):
            pl.when(has_rows(e))(lambda: zero_copy(0).wait())
        lax.fori_loop(nt_ref[0], n_tiles_max, wait_tail, 0)

    last = pl.num_programs(0) - 1
    nbuf = len(stage)

    def load(tile, b):
        row0 = pl.multiple_of(tile * tm, tm)
        return pltpu.make_async_copy(src_hbm.at[pl.ds(row0, tm), :], stage[b], lsem.at[b])

    def wait_rows(b):
        for _ in range(TOP_K_FINE):
            pltpu.make_async_copy(stage[b], dst_hbm.at[pl.ds(0, tm), :], sem.at[b]).wait()

    @pl.when(i == 0)
    def _():
        load(0, 0).start()

    def step(b):
        nb = (b + 1) % nbuf

        @pl.when(i >= nbuf - 1)
        def _():
            wait_rows(nb)

        @pl.when(i < last)
        def _():
            load(i + 1, nb).start()

        load(i, b).wait()
        for r in range(tm):
            for k in range(TOP_K_FINE):
                p = pos_ref[(i * tm + r) * TOP_K_FINE + k]
                pltpu.make_async_copy(stage[b].at[pl.ds(r, 1), :], dst_hbm.at[pl.ds(p, 1), :],
                                      sem.at[b]).start(priority=k % 2)

        @pl.when(i == last)
        def _():
            for d in range(nbuf - 1):
                @pl.when(i >= d)
                def _(d=d):
                    wait_rows((b - d) % nbuf)

    for b in range(nbuf):
        pl.when(i % nbuf == b)(functools.partial(step, b))


def _scatter_rows(plan, src, tm=256, row_tile=ROW_TILE):
    t, w = src.shape
    any_spec = pl.BlockSpec(memory_space=pl.ANY)
    grid_spec = pltpu.PrefetchScalarGridSpec(
        num_scalar_prefetch=3,
        grid=(t // tm,),
        in_specs=[any_spec],
        out_specs=any_spec,
        scratch_shapes=[pltpu.VMEM((row_tile, w), src.dtype),
                        pltpu.VMEM((tm, w), src.dtype), pltpu.VMEM((tm, w), src.dtype),
                        pltpu.VMEM((tm, w), src.dtype),
                        pltpu.SemaphoreType.DMA((3,)), pltpu.SemaphoreType.DMA((3,)),
                        pltpu.SemaphoreType.DMA(())],
    )
    return pl.pallas_call(
        functools.partial(_scatter_rows_kernel, tm=tm, row_tile=row_tile),
        grid_spec=grid_spec,
        out_shape=jax.ShapeDtypeStruct((plan["n_rows"], w), src.dtype),
        compiler_params=pltpu.CompilerParams(dimension_semantics=("arbitrary",)),
        name="scatter_rows",
    )(plan["pos"], plan["end_tile"], plan["n_tiles"], src)


def _sort_plan(ri, tm=ROW_TILE):
    t = ri.shape[0]
    k = TOP_K_FINE
    a = t * k
    nt = a // tm + N_EXPERTS
    meta, cnt = _rank(ri)
    counts = cnt[:, 0]
    padded = ((counts + tm - 1) // tm) * tm
    off_end = jnp.cumsum(padded)
    off = off_end - padded
    ef = meta[:, 0:k, :]
    eid = jnp.arange(N_EXPERTS, dtype=jnp.int32)
    pos = jnp.sum(jnp.where(ef[..., None] == eid, off, 0), axis=-1) + meta[:, k:2 * k, :]
    pos = jnp.transpose(pos, (0, 2, 1)).reshape(a)
    n_tiles = off_end[-1] // tm
    tile_idx = jnp.arange(nt, dtype=jnp.int32)
    te = jnp.sum((off_end[None, :] <= (tile_idx * tm)[:, None]).astype(jnp.int32), axis=1)
    te = jnp.minimum(te, N_EXPERTS - 1)
    te = jnp.where(tile_idx < n_tiles, te, te[n_tiles - 1]).astype(jnp.int32)
    first = ((tile_idx == 0) | (te != jnp.roll(te, 1))) & (tile_idx < n_tiles)
    slot = (jnp.cumsum(first.astype(jnp.int32)) - 1) % WEIGHT_SLOTS
    eid = jnp.arange(N_EXPERTS, dtype=jnp.int32)
    later = (eid[None, :] > eid[:, None]) & (counts[None, :] > 0)
    next_e = jnp.min(jnp.where(later, eid[None, :], N_EXPERTS), axis=1)
    next_e = jnp.where(next_e < N_EXPERTS, next_e, -1)
    next2_e = jnp.where(next_e >= 0, next_e[jnp.maximum(next_e, 0)], -1)
    return dict(
        next2_expert=next2_e[te].astype(jnp.int32),
        end_tile=(off_end // tm).astype(jnp.int32),
        pos=pos.astype(jnp.int32), n_rows=nt * tm, tile_expert=te,
        n_tiles=n_tiles.reshape(1).astype(jnp.int32), first=first.astype(jnp.int32),
        slot=slot.astype(jnp.int32), next_expert=next_e[te].astype(jnp.int32))


def kernel(x, norm1_g, w_in, lambda_q1, lambda_k1, lambda_q2, lambda_k2, subln_g, pool_w, pool_scale, w_out, norm2_g, w_coarse, b_coarse, w_fine, b_fine, w_gate, w_up, w_down, final_norm_g):
    b, s, d = x.shape
    t = b * s
    assert norm1_g.shape[0] == 1
    x2 = x.reshape(t, d)
    proj, pool, v_t = _inproj(x2, norm1_g[0], w_in[0], pool_w[0], pool_scale[0], s)
    proj3 = proj.reshape(b, s, proj.shape[1])
    attn = _attention(proj3, v_t, lambda_q1, lambda_k1, lambda_q2, lambda_k2, subln_g[0])

    ng, epg = N_EXPERT_GROUPS, EXPERTS_PER_GROUP
    wr = jnp.concatenate(
        [w_coarse[0], jnp.transpose(w_fine[0], (1, 0, 2)).reshape(d, ng * epg)], axis=1)
    wr = jnp.pad(wr, ((0, 0), (0, LANES - wr.shape[1])))
    br = jnp.concatenate([b_coarse[0], b_fine[0].reshape(ng * epg)])
    br = jnp.pad(br, (0, LANES - br.shape[0])).reshape(1, LANES)
    wr_hi = wr.astype(BF16)
    wr_lo = (wr - wr_hi.astype(F32)).astype(BF16)
    h, hn, ri, rw = _outproj(attn.reshape(t, ATTN_WIDTH), pool, x2,
                             w_out[0], norm2_g[0], jnp.concatenate([wr_hi, wr_lo], axis=1), br)

    plan = _sort_plan(ri)
    xs = _scatter_rows(plan, hn)
    ys = _experts(plan, xs, w_gate[0], w_up[0], w_down[0])
    out = _combine(plan["pos"], h, rw, ys, final_norm_g)
    return out.reshape(b, s, d)
```

```python
import functools
import math

import jax
import jax.numpy as jnp
from jax import lax
from jax.experimental import pallas as pl
from jax.experimental.pallas import tpu as pltpu

N_DIFF_HEADS = 8
DIFF_HEAD_DIM = 64
DIFF_V_DIM = 2 * DIFF_HEAD_DIM
ATTN_WIDTH = N_DIFF_HEADS * DIFF_V_DIM
POOL_WINDOWS = (2, 4, 8, 16)
POOL_GROUP_DIM = 256
POOL_WIDTH = len(POOL_WINDOWS) * POOL_GROUP_DIM
N_EXPERT_GROUPS = 4
EXPERTS_PER_GROUP = 8
N_EXPERTS = N_EXPERT_GROUPS * EXPERTS_PER_GROUP
TOP_K_FINE = 2
RMS_EPS = 1e-6
NEG_INF = -1e30
LAM_INIT = 0.8 - 0.6 * math.exp(-0.3 * 0)
LOG2E = math.log2(math.e)
Q_SCALE = DIFF_HEAD_DIM ** -0.5 * LOG2E

LANES = 128
SUBLANES = 8
VMEM_LIMIT = 56 * 1024 * 1024
ROW_TILE = 256
WEIGHT_SLOTS = 3

F32 = jnp.float32
BF16 = jnp.bfloat16


def _rms(x, g):
    return x * lax.rsqrt(jnp.mean(x * x, axis=-1, keepdims=True) + RMS_EPS) * g


def _load_weight_bf16(w_hbm, wb_ref, stage, sem):
    ck = stage.shape[1]
    nchunk = w_hbm.shape[0] // ck

    def copy(c):
        return pltpu.make_async_copy(w_hbm.at[c * ck:(c + 1) * ck, :], stage.at[c % 2], sem.at[c % 2])

    copy(0).start()
    for c in range(nchunk):
        if c + 1 < nchunk:
            copy(c + 1).start()
        copy(c).wait()
        wb_ref[c * ck:(c + 1) * ck, :] = stage[c % 2].astype(BF16)


def _pool_mix(u, prev, seq_pos, pw_ref, ps_ref):
    halo = prev.shape[0]
    c = POOL_GROUP_DIM
    ext = jnp.concatenate([prev, u], axis=0)
    outs = []
    for gi, win in enumerate(POOL_WINDOWS):
        assert win & (win - 1) == 0 and win <= halo
        run = ext[:, gi * c:(gi + 1) * c]
        span = 1
        while span < win:
            run = run + pltpu.roll(run, span, axis=0)
            span *= 2
        count = jnp.minimum(seq_pos + 1, win).astype(F32)
        pooled = run[halo:, :] / count - u[:, gi * c:(gi + 1) * c]
        mixed = jnp.dot(pooled.astype(BF16), pw_ref[gi].astype(BF16), preferred_element_type=F32)
        outs.append(mixed * ps_ref[:, gi * c:(gi + 1) * c])
    return outs


def _inproj_kernel(x_ref, g_ref, w_hbm, pw_ref, ps_ref, o_ref, pool_ref, vt_ref,
                   wb_ref, stage, sem, halo_ref, *, tn, seq_tiles):
    i = pl.program_id(0)
    tm = x_ref.shape[0]

    @pl.when(i == 0)
    def _():
        halo_ref[...] = jnp.zeros_like(halo_ref)
        _load_weight_bf16(w_hbm, wb_ref, stage, sem)

    xn = _rms(x_ref[...], g_ref[...]).astype(BF16)
    q = jnp.dot(xn, wb_ref[:, 0 * tn:1 * tn], preferred_element_type=F32) * Q_SCALE
    o_ref[:, 0 * tn:1 * tn] = q.astype(o_ref.dtype)
    k = jnp.dot(xn, wb_ref[:, 1 * tn:2 * tn], preferred_element_type=F32)
    o_ref[:, 1 * tn:2 * tn] = k.astype(o_ref.dtype)
    v = jnp.dot(xn, wb_ref[:, 2 * tn:3 * tn], preferred_element_type=F32)
    vt_ref[...] = v.T.astype(vt_ref.dtype)
    u = jnp.dot(xn, wb_ref[:, 3 * tn:4 * tn], preferred_element_type=F32)

    tile_in_seq = i % seq_tiles
    prev = jnp.where(tile_in_seq == 0, 0.0, halo_ref[...])
    halo_ref[...] = u[tm - halo_ref.shape[0]:, :]
    seq_pos = tile_in_seq * tm + lax.broadcasted_iota(jnp.int32, (tm, 1), 0)
    c = POOL_GROUP_DIM
    for gi, mixed in enumerate(_pool_mix(u, prev, seq_pos, pw_ref, ps_ref)):
        pool_ref[:, gi * c:(gi + 1) * c] = mixed.astype(pool_ref.dtype)


def _inproj(x2, g, w, pool_w, pool_scale, seq_len, tm=512, tn=ATTN_WIDTH, chunk_rows=256):
    t, d = x2.shape
    n = w.shape[1]
    assert n == 4 * tn and tn == POOL_WIDTH and seq_len % tm == 0
    halo = 2 * SUBLANES
    assert max(POOL_WINDOWS) <= halo
    return pl.pallas_call(
        functools.partial(_inproj_kernel, tn=tn, seq_tiles=seq_len // tm),
        grid=(t // tm,),
        in_specs=[
            pl.BlockSpec((tm, d), lambda i: (i, 0)),
            pl.BlockSpec((1, d), lambda i: (0, 0)),
            pl.BlockSpec(memory_space=pl.ANY),
            pl.BlockSpec(pool_w.shape, lambda i: (0, 0, 0)),
            pl.BlockSpec((1, tn), lambda i: (0, 0)),
        ],
        out_specs=[pl.BlockSpec((tm, 2 * tn), lambda i: (i, 0)),
                   pl.BlockSpec((tm, tn), lambda i: (i, 0)),
                   pl.BlockSpec((tn, tm), lambda i: (0, i))],
        out_shape=[jax.ShapeDtypeStruct((t, 2 * tn), BF16), jax.ShapeDtypeStruct((t, tn), BF16),
                   jax.ShapeDtypeStruct((tn, t), BF16)],
        scratch_shapes=[pltpu.VMEM((d, n), BF16), pltpu.VMEM((2, chunk_rows, n), F32),
                        pltpu.SemaphoreType.DMA((2,)), pltpu.VMEM((halo, tn), F32)],
        compiler_params=pltpu.CompilerParams(
            dimension_semantics=("arbitrary",), vmem_limit_bytes=VMEM_LIMIT),
        name="inproj",
    )(x2, g.reshape(1, d), w, pool_w, pool_scale.reshape(1, tn))


def _attn_kernel(lq1_ref, lk1_ref, lq2_ref, lk2_ref, q_ref, k_ref, v_ref, g_ref, o_ref,
                 tab_ref, vt_ref, *, tq):
    h = pl.program_id(0)
    s_len = q_ref.shape[0]
    nq = s_len // tq
    d, dv = DIFF_HEAD_DIM, DIFF_V_DIM

    @pl.when(pl.program_id(1) == 0)
    def _():
        slope = jnp.exp2(jnp.full((1, 1), -8.0 / N_DIFF_HEADS, F32) * (h + 1).astype(F32))
        rel = ((nq - 1) * tq + lax.broadcasted_iota(jnp.int32, (s_len, tq), 1)
               - lax.broadcasted_iota(jnp.int32, (s_len, tq), 0))
        tab_ref[...] = jnp.where(rel >= 0, (-LOG2E * slope) * rel.astype(F32), NEG_INF)

    vt_ref[:dv, :] = v_ref[...]
    vt_ref[dv:, :] = jnp.ones((vt_ref.shape[0] - dv, s_len), BF16)

    lam = (jnp.exp(jnp.sum(lq1_ref[...] * lk1_ref[...], axis=1, keepdims=True))
           - jnp.exp(jnp.sum(lq2_ref[...] * lk2_ref[...], axis=1, keepdims=True)) + LAM_INIT)
    dn = (((1,), (1,)), ((), ()))
    lane = lax.broadcasted_iota(jnp.int32, (tq, 2 * d), 1)

    ck = 4 * tq

    def key_chunks(qi):
        n = (qi + 1) * tq
        return [(lo, min(lo + ck, n)) for lo in range(0, n, ck)]

    def query_halves(qi):
        q = q_ref[qi * tq:(qi + 1) * tq, :]
        zero = jnp.zeros_like(q)
        return (jnp.where(lane < d, q, zero), jnp.where(lane >= d, q, zero))

    def score_chunk(qi, qh, lo, hi):
        base = (nq - 1 - qi) * tq
        bias = tab_ref[base + lo:base + hi, :]
        return [lax.dot_general(k_ref[lo:hi, :], qh[a], dn, preferred_element_type=F32) + bias
                for a in range(2)]

    s_next = [score_chunk(0, query_halves(0), lo, hi) for lo, hi in key_chunks(0)]
    for qi in range(nq):
        s_cur = s_next
        cur_chunks = key_chunks(qi)
        m = [functools.reduce(jnp.maximum, [jnp.max(c[a], axis=0, keepdims=True) for c in s_cur])
             for a in range(2)]
        nxt_chunks = key_chunks(qi + 1) if qi + 1 < nq else []
        qh_next = query_halves(qi + 1) if nxt_chunks else None
        s_next = []
        acc = [None, None]
        for c in range(max(len(cur_chunks), len(nxt_chunks))):
            if c < len(nxt_chunks):
                s_next.append(score_chunk(qi + 1, qh_next, *nxt_chunks[c]))
            if c < len(cur_chunks):
                lo, hi = cur_chunks[c]
                for a in range(2):
                    p = jnp.exp2(s_cur[c][a] - m[a]).astype(BF16)
                    part = jnp.dot(vt_ref[:, lo:hi], p, preferred_element_type=F32)
                    acc[a] = part if acc[a] is None else acc[a] + part
        outs = [acc[a][:dv] / acc[a][dv:dv + 1] for a in range(2)]
        ot = outs[0] - lam * outs[1]
        yt = ot * lax.rsqrt(jnp.mean(ot * ot, axis=0, keepdims=True) + RMS_EPS) * g_ref[...]
        o_ref[qi * tq:(qi + 1) * tq, :] = (yt * (1.0 - LAM_INIT)).T.astype(o_ref.dtype)


def _attention(proj3, v_t, lq1, lk1, lq2, lk2, subln_g, tq=256):
    b, s, _ = proj3.shape
    dv = DIFF_V_DIM
    nh = N_DIFF_HEADS
    ones_rows = 16
    lam_spec = pl.BlockSpec((1, DIFF_HEAD_DIM), lambda hi, bi: (0, 0))
    return pl.pallas_call(
        functools.partial(_attn_kernel, tq=tq),
        grid=(nh, b),
        in_specs=[
            lam_spec, lam_spec, lam_spec, lam_spec,
            pl.BlockSpec((None, s, dv), lambda hi, bi: (bi, 0, hi)),
            pl.BlockSpec((None, s, dv), lambda hi, bi: (bi, 0, nh + hi)),
            pl.BlockSpec((dv, s), lambda hi, bi: (hi, bi)),
            pl.BlockSpec((dv, 1), lambda hi, bi: (0, 0)),
        ],
        out_specs=pl.BlockSpec((None, s, dv), lambda hi, bi: (bi, 0, hi)),
        out_shape=jax.ShapeDtypeStruct((b, s, ATTN_WIDTH), BF16),
        scratch_shapes=[
            pltpu.VMEM((s, tq), F32),
            pltpu.VMEM((dv + ones_rows, s), BF16),
        ],
        compiler_params=pltpu.CompilerParams(
            dimension_semantics=("arbitrary", "arbitrary"),
            vmem_limit_bytes=VMEM_LIMIT),
        name="diff_attn",
    )(lq1, lk1, lq2, lk2, proj3, proj3, v_t, subln_g.reshape(dv, 1))


def _pack_bf16_pair(x):
    bits = lax.bitcast_convert_type(x.astype(BF16).astype(F32), jnp.uint32)
    half = bits.shape[1] // 2
    return bits[:, :half] | (bits[:, half:] >> 16)


def _outproj_kernel(a_ref, p_ref, x_ref, w_hbm, g_ref, wr_ref, br_ref,
                    h_ref, hn_ref, ri_ref, rw_ref, w_ref, stage, sem, *, n_sub):
    @pl.when(pl.program_id(0) == 0)
    def _():
        _load_weight_bf16(w_hbm, w_ref, stage, sem)

    ka = a_ref.shape[1]
    hs = x_ref.shape[0] // n_sub
    normed = []
    for r in range(n_sub):
        rows = pl.ds(r * hs, hs)
        mixed = (jnp.dot(a_ref[rows, :], w_ref[:ka, :], preferred_element_type=F32)
                 + jnp.dot(p_ref[rows, :], w_ref[ka:, :], preferred_element_type=F32))
        h = x_ref[rows, :] + mixed
        h_ref[rows, :] = h
        hn = _rms(h, g_ref[...])
        hn_ref[rows, :] = _pack_bf16_pair(hn)
        hn_hi = hn.astype(BF16)
        normed.append((hn_hi, (hn - hn_hi.astype(F32)).astype(BF16)))
    for r in range(n_sub):
        rows = pl.ds(r * hs, hs)
        ri, rw = _route(normed[r][0], normed[r][1], wr_ref, br_ref)
        ri_ref[rows, :] = ri
        rw_ref[rows, :] = rw


def _route(hn_hi, hn_lo, wr_ref, br_ref):
    both = jnp.dot(hn_hi, wr_ref[...], preferred_element_type=F32)
    logits = (both[:, :LANES] + both[:, LANES:]
              + jnp.dot(hn_lo, wr_ref[:, :LANES], preferred_element_type=F32)) + br_ref[...]
    ng, epg = N_EXPERT_GROUPS, EXPERTS_PER_GROUP
    lane = lax.broadcasted_iota(jnp.int32, logits.shape, 1)
    big = jnp.int32(LANES)
    low = jnp.float32(-3.0e38)
    cm = lane < ng
    c = jnp.where(cm, logits, low)
    cmax = jnp.max(c, axis=1, keepdims=True)
    gsel = jnp.min(jnp.where(c == cmax, lane, big), axis=1, keepdims=True)
    p_group = 1.0 / jnp.sum(jnp.where(cm, jnp.exp(c - cmax), 0.0), axis=1, keepdims=True)
    f_lo = ng + epg * gsel
    fm = (lane >= f_lo) & (lane < f_lo + epg)
    f = jnp.where(fm, logits, low)
    v1 = jnp.max(f, axis=1, keepdims=True)
    i1 = jnp.min(jnp.where(fm & (f == v1), lane, big), axis=1, keepdims=True)
    fm2 = fm & (lane != i1)
    f2 = jnp.where(fm2, logits, low)
    v2 = jnp.max(f2, axis=1, keepdims=True)
    i2 = jnp.min(jnp.where(fm2 & (f2 == v2), lane, big), axis=1, keepdims=True)
    e21 = jnp.exp(v2 - v1)
    w1 = p_group / (1.0 + e21)
    w2 = p_group * e21 / (1.0 + e21)
    return (jnp.where(lane == 0, i1 - ng, jnp.where(lane == 1, i2 - ng, 0)),
            jnp.where(lane == 0, w1, jnp.where(lane == 1, w2, 0.0)))


def _outproj(attn2, pool2, x2, w_out, g2, wr, br, tm=512, n_sub=2, chunk_rows=512):
    t, d = x2.shape
    ka, kp = attn2.shape[1], pool2.shape[1]
    row = lambda i: (i, 0)
    const = lambda i: (0, 0)
    return pl.pallas_call(
        functools.partial(_outproj_kernel, n_sub=n_sub),
        grid=(t // tm,),
        in_specs=[
            pl.BlockSpec((tm, ka), row),
            pl.BlockSpec((tm, kp), row),
            pl.BlockSpec((tm, d), row),
            pl.BlockSpec(memory_space=pl.ANY),
            pl.BlockSpec((1, d), const),
            pl.BlockSpec((d, 2 * LANES), const),
            pl.BlockSpec((1, LANES), const),
        ],
        out_specs=[
            pl.BlockSpec((tm, d), row),
            pl.BlockSpec((tm, d // 2), row),
            pl.BlockSpec((tm, LANES), row),
            pl.BlockSpec((tm, LANES), row),
        ],
        out_shape=[
            jax.ShapeDtypeStruct((t, d), F32),
            jax.ShapeDtypeStruct((t, d // 2), jnp.uint32),
            jax.ShapeDtypeStruct((t, LANES), jnp.int32),
            jax.ShapeDtypeStruct((t, LANES), F32),
        ],
        scratch_shapes=[pltpu.VMEM((ka + kp, d), BF16), pltpu.VMEM((2, chunk_rows, d), F32),
                        pltpu.SemaphoreType.DMA((2,))],
        compiler_params=pltpu.CompilerParams(
            dimension_semantics=("arbitrary",), vmem_limit_bytes=VMEM_LIMIT),
        name="outproj_router",
    )(attn2, pool2, x2, w_out, g2.reshape(1, d), wr, br)


def _unpack_bf16_pair(words):
    hi = lax.bitcast_convert_type(words & jnp.uint32(0xFFFF0000), F32).astype(BF16)
    lo = lax.bitcast_convert_type(words << 16, F32).astype(BF16)
    return hi, lo


def _expert_kernel(te_ref, nt_ref, first_ref, slot_ref, nxt_ref, nxt2_ref,
                   x_ref, wg_hbm, wu_hbm, wd_hbm, y_ref,
                   wgb, wub, wdb, wsem, *, tm, tiles_per_step):
    nt = nt_ref[0]
    n_slots = wgb.shape[0]
    kh = x_ref.shape[1]
    fh = wdb.shape[1] // 2

    def weight_copies(e, s):
        return ((pltpu.make_async_copy(wg_hbm.at[e], wgb.at[s], wsem.at[s]), 0),
                (pltpu.make_async_copy(wu_hbm.at[e], wub.at[s], wsem.at[s]), 1),
                (pltpu.make_async_copy(wd_hbm.at[e, :fh, :], wdb.at[s, :fh, :], wsem.at[s]), 0),
                (pltpu.make_async_copy(wd_hbm.at[e, fh:, :], wdb.at[s, fh:, :], wsem.at[s]), 1))

    @pl.when(pl.program_id(0) == 0)
    def _():
        for c, prio in weight_copies(te_ref[0], 0):
            c.start(priority=prio)

        @pl.when(nxt_ref[0] >= 0)
        def _():
            for c, prio in weight_copies(nxt_ref[0], 1):
                c.start(priority=prio)

    def tile(j, rows):
        @pl.when(j < nt)
        def _():
            ws = slot_ref[j]

            @pl.when(first_ref[j] == 1)
            def _():
                for c, _ in weight_copies(te_ref[j], ws):
                    c.wait()

                @pl.when(nxt2_ref[j] >= 0)
                def _():
                    for c, prio in weight_copies(nxt2_ref[j], (ws + 2) % n_slots):
                        c.start(priority=prio)

            xa, xb = _unpack_bf16_pair(x_ref[rows, :])
            hg = (jnp.dot(xa, wgb[ws, :kh, :].astype(BF16), preferred_element_type=F32)
                  + jnp.dot(xb, wgb[ws, kh:, :].astype(BF16), preferred_element_type=F32))
            hu = (jnp.dot(xa, wub[ws, :kh, :].astype(BF16), preferred_element_type=F32)
                  + jnp.dot(xb, wub[ws, kh:, :].astype(BF16), preferred_element_type=F32))
            act = (hg / (1.0 + jnp.exp(-hg))) * hu
            y_ref[rows, :] = _pack_bf16_pair(
                jnp.dot(act.astype(BF16), wdb[ws].astype(BF16), preferred_element_type=F32))

        @pl.when(j >= nt)
        def _():
            y_ref[rows, :] = jnp.zeros((tm, kh), y_ref.dtype)

    for sub in range(tiles_per_step):
        tile(pl.program_id(0) * tiles_per_step + sub, pl.ds(sub * tm, tm))


def _experts(plan, xs, w_gate, w_up, w_down, tm=ROW_TILE, tiles_per_step=2):
    d, f = w_gate.shape[1], w_gate.shape[2]
    assert xs.shape[1] * 2 == d
    nt = plan["tile_expert"].shape[0]
    assert nt % tiles_per_step == 0
    rows = tm * tiles_per_step
    any_spec = pl.BlockSpec(memory_space=pl.ANY)
    grid_spec = pltpu.PrefetchScalarGridSpec(
        num_scalar_prefetch=6,
        grid=(nt // tiles_per_step,),
        in_specs=[
            pl.BlockSpec((rows, d // 2),
                         lambda j, te, n, *_: (jnp.minimum(j, (n[0] - 1) // tiles_per_step), 0)),
            any_spec, any_spec, any_spec],
        out_specs=pl.BlockSpec((rows, d // 2), lambda j, *_: (j, 0)),
        scratch_shapes=[
            pltpu.VMEM((WEIGHT_SLOTS, d, f), F32),
            pltpu.VMEM((WEIGHT_SLOTS, d, f), F32),
            pltpu.VMEM((WEIGHT_SLOTS, f, d), F32),
            pltpu.SemaphoreType.DMA((WEIGHT_SLOTS,)),
        ],
    )
    return pl.pallas_call(
        functools.partial(_expert_kernel, tm=tm, tiles_per_step=tiles_per_step),
        grid_spec=grid_spec,
        out_shape=jax.ShapeDtypeStruct((nt * tm, d // 2), jnp.uint32),
        compiler_params=pltpu.CompilerParams(
            dimension_semantics=("arbitrary",), vmem_limit_bytes=VMEM_LIMIT),
        name="experts",
    )(plan["tile_expert"], plan["n_tiles"], plan["first"], plan["slot"], plan["next_expert"],
      plan["next2_expert"], xs, w_gate, w_up, w_down)


def _combine_kernel(pos_ref, h_ref, rw_ref, ys_hbm, g_ref, o_ref, ybuf0, ybuf1, ybuf2, sem):
    i = pl.program_id(0)
    last = pl.num_programs(0) - 1
    tm = h_ref.shape[0]
    kh = ybuf0.shape[2]
    bufs = (ybuf0, ybuf1, ybuf2)
    nbuf = len(bufs)

    def start_gather(tile, buf, sm):
        for r in range(tm):
            for k in range(TOP_K_FINE):
                p = pos_ref[(tile * tm + r) * TOP_K_FINE + k]
                pltpu.make_async_copy(ys_hbm.at[pl.ds(p, 1), :], buf.at[k, pl.ds(r, 1), :],
                                      sm).start(priority=k % 2)

    def wait_gather(buf, sm):
        for k in range(TOP_K_FINE):
            pltpu.make_async_copy(ys_hbm.at[pl.ds(0, tm), :], buf.at[k], sm).wait()

    @pl.when(i == 0)
    def _():
        start_gather(0, ybuf0, sem.at[0])
        start_gather(1, ybuf1, sem.at[1])

    def step(par):
        cur = bufs[par]
        ahead = (par + nbuf - 1) % nbuf
        wait_gather(cur, sem.at[par])
        start_gather(jnp.minimum(i + nbuf - 1, last), bufs[ahead], sem.at[ahead])
        w = rw_ref[...]
        halves = []
        for part in range(2):
            y = None
            for k in range(TOP_K_FINE):
                words = cur[k]
                bits = (words & jnp.uint32(0xFFFF0000)) if part == 0 else (words << 16)
                term = w[:, k:k + 1] * lax.bitcast_convert_type(bits, F32)
                y = term if y is None else y + term
            halves.append(h_ref[:, part * kh:(part + 1) * kh] + y)
        ms = sum(jnp.sum(v * v, axis=-1, keepdims=True) for v in halves) / (2 * kh)
        inv = lax.rsqrt(ms + RMS_EPS)
        for part in range(2):
            o_ref[:, part * kh:(part + 1) * kh] = (
                halves[part] * inv * g_ref[:, part * kh:(part + 1) * kh])

        @pl.when(i == last)
        def _():
            for other in range(nbuf):
                if other != par:
                    wait_gather(bufs[other], sem.at[other])

    for par in range(nbuf):
        pl.when(i % nbuf == par)(functools.partial(step, par))


def _combine(pos, h, rw, ys, g, tm=256):
    t, d = h.shape
    grid_spec = pltpu.PrefetchScalarGridSpec(
        num_scalar_prefetch=1,
        grid=(t // tm,),
        in_specs=[
            pl.BlockSpec((tm, d), lambda i, p: (i, 0)),
            pl.BlockSpec((tm, LANES), lambda i, p: (i, 0)),
            pl.BlockSpec(memory_space=pl.ANY),
            pl.BlockSpec((1, d), lambda i, p: (0, 0)),
        ],
        out_specs=pl.BlockSpec((tm, d), lambda i, p: (i, 0)),
        scratch_shapes=[pltpu.VMEM((TOP_K_FINE, tm, d // 2), jnp.uint32),
                        pltpu.VMEM((TOP_K_FINE, tm, d // 2), jnp.uint32),
                        pltpu.VMEM((TOP_K_FINE, tm, d // 2), jnp.uint32),
                        pltpu.SemaphoreType.DMA((3,))],
    )
    return pl.pallas_call(
        _combine_kernel,
        grid_spec=grid_spec,
        out_shape=jax.ShapeDtypeStruct((t, d), F32),
        compiler_params=pltpu.CompilerParams(
            dimension_semantics=("arbitrary",), vmem_limit_bytes=VMEM_LIMIT),
        name="combine",
    )(pos, h, rw, ys, g.reshape(1, d))


def _rank_kernel(ri_ref, meta_ref, cnt_ref, carry_ref, tri_ref):
    tt = ri_ref.shape[0]
    ne = cnt_ref.shape[0]

    @pl.when(pl.program_id(0) == 0)
    def _():
        carry_ref[...] = jnp.zeros_like(carry_ref)
        earlier = (lax.broadcasted_iota(jnp.int32, (tt, tt), 0)
                   < lax.broadcasted_iota(jnp.int32, (tt, tt), 1))
        tri_ref[...] = jnp.where(earlier, 1.0, 0.0).astype(BF16)

    rit = ri_ref[...].astype(F32).T
    expert = lax.broadcasted_iota(jnp.int32, (ne, tt), 0).astype(F32)
    hits = []
    for k in range(TOP_K_FINE):
        hits.append(expert == rit[k:k + 1, :])
    chosen = jnp.where(hits[0] | hits[1], 1.0, 0.0)
    before = (jnp.dot(chosen.astype(BF16), tri_ref[...], preferred_element_type=F32)
              + carry_ref[:, 0:1])
    for k in range(TOP_K_FINE):
        meta_ref[k:k + 1, :] = rit[k:k + 1, :].astype(jnp.int32)
        rank = jnp.sum(jnp.where(hits[k], before, 0.0), axis=0, keepdims=True)
        meta_ref[TOP_K_FINE + k:TOP_K_FINE + k + 1, :] = rank.astype(jnp.int32)
    meta_ref[2 * TOP_K_FINE:, :] = jnp.zeros((meta_ref.shape[0] - 2 * TOP_K_FINE, tt), jnp.int32)
    carry_ref[...] = carry_ref[...] + jnp.sum(chosen, axis=1, keepdims=True)
    cnt_ref[...] = carry_ref[...].astype(jnp.int32)


def _rank(ri, tt=1024):
    t = ri.shape[0]
    assert TOP_K_FINE == 2
    return pl.pallas_call(
        _rank_kernel,
        grid=(t // tt,),
        in_specs=[pl.BlockSpec((tt, LANES), lambda i: (i, 0))],
        out_specs=[pl.BlockSpec((None, 8, tt), lambda i: (i, 0, 0)),
                   pl.BlockSpec((N_EXPERTS, LANES), lambda i: (0, 0))],
        out_shape=[jax.ShapeDtypeStruct((t // tt, 8, tt), jnp.int32),
                   jax.ShapeDtypeStruct((N_EXPERTS, LANES), jnp.int32)],
        scratch_shapes=[pltpu.VMEM((N_EXPERTS, LANES), F32), pltpu.VMEM((tt, tt), BF16)],
        compiler_params=pltpu.CompilerParams(
            dimension_semantics=("arbitrary",), vmem_limit_bytes=VMEM_LIMIT),
        name="route_rank",
    )(ri)


def _scatter_rows_kernel(pos_ref, end_ref, nt_ref, src_hbm, dst_hbm, zbuf, st0, st1, st2,
                         sem, lsem, zsem, *, tm, row_tile):
    i = pl.program_id(0)
    stage = (st0, st1, st2)
    n_tiles_max = dst_hbm.shape[0] // row_tile

    def zero_copy(tile):
        row0 = pl.multiple_of(tile * row_tile, row_tile)
        return pltpu.make_async_copy(zbuf, dst_hbm.at[pl.ds(row0, row_tile), :], zsem)

    @pl.when(i == 0)
    def _():
        zbuf[...] = jnp.zeros_like(zbuf)
        n_exp = end_ref.shape[0]

        def has_rows(e):
            return end_ref[e] > (end_ref[e - 1] if e else 0)

        for e in range(n_exp):
            pl.when(has_rows(e))(lambda e=e: zero_copy(end_ref[e] - 1).start())

        def start_tail(tile, c):
            zero_copy(tile).start()
            return c

        def wait_tail(tile, c):
            zero_copy(tile).wait()
            return c

        lax.fori_loop(nt_ref[0], n_tiles_max, start_tail, 0)
        for e in range(n_exp):
            pl.when(has_rows(e))(lambda: zero_copy(0).wait())
        lax.fori_loop(nt_ref[0], n_tiles_max, wait_tail, 0)

    last = pl.num_programs(0) - 1
    nbuf = len(stage)

    def load(tile, b):
        row0 = pl.multiple_of(tile * tm, tm)
        return pltpu.make_async_copy(src_hbm.at[pl.ds(row0, tm), :], stage[b], lsem.at[b])

    def wait_rows(b):
        for _ in range(TOP_K_FINE):
            pltpu.make_async_copy(stage[b], dst_hbm.at[pl.ds(0, tm), :], sem.at[b]).wait()

    @pl.when(i == 0)
    def _():
        load(0, 0).start()

    def step(b):
        nb = (b + 1) % nbuf

        @pl.when(i >= nbuf - 1)
        def _():
            wait_rows(nb)

        @pl.when(i < last)
        def _():
            load(i + 1, nb).start()

        load(i, b).wait()
        for r in range(tm):
            for k in range(TOP_K_FINE):
                p = pos_ref[(i * tm + r) * TOP_K_FINE + k]
                pltpu.make_async_copy(stage[b].at[pl.ds(r, 1), :], dst_hbm.at[pl.ds(p, 1), :],
                                      sem.at[b]).start(priority=k % 2)

        @pl.when(i == last)
        def _():
            for d in range(nbuf - 1):
                @pl.when(i >= d)
                def _(d=d):
                    wait_rows((b - d) % nbuf)

    for b in range(nbuf):
        pl.when(i % nbuf == b)(functools.partial(step, b))


def _scatter_rows(plan, src, tm=256, row_tile=ROW_TILE):
    t, w = src.shape
    any_spec = pl.BlockSpec(memory_space=pl.ANY)
    grid_spec = pltpu.PrefetchScalarGridSpec(
        num_scalar_prefetch=3,
        grid=(t // tm,),
        in_specs=[any_spec],
        out_specs=any_spec,
        scratch_shapes=[pltpu.VMEM((row_tile, w), src.dtype),
                        pltpu.VMEM((tm, w), src.dtype), pltpu.VMEM((tm, w), src.dtype),
                        pltpu.VMEM((tm, w), src.dtype),
                        pltpu.SemaphoreType.DMA((3,)), pltpu.SemaphoreType.DMA((3,)),
                        pltpu.SemaphoreType.DMA(())],
    )
    return pl.pallas_call(
        functools.partial(_scatter_rows_kernel, tm=tm, row_tile=row_tile),
        grid_spec=grid_spec,
        out_shape=jax.ShapeDtypeStruct((plan["n_rows"], w), src.dtype),
        compiler_params=pltpu.CompilerParams(dimension_semantics=("arbitrary",)),
        name="scatter_rows",
    )(plan["pos"], plan["end_tile"], plan["n_tiles"], src)


def _sort_plan(ri, tm=ROW_TILE):
    t = ri.shape[0]
    k = TOP_K_FINE
    a = t * k
    nt = a // tm + N_EXPERTS
    meta, cnt = _rank(ri)
    counts = cnt[:, 0]
    padded = ((counts + tm - 1) // tm) * tm
    off_end = jnp.cumsum(padded)
    off = off_end - padded
    ef = meta[:, 0:k, :]
    eid = jnp.arange(N_EXPERTS, dtype=jnp.int32)
    pos = jnp.sum(jnp.where(ef[..., None] == eid, off, 0), axis=-1) + meta[:, k:2 * k, :]
    pos = jnp.transpose(pos, (0, 2, 1)).reshape(a)
    n_tiles = off_end[-1] // tm
    tile_idx = jnp.arange(nt, dtype=jnp.int32)
    te = jnp.sum((off_end[None, :] <= (tile_idx * tm)[:, None]).astype(jnp.int32), axis=1)
    te = jnp.minimum(te, N_EXPERTS - 1)
    te = jnp.where(tile_idx < n_tiles, te, te[n_tiles - 1]).astype(jnp.int32)
    first = ((tile_idx == 0) | (te != jnp.roll(te, 1))) & (tile_idx < n_tiles)
    slot = (jnp.cumsum(first.astype(jnp.int32)) - 1) % WEIGHT_SLOTS
    eid = jnp.arange(N_EXPERTS, dtype=jnp.int32)
    later = (eid[None, :] > eid[:, None]) & (counts[None, :] > 0)
    next_e = jnp.min(jnp.where(later, eid[None, :], N_EXPERTS), axis=1)
    next_e = jnp.where(next_e < N_EXPERTS, next_e, -1)
    next2_e = jnp.where(next_e >= 0, next_e[jnp.maximum(next_e, 0)], -1)
    return dict(
        next2_expert=next2_e[te].astype(jnp.int32),
        end_tile=(off_end // tm).astype(jnp.int32),
        pos=pos.astype(jnp.int32), n_rows=nt * tm, tile_expert=te,
        n_tiles=n_tiles.reshape(1).astype(jnp.int32), first=first.astype(jnp.int32),
        slot=slot.astype(jnp.int32), next_expert=next_e[te].astype(jnp.int32))


def kernel(x, norm1_g, w_in, lambda_q1, lambda_k1, lambda_q2, lambda_k2, subln_g, pool_w, pool_scale, w_out, norm2_g, w_coarse, b_coarse, w_fine, b_fine, w_gate, w_up, w_down, final_norm_g):
    b, s, d = x.shape
    t = b * s
    assert norm1_g.shape[0] == 1
    x2 = x.reshape(t, d)
    proj, pool, v_t = _inproj(x2, norm1_g[0], w_in[0], pool_w[0], pool_scale[0], s)
    proj3 = proj.reshape(b, s, proj.shape[1])
    attn = _attention(proj3, v_t, lambda_q1, lambda_k1, lambda_q2, lambda_k2, subln_g[0])

    ng, epg = N_EXPERT_GROUPS, EXPERTS_PER_GROUP
    wr = jnp.concatenate(
        [w_coarse[0], jnp.transpose(w_fine[0], (1, 0, 2)).reshape(d, ng * epg)], axis=1)
    wr = jnp.pad(wr, ((0, 0), (0, LANES - wr.shape[1])))
    br = jnp.concatenate([b_coarse[0], b_fine[0].reshape(ng * epg)])
    br = jnp.pad(br, (0, LANES - br.shape[0])).reshape(1, LANES)
    wr_hi = wr.astype(BF16)
    wr_lo = (wr - wr_hi.astype(F32)).astype(BF16)
    h, hn, ri, rw = _outproj(attn.reshape(t, ATTN_WIDTH), pool, x2,
                             w_out[0], norm2_g[0], jnp.concatenate([wr_hi, wr_lo], axis=1), br)

    plan = _sort_plan(ri)
    xs = _scatter_rows(plan, hn)
    ys = _experts(plan, xs, w_gate[0], w_up[0], w_down[0])
    out = _combine(plan["pos"], h, rw, ys, final_norm_g)
    return out.reshape(b, s, d)
```

```python
import functools
import math

import jax
import jax.numpy as jnp
from jax import lax
from jax.experimental import pallas as pl
from jax.experimental.pallas import tpu as pltpu

N_DIFF_HEADS = 8
DIFF_HEAD_DIM = 64
DIFF_V_DIM = 2 * DIFF_HEAD_DIM
ATTN_WIDTH = N_DIFF_HEADS * DIFF_V_DIM
POOL_WINDOWS = (2, 4, 8, 16)
POOL_GROUP_DIM = 256
POOL_WIDTH = len(POOL_WINDOWS) * POOL_GROUP_DIM
N_EXPERT_GROUPS = 4
EXPERTS_PER_GROUP = 8
N_EXPERTS = N_EXPERT_GROUPS * EXPERTS_PER_GROUP
TOP_K_FINE = 2
RMS_EPS = 1e-6
NEG_INF = -1e30
LAM_INIT = 0.8 - 0.6 * math.exp(-0.3 * 0)
LOG2E = math.log2(math.e)
Q_SCALE = DIFF_HEAD_DIM ** -0.5 * LOG2E

LANES = 128
SUBLANES = 8
VMEM_LIMIT = 56 * 1024 * 1024
ROW_TILE = 256
WEIGHT_SLOTS = 3

F32 = jnp.float32
BF16 = jnp.bfloat16


def _rms(x, g):
    return x * lax.rsqrt(jnp.mean(x * x, axis=-1, keepdims=True) + RMS_EPS) * g


def _load_weight_bf16(w_hbm, wb_ref, stage, sem):
    ck = stage.shape[1]
    nchunk = w_hbm.shape[0] // ck

    def copy(c):
        return pltpu.make_async_copy(w_hbm.at[c * ck:(c + 1) * ck, :], stage.at[c % 2], sem.at[c % 2])

    copy(0).start()
    for c in range(nchunk):
        if c + 1 < nchunk:
            copy(c + 1).start()
        copy(c).wait()
        wb_ref[c * ck:(c + 1) * ck, :] = stage[c % 2].astype(BF16)


def _pool_mix(u, prev, seq_pos, pw_ref, ps_ref):
    halo = prev.shape[0]
    c = POOL_GROUP_DIM
    ext = jnp.concatenate([prev, u], axis=0)
    outs = []
    for gi, win in enumerate(POOL_WINDOWS):
        assert win & (win - 1) == 0 and win <= halo
        run = ext[:, gi * c:(gi + 1) * c]
        span = 1
        while span < win:
            run = run + pltpu.roll(run, span, axis=0)
            span *= 2
        count = jnp.minimum(seq_pos + 1, win).astype(F32)
        pooled = run[halo:, :] / count - u[:, gi * c:(gi + 1) * c]
        mixed = jnp.dot(pooled.astype(BF16), pw_ref[gi].astype(BF16), preferred_element_type=F32)
        outs.append(mixed * ps_ref[:, gi * c:(gi + 1) * c])
    return outs


def _inproj_kernel(x_ref, g_ref, w_hbm, pw_ref, ps_ref, o_ref, pool_ref, vt_ref,
                   wb_ref, stage, sem, halo_ref, *, tn, seq_tiles):
    i = pl.program_id(0)
    tm = x_ref.shape[0]

    @pl.when(i == 0)
    def _():
        halo_ref[...] = jnp.zeros_like(halo_ref)
        _load_weight_bf16(w_hbm, wb_ref, stage, sem)

    xn = _rms(x_ref[...], g_ref[...]).astype(BF16)
    u = jnp.dot(xn, wb_ref[:, 3 * tn:4 * tn], preferred_element_type=F32)
    v = jnp.dot(xn, wb_ref[:, 2 * tn:3 * tn], preferred_element_type=F32)
    vt_ref[...] = v.T.astype(vt_ref.dtype)
    q = jnp.dot(xn, wb_ref[:, 0 * tn:1 * tn], preferred_element_type=F32) * Q_SCALE
    o_ref[:, 0 * tn:1 * tn] = q.astype(o_ref.dtype)

    tile_in_seq = i % seq_tiles
    prev = jnp.where(tile_in_seq == 0, 0.0, halo_ref[...])
    halo_ref[...] = u[tm - halo_ref.shape[0]:, :]
    seq_pos = tile_in_seq * tm + lax.broadcasted_iota(jnp.int32, (tm, 1), 0)
    c = POOL_GROUP_DIM
    for gi, mixed in enumerate(_pool_mix(u, prev, seq_pos, pw_ref, ps_ref)):
        pool_ref[:, gi * c:(gi + 1) * c] = mixed.astype(pool_ref.dtype)

    k = jnp.dot(xn, wb_ref[:, 1 * tn:2 * tn], preferred_element_type=F32)
    o_ref[:, 1 * tn:2 * tn] = k.astype(o_ref.dtype)


def _inproj(x2, g, w, pool_w, pool_scale, seq_len, tm=512, tn=ATTN_WIDTH, chunk_rows=256):
    t, d = x2.shape
    n = w.shape[1]
    assert n == 4 * tn and tn == POOL_WIDTH and seq_len % tm == 0
    halo = 2 * SUBLANES
    assert max(POOL_WINDOWS) <= halo
    return pl.pallas_call(
        functools.partial(_inproj_kernel, tn=tn, seq_tiles=seq_len // tm),
        grid=(t // tm,),
        in_specs=[
            pl.BlockSpec((tm, d), lambda i: (i, 0)),
            pl.BlockSpec((1, d), lambda i: (0, 0)),
            pl.BlockSpec(memory_space=pl.ANY),
            pl.BlockSpec(pool_w.shape, lambda i: (0, 0, 0)),
            pl.BlockSpec((1, tn), lambda i: (0, 0)),
        ],
        out_specs=[pl.BlockSpec((tm, 2 * tn), lambda i: (i, 0)),
                   pl.BlockSpec((tm, tn), lambda i: (i, 0)),
                   pl.BlockSpec((tn, tm), lambda i: (0, i))],
        out_shape=[jax.ShapeDtypeStruct((t, 2 * tn), BF16), jax.ShapeDtypeStruct((t, tn), BF16),
                   jax.ShapeDtypeStruct((tn, t), BF16)],
        scratch_shapes=[pltpu.VMEM((d, n), BF16), pltpu.VMEM((2, chunk_rows, n), F32),
                        pltpu.SemaphoreType.DMA((2,)), pltpu.VMEM((halo, tn), F32)],
        compiler_params=pltpu.CompilerParams(
            dimension_semantics=("arbitrary",), vmem_limit_bytes=VMEM_LIMIT),
        name="inproj",
    )(x2, g.reshape(1, d), w, pool_w, pool_scale.reshape(1, tn))


def _attn_kernel(lq1_ref, lk1_ref, lq2_ref, lk2_ref, q_ref, k_ref, v_ref, g_ref, o_ref,
                 tab_ref, vt_ref, *, tq):
    h = pl.program_id(0)
    s_len = q_ref.shape[0]
    nq = s_len // tq
    d, dv = DIFF_HEAD_DIM, DIFF_V_DIM

    @pl.when(pl.program_id(1) == 0)
    def _():
        slope = jnp.exp2(jnp.full((1, 1), -8.0 / N_DIFF_HEADS, F32) * (h + 1).astype(F32))
        rel = ((nq - 1) * tq + lax.broadcasted_iota(jnp.int32, (s_len, tq), 1)
               - lax.broadcasted_iota(jnp.int32, (s_len, tq), 0))
        tab_ref[...] = jnp.where(rel >= 0, (-LOG2E * slope) * rel.astype(F32), NEG_INF)

    vt_ref[:dv, :] = v_ref[...]
    vt_ref[dv:, :] = jnp.ones((vt_ref.shape[0] - dv, s_len), BF16)

    lam = (jnp.exp(jnp.sum(lq1_ref[...] * lk1_ref[...], axis=1, keepdims=True))
           - jnp.exp(jnp.sum(lq2_ref[...] * lk2_ref[...], axis=1, keepdims=True)) + LAM_INIT)
    dn = (((1,), (1,)), ((), ()))
    lane = lax.broadcasted_iota(jnp.int32, (tq, 2 * d), 1)

    ck = 4 * tq

    def key_chunks(qi):
        n = (qi + 1) * tq
        return [(lo, min(lo + ck, n)) for lo in range(0, n, ck)]

    def query_halves(qi):
        q = q_ref[qi * tq:(qi + 1) * tq, :]
        zero = jnp.zeros_like(q)
        return (jnp.where(lane < d, q, zero), jnp.where(lane >= d, q, zero))

    def score_chunk(qi, qh, lo, hi):
        base = (nq - 1 - qi) * tq
        bias = tab_ref[base + lo:base + hi, :]
        return [lax.dot_general(k_ref[lo:hi, :], qh[a], dn, preferred_element_type=F32) + bias
                for a in range(2)]

    s_next = [score_chunk(0, query_halves(0), lo, hi) for lo, hi in key_chunks(0)]
    for qi in range(nq):
        s_cur = s_next
        cur_chunks = key_chunks(qi)
        m = [functools.reduce(jnp.maximum, [jnp.max(c[a], axis=0, keepdims=True) for c in s_cur])
             for a in range(2)]
        nxt_chunks = key_chunks(qi + 1) if qi + 1 < nq else []
        qh_next = query_halves(qi + 1) if nxt_chunks else None
        s_next = []
        acc = [None, None]
        for c in range(max(len(cur_chunks), len(nxt_chunks))):
            if c < len(nxt_chunks):
                s_next.append(score_chunk(qi + 1, qh_next, *nxt_chunks[c]))
            if c < len(cur_chunks):
                lo, hi = cur_chunks[c]
                for a in range(2):
                    p = jnp.exp2(s_cur[c][a] - m[a]).astype(BF16)
                    part = jnp.dot(vt_ref[:, lo:hi], p, preferred_element_type=F32)
                    acc[a] = part if acc[a] is None else acc[a] + part
        outs = [acc[a][:dv] / acc[a][dv:dv + 1] for a in range(2)]
        ot = outs[0] - lam * outs[1]
        yt = ot * lax.rsqrt(jnp.mean(ot * ot, axis=0, keepdims=True) + RMS_EPS) * g_ref[...]
        o_ref[qi * tq:(qi + 1) * tq, :] = (yt * (1.0 - LAM_INIT)).T.astype(o_ref.dtype)


def _attention(proj3, v_t, lq1, lk1, lq2, lk2, subln_g, tq=256):
    b, s, _ = proj3.shape
    dv = DIFF_V_DIM
    nh = N_DIFF_HEADS
    ones_rows = 16
    lam_spec = pl.BlockSpec((1, DIFF_HEAD_DIM), lambda hi, bi: (0, 0))
    return pl.pallas_call(
        functools.partial(_attn_kernel, tq=tq),
        grid=(nh, b),
        in_specs=[
            lam_spec, lam_spec, lam_spec, lam_spec,
            pl.BlockSpec((None, s, dv), lambda hi, bi: (bi, 0, hi)),
            pl.BlockSpec((None, s, dv), lambda hi, bi: (bi, 0, nh + hi)),
            pl.BlockSpec((dv, s), lambda hi, bi: (hi, bi)),
            pl.BlockSpec((dv, 1), lambda hi, bi: (0, 0)),
        ],
        out_specs=pl.BlockSpec((None, s, dv), lambda hi, bi: (bi, 0, hi)),
        out_shape=jax.ShapeDtypeStruct((b, s, ATTN_WIDTH), BF16),
        scratch_shapes=[
            pltpu.VMEM((s, tq), F32),
            pltpu.VMEM((dv + ones_rows, s), BF16),
        ],
        compiler_params=pltpu.CompilerParams(
            dimension_semantics=("arbitrary", "arbitrary"),
            vmem_limit_bytes=VMEM_LIMIT),
        name="diff_attn",
    )(lq1, lk1, lq2, lk2, proj3, proj3, v_t, subln_g.reshape(dv, 1))


def _pack_bf16_pair(x):
    bits = lax.bitcast_convert_type(x.astype(BF16).astype(F32), jnp.uint32)
    half = bits.shape[1] // 2
    return bits[:, :half] | (bits[:, half:] >> 16)


def _outproj_kernel(a_ref, p_ref, x_ref, w_hbm, g_ref, wr_ref, br_ref,
                    h_ref, hn_ref, ri_ref, rw_ref, w_ref, stage, sem, *, n_sub):
    @pl.when(pl.program_id(0) == 0)
    def _():
        _load_weight_bf16(w_hbm, w_ref, stage, sem)

    ka = a_ref.shape[1]
    hs = x_ref.shape[0] // n_sub
    normed = []
    for r in range(n_sub):
        rows = pl.ds(r * hs, hs)
        mixed = (jnp.dot(a_ref[rows, :], w_ref[:ka, :], preferred_element_type=F32)
                 + jnp.dot(p_ref[rows, :], w_ref[ka:, :], preferred_element_type=F32))
        h = x_ref[rows, :] + mixed
        h_ref[rows, :] = h
        hn = _rms(h, g_ref[...])
        hn_ref[rows, :] = _pack_bf16_pair(hn)
        hn_hi = hn.astype(BF16)
        normed.append((hn_hi, (hn - hn_hi.astype(F32)).astype(BF16)))
    for r in range(n_sub):
        rows = pl.ds(r * hs, hs)
        ri, rw = _route(normed[r][0], normed[r][1], wr_ref, br_ref)
        ri_ref[rows, :] = ri
        rw_ref[rows, :] = rw


def _route(hn_hi, hn_lo, wr_ref, br_ref):
    both = jnp.dot(hn_hi, wr_ref[...], preferred_element_type=F32)
    logits = (both[:, :LANES] + both[:, LANES:]
              + jnp.dot(hn_lo, wr_ref[:, :LANES], preferred_element_type=F32)) + br_ref[...]
    ng, epg = N_EXPERT_GROUPS, EXPERTS_PER_GROUP
    lane = lax.broadcasted_iota(jnp.int32, logits.shape, 1)
    big = jnp.int32(LANES)
    low = jnp.float32(-3.0e38)
    cm = lane < ng
    c = jnp.where(cm, logits, low)
    cmax = jnp.max(c, axis=1, keepdims=True)
    gsel = jnp.min(jnp.where(c == cmax, lane, big), axis=1, keepdims=True)
    p_group = 1.0 / jnp.sum(jnp.where(cm, jnp.exp(c - cmax), 0.0), axis=1, keepdims=True)
    f_lo = ng + epg * gsel
    fm = (lane >= f_lo) & (lane < f_lo + epg)
    f = jnp.where(fm, logits, low)
    v1 = jnp.max(f, axis=1, keepdims=True)
    i1 = jnp.min(jnp.where(fm & (f == v1), lane, big), axis=1, keepdims=True)
    fm2 = fm & (lane != i1)
    f2 = jnp.where(fm2, logits, low)
    v2 = jnp.max(f2, axis=1, keepdims=True)
    i2 = jnp.min(jnp.where(fm2 & (f2 == v2), lane, big), axis=1, keepdims=True)
    e21 = jnp.exp(v2 - v1)
    w1 = p_group / (1.0 + e21)
    w2 = p_group * e21 / (1.0 + e21)
    return (jnp.where(lane == 0, i1 - ng, jnp.where(lane == 1, i2 - ng, 0)),
            jnp.where(lane == 0, w1, jnp.where(lane == 1, w2, 0.0)))


def _outproj(attn2, pool2, x2, w_out, g2, wr, br, tm=512, n_sub=2, chunk_rows=512):
    t, d = x2.shape
    ka, kp = attn2.shape[1], pool2.shape[1]
    row = lambda i: (i, 0)
    const = lambda i: (0, 0)
    return pl.pallas_call(
        functools.partial(_outproj_kernel, n_sub=n_sub),
        grid=(t // tm,),
        in_specs=[
            pl.BlockSpec((tm, ka), row),
            pl.BlockSpec((tm, kp), row),
            pl.BlockSpec((tm, d), row),
            pl.BlockSpec(memory_space=pl.ANY),
            pl.BlockSpec((1, d), const),
            pl.BlockSpec((d, 2 * LANES), const),
            pl.BlockSpec((1, LANES), const),
        ],
        out_specs=[
            pl.BlockSpec((tm, d), row),
            pl.BlockSpec((tm, d // 2), row),
            pl.BlockSpec((tm, LANES), row),
            pl.BlockSpec((tm, LANES), row),
        ],
        out_shape=[
            jax.ShapeDtypeStruct((t, d), F32),
            jax.ShapeDtypeStruct((t, d // 2), jnp.uint32),
            jax.ShapeDtypeStruct((t, LANES), jnp.int32),
            jax.ShapeDtypeStruct((t, LANES), F32),
        ],
        scratch_shapes=[pltpu.VMEM((ka + kp, d), BF16), pltpu.VMEM((2, chunk_rows, d), F32),
                        pltpu.SemaphoreType.DMA((2,))],
        compiler_params=pltpu.CompilerParams(
            dimension_semantics=("arbitrary",), vmem_limit_bytes=VMEM_LIMIT),
        name="outproj_router",
    )(attn2, pool2, x2, w_out, g2.reshape(1, d), wr, br)


def _unpack_bf16_pair(words):
    hi = lax.bitcast_convert_type(words & jnp.uint32(0xFFFF0000), F32).astype(BF16)
    lo = lax.bitcast_convert_type(words << 16, F32).astype(BF16)
    return hi, lo


def _expert_kernel(te_ref, nt_ref, first_ref, slot_ref, nxt_ref, nxt2_ref,
                   x_ref, wg_hbm, wu_hbm, wd_hbm, y_ref,
                   wgb, wub, wdb, wsem, *, tm, tiles_per_step):
    nt = nt_ref[0]
    n_slots = wgb.shape[0]
    kh = x_ref.shape[1]
    fh = wdb.shape[1] // 2

    def weight_copies(e, s):
        return ((pltpu.make_async_copy(wg_hbm.at[e], wgb.at[s], wsem.at[s]), 0),
                (pltpu.make_async_copy(wu_hbm.at[e], wub.at[s], wsem.at[s]), 1),
                (pltpu.make_async_copy(wd_hbm.at[e, :fh, :], wdb.at[s, :fh, :], wsem.at[s]), 0),
                (pltpu.make_async_copy(wd_hbm.at[e, fh:, :], wdb.at[s, fh:, :], wsem.at[s]), 1))

    @pl.when(pl.program_id(0) == 0)
    def _():
        for c, prio in weight_copies(te_ref[0], 0):
            c.start(priority=prio)

        @pl.when(nxt_ref[0] >= 0)
        def _():
            for c, prio in weight_copies(nxt_ref[0], 1):
                c.start(priority=prio)

    def tile(j, rows):
        @pl.when(j < nt)
        def _():
            ws = slot_ref[j]

            @pl.when(first_ref[j] == 1)
            def _():
                for c, _ in weight_copies(te_ref[j], ws):
                    c.wait()

                @pl.when(nxt2_ref[j] >= 0)
                def _():
                    for c, prio in weight_copies(nxt2_ref[j], (ws + 2) % n_slots):
                        c.start(priority=prio)

            xa, xb = _unpack_bf16_pair(x_ref[rows, :])
            hg = (jnp.dot(xa, wgb[ws, :kh, :].astype(BF16), preferred_element_type=F32)
                  + jnp.dot(xb, wgb[ws, kh:, :].astype(BF16), preferred_element_type=F32))
            hu = (jnp.dot(xa, wub[ws, :kh, :].astype(BF16), preferred_element_type=F32)
                  + jnp.dot(xb, wub[ws, kh:, :].astype(BF16), preferred_element_type=F32))
            act = (hg / (1.0 + jnp.exp(-hg))) * hu
            y_ref[rows, :] = _pack_bf16_pair(
                jnp.dot(act.astype(BF16), wdb[ws].astype(BF16), preferred_element_type=F32))

        @pl.when(j >= nt)
        def _():
            y_ref[rows, :] = jnp.zeros((tm, kh), y_ref.dtype)

    for sub in range(tiles_per_step):
        tile(pl.program_id(0) * tiles_per_step + sub, pl.ds(sub * tm, tm))


def _experts(plan, xs, w_gate, w_up, w_down, tm=ROW_TILE, tiles_per_step=2):
    d, f = w_gate.shape[1], w_gate.shape[2]
    assert xs.shape[1] * 2 == d
    nt = plan["tile_expert"].shape[0]
    assert nt % tiles_per_step == 0
    rows = tm * tiles_per_step
    any_spec = pl.BlockSpec(memory_space=pl.ANY)
    grid_spec = pltpu.PrefetchScalarGridSpec(
        num_scalar_prefetch=6,
        grid=(nt // tiles_per_step,),
        in_specs=[
            pl.BlockSpec((rows, d // 2),
                         lambda j, te, n, *_: (jnp.minimum(j, (n[0] - 1) // tiles_per_step), 0)),
            any_spec, any_spec, any_spec],
        out_specs=pl.BlockSpec((rows, d // 2), lambda j, *_: (j, 0)),
        scratch_shapes=[
            pltpu.VMEM((WEIGHT_SLOTS, d, f), F32),
            pltpu.VMEM((WEIGHT_SLOTS, d, f), F32),
            pltpu.VMEM((WEIGHT_SLOTS, f, d), F32),
            pltpu.SemaphoreType.DMA((WEIGHT_SLOTS,)),
        ],
    )
    return pl.pallas_call(
        functools.partial(_expert_kernel, tm=tm, tiles_per_step=tiles_per_step),
        grid_spec=grid_spec,
        out_shape=jax.ShapeDtypeStruct((nt * tm, d // 2), jnp.uint32),
        compiler_params=pltpu.CompilerParams(
            dimension_semantics=("arbitrary",), vmem_limit_bytes=VMEM_LIMIT),
        name="experts",
    )(plan["tile_expert"], plan["n_tiles"], plan["first"], plan["slot"], plan["next_expert"],
      plan["next2_expert"], xs, w_gate, w_up, w_down)


def _combine_kernel(pos_ref, h_ref, rw_ref, ys_hbm, g_ref, o_ref, ybuf0, ybuf1, ybuf2, sem):
    i = pl.program_id(0)
    last = pl.num_programs(0) - 1
    tm = h_ref.shape[0]
    kh = ybuf0.shape[2]
    bufs = (ybuf0, ybuf1, ybuf2)
    nbuf = len(bufs)

    def start_gather(tile, buf, sm):
        for r in range(tm):
            for k in range(TOP_K_FINE):
                p = pos_ref[(tile * tm + r) * TOP_K_FINE + k]
                pltpu.make_async_copy(ys_hbm.at[pl.ds(p, 1), :], buf.at[k, pl.ds(r, 1), :],
                                      sm).start(priority=k % 2)

    def wait_gather(buf, sm):
        for k in range(TOP_K_FINE):
            pltpu.make_async_copy(ys_hbm.at[pl.ds(0, tm), :], buf.at[k], sm).wait()

    @pl.when(i == 0)
    def _():
        start_gather(0, ybuf0, sem.at[0])
        start_gather(1, ybuf1, sem.at[1])

    def step(par):
        cur = bufs[par]
        ahead = (par + nbuf - 1) % nbuf
        wait_gather(cur, sem.at[par])
        start_gather(jnp.minimum(i + nbuf - 1, last), bufs[ahead], sem.at[ahead])
        w = rw_ref[...]
        halves = []
        for part in range(2):
            y = None
            for k in range(TOP_K_FINE):
                words = cur[k]
                bits = (words & jnp.uint32(0xFFFF0000)) if part == 0 else (words << 16)
                term = w[:, k:k + 1] * lax.bitcast_convert_type(bits, F32)
                y = term if y is None else y + term
            halves.append(h_ref[:, part * kh:(part + 1) * kh] + y)
        ms = sum(jnp.sum(v * v, axis=-1, keepdims=True) for v in halves) / (2 * kh)
        inv = lax.rsqrt(ms + RMS_EPS)
        for part in range(2):
            o_ref[:, part * kh:(part + 1) * kh] = (
                halves[part] * inv * g_ref[:, part * kh:(part + 1) * kh])

        @pl.when(i == last)
        def _():
            for other in range(nbuf):
                if other != par:
                    wait_gather(bufs[other], sem.at[other])

    for par in range(nbuf):
        pl.when(i % nbuf == par)(functools.partial(step, par))


def _combine(pos, h, rw, ys, g, tm=256):
    t, d = h.shape
    grid_spec = pltpu.PrefetchScalarGridSpec(
        num_scalar_prefetch=1,
        grid=(t // tm,),
        in_specs=[
            pl.BlockSpec((tm, d), lambda i, p: (i, 0)),
            pl.BlockSpec((tm, LANES), lambda i, p: (i, 0)),
            pl.BlockSpec(memory_space=pl.ANY),
            pl.BlockSpec((1, d), lambda i, p: (0, 0)),
        ],
        out_specs=pl.BlockSpec((tm, d), lambda i, p: (i, 0)),
        scratch_shapes=[pltpu.VMEM((TOP_K_FINE, tm, d // 2), jnp.uint32),
                        pltpu.VMEM((TOP_K_FINE, tm, d // 2), jnp.uint32),
                        pltpu.VMEM((TOP_K_FINE, tm, d // 2), jnp.uint32),
                        pltpu.SemaphoreType.DMA((3,))],
    )
    return pl.pallas_call(
        _combine_kernel,
        grid_spec=grid_spec,
        out_shape=jax.ShapeDtypeStruct((t, d), F32),
        compiler_params=pltpu.CompilerParams(
            dimension_semantics=("arbitrary",), vmem_limit_bytes=VMEM_LIMIT),
        name="combine",
    )(pos, h, rw, ys, g.reshape(1, d))


def _rank_kernel(ri_ref, meta_ref, cnt_ref, carry_ref, tri_ref):
    tt = ri_ref.shape[0]
    ne = cnt_ref.shape[0]

    @pl.when(pl.program_id(0) == 0)
    def _():
        carry_ref[...] = jnp.zeros_like(carry_ref)
        earlier = (lax.broadcasted_iota(jnp.int32, (tt, tt), 0)
                   < lax.broadcasted_iota(jnp.int32, (tt, tt), 1))
        tri_ref[...] = jnp.where(earlier, 1.0, 0.0).astype(BF16)

    rit = ri_ref[...].astype(F32).T
    expert = lax.broadcasted_iota(jnp.int32, (ne, tt), 0).astype(F32)
    hits = []
    for k in range(TOP_K_FINE):
        hits.append(expert == rit[k:k + 1, :])
    chosen = jnp.where(hits[0] | hits[1], 1.0, 0.0)
    before = (jnp.dot(chosen.astype(BF16), tri_ref[...], preferred_element_type=F32)
              + carry_ref[:, 0:1])
    for k in range(TOP_K_FINE):
        meta_ref[k:k + 1, :] = rit[k:k + 1, :].astype(jnp.int32)
        rank = jnp.sum(jnp.where(hits[k], before, 0.0), axis=0, keepdims=True)
        meta_ref[TOP_K_FINE + k:TOP_K_FINE + k + 1, :] = rank.astype(jnp.int32)
    meta_ref[2 * TOP_K_FINE:, :] = jnp.zeros((meta_ref.shape[0] - 2 * TOP_K_FINE, tt), jnp.int32)
    carry_ref[...] = carry_ref[...] + jnp.sum(chosen, axis=1, keepdims=True)
    cnt_ref[...] = carry_ref[...].astype(jnp.int32)


def _rank(ri, tt=1024):
    t = ri.shape[0]
    assert TOP_K_FINE == 2
    return pl.pallas_call(
        _rank_kernel,
        grid=(t // tt,),
        in_specs=[pl.BlockSpec((tt, LANES), lambda i: (i, 0))],
        out_specs=[pl.BlockSpec((None, 8, tt), lambda i: (i, 0, 0)),
                   pl.BlockSpec((N_EXPERTS, LANES), lambda i: (0, 0))],
        out_shape=[jax.ShapeDtypeStruct((t // tt, 8, tt), jnp.int32),
                   jax.ShapeDtypeStruct((N_EXPERTS, LANES), jnp.int32)],
        scratch_shapes=[pltpu.VMEM((N_EXPERTS, LANES), F32), pltpu.VMEM((tt, tt), BF16)],
        compiler_params=pltpu.CompilerParams(
            dimension_semantics=("arbitrary",), vmem_limit_bytes=VMEM_LIMIT),
        name="route_rank",
    )(ri)


def _scatter_rows_kernel(pos_ref, end_ref, nt_ref, src_hbm, dst_hbm, zbuf, st0, st1, st2,
                         sem, lsem, zsem, *, tm, row_tile):
    i = pl.program_id(0)
    stage = (st0, st1, st2)
    n_tiles_max = dst_hbm.shape[0] // row_tile

    def zero_copy(tile):
        row0 = pl.multiple_of(tile * row_tile, row_tile)
        return pltpu.make_async_copy(zbuf, dst_hbm.at[pl.ds(row0, row_tile), :], zsem)

    @pl.when(i == 0)
    def _():
        zbuf[...] = jnp.zeros_like(zbuf)
        n_exp = end_ref.shape[0]

        def has_rows(e):
            return end_ref[e] > (end_ref[e - 1] if e else 0)

        for e in range(n_exp):
            pl.when(has_rows(e))(lambda e=e: zero_copy(end_ref[e] - 1).start())

        def start_tail(tile, c):
            zero_copy(tile).start()
            return c

        def wait_tail(tile, c):
            zero_copy(tile).wait()
            return c

        lax.fori_loop(nt_ref[0], n_tiles_max, start_tail, 0)
        for e in range(n_exp):
            pl.when(has_rows(e))(lambda: zero_copy(0).wait())
        lax.fori_loop(nt_ref[0], n_tiles_max, wait_tail, 0)

    last = pl.num_programs(0) - 1
    nbuf = len(stage)

    def load(tile, b):
        row0 = pl.multiple_of(tile * tm, tm)
        return pltpu.make_async_copy(src_hbm.at[pl.ds(row0, tm), :], stage[b], lsem.at[b])

    def wait_rows(b):
        for _ in range(TOP_K_FINE):
            pltpu.make_async_copy(stage[b], dst_hbm.at[pl.ds(0, tm), :], sem.at[b]).wait()

    @pl.when(i == 0)
    def _():
        load(0, 0).start()

    def step(b):
        nb = (b + 1) % nbuf

        @pl.when(i >= nbuf - 1)
        def _():
            wait_rows(nb)

        @pl.when(i < last)
        def _():
            load(i + 1, nb).start()

        load(i, b).wait()
        for r in range(tm):
            for k in range(TOP_K_FINE):
                p = pos_ref[(i * tm + r) * TOP_K_FINE + k]
                pltpu.make_async_copy(stage[b].at[pl.ds(r, 1), :], dst_hbm.at[pl.ds(p, 1), :],
                                      sem.at[b]).start(priority=k % 2)

        @pl.when(i == last)
        def _():
            for d in range(nbuf - 1):
                @pl.when(i >= d)
                def _(d=d):
                    wait_rows((b - d) % nbuf)

    for b in range(nbuf):
        pl.when(i % nbuf == b)(functools.partial(step, b))


def _scatter_rows(plan, src, tm=256, row_tile=ROW_TILE):
    t, w = src.shape
    any_spec = pl.BlockSpec(memory_space=pl.ANY)
    grid_spec = pltpu.PrefetchScalarGridSpec(
        num_scalar_prefetch=3,
        grid=(t // tm,),
        in_specs=[any_spec],
        out_specs=any_spec,
        scratch_shapes=[pltpu.VMEM((row_tile, w), src.dtype),
                        pltpu.VMEM((tm, w), src.dtype), pltpu.VMEM((tm, w), src.dtype),
                        pltpu.VMEM((tm, w), src.dtype),
                        pltpu.SemaphoreType.DMA((3,)), pltpu.SemaphoreType.DMA((3,)),
                        pltpu.SemaphoreType.DMA(())],
    )
    return pl.pallas_call(
        functools.partial(_scatter_rows_kernel, tm=tm, row_tile=row_tile),
        grid_spec=grid_spec,
        out_shape=jax.ShapeDtypeStruct((plan["n_rows"], w), src.dtype),
        compiler_params=pltpu.CompilerParams(dimension_semantics=("arbitrary",)),
        name="scatter_rows",
    )(plan["pos"], plan["end_tile"], plan["n_tiles"], src)


def _sort_plan(ri, tm=ROW_TILE):
    t = ri.shape[0]
    k = TOP_K_FINE
    a = t * k
    nt = a // tm + N_EXPERTS
    meta, cnt = _rank(ri)
    counts = cnt[:, 0]
    padded = ((counts + tm - 1) // tm) * tm
    off_end = jnp.cumsum(padded)
    off = off_end - padded
    ef = meta[:, 0:k, :]
    eid = jnp.arange(N_EXPERTS, dtype=jnp.int32)
    pos = jnp.sum(jnp.where(ef[..., None] == eid, off, 0), axis=-1) + meta[:, k:2 * k, :]
    pos = jnp.transpose(pos, (0, 2, 1)).reshape(a)
    n_tiles = off_end[-1] // tm
    tile_idx = jnp.arange(nt, dtype=jnp.int32)
    te = jnp.sum((off_end[None, :] <= (tile_idx * tm)[:, None]).astype(jnp.int32), axis=1)
    te = jnp.minimum(te, N_EXPERTS - 1)
    te = jnp.where(tile_idx < n_tiles, te, te[n_tiles - 1]).astype(jnp.int32)
    first = ((tile_idx == 0) | (te != jnp.roll(te, 1))) & (tile_idx < n_tiles)
    slot = (jnp.cumsum(first.astype(jnp.int32)) - 1) % WEIGHT_SLOTS
    eid = jnp.arange(N_EXPERTS, dtype=jnp.int32)
    later = (eid[None, :] > eid[:, None]) & (counts[None, :] > 0)
    next_e = jnp.min(jnp.where(later, eid[None, :], N_EXPERTS), axis=1)
    next_e = jnp.where(next_e < N_EXPERTS, next_e, -1)
    next2_e = jnp.where(next_e >= 0, next_e[jnp.maximum(next_e, 0)], -1)
    return dict(
        next2_expert=next2_e[te].astype(jnp.int32),
        end_tile=(off_end // tm).astype(jnp.int32),
        pos=pos.astype(jnp.int32), n_rows=nt * tm, tile_expert=te,
        n_tiles=n_tiles.reshape(1).astype(jnp.int32), first=first.astype(jnp.int32),
        slot=slot.astype(jnp.int32), next_expert=next_e[te].astype(jnp.int32))


def kernel(x, norm1_g, w_in, lambda_q1, lambda_k1, lambda_q2, lambda_k2, subln_g, pool_w, pool_scale, w_out, norm2_g, w_coarse, b_coarse, w_fine, b_fine, w_gate, w_up, w_down, final_norm_g):
    b, s, d = x.shape
    t = b * s
    assert norm1_g.shape[0] == 1
    x2 = x.reshape(t, d)
    proj, pool, v_t = _inproj(x2, norm1_g[0], w_in[0], pool_w[0], pool_scale[0], s)
    proj3 = proj.reshape(b, s, proj.shape[1])
    attn = _attention(proj3, v_t, lambda_q1, lambda_k1, lambda_q2, lambda_k2, subln_g[0])

    ng, epg = N_EXPERT_GROUPS, EXPERTS_PER_GROUP
    wr = jnp.concatenate(
        [w_coarse[0], jnp.transpose(w_fine[0], (1, 0, 2)).reshape(d, ng * epg)], axis=1)
    wr = jnp.pad(wr, ((0, 0), (0, LANES - wr.shape[1])))
    br = jnp.concatenate([b_coarse[0], b_fine[0].reshape(ng * epg)])
    br = jnp.pad(br, (0, LANES - br.shape[0])).reshape(1, LANES)
    wr_hi = wr.astype(BF16)
    wr_lo = (wr - wr_hi.astype(F32)).astype(BF16)
    h, hn, ri, rw = _outproj(attn.reshape(t, ATTN_WIDTH), pool, x2,
                             w_out[0], norm2_g[0], jnp.concatenate([wr_hi, wr_lo], axis=1), br)

    plan = _sort_plan(ri)
    xs = _scatter_rows(plan, hn)
    ys = _experts(plan, xs, w_gate[0], w_up[0], w_down[0])
    out = _combine(plan["pos"], h, rw, ys, final_norm_g)
    return out.reshape(b, s, d)
```

```python
import functools
import math

import jax
import jax.numpy as jnp
from jax import lax
from jax.experimental import pallas as pl
from jax.experimental.pallas import tpu as pltpu

N_DIFF_HEADS = 8
DIFF_HEAD_DIM = 64
DIFF_V_DIM = 2 * DIFF_HEAD_DIM
ATTN_WIDTH = N_DIFF_HEADS * DIFF_V_DIM
POOL_WINDOWS = (2, 4, 8, 16)
POOL_GROUP_DIM = 256
POOL_WIDTH = len(POOL_WINDOWS) * POOL_GROUP_DIM
N_EXPERT_GROUPS = 4
EXPERTS_PER_GROUP = 8
N_EXPERTS = N_EXPERT_GROUPS * EXPERTS_PER_GROUP
TOP_K_FINE = 2
RMS_EPS = 1e-6
NEG_INF = -1e30
LAM_INIT = 0.8 - 0.6 * math.exp(-0.3 * 0)
LOG2E = math.log2(math.e)
Q_SCALE = DIFF_HEAD_DIM ** -0.5 * LOG2E

LANES = 128
SUBLANES = 8
VMEM_LIMIT = 56 * 1024 * 1024
ROW_TILE = 256
WEIGHT_SLOTS = 4

F32 = jnp.float32
BF16 = jnp.bfloat16


def _rms(x, g):
    return x * lax.rsqrt(jnp.mean(x * x, axis=-1, keepdims=True) + RMS_EPS) * g


def _load_weight_bf16(w_hbm, wb_ref, stage, sem):
    ck = stage.shape[1]
    nchunk = w_hbm.shape[0] // ck

    def copy(c):
        return pltpu.make_async_copy(w_hbm.at[c * ck:(c + 1) * ck, :], stage.at[c % 2], sem.at[c % 2])

    copy(0).start()
    for c in range(nchunk):
        if c + 1 < nchunk:
            copy(c + 1).start()
        copy(c).wait()
        wb_ref[c * ck:(c + 1) * ck, :] = stage[c % 2].astype(BF16)


def _pool_mix(u, prev, seq_pos, pw_ref, ps_ref):
    halo = prev.shape[0]
    c = POOL_GROUP_DIM
    ext = jnp.concatenate([prev, u], axis=0)
    outs = []
    for gi, win in enumerate(POOL_WINDOWS):
        assert win & (win - 1) == 0 and win <= halo
        run = ext[:, gi * c:(gi + 1) * c]
        span = 1
        while span < win:
            run = run + pltpu.roll(run, span, axis=0)
            span *= 2
        count = jnp.minimum(seq_pos + 1, win).astype(F32)
        pooled = run[halo:, :] / count - u[:, gi * c:(gi + 1) * c]
        mixed = jnp.dot(pooled.astype(BF16), pw_ref[gi].astype(BF16), preferred_element_type=F32)
        outs.append(mixed * ps_ref[:, gi * c:(gi + 1) * c])
    return outs


def _inproj_kernel(x_ref, g_ref, w_hbm, pw_ref, ps_ref, o_ref, pool_ref, vt_ref,
                   wb_ref, stage, sem, halo_ref, *, tn, seq_tiles):
    i = pl.program_id(0)
    tm = x_ref.shape[0]

    @pl.when(i == 0)
    def _():
        halo_ref[...] = jnp.zeros_like(halo_ref)
        _load_weight_bf16(w_hbm, wb_ref, stage, sem)

    xn = _rms(x_ref[...], g_ref[...]).astype(BF16)
    u = jnp.dot(xn, wb_ref[:, 3 * tn:4 * tn], preferred_element_type=F32)
    v = jnp.dot(xn, wb_ref[:, 2 * tn:3 * tn], preferred_element_type=F32)
    vt_ref[...] = v.T.astype(vt_ref.dtype)
    q = jnp.dot(xn, wb_ref[:, 0 * tn:1 * tn], preferred_element_type=F32) * Q_SCALE
    o_ref[:, 0 * tn:1 * tn] = q.astype(o_ref.dtype)

    tile_in_seq = i % seq_tiles
    prev = jnp.where(tile_in_seq == 0, 0.0, halo_ref[...])
    halo_ref[...] = u[tm - halo_ref.shape[0]:, :]
    seq_pos = tile_in_seq * tm + lax.broadcasted_iota(jnp.int32, (tm, 1), 0)
    c = POOL_GROUP_DIM
    for gi, mixed in enumerate(_pool_mix(u, prev, seq_pos, pw_ref, ps_ref)):
        pool_ref[:, gi * c:(gi + 1) * c] = mixed.astype(pool_ref.dtype)

    k = jnp.dot(xn, wb_ref[:, 1 * tn:2 * tn], preferred_element_type=F32)
    o_ref[:, 1 * tn:2 * tn] = k.astype(o_ref.dtype)


def _inproj(x2, g, w, pool_w, pool_scale, seq_len, tm=512, tn=ATTN_WIDTH, chunk_rows=256):
    t, d = x2.shape
    n = w.shape[1]
    assert n == 4 * tn and tn == POOL_WIDTH and seq_len % tm == 0
    halo = 2 * SUBLANES
    assert max(POOL_WINDOWS) <= halo
    return pl.pallas_call(
        functools.partial(_inproj_kernel, tn=tn, seq_tiles=seq_len // tm),
        grid=(t // tm,),
        in_specs=[
            pl.BlockSpec((tm, d), lambda i: (i, 0)),
            pl.BlockSpec((1, d), lambda i: (0, 0)),
            pl.BlockSpec(memory_space=pl.ANY),
            pl.BlockSpec(pool_w.shape, lambda i: (0, 0, 0)),
            pl.BlockSpec((1, tn), lambda i: (0, 0)),
        ],
        out_specs=[pl.BlockSpec((tm, 2 * tn), lambda i: (i, 0)),
                   pl.BlockSpec((tm, tn), lambda i: (i, 0)),
                   pl.BlockSpec((tn, tm), lambda i: (0, i))],
        out_shape=[jax.ShapeDtypeStruct((t, 2 * tn), BF16), jax.ShapeDtypeStruct((t, tn), BF16),
                   jax.ShapeDtypeStruct((tn, t), BF16)],
        scratch_shapes=[pltpu.VMEM((d, n), BF16), pltpu.VMEM((2, chunk_rows, n), F32),
                        pltpu.SemaphoreType.DMA((2,)), pltpu.VMEM((halo, tn), F32)],
        compiler_params=pltpu.CompilerParams(
            dimension_semantics=("arbitrary",), vmem_limit_bytes=VMEM_LIMIT),
        name="inproj",
    )(x2, g.reshape(1, d), w, pool_w, pool_scale.reshape(1, tn))


def _attn_kernel(lq1_ref, lk1_ref, lq2_ref, lk2_ref, q_ref, k_ref, v_ref, g_ref, o_ref,
                 tab_ref, vt_ref, *, tq):
    h = pl.program_id(0)
    s_len = q_ref.shape[0]
    nq = s_len // tq
    d, dv = DIFF_HEAD_DIM, DIFF_V_DIM

    @pl.when(pl.program_id(1) == 0)
    def _():
        slope = jnp.exp2(jnp.full((1, 1), -8.0 / N_DIFF_HEADS, F32) * (h + 1).astype(F32))
        rel = ((nq - 1) * tq + lax.broadcasted_iota(jnp.int32, (s_len, tq), 1)
               - lax.broadcasted_iota(jnp.int32, (s_len, tq), 0))
        tab_ref[...] = jnp.where(rel >= 0, (-LOG2E * slope) * rel.astype(F32), NEG_INF)

    vt_ref[:dv, :] = v_ref[...]
    vt_ref[dv:, :] = jnp.ones((vt_ref.shape[0] - dv, s_len), BF16)

    lam = (jnp.exp(jnp.sum(lq1_ref[...] * lk1_ref[...], axis=1, keepdims=True))
           - jnp.exp(jnp.sum(lq2_ref[...] * lk2_ref[...], axis=1, keepdims=True)) + LAM_INIT)
    dn = (((1,), (1,)), ((), ()))
    lane = lax.broadcasted_iota(jnp.int32, (tq, 2 * d), 1)

    ck = 4 * tq

    def key_chunks(qi):
        n = (qi + 1) * tq
        return [(lo, min(lo + ck, n)) for lo in range(0, n, ck)]

    def query_halves(qi):
        q = q_ref[qi * tq:(qi + 1) * tq, :]
        zero = jnp.zeros_like(q)
        return (jnp.where(lane < d, q, zero), jnp.where(lane >= d, q, zero))

    def score_chunk(qi, qh, lo, hi):
        base = (nq - 1 - qi) * tq
        bias = tab_ref[base + lo:base + hi, :]
        return [lax.dot_general(k_ref[lo:hi, :], qh[a], dn, preferred_element_type=F32) + bias
                for a in range(2)]

    s_next = [score_chunk(0, query_halves(0), lo, hi) for lo, hi in key_chunks(0)]
    for qi in range(nq):
        s_cur = s_next
        cur_chunks = key_chunks(qi)
        m = [functools.reduce(jnp.maximum, [jnp.max(c[a], axis=0, keepdims=True) for c in s_cur])
             for a in range(2)]
        nxt_chunks = key_chunks(qi + 1) if qi + 1 < nq else []
        qh_next = query_halves(qi + 1) if nxt_chunks else None
        s_next = []
        acc = [None, None]
        for c in range(max(len(cur_chunks), len(nxt_chunks))):
            if c < len(nxt_chunks):
                s_next.append(score_chunk(qi + 1, qh_next, *nxt_chunks[c]))
            if c < len(cur_chunks):
                lo, hi = cur_chunks[c]
                for a in range(2):
                    p = jnp.exp2(s_cur[c][a] - m[a]).astype(BF16)
                    part = jnp.dot(vt_ref[:, lo:hi], p, preferred_element_type=F32)
                    acc[a] = part if acc[a] is None else acc[a] + part
        outs = [acc[a][:dv] / acc[a][dv:dv + 1] for a in range(2)]
        ot = outs[0] - lam * outs[1]
        yt = ot * lax.rsqrt(jnp.mean(ot * ot, axis=0, keepdims=True) + RMS_EPS) * g_ref[...]
        o_ref[qi * tq:(qi + 1) * tq, :] = (yt * (1.0 - LAM_INIT)).T.astype(o_ref.dtype)


def _attention(proj3, v_t, lq1, lk1, lq2, lk2, subln_g, tq=256):
    b, s, _ = proj3.shape
    dv = DIFF_V_DIM
    nh = N_DIFF_HEADS
    ones_rows = 16
    lam_spec = pl.BlockSpec((1, DIFF_HEAD_DIM), lambda hi, bi: (0, 0))
    return pl.pallas_call(
        functools.partial(_attn_kernel, tq=tq),
        grid=(nh, b),
        in_specs=[
            lam_spec, lam_spec, lam_spec, lam_spec,
            pl.BlockSpec((None, s, dv), lambda hi, bi: (bi, 0, hi)),
            pl.BlockSpec((None, s, dv), lambda hi, bi: (bi, 0, nh + hi)),
            pl.BlockSpec((dv, s), lambda hi, bi: (hi, bi)),
            pl.BlockSpec((dv, 1), lambda hi, bi: (0, 0)),
        ],
        out_specs=pl.BlockSpec((None, s, dv), lambda hi, bi: (bi, 0, hi)),
        out_shape=jax.ShapeDtypeStruct((b, s, ATTN_WIDTH), BF16),
        scratch_shapes=[
            pltpu.VMEM((s, tq), F32),
            pltpu.VMEM((dv + ones_rows, s), BF16),
        ],
        compiler_params=pltpu.CompilerParams(
            dimension_semantics=("arbitrary", "arbitrary"),
            vmem_limit_bytes=VMEM_LIMIT),
        name="diff_attn",
    )(lq1, lk1, lq2, lk2, proj3, proj3, v_t, subln_g.reshape(dv, 1))


def _pack_bf16_pair(x):
    bits = lax.bitcast_convert_type(x.astype(BF16).astype(F32), jnp.uint32)
    half = bits.shape[1] // 2
    return bits[:, :half] | (bits[:, half:] >> 16)


def _outproj_kernel(a_ref, p_ref, x_ref, w_hbm, g_ref, wr_ref, br_ref,
                    h_ref, hn_ref, ri_ref, rw_ref, w_ref, stage, sem, *, n_sub):
    @pl.when(pl.program_id(0) == 0)
    def _():
        _load_weight_bf16(w_hbm, w_ref, stage, sem)

    ka = a_ref.shape[1]
    hs = x_ref.shape[0] // n_sub
    normed = []
    for r in range(n_sub):
        rows = pl.ds(r * hs, hs)
        mixed = (jnp.dot(a_ref[rows, :], w_ref[:ka, :], preferred_element_type=F32)
                 + jnp.dot(p_ref[rows, :], w_ref[ka:, :], preferred_element_type=F32))
        h = x_ref[rows, :] + mixed
        h_ref[rows, :] = h
        hn = _rms(h, g_ref[...])
        hn_ref[rows, :] = _pack_bf16_pair(hn)
        hn_hi = hn.astype(BF16)
        normed.append((hn_hi, (hn - hn_hi.astype(F32)).astype(BF16)))
    for r in range(n_sub):
        rows = pl.ds(r * hs, hs)
        ri, rw = _route(normed[r][0], normed[r][1], wr_ref, br_ref)
        ri_ref[rows, :] = ri
        rw_ref[rows, :] = rw


def _route(hn_hi, hn_lo, wr_ref, br_ref):
    both = jnp.dot(hn_hi, wr_ref[...], preferred_element_type=F32)
    logits = (both[:, :LANES] + both[:, LANES:]
              + jnp.dot(hn_lo, wr_ref[:, :LANES], preferred_element_type=F32)) + br_ref[...]
    ng, epg = N_EXPERT_GROUPS, EXPERTS_PER_GROUP
    lane = lax.broadcasted_iota(jnp.int32, logits.shape, 1)
    big = jnp.int32(LANES)
    low = jnp.float32(-3.0e38)
    cm = lane < ng
    c = jnp.where(cm, logits, low)
    cmax = jnp.max(c, axis=1, keepdims=True)
    gsel = jnp.min(jnp.where(c == cmax, lane, big), axis=1, keepdims=True)
    p_group = 1.0 / jnp.sum(jnp.where(cm, jnp.exp(c - cmax), 0.0), axis=1, keepdims=True)
    f_lo = ng + epg * gsel
    fm = (lane >= f_lo) & (lane < f_lo + epg)
    f = jnp.where(fm, logits, low)
    v1 = jnp.max(f, axis=1, keepdims=True)
    i1 = jnp.min(jnp.where(fm & (f == v1), lane, big), axis=1, keepdims=True)
    fm2 = fm & (lane != i1)
    f2 = jnp.where(fm2, logits, low)
    v2 = jnp.max(f2, axis=1, keepdims=True)
    i2 = jnp.min(jnp.where(fm2 & (f2 == v2), lane, big), axis=1, keepdims=True)
    e21 = jnp.exp(v2 - v1)
    w1 = p_group / (1.0 + e21)
    w2 = p_group * e21 / (1.0 + e21)
    return (jnp.where(lane == 0, i1 - ng, jnp.where(lane == 1, i2 - ng, 0)),
            jnp.where(lane == 0, w1, jnp.where(lane == 1, w2, 0.0)))


def _outproj(attn2, pool2, x2, w_out, g2, wr, br, tm=512, n_sub=2, chunk_rows=512):
    t, d = x2.shape
    ka, kp = attn2.shape[1], pool2.shape[1]
    row = lambda i: (i, 0)
    const = lambda i: (0, 0)
    return pl.pallas_call(
        functools.partial(_outproj_kernel, n_sub=n_sub),
        grid=(t // tm,),
        in_specs=[
            pl.BlockSpec((tm, ka), row),
            pl.BlockSpec((tm, kp), row),
            pl.BlockSpec((tm, d), row),
            pl.BlockSpec(memory_space=pl.ANY),
            pl.BlockSpec((1, d), const),
            pl.BlockSpec((d, 2 * LANES), const),
            pl.BlockSpec((1, LANES), const),
        ],
        out_specs=[
            pl.BlockSpec((tm, d), row),
            pl.BlockSpec((tm, d // 2), row),
            pl.BlockSpec((tm, LANES), row),
            pl.BlockSpec((tm, LANES), row),
        ],
        out_shape=[
            jax.ShapeDtypeStruct((t, d), F32),
            jax.ShapeDtypeStruct((t, d // 2), jnp.uint32),
            jax.ShapeDtypeStruct((t, LANES), jnp.int32),
            jax.ShapeDtypeStruct((t, LANES), F32),
        ],
        scratch_shapes=[pltpu.VMEM((ka + kp, d), BF16), pltpu.VMEM((2, chunk_rows, d), F32),
                        pltpu.SemaphoreType.DMA((2,))],
        compiler_params=pltpu.CompilerParams(
            dimension_semantics=("arbitrary",), vmem_limit_bytes=VMEM_LIMIT),
        name="outproj_router",
    )(attn2, pool2, x2, w_out, g2.reshape(1, d), wr, br)


def _unpack_bf16_pair(words):
    hi = lax.bitcast_convert_type(words & jnp.uint32(0xFFFF0000), F32).astype(BF16)
    lo = lax.bitcast_convert_type(words << 16, F32).astype(BF16)
    return hi, lo


def _expert_kernel(te_ref, nt_ref, first_ref, slot_ref, ahead_ref,
                   x_ref, wg_hbm, wu_hbm, wd_hbm, y_ref,
                   wgb, wub, wdb, wsem, *, tm, tiles_per_step):
    nt = nt_ref[0]
    n_slots = wgb.shape[0]
    kh = x_ref.shape[1]
    fh = wdb.shape[1] // 2

    def weight_copies(e, s):
        return ((pltpu.make_async_copy(wg_hbm.at[e], wgb.at[s], wsem.at[s]), 0),
                (pltpu.make_async_copy(wu_hbm.at[e], wub.at[s], wsem.at[s]), 1),
                (pltpu.make_async_copy(wd_hbm.at[e, :fh, :], wdb.at[s, :fh, :], wsem.at[s]), 0),
                (pltpu.make_async_copy(wd_hbm.at[e, fh:, :], wdb.at[s, fh:, :], wsem.at[s]), 1))

    n_tab = te_ref.shape[0]

    def expert_ahead(j, k):
        return ahead_ref[(k - 1) * n_tab + j]

    def start_weights(e, s):
        @pl.when(e >= 0)
        def _():
            for c, prio in weight_copies(e, s):
                c.start(priority=prio)

    @pl.when(pl.program_id(0) == 0)
    def _():
        start_weights(te_ref[0], 0)
        for k in range(1, n_slots - 1):
            start_weights(expert_ahead(0, k), k)

    def tile(j, rows):
        @pl.when(j < nt)
        def _():
            ws = slot_ref[j]

            @pl.when(first_ref[j] == 1)
            def _():
                for c, _ in weight_copies(te_ref[j], ws):
                    c.wait()
                start_weights(expert_ahead(j, n_slots - 1), (ws + n_slots - 1) % n_slots)

            xa, xb = _unpack_bf16_pair(x_ref[rows, :])
            hg = (jnp.dot(xa, wgb[ws, :kh, :].astype(BF16), preferred_element_type=F32)
                  + jnp.dot(xb, wgb[ws, kh:, :].astype(BF16), preferred_element_type=F32))
            hu = (jnp.dot(xa, wub[ws, :kh, :].astype(BF16), preferred_element_type=F32)
                  + jnp.dot(xb, wub[ws, kh:, :].astype(BF16), preferred_element_type=F32))
            act = (hg / (1.0 + jnp.exp(-hg))) * hu
            y_ref[rows, :] = _pack_bf16_pair(
                jnp.dot(act.astype(BF16), wdb[ws].astype(BF16), preferred_element_type=F32))

        @pl.when(j >= nt)
        def _():
            y_ref[rows, :] = jnp.zeros((tm, kh), y_ref.dtype)

    for sub in range(tiles_per_step):
        tile(pl.program_id(0) * tiles_per_step + sub, pl.ds(sub * tm, tm))


def _experts(plan, xs, w_gate, w_up, w_down, tm=ROW_TILE, tiles_per_step=1):
    d, f = w_gate.shape[1], w_gate.shape[2]
    assert xs.shape[1] * 2 == d
    nt = plan["tile_expert"].shape[0]
    assert nt % tiles_per_step == 0
    rows = tm * tiles_per_step
    any_spec = pl.BlockSpec(memory_space=pl.ANY)
    grid_spec = pltpu.PrefetchScalarGridSpec(
        num_scalar_prefetch=5,
        grid=(nt // tiles_per_step,),
        in_specs=[
            pl.BlockSpec((rows, d // 2),
                         lambda j, te, n, *_: (jnp.minimum(j, (n[0] - 1) // tiles_per_step), 0)),
            any_spec, any_spec, any_spec],
        out_specs=pl.BlockSpec((rows, d // 2), lambda j, *_: (j, 0)),
        scratch_shapes=[
            pltpu.VMEM((WEIGHT_SLOTS, d, f), F32),
            pltpu.VMEM((WEIGHT_SLOTS, d, f), F32),
            pltpu.VMEM((WEIGHT_SLOTS, f, d), F32),
            pltpu.SemaphoreType.DMA((WEIGHT_SLOTS,)),
        ],
    )
    return pl.pallas_call(
        functools.partial(_expert_kernel, tm=tm, tiles_per_step=tiles_per_step),
        grid_spec=grid_spec,
        out_shape=jax.ShapeDtypeStruct((nt * tm, d // 2), jnp.uint32),
        compiler_params=pltpu.CompilerParams(
            dimension_semantics=("arbitrary",), vmem_limit_bytes=VMEM_LIMIT),
        name="experts",
    )(plan["tile_expert"], plan["n_tiles"], plan["first"], plan["slot"], plan["ahead"],
      xs, w_gate, w_up, w_down)


def _combine_kernel(pos_ref, h_ref, rw_ref, ys_hbm, g_ref, o_ref, ybuf0, ybuf1, ybuf2, sem):
    i = pl.program_id(0)
    last = pl.num_programs(0) - 1
    tm = h_ref.shape[0]
    kh = ybuf0.shape[2]
    bufs = (ybuf0, ybuf1, ybuf2)
    nbuf = len(bufs)

    def start_gather(tile, buf, sm):
        for r in range(tm):
            for k in range(TOP_K_FINE):
                p = pos_ref[(tile * tm + r) * TOP_K_FINE + k]
                pltpu.make_async_copy(ys_hbm.at[pl.ds(p, 1), :], buf.at[k, pl.ds(r, 1), :],
                                      sm).start(priority=k % 2)

    def wait_gather(buf, sm):
        for k in range(TOP_K_FINE):
            pltpu.make_async_copy(ys_hbm.at[pl.ds(0, tm), :], buf.at[k], sm).wait()

    @pl.when(i == 0)
    def _():
        start_gather(0, ybuf0, sem.at[0])
        start_gather(1, ybuf1, sem.at[1])

    def step(par):
        cur = bufs[par]
        ahead = (par + nbuf - 1) % nbuf
        wait_gather(cur, sem.at[par])
        start_gather(jnp.minimum(i + nbuf - 1, last), bufs[ahead], sem.at[ahead])
        w = rw_ref[...]
        halves = []
        for part in range(2):
            y = None
            for k in range(TOP_K_FINE):
                words = cur[k]
                bits = (words & jnp.uint32(0xFFFF0000)) if part == 0 else (words << 16)
                term = w[:, k:k + 1] * lax.bitcast_convert_type(bits, F32)
                y = term if y is None else y + term
            halves.append(h_ref[:, part * kh:(part + 1) * kh] + y)
        ms = sum(jnp.sum(v * v, axis=-1, keepdims=True) for v in halves) / (2 * kh)
        inv = lax.rsqrt(ms + RMS_EPS)
        for part in range(2):
            o_ref[:, part * kh:(part + 1) * kh] = (
                halves[part] * inv * g_ref[:, part * kh:(part + 1) * kh])

        @pl.when(i == last)
        def _():
            for other in range(nbuf):
                if other != par:
                    wait_gather(bufs[other], sem.at[other])

    for par in range(nbuf):
        pl.when(i % nbuf == par)(functools.partial(step, par))


def _combine(pos, h, rw, ys, g, tm=256):
    t, d = h.shape
    grid_spec = pltpu.PrefetchScalarGridSpec(
        num_scalar_prefetch=1,
        grid=(t // tm,),
        in_specs=[
            pl.BlockSpec((tm, d), lambda i, p: (i, 0)),
            pl.BlockSpec((tm, LANES), lambda i, p: (i, 0)),
            pl.BlockSpec(memory_space=pl.ANY),
            pl.BlockSpec((1, d), lambda i, p: (0, 0)),
        ],
        out_specs=pl.BlockSpec((tm, d), lambda i, p: (i, 0)),
        scratch_shapes=[pltpu.VMEM((TOP_K_FINE, tm, d // 2), jnp.uint32),
                        pltpu.VMEM((TOP_K_FINE, tm, d // 2), jnp.uint32),
                        pltpu.VMEM((TOP_K_FINE, tm, d // 2), jnp.uint32),
                        pltpu.SemaphoreType.DMA((3,))],
    )
    return pl.pallas_call(
        _combine_kernel,
        grid_spec=grid_spec,
        out_shape=jax.ShapeDtypeStruct((t, d), F32),
        compiler_params=pltpu.CompilerParams(
            dimension_semantics=("arbitrary",), vmem_limit_bytes=VMEM_LIMIT),
        name="combine",
    )(pos, h, rw, ys, g.reshape(1, d))


def _rank_kernel(ri_ref, meta_ref, cnt_ref, carry_ref, tri_ref):
    tt = ri_ref.shape[0]
    ne = cnt_ref.shape[0]

    @pl.when(pl.program_id(0) == 0)
    def _():
        carry_ref[...] = jnp.zeros_like(carry_ref)
        earlier = (lax.broadcasted_iota(jnp.int32, (tt, tt), 0)
                   < lax.broadcasted_iota(jnp.int32, (tt, tt), 1))
        tri_ref[...] = jnp.where(earlier, 1.0, 0.0).astype(BF16)

    rit = ri_ref[...].astype(F32).T
    expert = lax.broadcasted_iota(jnp.int32, (ne, tt), 0).astype(F32)
    hits = []
    for k in range(TOP_K_FINE):
        hits.append(expert == rit[k:k + 1, :])
    chosen = jnp.where(hits[0] | hits[1], 1.0, 0.0)
    before = (jnp.dot(chosen.astype(BF16), tri_ref[...], preferred_element_type=F32)
              + carry_ref[:, 0:1])
    for k in range(TOP_K_FINE):
        meta_ref[k:k + 1, :] = rit[k:k + 1, :].astype(jnp.int32)
        rank = jnp.sum(jnp.where(hits[k], before, 0.0), axis=0, keepdims=True)
        meta_ref[TOP_K_FINE + k:TOP_K_FINE + k + 1, :] = rank.astype(jnp.int32)
    meta_ref[2 * TOP_K_FINE:, :] = jnp.zeros((meta_ref.shape[0] - 2 * TOP_K_FINE, tt), jnp.int32)
    carry_ref[...] = carry_ref[...] + jnp.sum(chosen, axis=1, keepdims=True)
    cnt_ref[...] = carry_ref[...].astype(jnp.int32)


def _rank(ri, tt=1024):
    t = ri.shape[0]
    assert TOP_K_FINE == 2
    return pl.pallas_call(
        _rank_kernel,
        grid=(t // tt,),
        in_specs=[pl.BlockSpec((tt, LANES), lambda i: (i, 0))],
        out_specs=[pl.BlockSpec((None, 8, tt), lambda i: (i, 0, 0)),
                   pl.BlockSpec((N_EXPERTS, LANES), lambda i: (0, 0))],
        out_shape=[jax.ShapeDtypeStruct((t // tt, 8, tt), jnp.int32),
                   jax.ShapeDtypeStruct((N_EXPERTS, LANES), jnp.int32)],
        scratch_shapes=[pltpu.VMEM((N_EXPERTS, LANES), F32), pltpu.VMEM((tt, tt), BF16)],
        compiler_params=pltpu.CompilerParams(
            dimension_semantics=("arbitrary",), vmem_limit_bytes=VMEM_LIMIT),
        name="route_rank",
    )(ri)


def _scatter_rows_kernel(pos_ref, end_ref, nt_ref, src_hbm, dst_hbm, zbuf, st0, st1, st2,
                         sem, lsem, zsem, *, tm, row_tile):
    i = pl.program_id(0)
    stage = (st0, st1, st2)
    n_tiles_max = dst_hbm.shape[0] // row_tile

    def zero_copy(tile):
        row0 = pl.multiple_of(tile * row_tile, row_tile)
        return pltpu.make_async_copy(zbuf, dst_hbm.at[pl.ds(row0, row_tile), :], zsem)

    @pl.when(i == 0)
    def _():
        zbuf[...] = jnp.zeros_like(zbuf)
        n_exp = end_ref.shape[0]

        def has_rows(e):
            return end_ref[e] > (end_ref[e - 1] if e else 0)

        for e in range(n_exp):
            pl.when(has_rows(e))(lambda e=e: zero_copy(end_ref[e] - 1).start())

        def start_tail(tile, c):
            zero_copy(tile).start()
            return c

        def wait_tail(tile, c):
            zero_copy(tile).wait()
            return c

        lax.fori_loop(nt_ref[0], n_tiles_max, start_tail, 0)
        for e in range(n_exp):
            pl.when(has_rows(e))(lambda: zero_copy(0).wait())
        lax.fori_loop(nt_ref[0], n_tiles_max, wait_tail, 0)

    last = pl.num_programs(0) - 1
    nbuf = len(stage)

    def load(tile, b):
        row0 = pl.multiple_of(tile * tm, tm)
        return pltpu.make_async_copy(src_hbm.at[pl.ds(row0, tm), :], stage[b], lsem.at[b])

    def wait_rows(b):
        for _ in range(TOP_K_FINE):
            pltpu.make_async_copy(stage[b], dst_hbm.at[pl.ds(0, tm), :], sem.at[b]).wait()

    @pl.when(i == 0)
    def _():
        load(0, 0).start()

    def step(b):
        nb = (b + 1) % nbuf

        @pl.when(i >= nbuf - 1)
        def _():
            wait_rows(nb)

        @pl.when(i < last)
        def _():
            load(i + 1, nb).start()

        load(i, b).wait()
        for r in range(tm):
            for k in range(TOP_K_FINE):
                p = pos_ref[(i * tm + r) * TOP_K_FINE + k]
                pltpu.make_async_copy(stage[b].at[pl.ds(r, 1), :], dst_hbm.at[pl.ds(p, 1), :],
                                      sem.at[b]).start(priority=k % 2)

        @pl.when(i == last)
        def _():
            for d in range(nbuf - 1):
                @pl.when(i >= d)
                def _(d=d):
                    wait_rows((b - d) % nbuf)

    for b in range(nbuf):
        pl.when(i % nbuf == b)(functools.partial(step, b))


def _scatter_rows(plan, src, tm=256, row_tile=ROW_TILE):
    t, w = src.shape
    any_spec = pl.BlockSpec(memory_space=pl.ANY)
    grid_spec = pltpu.PrefetchScalarGridSpec(
        num_scalar_prefetch=3,
        grid=(t // tm,),
        in_specs=[any_spec],
        out_specs=any_spec,
        scratch_shapes=[pltpu.VMEM((row_tile, w), src.dtype),
                        pltpu.VMEM((tm, w), src.dtype), pltpu.VMEM((tm, w), src.dtype),
                        pltpu.VMEM((tm, w), src.dtype),
                        pltpu.SemaphoreType.DMA((3,)), pltpu.SemaphoreType.DMA((3,)),
                        pltpu.SemaphoreType.DMA(())],
    )
    return pl.pallas_call(
        functools.partial(_scatter_rows_kernel, tm=tm, row_tile=row_tile),
        grid_spec=grid_spec,
        out_shape=jax.ShapeDtypeStruct((plan["n_rows"], w), src.dtype),
        compiler_params=pltpu.CompilerParams(dimension_semantics=("arbitrary",)),
        name="scatter_rows",
    )(plan["pos"], plan["end_tile"], plan["n_tiles"], src)


def _sort_plan(ri, tm=ROW_TILE):
    t = ri.shape[0]
    k = TOP_K_FINE
    a = t * k
    nt = a // tm + N_EXPERTS
    meta, cnt = _rank(ri)
    counts = cnt[:, 0]
    padded = ((counts + tm - 1) // tm) * tm
    off_end = jnp.cumsum(padded)
    off = off_end - padded
    ef = meta[:, 0:k, :]
    eid = jnp.arange(N_EXPERTS, dtype=jnp.int32)
    pos = jnp.sum(jnp.where(ef[..., None] == eid, off, 0), axis=-1) + meta[:, k:2 * k, :]
    pos = jnp.transpose(pos, (0, 2, 1)).reshape(a)
    n_tiles = off_end[-1] // tm
    tile_idx = jnp.arange(nt, dtype=jnp.int32)
    te = jnp.sum((off_end[None, :] <= (tile_idx * tm)[:, None]).astype(jnp.int32), axis=1)
    te = jnp.minimum(te, N_EXPERTS - 1)
    te = jnp.where(tile_idx < n_tiles, te, te[n_tiles - 1]).astype(jnp.int32)
    first = ((tile_idx == 0) | (te != jnp.roll(te, 1))) & (tile_idx < n_tiles)
    slot = (jnp.cumsum(first.astype(jnp.int32)) - 1) % WEIGHT_SLOTS
    eid = jnp.arange(N_EXPERTS, dtype=jnp.int32)
    later = (eid[None, :] > eid[:, None]) & (counts[None, :] > 0)
    next_e = jnp.min(jnp.where(later, eid[None, :], N_EXPERTS), axis=1)
    next_e = jnp.where(next_e < N_EXPERTS, next_e, -1)
    ahead = []
    nxt = next_e
    for _ in range(WEIGHT_SLOTS - 1):
        ahead.append(nxt[te])
        nxt = jnp.where(nxt >= 0, next_e[jnp.maximum(nxt, 0)], -1)
    return dict(
        ahead=jnp.concatenate(ahead).astype(jnp.int32),
        end_tile=(off_end // tm).astype(jnp.int32),
        pos=pos.astype(jnp.int32), n_rows=nt * tm, tile_expert=te,
        n_tiles=n_tiles.reshape(1).astype(jnp.int32), first=first.astype(jnp.int32),
        slot=slot.astype(jnp.int32))


def kernel(x, norm1_g, w_in, lambda_q1, lambda_k1, lambda_q2, lambda_k2, subln_g, pool_w, pool_scale, w_out, norm2_g, w_coarse, b_coarse, w_fine, b_fine, w_gate, w_up, w_down, final_norm_g):
    b, s, d = x.shape
    t = b * s
    assert norm1_g.shape[0] == 1
    x2 = x.reshape(t, d)
    proj, pool, v_t = _inproj(x2, norm1_g[0], w_in[0], pool_w[0], pool_scale[0], s)
    proj3 = proj.reshape(b, s, proj.shape[1])
    attn = _attention(proj3, v_t, lambda_q1, lambda_k1, lambda_q2, lambda_k2, subln_g[0])

    ng, epg = N_EXPERT_GROUPS, EXPERTS_PER_GROUP
    wr = jnp.concatenate(
        [w_coarse[0], jnp.transpose(w_fine[0], (1, 0, 2)).reshape(d, ng * epg)], axis=1)
    wr = jnp.pad(wr, ((0, 0), (0, LANES - wr.shape[1])))
    br = jnp.concatenate([b_coarse[0], b_fine[0].reshape(ng * epg)])
    br = jnp.pad(br, (0, LANES - br.shape[0])).reshape(1, LANES)
    wr_hi = wr.astype(BF16)
    wr_lo = (wr - wr_hi.astype(F32)).astype(BF16)
    h, hn, ri, rw = _outproj(attn.reshape(t, ATTN_WIDTH), pool, x2,
                             w_out[0], norm2_g[0], jnp.concatenate([wr_hi, wr_lo], axis=1), br)

    plan = _sort_plan(ri)
    xs = _scatter_rows(plan, hn)
    ys = _experts(plan, xs, w_gate[0], w_up[0], w_down[0])
    out = _combine(plan["pos"], h, rw, ys, final_norm_g)
    return out.reshape(b, s, d)
```

```python
import functools
import math

import jax
import jax.numpy as jnp
from jax import lax
from jax.experimental import pallas as pl
from jax.experimental.pallas import tpu as pltpu

N_DIFF_HEADS = 8
DIFF_HEAD_DIM = 64
DIFF_V_DIM = 2 * DIFF_HEAD_DIM
ATTN_WIDTH = N_DIFF_HEADS * DIFF_V_DIM
POOL_WINDOWS = (2, 4, 8, 16)
POOL_GROUP_DIM = 256
POOL_WIDTH = len(POOL_WINDOWS) * POOL_GROUP_DIM
N_EXPERT_GROUPS = 4
EXPERTS_PER_GROUP = 8
N_EXPERTS = N_EXPERT_GROUPS * EXPERTS_PER_GROUP
TOP_K_FINE = 2
RMS_EPS = 1e-6
NEG_INF = -1e30
LAM_INIT = 0.8 - 0.6 * math.exp(-0.3 * 0)
LOG2E = math.log2(math.e)
Q_SCALE = DIFF_HEAD_DIM ** -0.5 * LOG2E

LANES = 128
SUBLANES = 8
VMEM_LIMIT = 56 * 1024 * 1024
ROW_TILE = 256
WEIGHT_SLOTS = 3

F32 = jnp.float32
BF16 = jnp.bfloat16


def _rms(x, g):
    return x * lax.rsqrt(jnp.mean(x * x, axis=-1, keepdims=True) + RMS_EPS) * g


def _load_weight_bf16(w_hbm, wb_ref, stage, sem):
    ck = stage.shape[1]
    nchunk = w_hbm.shape[0] // ck

    def copy(c):
        return pltpu.make_async_copy(w_hbm.at[c * ck:(c + 1) * ck, :], stage.at[c % 2], sem.at[c % 2])

    copy(0).start()
    for c in range(nchunk):
        if c + 1 < nchunk:
            copy(c + 1).start()
        copy(c).wait()
        wb_ref[c * ck:(c + 1) * ck, :] = stage[c % 2].astype(BF16)


def _pool_mix(u, prev, seq_pos, pw_ref, ps_ref):
    halo = prev.shape[0]
    c = POOL_GROUP_DIM
    ext = jnp.concatenate([prev, u], axis=0)
    outs = []
    for gi, win in enumerate(POOL_WINDOWS):
        assert win & (win - 1) == 0 and win <= halo
        run = ext[:, gi * c:(gi + 1) * c]
        span = 1
        while span < win:
            run = run + pltpu.roll(run, span, axis=0)
            span *= 2
        count = jnp.minimum(seq_pos + 1, win).astype(F32)
        pooled = run[halo:, :] / count - u[:, gi * c:(gi + 1) * c]
        mixed = jnp.dot(pooled.astype(BF16), pw_ref[gi].astype(BF16), preferred_element_type=F32)
        outs.append(mixed * ps_ref[:, gi * c:(gi + 1) * c])
    return outs


def _inproj_kernel(x_ref, g_ref, w_hbm, pw_ref, ps_ref, o_ref, pool_ref, vt_ref,
                   wb_ref, stage, sem, halo_ref, *, tn, seq_tiles):
    i = pl.program_id(0)
    tm = x_ref.shape[0]

    @pl.when(i == 0)
    def _():
        halo_ref[...] = jnp.zeros_like(halo_ref)
        _load_weight_bf16(w_hbm, wb_ref, stage, sem)

    xn = _rms(x_ref[...], g_ref[...]).astype(BF16)
    u = jnp.dot(xn, wb_ref[:, 3 * tn:4 * tn], preferred_element_type=F32)
    v = jnp.dot(xn, wb_ref[:, 2 * tn:3 * tn], preferred_element_type=F32)
    vt_ref[...] = v.T.astype(vt_ref.dtype)
    q = jnp.dot(xn, wb_ref[:, 0 * tn:1 * tn], preferred_element_type=F32) * Q_SCALE
    o_ref[:, 0 * tn:1 * tn] = q.astype(o_ref.dtype)

    tile_in_seq = i % seq_tiles
    prev = jnp.where(tile_in_seq == 0, 0.0, halo_ref[...])
    halo_ref[...] = u[tm - halo_ref.shape[0]:, :]
    seq_pos = tile_in_seq * tm + lax.broadcasted_iota(jnp.int32, (tm, 1), 0)
    c = POOL_GROUP_DIM
    for gi, mixed in enumerate(_pool_mix(u, prev, seq_pos, pw_ref, ps_ref)):
        pool_ref[:, gi * c:(gi + 1) * c] = mixed.astype(pool_ref.dtype)

    k = jnp.dot(xn, wb_ref[:, 1 * tn:2 * tn], preferred_element_type=F32)
    o_ref[:, 1 * tn:2 * tn] = k.astype(o_ref.dtype)


def _inproj(x2, g, w, pool_w, pool_scale, seq_len, tm=512, tn=ATTN_WIDTH, chunk_rows=256):
    t, d = x2.shape
    n = w.shape[1]
    assert n == 4 * tn and tn == POOL_WIDTH and seq_len % tm == 0
    halo = 2 * SUBLANES
    assert max(POOL_WINDOWS) <= halo
    return pl.pallas_call(
        functools.partial(_inproj_kernel, tn=tn, seq_tiles=seq_len // tm),
        grid=(t // tm,),
        in_specs=[
            pl.BlockSpec((tm, d), lambda i: (i, 0)),
            pl.BlockSpec((1, d), lambda i: (0, 0)),
            pl.BlockSpec(memory_space=pl.ANY),
            pl.BlockSpec(pool_w.shape, lambda i: (0, 0, 0)),
            pl.BlockSpec((1, tn), lambda i: (0, 0)),
        ],
        out_specs=[pl.BlockSpec((tm, 2 * tn), lambda i: (i, 0)),
                   pl.BlockSpec((tm, tn), lambda i: (i, 0)),
                   pl.BlockSpec((tn, tm), lambda i: (0, i))],
        out_shape=[jax.ShapeDtypeStruct((t, 2 * tn), BF16), jax.ShapeDtypeStruct((t, tn), BF16),
                   jax.ShapeDtypeStruct((tn, t), BF16)],
        scratch_shapes=[pltpu.VMEM((d, n), BF16), pltpu.VMEM((2, chunk_rows, n), F32),
                        pltpu.SemaphoreType.DMA((2,)), pltpu.VMEM((halo, tn), F32)],
        compiler_params=pltpu.CompilerParams(
            dimension_semantics=("arbitrary",), vmem_limit_bytes=VMEM_LIMIT),
        name="inproj",
    )(x2, g.reshape(1, d), w, pool_w, pool_scale.reshape(1, tn))


def _attn_kernel(lq1_ref, lk1_ref, lq2_ref, lk2_ref, q_ref, k_ref, v_ref, g_ref, o_ref,
                 tab_ref, vt_ref, *, tq):
    h = pl.program_id(0)
    s_len = q_ref.shape[0]
    nq = s_len // tq
    d, dv = DIFF_HEAD_DIM, DIFF_V_DIM

    @pl.when(pl.program_id(1) == 0)
    def _():
        slope = jnp.exp2(jnp.full((1, 1), -8.0 / N_DIFF_HEADS, F32) * (h + 1).astype(F32))
        rel = ((nq - 1) * tq + lax.broadcasted_iota(jnp.int32, (s_len, tq), 1)
               - lax.broadcasted_iota(jnp.int32, (s_len, tq), 0))
        tab_ref[...] = jnp.where(rel >= 0, (-LOG2E * slope) * rel.astype(F32), NEG_INF)

    vt_ref[:dv, :] = v_ref[...]
    vt_ref[dv:, :] = jnp.ones((vt_ref.shape[0] - dv, s_len), BF16)

    lam = (jnp.exp(jnp.sum(lq1_ref[...] * lk1_ref[...], axis=1, keepdims=True))
           - jnp.exp(jnp.sum(lq2_ref[...] * lk2_ref[...], axis=1, keepdims=True)) + LAM_INIT)
    dn = (((1,), (1,)), ((), ()))
    lane = lax.broadcasted_iota(jnp.int32, (tq, 2 * d), 1)

    ck = 4 * tq

    def key_chunks(qi):
        n = (qi + 1) * tq
        return [(lo, min(lo + ck, n)) for lo in range(0, n, ck)]

    def query_halves(qi):
        q = q_ref[qi * tq:(qi + 1) * tq, :]
        zero = jnp.zeros_like(q)
        return (jnp.where(lane < d, q, zero), jnp.where(lane >= d, q, zero))

    def score_chunk(qi, qh, lo, hi):
        base = (nq - 1 - qi) * tq
        bias = tab_ref[base + lo:base + hi, :]
        return [lax.dot_general(k_ref[lo:hi, :], qh[a], dn, preferred_element_type=F32) + bias
                for a in range(2)]

    s_next = [score_chunk(0, query_halves(0), lo, hi) for lo, hi in key_chunks(0)]
    for qi in range(nq):
        s_cur = s_next
        cur_chunks = key_chunks(qi)
        m = [functools.reduce(jnp.maximum, [jnp.max(c[a], axis=0, keepdims=True) for c in s_cur])
             for a in range(2)]
        nxt_chunks = key_chunks(qi + 1) if qi + 1 < nq else []
        qh_next = query_halves(qi + 1) if nxt_chunks else None
        s_next = []
        acc = [None, None]
        for c in range(max(len(cur_chunks), len(nxt_chunks))):
            if c < len(nxt_chunks):
                s_next.append(score_chunk(qi + 1, qh_next, *nxt_chunks[c]))
            if c < len(cur_chunks):
                lo, hi = cur_chunks[c]
                for a in range(2):
                    p = jnp.exp2(s_cur[c][a] - m[a]).astype(BF16)
                    part = jnp.dot(vt_ref[:, lo:hi], p, preferred_element_type=F32)
                    acc[a] = part if acc[a] is None else acc[a] + part
        outs = [acc[a][:dv] / acc[a][dv:dv + 1] for a in range(2)]
        ot = outs[0] - lam * outs[1]
        yt = ot * lax.rsqrt(jnp.mean(ot * ot, axis=0, keepdims=True) + RMS_EPS) * g_ref[...]
        o_ref[qi * tq:(qi + 1) * tq, :] = (yt * (1.0 - LAM_INIT)).T.astype(o_ref.dtype)


def _attention(proj3, v_t, lq1, lk1, lq2, lk2, subln_g, tq=256):
    b, s, _ = proj3.shape
    dv = DIFF_V_DIM
    nh = N_DIFF_HEADS
    ones_rows = 16
    lam_spec = pl.BlockSpec((1, DIFF_HEAD_DIM), lambda hi, bi: (0, 0))
    return pl.pallas_call(
        functools.partial(_attn_kernel, tq=tq),
        grid=(nh, b),
        in_specs=[
            lam_spec, lam_spec, lam_spec, lam_spec,
            pl.BlockSpec((None, s, dv), lambda hi, bi: (bi, 0, hi)),
            pl.BlockSpec((None, s, dv), lambda hi, bi: (bi, 0, nh + hi)),
            pl.BlockSpec((dv, s), lambda hi, bi: (hi, bi)),
            pl.BlockSpec((dv, 1), lambda hi, bi: (0, 0)),
        ],
        out_specs=pl.BlockSpec((None, s, dv), lambda hi, bi: (bi, 0, hi)),
        out_shape=jax.ShapeDtypeStruct((b, s, ATTN_WIDTH), BF16),
        scratch_shapes=[
            pltpu.VMEM((s, tq), F32),
            pltpu.VMEM((dv + ones_rows, s), BF16),
        ],
        compiler_params=pltpu.CompilerParams(
            dimension_semantics=("arbitrary", "arbitrary"),
            vmem_limit_bytes=VMEM_LIMIT),
        name="diff_attn",
    )(lq1, lk1, lq2, lk2, proj3, proj3, v_t, subln_g.reshape(dv, 1))


def _pack_bf16_pair(x):
    bits = lax.bitcast_convert_type(x.astype(BF16).astype(F32), jnp.uint32)
    half = bits.shape[1] // 2
    return bits[:, :half] | (bits[:, half:] >> 16)


def _outproj_kernel(a_ref, p_ref, x_ref, w_hbm, g_ref, wr_ref, br_ref,
                    h_ref, hn_ref, ri_ref, rw_ref, xz_hbm, w_ref, stage, sem, zbuf, zsem,
                    *, n_sub, z_per_step):
    i = pl.program_id(0)

    @pl.when(i == 0)
    def _():
        zbuf[...] = jnp.zeros_like(zbuf)
        _load_weight_bf16(w_hbm, w_ref, stage, sem)

    zr = zbuf.shape[0]

    def zero_copy(c):
        row0 = pl.multiple_of((i * z_per_step + c) * zr, zr)
        return pltpu.make_async_copy(zbuf, xz_hbm.at[pl.ds(row0, zr), :], zsem)

    for c in range(z_per_step):
        zero_copy(c).start()

    ka = a_ref.shape[1]
    hs = x_ref.shape[0] // n_sub
    normed = []
    for r in range(n_sub):
        rows = pl.ds(r * hs, hs)
        mixed = (jnp.dot(a_ref[rows, :], w_ref[:ka, :], preferred_element_type=F32)
                 + jnp.dot(p_ref[rows, :], w_ref[ka:, :], preferred_element_type=F32))
        h = x_ref[rows, :] + mixed
        h_ref[rows, :] = h
        hn = _rms(h, g_ref[...])
        hn_ref[rows, :] = _pack_bf16_pair(hn)
        hn_hi = hn.astype(BF16)
        normed.append((hn_hi, (hn - hn_hi.astype(F32)).astype(BF16)))
    for r in range(n_sub):
        rows = pl.ds(r * hs, hs)
        ri, rw = _route(normed[r][0], normed[r][1], wr_ref, br_ref)
        ri_ref[rows, :] = ri
        rw_ref[rows, :] = rw
    for c in range(z_per_step):
        zero_copy(c).wait()


def _route(hn_hi, hn_lo, wr_ref, br_ref):
    both = jnp.dot(hn_hi, wr_ref[...], preferred_element_type=F32)
    logits = (both[:, :LANES] + both[:, LANES:]
              + jnp.dot(hn_lo, wr_ref[:, :LANES], preferred_element_type=F32)) + br_ref[...]
    ng, epg = N_EXPERT_GROUPS, EXPERTS_PER_GROUP
    lane = lax.broadcasted_iota(jnp.int32, logits.shape, 1)
    big = jnp.int32(LANES)
    low = jnp.float32(-3.0e38)
    cm = lane < ng
    c = jnp.where(cm, logits, low)
    cmax = jnp.max(c, axis=1, keepdims=True)
    gsel = jnp.min(jnp.where(c == cmax, lane, big), axis=1, keepdims=True)
    p_group = 1.0 / jnp.sum(jnp.where(cm, jnp.exp(c - cmax), 0.0), axis=1, keepdims=True)
    f_lo = ng + epg * gsel
    fm = (lane >= f_lo) & (lane < f_lo + epg)
    f = jnp.where(fm, logits, low)
    v1 = jnp.max(f, axis=1, keepdims=True)
    i1 = jnp.min(jnp.where(fm & (f == v1), lane, big), axis=1, keepdims=True)
    fm2 = fm & (lane != i1)
    f2 = jnp.where(fm2, logits, low)
    v2 = jnp.max(f2, axis=1, keepdims=True)
    i2 = jnp.min(jnp.where(fm2 & (f2 == v2), lane, big), axis=1, keepdims=True)
    e21 = jnp.exp(v2 - v1)
    w1 = p_group / (1.0 + e21)
    w2 = p_group * e21 / (1.0 + e21)
    return (jnp.where(lane == 0, i1 - ng, jnp.where(lane == 1, i2 - ng, 0)),
            jnp.where(lane == 0, w1, jnp.where(lane == 1, w2, 0.0)))


def _outproj(attn2, pool2, x2, w_out, g2, wr, br, sorted_rows, tm=512, n_sub=2, chunk_rows=512,
             zero_rows=ROW_TILE):
    t, d = x2.shape
    ka, kp = attn2.shape[1], pool2.shape[1]
    steps = t // tm
    assert sorted_rows % (steps * zero_rows) == 0
    row = lambda i: (i, 0)
    const = lambda i: (0, 0)
    return pl.pallas_call(
        functools.partial(_outproj_kernel, n_sub=n_sub, z_per_step=sorted_rows // (steps * zero_rows)),
        grid=(steps,),
        in_specs=[
            pl.BlockSpec((tm, ka), row),
            pl.BlockSpec((tm, kp), row),
            pl.BlockSpec((tm, d), row),
            pl.BlockSpec(memory_space=pl.ANY),
            pl.BlockSpec((1, d), const),
            pl.BlockSpec((d, 2 * LANES), const),
            pl.BlockSpec((1, LANES), const),
        ],
        out_specs=[
            pl.BlockSpec((tm, d), row),
            pl.BlockSpec((tm, d // 2), row),
            pl.BlockSpec((tm, LANES), row),
            pl.BlockSpec((tm, LANES), row),
            pl.BlockSpec(memory_space=pl.ANY),
        ],
        out_shape=[
            jax.ShapeDtypeStruct((t, d), F32),
            jax.ShapeDtypeStruct((t, d // 2), jnp.uint32),
            jax.ShapeDtypeStruct((t, LANES), jnp.int32),
            jax.ShapeDtypeStruct((t, LANES), F32),
            jax.ShapeDtypeStruct((sorted_rows, d // 2), jnp.uint32),
        ],
        scratch_shapes=[pltpu.VMEM((ka + kp, d), BF16), pltpu.VMEM((2, chunk_rows, d), F32),
                        pltpu.SemaphoreType.DMA((2,)),
                        pltpu.VMEM((zero_rows, d // 2), jnp.uint32), pltpu.SemaphoreType.DMA(())],
        compiler_params=pltpu.CompilerParams(
            dimension_semantics=("arbitrary",), vmem_limit_bytes=VMEM_LIMIT),
        name="outproj_router",
    )(attn2, pool2, x2, w_out, g2.reshape(1, d), wr, br)


def _unpack_bf16_pair(words):
    hi = lax.bitcast_convert_type(words & jnp.uint32(0xFFFF0000), F32).astype(BF16)
    lo = lax.bitcast_convert_type(words << 16, F32).astype(BF16)
    return hi, lo


def _expert_kernel(te_ref, nt_ref, first_ref, slot_ref, nxt_ref, nxt2_ref,
                   x_ref, wg_hbm, wu_hbm, wd_hbm, y_ref,
                   wgb, wub, wdb, wsem, *, tm, tiles_per_step):
    nt = nt_ref[0]
    n_slots = wgb.shape[0]
    kh = x_ref.shape[1]
    fh = wdb.shape[1] // 2

    def weight_copies(e, s):
        return ((pltpu.make_async_copy(wg_hbm.at[e], wgb.at[s], wsem.at[s]), 0),
                (pltpu.make_async_copy(wu_hbm.at[e], wub.at[s], wsem.at[s]), 1),
                (pltpu.make_async_copy(wd_hbm.at[e, :fh, :], wdb.at[s, :fh, :], wsem.at[s]), 0),
                (pltpu.make_async_copy(wd_hbm.at[e, fh:, :], wdb.at[s, fh:, :], wsem.at[s]), 1))

    @pl.when(pl.program_id(0) == 0)
    def _():
        for c, prio in weight_copies(te_ref[0], 0):
            c.start(priority=prio)

        @pl.when(nxt_ref[0] >= 0)
        def _():
            for c, prio in weight_copies(nxt_ref[0], 1):
                c.start(priority=prio)

    def tile(j, rows):
        @pl.when(j < nt)
        def _():
            ws = slot_ref[j]

            @pl.when(first_ref[j] == 1)
            def _():
                for c, _ in weight_copies(te_ref[j], ws):
                    c.wait()

                @pl.when(nxt2_ref[j] >= 0)
                def _():
                    for c, prio in weight_copies(nxt2_ref[j], (ws + 2) % n_slots):
                        c.start(priority=prio)

            xa, xb = _unpack_bf16_pair(x_ref[rows, :])
            hg = (jnp.dot(xa, wgb[ws, :kh, :].astype(BF16), preferred_element_type=F32)
                  + jnp.dot(xb, wgb[ws, kh:, :].astype(BF16), preferred_element_type=F32))
            hu = (jnp.dot(xa, wub[ws, :kh, :].astype(BF16), preferred_element_type=F32)
                  + jnp.dot(xb, wub[ws, kh:, :].astype(BF16), preferred_element_type=F32))
            act = (hg / (1.0 + jnp.exp(-hg))) * hu
            y_ref[rows, :] = _pack_bf16_pair(
                jnp.dot(act.astype(BF16), wdb[ws].astype(BF16), preferred_element_type=F32))

        @pl.when(j >= nt)
        def _():
            y_ref[rows, :] = jnp.zeros((tm, kh), y_ref.dtype)

    for sub in range(tiles_per_step):
        tile(pl.program_id(0) * tiles_per_step + sub, pl.ds(sub * tm, tm))


def _experts(plan, xs, w_gate, w_up, w_down, tm=ROW_TILE, tiles_per_step=2):
    d, f = w_gate.shape[1], w_gate.shape[2]
    assert xs.shape[1] * 2 == d
    nt = plan["tile_expert"].shape[0]
    assert nt % tiles_per_step == 0
    rows = tm * tiles_per_step
    any_spec = pl.BlockSpec(memory_space=pl.ANY)
    grid_spec = pltpu.PrefetchScalarGridSpec(
        num_scalar_prefetch=6,
        grid=(nt // tiles_per_step,),
        in_specs=[
            pl.BlockSpec((rows, d // 2),
                         lambda j, te, n, *_: (jnp.minimum(j, (n[0] - 1) // tiles_per_step), 0)),
            any_spec, any_spec, any_spec],
        out_specs=pl.BlockSpec((rows, d // 2), lambda j, *_: (j, 0)),
        scratch_shapes=[
            pltpu.VMEM((WEIGHT_SLOTS, d, f), F32),
            pltpu.VMEM((WEIGHT_SLOTS, d, f), F32),
            pltpu.VMEM((WEIGHT_SLOTS, f, d), F32),
            pltpu.SemaphoreType.DMA((WEIGHT_SLOTS,)),
        ],
    )
    return pl.pallas_call(
        functools.partial(_expert_kernel, tm=tm, tiles_per_step=tiles_per_step),
        grid_spec=grid_spec,
        out_shape=jax.ShapeDtypeStruct((nt * tm, d // 2), jnp.uint32),
        compiler_params=pltpu.CompilerParams(
            dimension_semantics=("arbitrary",), vmem_limit_bytes=VMEM_LIMIT),
        name="experts",
    )(plan["tile_expert"], plan["n_tiles"], plan["first"], plan["slot"], plan["next_expert"],
      plan["next2_expert"], xs, w_gate, w_up, w_down)


def _combine_kernel(pos_ref, h_ref, rw_ref, ys_hbm, g_ref, o_ref, ybuf0, ybuf1, ybuf2, sem):
    i = pl.program_id(0)
    last = pl.num_programs(0) - 1
    tm = h_ref.shape[0]
    kh = ybuf0.shape[2]
    bufs = (ybuf0, ybuf1, ybuf2)
    nbuf = len(bufs)

    def start_gather(tile, buf, sm):
        for r in range(tm):
            for k in range(TOP_K_FINE):
                p = pos_ref[(tile * tm + r) * TOP_K_FINE + k]
                pltpu.make_async_copy(ys_hbm.at[pl.ds(p, 1), :], buf.at[k, pl.ds(r, 1), :],
                                      sm).start(priority=k % 2)

    def wait_gather(buf, sm):
        for k in range(TOP_K_FINE):
            pltpu.make_async_copy(ys_hbm.at[pl.ds(0, tm), :], buf.at[k], sm).wait()

    @pl.when(i == 0)
    def _():
        start_gather(0, ybuf0, sem.at[0])
        start_gather(1, ybuf1, sem.at[1])

    def step(par):
        cur = bufs[par]
        ahead = (par + nbuf - 1) % nbuf
        wait_gather(cur, sem.at[par])
        start_gather(jnp.minimum(i + nbuf - 1, last), bufs[ahead], sem.at[ahead])
        w = rw_ref[...]
        halves = []
        for part in range(2):
            y = None
            for k in range(TOP_K_FINE):
                words = cur[k]
                bits = (words & jnp.uint32(0xFFFF0000)) if part == 0 else (words << 16)
                term = w[:, k:k + 1] * lax.bitcast_convert_type(bits, F32)
                y = term if y is None else y + term
            halves.append(h_ref[:, part * kh:(part + 1) * kh] + y)
        ms = sum(jnp.sum(v * v, axis=-1, keepdims=True) for v in halves) / (2 * kh)
        inv = lax.rsqrt(ms + RMS_EPS)
        for part in range(2):
            o_ref[:, part * kh:(part + 1) * kh] = (
                halves[part] * inv * g_ref[:, part * kh:(part + 1) * kh])

        @pl.when(i == last)
        def _():
            for other in range(nbuf):
                if other != par:
                    wait_gather(bufs[other], sem.at[other])

    for par in range(nbuf):
        pl.when(i % nbuf == par)(functools.partial(step, par))


def _combine(pos, h, rw, ys, g, tm=256):
    t, d = h.shape
    grid_spec = pltpu.PrefetchScalarGridSpec(
        num_scalar_prefetch=1,
        grid=(t // tm,),
        in_specs=[
            pl.BlockSpec((tm, d), lambda i, p: (i, 0)),
            pl.BlockSpec((tm, LANES), lambda i, p: (i, 0)),
            pl.BlockSpec(memory_space=pl.ANY),
            pl.BlockSpec((1, d), lambda i, p: (0, 0)),
        ],
        out_specs=pl.BlockSpec((tm, d), lambda i, p: (i, 0)),
        scratch_shapes=[pltpu.VMEM((TOP_K_FINE, tm, d // 2), jnp.uint32),
                        pltpu.VMEM((TOP_K_FINE, tm, d // 2), jnp.uint32),
                        pltpu.VMEM((TOP_K_FINE, tm, d // 2), jnp.uint32),
                        pltpu.SemaphoreType.DMA((3,))],
    )
    return pl.pallas_call(
        _combine_kernel,
        grid_spec=grid_spec,
        out_shape=jax.ShapeDtypeStruct((t, d), F32),
        compiler_params=pltpu.CompilerParams(
            dimension_semantics=("arbitrary",), vmem_limit_bytes=VMEM_LIMIT),
        name="combine",
    )(pos, h, rw, ys, g.reshape(1, d))


def _rank_kernel(ri_ref, meta_ref, cnt_ref, carry_ref, tri_ref):
    tt = ri_ref.shape[0]
    ne = cnt_ref.shape[0]

    @pl.when(pl.program_id(0) == 0)
    def _():
        carry_ref[...] = jnp.zeros_like(carry_ref)
        earlier = (lax.broadcasted_iota(jnp.int32, (tt, tt), 0)
                   < lax.broadcasted_iota(jnp.int32, (tt, tt), 1))
        tri_ref[...] = jnp.where(earlier, 1.0, 0.0).astype(BF16)

    rit = ri_ref[...].astype(F32).T
    expert = lax.broadcasted_iota(jnp.int32, (ne, tt), 0).astype(F32)
    hits = []
    for k in range(TOP_K_FINE):
        hits.append(expert == rit[k:k + 1, :])
    chosen = jnp.where(hits[0] | hits[1], 1.0, 0.0)
    before = (jnp.dot(chosen.astype(BF16), tri_ref[...], preferred_element_type=F32)
              + carry_ref[:, 0:1])
    for k in range(TOP_K_FINE):
        meta_ref[k:k + 1, :] = rit[k:k + 1, :].astype(jnp.int32)
        rank = jnp.sum(jnp.where(hits[k], before, 0.0), axis=0, keepdims=True)
        meta_ref[TOP_K_FINE + k:TOP_K_FINE + k + 1, :] = rank.astype(jnp.int32)
    meta_ref[2 * TOP_K_FINE:, :] = jnp.zeros((meta_ref.shape[0] - 2 * TOP_K_FINE, tt), jnp.int32)
    carry_ref[...] = carry_ref[...] + jnp.sum(chosen, axis=1, keepdims=True)
    cnt_ref[...] = carry_ref[...].astype(jnp.int32)


def _rank(ri, tt=1024):
    t = ri.shape[0]
    assert TOP_K_FINE == 2
    return pl.pallas_call(
        _rank_kernel,
        grid=(t // tt,),
        in_specs=[pl.BlockSpec((tt, LANES), lambda i: (i, 0))],
        out_specs=[pl.BlockSpec((None, 8, tt), lambda i: (i, 0, 0)),
                   pl.BlockSpec((N_EXPERTS, LANES), lambda i: (0, 0))],
        out_shape=[jax.ShapeDtypeStruct((t // tt, 8, tt), jnp.int32),
                   jax.ShapeDtypeStruct((N_EXPERTS, LANES), jnp.int32)],
        scratch_shapes=[pltpu.VMEM((N_EXPERTS, LANES), F32), pltpu.VMEM((tt, tt), BF16)],
        compiler_params=pltpu.CompilerParams(
            dimension_semantics=("arbitrary",), vmem_limit_bytes=VMEM_LIMIT),
        name="route_rank",
    )(ri)


def _scatter_rows_kernel(pos_ref, src_hbm, zeros_hbm, dst_hbm, st0, st1, st2, sem, lsem, *, tm):
    del zeros_hbm
    i = pl.program_id(0)
    stage = (st0, st1, st2)

    last = pl.num_programs(0) - 1
    nbuf = len(stage)

    def load(tile, b):
        row0 = pl.multiple_of(tile * tm, tm)
        return pltpu.make_async_copy(src_hbm.at[pl.ds(row0, tm), :], stage[b], lsem.at[b])

    def wait_rows(b):
        for _ in range(TOP_K_FINE):
            pltpu.make_async_copy(stage[b], dst_hbm.at[pl.ds(0, tm), :], sem.at[b]).wait()

    @pl.when(i == 0)
    def _():
        load(0, 0).start()

    def step(b):
        nb = (b + 1) % nbuf

        @pl.when(i >= nbuf - 1)
        def _():
            wait_rows(nb)

        @pl.when(i < last)
        def _():
            load(i + 1, nb).start()

        load(i, b).wait()
        for r in range(tm):
            for k in range(TOP_K_FINE):
                p = pos_ref[(i * tm + r) * TOP_K_FINE + k]
                pltpu.make_async_copy(stage[b].at[pl.ds(r, 1), :], dst_hbm.at[pl.ds(p, 1), :],
                                      sem.at[b]).start(priority=k % 2)

        @pl.when(i == last)
        def _():
            for d in range(nbuf - 1):
                @pl.when(i >= d)
                def _(d=d):
                    wait_rows((b - d) % nbuf)

    for b in range(nbuf):
        pl.when(i % nbuf == b)(functools.partial(step, b))


def _scatter_rows(pos, src, zeros, tm=256):
    t, w = src.shape
    any_spec = pl.BlockSpec(memory_space=pl.ANY)
    grid_spec = pltpu.PrefetchScalarGridSpec(
        num_scalar_prefetch=1,
        grid=(t // tm,),
        in_specs=[any_spec, any_spec],
        out_specs=any_spec,
        scratch_shapes=[pltpu.VMEM((tm, w), src.dtype), pltpu.VMEM((tm, w), src.dtype),
                        pltpu.VMEM((tm, w), src.dtype),
                        pltpu.SemaphoreType.DMA((3,)), pltpu.SemaphoreType.DMA((3,))],
    )
    return pl.pallas_call(
        functools.partial(_scatter_rows_kernel, tm=tm),
        grid_spec=grid_spec,
        out_shape=jax.ShapeDtypeStruct(zeros.shape, zeros.dtype),
        input_output_aliases={2: 0},
        compiler_params=pltpu.CompilerParams(dimension_semantics=("arbitrary",)),
        name="scatter_rows",
    )(pos, src, zeros)


def _max_tiles(t, tm=ROW_TILE):
    return t * TOP_K_FINE // tm + N_EXPERTS


def _sort_plan(ri, tm=ROW_TILE):
    t = ri.shape[0]
    k = TOP_K_FINE
    a = t * k
    nt = _max_tiles(t, tm)
    meta, cnt = _rank(ri)
    counts = cnt[:, 0]
    padded = ((counts + tm - 1) // tm) * tm
    off_end = jnp.cumsum(padded)
    off = off_end - padded
    ef = meta[:, 0:k, :]
    eid = jnp.arange(N_EXPERTS, dtype=jnp.int32)
    pos = jnp.sum(jnp.where(ef[..., None] == eid, off, 0), axis=-1) + meta[:, k:2 * k, :]
    pos = jnp.transpose(pos, (0, 2, 1)).reshape(a)
    n_tiles = off_end[-1] // tm
    tile_idx = jnp.arange(nt, dtype=jnp.int32)
    te = jnp.sum((off_end[None, :] <= (tile_idx * tm)[:, None]).astype(jnp.int32), axis=1)
    te = jnp.minimum(te, N_EXPERTS - 1)
    te = jnp.where(tile_idx < n_tiles, te, te[n_tiles - 1]).astype(jnp.int32)
    first = ((tile_idx == 0) | (te != jnp.roll(te, 1))) & (tile_idx < n_tiles)
    slot = (jnp.cumsum(first.astype(jnp.int32)) - 1) % WEIGHT_SLOTS
    eid = jnp.arange(N_EXPERTS, dtype=jnp.int32)
    later = (eid[None, :] > eid[:, None]) & (counts[None, :] > 0)
    next_e = jnp.min(jnp.where(later, eid[None, :], N_EXPERTS), axis=1)
    next_e = jnp.where(next_e < N_EXPERTS, next_e, -1)
    next2_e = jnp.where(next_e >= 0, next_e[jnp.maximum(next_e, 0)], -1)
    return dict(
        next2_expert=next2_e[te].astype(jnp.int32),
        pos=pos.astype(jnp.int32), tile_expert=te,
        n_tiles=n_tiles.reshape(1).astype(jnp.int32), first=first.astype(jnp.int32),
        slot=slot.astype(jnp.int32), next_expert=next_e[te].astype(jnp.int32))


def kernel(x, norm1_g, w_in, lambda_q1, lambda_k1, lambda_q2, lambda_k2, subln_g, pool_w, pool_scale, w_out, norm2_g, w_coarse, b_coarse, w_fine, b_fine, w_gate, w_up, w_down, final_norm_g):
    b, s, d = x.shape
    t = b * s
    assert norm1_g.shape[0] == 1
    x2 = x.reshape(t, d)
    proj, pool, v_t = _inproj(x2, norm1_g[0], w_in[0], pool_w[0], pool_scale[0], s)
    proj3 = proj.reshape(b, s, proj.shape[1])
    attn = _attention(proj3, v_t, lambda_q1, lambda_k1, lambda_q2, lambda_k2, subln_g[0])

    ng, epg = N_EXPERT_GROUPS, EXPERTS_PER_GROUP
    wr = jnp.concatenate(
        [w_coarse[0], jnp.transpose(w_fine[0], (1, 0, 2)).reshape(d, ng * epg)], axis=1)
    wr = jnp.pad(wr, ((0, 0), (0, LANES - wr.shape[1])))
    br = jnp.concatenate([b_coarse[0], b_fine[0].reshape(ng * epg)])
    br = jnp.pad(br, (0, LANES - br.shape[0])).reshape(1, LANES)
    wr_hi = wr.astype(BF16)
    wr_lo = (wr - wr_hi.astype(F32)).astype(BF16)
    sorted_rows = _max_tiles(t) * ROW_TILE
    h, hn, ri, rw, xz = _outproj(attn.reshape(t, ATTN_WIDTH), pool, x2, w_out[0], norm2_g[0],
                                 jnp.concatenate([wr_hi, wr_lo], axis=1), br, sorted_rows)

    plan = _sort_plan(ri)
    xs = _scatter_rows(plan["pos"], hn, xz)
    ys = _experts(plan, xs, w_gate[0], w_up[0], w_down[0])
    out = _combine(plan["pos"], h, rw, ys, final_norm_g)
    return out.reshape(b, s, d)
```

```python
import functools
import math

import jax
import jax.numpy as jnp
from jax import lax
from jax.experimental import pallas as pl
from jax.experimental.pallas import tpu as pltpu

N_DIFF_HEADS = 8
DIFF_HEAD_DIM = 64
DIFF_V_DIM = 2 * DIFF_HEAD_DIM
ATTN_WIDTH = N_DIFF_HEADS * DIFF_V_DIM
POOL_WINDOWS = (2, 4, 8, 16)
POOL_GROUP_DIM = 256
POOL_WIDTH = len(POOL_WINDOWS) * POOL_GROUP_DIM
N_EXPERT_GROUPS = 4
EXPERTS_PER_GROUP = 8
N_EXPERTS = N_EXPERT_GROUPS * EXPERTS_PER_GROUP
TOP_K_FINE = 2
RMS_EPS = 1e-6
NEG_INF = -1e30
LAM_INIT = 0.8 - 0.6 * math.exp(-0.3 * 0)
LOG2E = math.log2(math.e)
Q_SCALE = DIFF_HEAD_DIM ** -0.5 * LOG2E

LANES = 128
SUBLANES = 8
VMEM_LIMIT = 56 * 1024 * 1024
ROW_TILE = 256
WEIGHT_SLOTS = 3

F32 = jnp.float32
BF16 = jnp.bfloat16


def _rms(x, g):
    return x * lax.rsqrt(jnp.mean(x * x, axis=-1, keepdims=True) + RMS_EPS) * g


def _load_weight_bf16(w_hbm, wb_ref, stage, sem):
    ck = stage.shape[1]
    nchunk = w_hbm.shape[0] // ck

    def copy(c):
        return pltpu.make_async_copy(w_hbm.at[c * ck:(c + 1) * ck, :], stage.at[c % 2], sem.at[c % 2])

    copy(0).start()
    for c in range(nchunk):
        if c + 1 < nchunk:
            copy(c + 1).start()
        copy(c).wait()
        wb_ref[c * ck:(c + 1) * ck, :] = stage[c % 2].astype(BF16)


def _pool_mix(u, prev, seq_pos, pw_ref, ps_ref):
    halo = prev.shape[0]
    c = POOL_GROUP_DIM
    ext = jnp.concatenate([prev, u], axis=0)
    outs = []
    for gi, win in enumerate(POOL_WINDOWS):
        assert win & (win - 1) == 0 and win <= halo
        run = ext[:, gi * c:(gi + 1) * c]
        span = 1
        while span < win:
            run = run + pltpu.roll(run, span, axis=0)
            span *= 2
        count = jnp.minimum(seq_pos + 1, win).astype(F32)
        pooled = run[halo:, :] / count - u[:, gi * c:(gi + 1) * c]
        mixed = jnp.dot(pooled.astype(BF16), pw_ref[gi].astype(BF16), preferred_element_type=F32)
        outs.append(mixed * ps_ref[:, gi * c:(gi + 1) * c])
    return outs


def _inproj_kernel(x_ref, g_ref, w_hbm, pw_ref, ps_ref, o_ref, pool_ref, vt_ref,
                   wb_ref, stage, sem, halo_ref, *, tn, seq_tiles):
    i = pl.program_id(0)
    tm = x_ref.shape[0]

    @pl.when(i == 0)
    def _():
        halo_ref[...] = jnp.zeros_like(halo_ref)
        _load_weight_bf16(w_hbm, wb_ref, stage, sem)

    xn = _rms(x_ref[...], g_ref[...]).astype(BF16)
    u = jnp.dot(xn, wb_ref[:, 3 * tn:4 * tn], preferred_element_type=F32)
    v = jnp.dot(xn, wb_ref[:, 2 * tn:3 * tn], preferred_element_type=F32)
    vt_ref[...] = v.T.astype(vt_ref.dtype)
    q = jnp.dot(xn, wb_ref[:, 0 * tn:1 * tn], preferred_element_type=F32) * Q_SCALE
    o_ref[:, 0 * tn:1 * tn] = q.astype(o_ref.dtype)

    tile_in_seq = i % seq_tiles
    prev = jnp.where(tile_in_seq == 0, 0.0, halo_ref[...])
    halo_ref[...] = u[tm - halo_ref.shape[0]:, :]
    seq_pos = tile_in_seq * tm + lax.broadcasted_iota(jnp.int32, (tm, 1), 0)
    c = POOL_GROUP_DIM
    for gi, mixed in enumerate(_pool_mix(u, prev, seq_pos, pw_ref, ps_ref)):
        pool_ref[:, gi * c:(gi + 1) * c] = mixed.astype(pool_ref.dtype)

    k = jnp.dot(xn, wb_ref[:, 1 * tn:2 * tn], preferred_element_type=F32)
    o_ref[:, 1 * tn:2 * tn] = k.astype(o_ref.dtype)


def _inproj(x2, g, w, pool_w, pool_scale, seq_len, tm=512, tn=ATTN_WIDTH, chunk_rows=256):
    t, d = x2.shape
    n = w.shape[1]
    assert n == 4 * tn and tn == POOL_WIDTH and seq_len % tm == 0
    halo = 2 * SUBLANES
    assert max(POOL_WINDOWS) <= halo
    return pl.pallas_call(
        functools.partial(_inproj_kernel, tn=tn, seq_tiles=seq_len // tm),
        grid=(t // tm,),
        in_specs=[
            pl.BlockSpec((tm, d), lambda i: (i, 0)),
            pl.BlockSpec((1, d), lambda i: (0, 0)),
            pl.BlockSpec(memory_space=pl.ANY),
            pl.BlockSpec(pool_w.shape, lambda i: (0, 0, 0)),
            pl.BlockSpec((1, tn), lambda i: (0, 0)),
        ],
        out_specs=[pl.BlockSpec((tm, 2 * tn), lambda i: (i, 0)),
                   pl.BlockSpec((tm, tn), lambda i: (i, 0)),
                   pl.BlockSpec((tn, tm), lambda i: (0, i))],
        out_shape=[jax.ShapeDtypeStruct((t, 2 * tn), BF16), jax.ShapeDtypeStruct((t, tn), BF16),
                   jax.ShapeDtypeStruct((tn, t), BF16)],
        scratch_shapes=[pltpu.VMEM((d, n), BF16), pltpu.VMEM((2, chunk_rows, n), F32),
                        pltpu.SemaphoreType.DMA((2,)), pltpu.VMEM((halo, tn), F32)],
        compiler_params=pltpu.CompilerParams(
            dimension_semantics=("arbitrary",), vmem_limit_bytes=VMEM_LIMIT),
        name="inproj",
    )(x2, g.reshape(1, d), w, pool_w, pool_scale.reshape(1, tn))


def _attn_kernel(lq1_ref, lk1_ref, lq2_ref, lk2_ref, q_ref, k_ref, v_ref, g_ref, o_ref, xz_hbm,
                 tab_ref, vt_ref, zbuf, zsem, *, tq, z_per_step):
    h = pl.program_id(0)
    s_len = q_ref.shape[0]
    nq = s_len // tq
    d, dv = DIFF_HEAD_DIM, DIFF_V_DIM

    @pl.when(pl.program_id(1) == 0)
    def _():
        slope = jnp.exp2(jnp.full((1, 1), -8.0 / N_DIFF_HEADS, F32) * (h + 1).astype(F32))
        rel = ((nq - 1) * tq + lax.broadcasted_iota(jnp.int32, (s_len, tq), 1)
               - lax.broadcasted_iota(jnp.int32, (s_len, tq), 0))
        tab_ref[...] = jnp.where(rel >= 0, (-LOG2E * slope) * rel.astype(F32), NEG_INF)

    step = h * pl.num_programs(1) + pl.program_id(1)

    @pl.when(step == 0)
    def _():
        zbuf[...] = jnp.zeros_like(zbuf)

    def zero_copy(c):
        zr = zbuf.shape[0]
        row0 = pl.multiple_of((step * z_per_step + c) * zr, zr)
        return pltpu.make_async_copy(zbuf, xz_hbm.at[pl.ds(row0, zr), :], zsem)

    for c in range(z_per_step):
        zero_copy(c).start()

    vt_ref[:dv, :] = v_ref[...]
    vt_ref[dv:, :] = jnp.ones((vt_ref.shape[0] - dv, s_len), BF16)

    lam = (jnp.exp(jnp.sum(lq1_ref[...] * lk1_ref[...], axis=1, keepdims=True))
           - jnp.exp(jnp.sum(lq2_ref[...] * lk2_ref[...], axis=1, keepdims=True)) + LAM_INIT)
    dn = (((1,), (1,)), ((), ()))
    lane = lax.broadcasted_iota(jnp.int32, (tq, 2 * d), 1)

    ck = 4 * tq

    def key_chunks(qi):
        n = (qi + 1) * tq
        return [(lo, min(lo + ck, n)) for lo in range(0, n, ck)]

    def query_halves(qi):
        q = q_ref[qi * tq:(qi + 1) * tq, :]
        zero = jnp.zeros_like(q)
        return (jnp.where(lane < d, q, zero), jnp.where(lane >= d, q, zero))

    def score_chunk(qi, qh, lo, hi):
        base = (nq - 1 - qi) * tq
        bias = tab_ref[base + lo:base + hi, :]
        return [lax.dot_general(k_ref[lo:hi, :], qh[a], dn, preferred_element_type=F32) + bias
                for a in range(2)]

    s_next = [score_chunk(0, query_halves(0), lo, hi) for lo, hi in key_chunks(0)]
    for qi in range(nq):
        s_cur = s_next
        cur_chunks = key_chunks(qi)
        m = [functools.reduce(jnp.maximum, [jnp.max(c[a], axis=0, keepdims=True) for c in s_cur])
             for a in range(2)]
        nxt_chunks = key_chunks(qi + 1) if qi + 1 < nq else []
        qh_next = query_halves(qi + 1) if nxt_chunks else None
        s_next = []
        acc = [None, None]
        for c in range(max(len(cur_chunks), len(nxt_chunks))):
            if c < len(nxt_chunks):
                s_next.append(score_chunk(qi + 1, qh_next, *nxt_chunks[c]))
            if c < len(cur_chunks):
                lo, hi = cur_chunks[c]
                for a in range(2):
                    p = jnp.exp2(s_cur[c][a] - m[a]).astype(BF16)
                    part = jnp.dot(vt_ref[:, lo:hi], p, preferred_element_type=F32)
                    acc[a] = part if acc[a] is None else acc[a] + part
        outs = [acc[a][:dv] / acc[a][dv:dv + 1] for a in range(2)]
        ot = outs[0] - lam * outs[1]
        yt = ot * lax.rsqrt(jnp.mean(ot * ot, axis=0, keepdims=True) + RMS_EPS) * g_ref[...]
        o_ref[qi * tq:(qi + 1) * tq, :] = (yt * (1.0 - LAM_INIT)).T.astype(o_ref.dtype)

    for c in range(z_per_step):
        zero_copy(c).wait()


def _attention(proj3, v_t, lq1, lk1, lq2, lk2, subln_g, sorted_rows, sorted_width, tq=256,
               zero_rows=ROW_TILE):
    b, s, _ = proj3.shape
    dv = DIFF_V_DIM
    nh = N_DIFF_HEADS
    ones_rows = 16
    assert sorted_rows % (nh * b * zero_rows) == 0
    lam_spec = pl.BlockSpec((1, DIFF_HEAD_DIM), lambda hi, bi: (0, 0))
    return pl.pallas_call(
        functools.partial(_attn_kernel, tq=tq, z_per_step=sorted_rows // (nh * b * zero_rows)),
        grid=(nh, b),
        in_specs=[
            lam_spec, lam_spec, lam_spec, lam_spec,
            pl.BlockSpec((None, s, dv), lambda hi, bi: (bi, 0, hi)),
            pl.BlockSpec((None, s, dv), lambda hi, bi: (bi, 0, nh + hi)),
            pl.BlockSpec((dv, s), lambda hi, bi: (hi, bi)),
            pl.BlockSpec((dv, 1), lambda hi, bi: (0, 0)),
        ],
        out_specs=[pl.BlockSpec((None, s, dv), lambda hi, bi: (bi, 0, hi)),
                   pl.BlockSpec(memory_space=pl.ANY)],
        out_shape=[jax.ShapeDtypeStruct((b, s, ATTN_WIDTH), BF16),
                   jax.ShapeDtypeStruct((sorted_rows, sorted_width), jnp.uint32)],
        scratch_shapes=[
            pltpu.VMEM((s, tq), F32),
            pltpu.VMEM((dv + ones_rows, s), BF16),
            pltpu.VMEM((zero_rows, sorted_width), jnp.uint32),
            pltpu.SemaphoreType.DMA(()),
        ],
        compiler_params=pltpu.CompilerParams(
            dimension_semantics=("arbitrary", "arbitrary"),
            vmem_limit_bytes=VMEM_LIMIT),
        name="diff_attn",
    )(lq1, lk1, lq2, lk2, proj3, proj3, v_t, subln_g.reshape(dv, 1))


def _pack_bf16_pair(x):
    bits = lax.bitcast_convert_type(x.astype(BF16).astype(F32), jnp.uint32)
    half = bits.shape[1] // 2
    return bits[:, :half] | (bits[:, half:] >> 16)


def _outproj_kernel(a_ref, p_ref, x_ref, w_hbm, g_ref, wr_ref, br_ref,
                    h_ref, hn_ref, ri_ref, rw_ref, w_ref, stage, sem, *, n_sub):
    @pl.when(pl.program_id(0) == 0)
    def _():
        _load_weight_bf16(w_hbm, w_ref, stage, sem)

    ka = a_ref.shape[1]
    hs = x_ref.shape[0] // n_sub
    normed = []
    for r in range(n_sub):
        rows = pl.ds(r * hs, hs)
        mixed = (jnp.dot(a_ref[rows, :], w_ref[:ka, :], preferred_element_type=F32)
                 + jnp.dot(p_ref[rows, :], w_ref[ka:, :], preferred_element_type=F32))
        h = x_ref[rows, :] + mixed
        h_ref[rows, :] = h
        hn = _rms(h, g_ref[...])
        hn_ref[rows, :] = _pack_bf16_pair(hn)
        hn_hi = hn.astype(BF16)
        normed.append((hn_hi, (hn - hn_hi.astype(F32)).astype(BF16)))
    for r in range(n_sub):
        rows = pl.ds(r * hs, hs)
        ri, rw = _route(normed[r][0], normed[r][1], wr_ref, br_ref)
        ri_ref[rows, :] = ri
        rw_ref[rows, :] = rw


def _route(hn_hi, hn_lo, wr_ref, br_ref):
    both = jnp.dot(hn_hi, wr_ref[...], preferred_element_type=F32)
    logits = (both[:, :LANES] + both[:, LANES:]
              + jnp.dot(hn_lo, wr_ref[:, :LANES], preferred_element_type=F32)) + br_ref[...]
    ng, epg = N_EXPERT_GROUPS, EXPERTS_PER_GROUP
    lane = lax.broadcasted_iota(jnp.int32, logits.shape, 1)
    big = jnp.int32(LANES)
    low = jnp.float32(-3.0e38)
    cm = lane < ng
    c = jnp.where(cm, logits, low)
    cmax = jnp.max(c, axis=1, keepdims=True)
    gsel = jnp.min(jnp.where(c == cmax, lane, big), axis=1, keepdims=True)
    p_group = 1.0 / jnp.sum(jnp.where(cm, jnp.exp(c - cmax), 0.0), axis=1, keepdims=True)
    f_lo = ng + epg * gsel
    fm = (lane >= f_lo) & (lane < f_lo + epg)
    f = jnp.where(fm, logits, low)
    v1 = jnp.max(f, axis=1, keepdims=True)
    i1 = jnp.min(jnp.where(fm & (f == v1), lane, big), axis=1, keepdims=True)
    fm2 = fm & (lane != i1)
    f2 = jnp.where(fm2, logits, low)
    v2 = jnp.max(f2, axis=1, keepdims=True)
    i2 = jnp.min(jnp.where(fm2 & (f2 == v2), lane, big), axis=1, keepdims=True)
    e21 = jnp.exp(v2 - v1)
    w1 = p_group / (1.0 + e21)
    w2 = p_group * e21 / (1.0 + e21)
    return (jnp.where(lane == 0, i1 - ng, jnp.where(lane == 1, i2 - ng, 0)),
            jnp.where(lane == 0, w1, jnp.where(lane == 1, w2, 0.0)))


def _outproj(attn2, pool2, x2, w_out, g2, wr, br, tm=512, n_sub=2, chunk_rows=512):
    t, d = x2.shape
    ka, kp = attn2.shape[1], pool2.shape[1]
    row = lambda i: (i, 0)
    const = lambda i: (0, 0)
    return pl.pallas_call(
        functools.partial(_outproj_kernel, n_sub=n_sub),
        grid=(t // tm,),
        in_specs=[
            pl.BlockSpec((tm, ka), row),
            pl.BlockSpec((tm, kp), row),
            pl.BlockSpec((tm, d), row),
            pl.BlockSpec(memory_space=pl.ANY),
            pl.BlockSpec((1, d), const),
            pl.BlockSpec((d, 2 * LANES), const),
            pl.BlockSpec((1, LANES), const),
        ],
        out_specs=[
            pl.BlockSpec((tm, d), row),
            pl.BlockSpec((tm, d // 2), row),
            pl.BlockSpec((tm, LANES), row),
            pl.BlockSpec((tm, LANES), row),
        ],
        out_shape=[
            jax.ShapeDtypeStruct((t, d), F32),
            jax.ShapeDtypeStruct((t, d // 2), jnp.uint32),
            jax.ShapeDtypeStruct((t, LANES), jnp.int32),
            jax.ShapeDtypeStruct((t, LANES), F32),
        ],
        scratch_shapes=[pltpu.VMEM((ka + kp, d), BF16), pltpu.VMEM((2, chunk_rows, d), F32),
                        pltpu.SemaphoreType.DMA((2,))],
        compiler_params=pltpu.CompilerParams(
            dimension_semantics=("arbitrary",), vmem_limit_bytes=VMEM_LIMIT),
        name="outproj_router",
    )(attn2, pool2, x2, w_out, g2.reshape(1, d), wr, br)


def _unpack_bf16_pair(words):
    hi = lax.bitcast_convert_type(words & jnp.uint32(0xFFFF0000), F32).astype(BF16)
    lo = lax.bitcast_convert_type(words << 16, F32).astype(BF16)
    return hi, lo


def _expert_kernel(te_ref, nt_ref, first_ref, slot_ref, nxt_ref, nxt2_ref,
                   x_ref, wg_hbm, wu_hbm, wd_hbm, y_ref,
                   wgb, wub, wdb, wsem, *, tm, tiles_per_step):
    nt = nt_ref[0]
    n_slots = wgb.shape[0]
    kh = x_ref.shape[1]
    fh = wdb.shape[1] // 2

    def weight_copies(e, s):
        return ((pltpu.make_async_copy(wg_hbm.at[e], wgb.at[s], wsem.at[s]), 0),
                (pltpu.make_async_copy(wu_hbm.at[e], wub.at[s], wsem.at[s]), 1),
                (pltpu.make_async_copy(wd_hbm.at[e, :fh, :], wdb.at[s, :fh, :], wsem.at[s]), 0),
                (pltpu.make_async_copy(wd_hbm.at[e, fh:, :], wdb.at[s, fh:, :], wsem.at[s]), 1))

    @pl.when(pl.program_id(0) == 0)
    def _():
        for c, prio in weight_copies(te_ref[0], 0):
            c.start(priority=prio)

        @pl.when(nxt_ref[0] >= 0)
        def _():
            for c, prio in weight_copies(nxt_ref[0], 1):
                c.start(priority=prio)

    def tile(j, rows):
        @pl.when(j < nt)
        def _():
            ws = slot_ref[j]

            @pl.when(first_ref[j] == 1)
            def _():
                for c, _ in weight_copies(te_ref[j], ws):
                    c.wait()

                @pl.when(nxt2_ref[j] >= 0)
                def _():
                    for c, prio in weight_copies(nxt2_ref[j], (ws + 2) % n_slots):
                        c.start(priority=prio)

            xa, xb = _unpack_bf16_pair(x_ref[rows, :])
            hg = (jnp.dot(xa, wgb[ws, :kh, :].astype(BF16), preferred_element_type=F32)
                  + jnp.dot(xb, wgb[ws, kh:, :].astype(BF16), preferred_element_type=F32))
            hu = (jnp.dot(xa, wub[ws, :kh, :].astype(BF16), preferred_element_type=F32)
                  + jnp.dot(xb, wub[ws, kh:, :].astype(BF16), preferred_element_type=F32))
            act = (hg / (1.0 + jnp.exp(-hg))) * hu
            y_ref[rows, :] = _pack_bf16_pair(
                jnp.dot(act.astype(BF16), wdb[ws].astype(BF16), preferred_element_type=F32))

        @pl.when(j >= nt)
        def _():
            y_ref[rows, :] = jnp.zeros((tm, kh), y_ref.dtype)

    for sub in range(tiles_per_step):
        tile(pl.program_id(0) * tiles_per_step + sub, pl.ds(sub * tm, tm))


def _experts(plan, xs, w_gate, w_up, w_down, tm=ROW_TILE, tiles_per_step=2):
    d, f = w_gate.shape[1], w_gate.shape[2]
    assert xs.shape[1] * 2 == d
    nt = plan["tile_expert"].shape[0]
    assert nt % tiles_per_step == 0
    rows = tm * tiles_per_step
    any_spec = pl.BlockSpec(memory_space=pl.ANY)
    grid_spec = pltpu.PrefetchScalarGridSpec(
        num_scalar_prefetch=6,
        grid=(nt // tiles_per_step,),
        in_specs=[
            pl.BlockSpec((rows, d // 2),
                         lambda j, te, n, *_: (jnp.minimum(j, (n[0] - 1) // tiles_per_step), 0)),
            any_spec, any_spec, any_spec],
        out_specs=pl.BlockSpec((rows, d // 2), lambda j, *_: (j, 0)),
        scratch_shapes=[
            pltpu.VMEM((WEIGHT_SLOTS, d, f), F32),
            pltpu.VMEM((WEIGHT_SLOTS, d, f), F32),
            pltpu.VMEM((WEIGHT_SLOTS, f, d), F32),
            pltpu.SemaphoreType.DMA((WEIGHT_SLOTS,)),
        ],
    )
    return pl.pallas_call(
        functools.partial(_expert_kernel, tm=tm, tiles_per_step=tiles_per_step),
        grid_spec=grid_spec,
        out_shape=jax.ShapeDtypeStruct((nt * tm, d // 2), jnp.uint32),
        compiler_params=pltpu.CompilerParams(
            dimension_semantics=("arbitrary",), vmem_limit_bytes=VMEM_LIMIT),
        name="experts",
    )(plan["tile_expert"], plan["n_tiles"], plan["first"], plan["slot"], plan["next_expert"],
      plan["next2_expert"], xs, w_gate, w_up, w_down)


def _combine_kernel(pos_ref, h_ref, rw_ref, ys_hbm, g_ref, o_ref, ybuf0, ybuf1, ybuf2, sem):
    i = pl.program_id(0)
    last = pl.num_programs(0) - 1
    tm = h_ref.shape[0]
    kh = ybuf0.shape[2]
    bufs = (ybuf0, ybuf1, ybuf2)
    nbuf = len(bufs)

    def start_gather(tile, buf, sm):
        for r in range(tm):
            for k in range(TOP_K_FINE):
                p = pos_ref[(tile * tm + r) * TOP_K_FINE + k]
                pltpu.make_async_copy(ys_hbm.at[pl.ds(p, 1), :], buf.at[k, pl.ds(r, 1), :],
                                      sm).start(priority=k % 2)

    def wait_gather(buf, sm):
        for k in range(TOP_K_FINE):
            pltpu.make_async_copy(ys_hbm.at[pl.ds(0, tm), :], buf.at[k], sm).wait()

    @pl.when(i == 0)
    def _():
        start_gather(0, ybuf0, sem.at[0])
        start_gather(1, ybuf1, sem.at[1])

    def step(par):
        cur = bufs[par]
        ahead = (par + nbuf - 1) % nbuf
        wait_gather(cur, sem.at[par])
        start_gather(jnp.minimum(i + nbuf - 1, last), bufs[ahead], sem.at[ahead])
        w = rw_ref[...]
        halves = []
        for part in range(2):
            y = None
            for k in range(TOP_K_FINE):
                words = cur[k]
                bits = (words & jnp.uint32(0xFFFF0000)) if part == 0 else (words << 16)
                term = w[:, k:k + 1] * lax.bitcast_convert_type(bits, F32)
                y = term if y is None else y + term
            halves.append(h_ref[:, part * kh:(part + 1) * kh] + y)
        ms = sum(jnp.sum(v * v, axis=-1, keepdims=True) for v in halves) / (2 * kh)
        inv = lax.rsqrt(ms + RMS_EPS)
        for part in range(2):
            o_ref[:, part * kh:(part + 1) * kh] = (
                halves[part] * inv * g_ref[:, part * kh:(part + 1) * kh])

        @pl.when(i == last)
        def _():
            for other in range(nbuf):
                if other != par:
                    wait_gather(bufs[other], sem.at[other])

    for par in range(nbuf):
        pl.when(i % nbuf == par)(functools.partial(step, par))


def _combine(pos, h, rw, ys, g, tm=256):
    t, d = h.shape
    grid_spec = pltpu.PrefetchScalarGridSpec(
        num_scalar_prefetch=1,
        grid=(t // tm,),
        in_specs=[
            pl.BlockSpec((tm, d), lambda i, p: (i, 0)),
            pl.BlockSpec((tm, LANES), lambda i, p: (i, 0)),
            pl.BlockSpec(memory_space=pl.ANY),
            pl.BlockSpec((1, d), lambda i, p: (0, 0)),
        ],
        out_specs=pl.BlockSpec((tm, d), lambda i, p: (i, 0)),
        scratch_shapes=[pltpu.VMEM((TOP_K_FINE, tm, d // 2), jnp.uint32),
                        pltpu.VMEM((TOP_K_FINE, tm, d // 2), jnp.uint32),
                        pltpu.VMEM((TOP_K_FINE, tm, d // 2), jnp.uint32),
                        pltpu.SemaphoreType.DMA((3,))],
    )
    return pl.pallas_call(
        _combine_kernel,
        grid_spec=grid_spec,
        out_shape=jax.ShapeDtypeStruct((t, d), F32),
        compiler_params=pltpu.CompilerParams(
            dimension_semantics=("arbitrary",), vmem_limit_bytes=VMEM_LIMIT),
        name="combine",
    )(pos, h, rw, ys, g.reshape(1, d))


def _rank_kernel(ri_ref, meta_ref, cnt_ref, carry_ref, tri_ref):
    tt = ri_ref.shape[0]
    ne = cnt_ref.shape[0]

    @pl.when(pl.program_id(0) == 0)
    def _():
        carry_ref[...] = jnp.zeros_like(carry_ref)
        earlier = (lax.broadcasted_iota(jnp.int32, (tt, tt), 0)
                   < lax.broadcasted_iota(jnp.int32, (tt, tt), 1))
        tri_ref[...] = jnp.where(earlier, 1.0, 0.0).astype(BF16)

    rit = ri_ref[...].astype(F32).T
    expert = lax.broadcasted_iota(jnp.int32, (ne, tt), 0).astype(F32)
    hits = []
    for k in range(TOP_K_FINE):
        hits.append(expert == rit[k:k + 1, :])
    chosen = jnp.where(hits[0] | hits[1], 1.0, 0.0)
    before = (jnp.dot(chosen.astype(BF16), tri_ref[...], preferred_element_type=F32)
              + carry_ref[:, 0:1])
    for k in range(TOP_K_FINE):
        meta_ref[k:k + 1, :] = rit[k:k + 1, :].astype(jnp.int32)
        rank = jnp.sum(jnp.where(hits[k], before, 0.0), axis=0, keepdims=True)
        meta_ref[TOP_K_FINE + k:TOP_K_FINE + k + 1, :] = rank.astype(jnp.int32)
    meta_ref[2 * TOP_K_FINE:, :] = jnp.zeros((meta_ref.shape[0] - 2 * TOP_K_FINE, tt), jnp.int32)
    carry_ref[...] = carry_ref[...] + jnp.sum(chosen, axis=1, keepdims=True)
    cnt_ref[...] = carry_ref[...].astype(jnp.int32)


def _rank(ri, tt=1024):
    t = ri.shape[0]
    assert TOP_K_FINE == 2
    return pl.pallas_call(
        _rank_kernel,
        grid=(t // tt,),
        in_specs=[pl.BlockSpec((tt, LANES), lambda i: (i, 0))],
        out_specs=[pl.BlockSpec((None, 8, tt), lambda i: (i, 0, 0)),
                   pl.BlockSpec((N_EXPERTS, LANES), lambda i: (0, 0))],
        out_shape=[jax.ShapeDtypeStruct((t // tt, 8, tt), jnp.int32),
                   jax.ShapeDtypeStruct((N_EXPERTS, LANES), jnp.int32)],
        scratch_shapes=[pltpu.VMEM((N_EXPERTS, LANES), F32), pltpu.VMEM((tt, tt), BF16)],
        compiler_params=pltpu.CompilerParams(
            dimension_semantics=("arbitrary",), vmem_limit_bytes=VMEM_LIMIT),
        name="route_rank",
    )(ri)


def _scatter_rows_kernel(pos_ref, src_hbm, zeros_hbm, dst_hbm, st0, st1, st2, sem, lsem, *, tm):
    del zeros_hbm
    i = pl.program_id(0)
    stage = (st0, st1, st2)

    last = pl.num_programs(0) - 1
    nbuf = len(stage)

    def load(tile, b):
        row0 = pl.multiple_of(tile * tm, tm)
        return pltpu.make_async_copy(src_hbm.at[pl.ds(row0, tm), :], stage[b], lsem.at[b])

    def wait_rows(b):
        for _ in range(TOP_K_FINE):
            pltpu.make_async_copy(stage[b], dst_hbm.at[pl.ds(0, tm), :], sem.at[b]).wait()

    @pl.when(i == 0)
    def _():
        load(0, 0).start()

    def step(b):
        nb = (b + 1) % nbuf

        @pl.when(i >= nbuf - 1)
        def _():
            wait_rows(nb)

        @pl.when(i < last)
        def _():
            load(i + 1, nb).start()

        load(i, b).wait()
        for r in range(tm):
            for k in range(TOP_K_FINE):
                p = pos_ref[(i * tm + r) * TOP_K_FINE + k]
                pltpu.make_async_copy(stage[b].at[pl.ds(r, 1), :], dst_hbm.at[pl.ds(p, 1), :],
                                      sem.at[b]).start(priority=k % 2)

        @pl.when(i == last)
        def _():
            for d in range(nbuf - 1):
                @pl.when(i >= d)
                def _(d=d):
                    wait_rows((b - d) % nbuf)

    for b in range(nbuf):
        pl.when(i % nbuf == b)(functools.partial(step, b))


def _scatter_rows(pos, src, zeros, tm=256):
    t, w = src.shape
    any_spec = pl.BlockSpec(memory_space=pl.ANY)
    grid_spec = pltpu.PrefetchScalarGridSpec(
        num_scalar_prefetch=1,
        grid=(t // tm,),
        in_specs=[any_spec, any_spec],
        out_specs=any_spec,
        scratch_shapes=[pltpu.VMEM((tm, w), src.dtype), pltpu.VMEM((tm, w), src.dtype),
                        pltpu.VMEM((tm, w), src.dtype),
                        pltpu.SemaphoreType.DMA((3,)), pltpu.SemaphoreType.DMA((3,))],
    )
    return pl.pallas_call(
        functools.partial(_scatter_rows_kernel, tm=tm),
        grid_spec=grid_spec,
        out_shape=jax.ShapeDtypeStruct(zeros.shape, zeros.dtype),
        input_output_aliases={2: 0},
        compiler_params=pltpu.CompilerParams(dimension_semantics=("arbitrary",)),
        name="scatter_rows",
    )(pos, src, zeros)


def _max_tiles(t, tm=ROW_TILE):
    return t * TOP_K_FINE // tm + N_EXPERTS


def _sort_plan(ri, tm=ROW_TILE):
    t = ri.shape[0]
    k = TOP_K_FINE
    a = t * k
    nt = _max_tiles(t, tm)
    meta, cnt = _rank(ri)
    counts = cnt[:, 0]
    padded = ((counts + tm - 1) // tm) * tm
    off_end = jnp.cumsum(padded)
    off = off_end - padded
    ef = meta[:, 0:k, :]
    eid = jnp.arange(N_EXPERTS, dtype=jnp.int32)
    pos = jnp.sum(jnp.where(ef[..., None] == eid, off, 0), axis=-1) + meta[:, k:2 * k, :]
    pos = jnp.transpose(pos, (0, 2, 1)).reshape(a)
    n_tiles = off_end[-1] // tm
    tile_idx = jnp.arange(nt, dtype=jnp.int32)
    te = jnp.sum((off_end[None, :] <= (tile_idx * tm)[:, None]).astype(jnp.int32), axis=1)
    te = jnp.minimum(te, N_EXPERTS - 1)
    te = jnp.where(tile_idx < n_tiles, te, te[n_tiles - 1]).astype(jnp.int32)
    first = ((tile_idx == 0) | (te != jnp.roll(te, 1))) & (tile_idx < n_tiles)
    slot = (jnp.cumsum(first.astype(jnp.int32)) - 1) % WEIGHT_SLOTS
    eid = jnp.arange(N_EXPERTS, dtype=jnp.int32)
    later = (eid[None, :] > eid[:, None]) & (counts[None, :] > 0)
    next_e = jnp.min(jnp.where(later, eid[None, :], N_EXPERTS), axis=1)
    next_e = jnp.where(next_e < N_EXPERTS, next_e, -1)
    next2_e = jnp.where(next_e >= 0, next_e[jnp.maximum(next_e, 0)], -1)
    return dict(
        next2_expert=next2_e[te].astype(jnp.int32),
        pos=pos.astype(jnp.int32), tile_expert=te,
        n_tiles=n_tiles.reshape(1).astype(jnp.int32), first=first.astype(jnp.int32),
        slot=slot.astype(jnp.int32), next_expert=next_e[te].astype(jnp.int32))


def kernel(x, norm1_g, w_in, lambda_q1, lambda_k1, lambda_q2, lambda_k2, subln_g, pool_w, pool_scale, w_out, norm2_g, w_coarse, b_coarse, w_fine, b_fine, w_gate, w_up, w_down, final_norm_g):
    b, s, d = x.shape
    t = b * s
    assert norm1_g.shape[0] == 1
    x2 = x.reshape(t, d)
    proj, pool, v_t = _inproj(x2, norm1_g[0], w_in[0], pool_w[0], pool_scale[0], s)
    proj3 = proj.reshape(b, s, proj.shape[1])
    attn, xz = _attention(proj3, v_t, lambda_q1, lambda_k1, lambda_q2, lambda_k2, subln_g[0],
                          _max_tiles(t) * ROW_TILE, d // 2)

    ng, epg = N_EXPERT_GROUPS, EXPERTS_PER_GROUP
    wr = jnp.concatenate(
        [w_coarse[0], jnp.transpose(w_fine[0], (1, 0, 2)).reshape(d, ng * epg)], axis=1)
    wr = jnp.pad(wr, ((0, 0), (0, LANES - wr.shape[1])))
    br = jnp.concatenate([b_coarse[0], b_fine[0].reshape(ng * epg)])
    br = jnp.pad(br, (0, LANES - br.shape[0])).reshape(1, LANES)
    wr_hi = wr.astype(BF16)
    wr_lo = (wr - wr_hi.astype(F32)).astype(BF16)
    h, hn, ri, rw = _outproj(attn.reshape(t, ATTN_WIDTH), pool, x2,
                             w_out[0], norm2_g[0], jnp.concatenate([wr_hi, wr_lo], axis=1), br)

    plan = _sort_plan(ri)
    xs = _scatter_rows(plan["pos"], hn, xz)
    ys = _experts(plan, xs, w_gate[0], w_up[0], w_down[0])
    out = _combine(plan["pos"], h, rw, ys, final_norm_g)
    return out.reshape(b, s, d)
```

```python
import functools
import math

import jax
import jax.numpy as jnp
from jax import lax
from jax.experimental import pallas as pl
from jax.experimental.pallas import tpu as pltpu

N_DIFF_HEADS = 8
DIFF_HEAD_DIM = 64
DIFF_V_DIM = 2 * DIFF_HEAD_DIM
ATTN_WIDTH = N_DIFF_HEADS * DIFF_V_DIM
POOL_WINDOWS = (2, 4, 8, 16)
POOL_GROUP_DIM = 256
POOL_WIDTH = len(POOL_WINDOWS) * POOL_GROUP_DIM
N_EXPERT_GROUPS = 4
EXPERTS_PER_GROUP = 8
N_EXPERTS = N_EXPERT_GROUPS * EXPERTS_PER_GROUP
TOP_K_FINE = 2
RMS_EPS = 1e-6
NEG_INF = -1e30
LAM_INIT = 0.8 - 0.6 * math.exp(-0.3 * 0)
LOG2E = math.log2(math.e)
Q_SCALE = DIFF_HEAD_DIM ** -0.5 * LOG2E

LANES = 128
SUBLANES = 8
VMEM_LIMIT = 56 * 1024 * 1024
ROW_TILE = 256
WEIGHT_SLOTS = 3

F32 = jnp.float32
BF16 = jnp.bfloat16


def _rms(x, g):
    return x * lax.rsqrt(jnp.mean(x * x, axis=-1, keepdims=True) + RMS_EPS) * g


def _load_weight_bf16(w_hbm, wb_ref, stage, sem):
    ck = stage.shape[1]
    nchunk = w_hbm.shape[0] // ck

    def copy(c):
        return pltpu.make_async_copy(w_hbm.at[c * ck:(c + 1) * ck, :], stage.at[c % 2], sem.at[c % 2])

    copy(0).start()
    for c in range(nchunk):
        if c + 1 < nchunk:
            copy(c + 1).start()
        copy(c).wait()
        wb_ref[c * ck:(c + 1) * ck, :] = stage[c % 2].astype(BF16)


def _pool_mix(u, prev, seq_pos, pw_ref, ps_ref):
    halo = prev.shape[0]
    c = POOL_GROUP_DIM
    ext = jnp.concatenate([prev, u], axis=0)
    outs = []
    for gi, win in enumerate(POOL_WINDOWS):
        assert win & (win - 1) == 0 and win <= halo
        run = ext[:, gi * c:(gi + 1) * c]
        span = 1
        while span < win:
            run = run + pltpu.roll(run, span, axis=0)
            span *= 2
        count = jnp.minimum(seq_pos + 1, win).astype(F32)
        pooled = run[halo:, :] / count - u[:, gi * c:(gi + 1) * c]
        mixed = jnp.dot(pooled.astype(BF16), pw_ref[gi].astype(BF16), preferred_element_type=F32)
        outs.append(mixed * ps_ref[:, gi * c:(gi + 1) * c])
    return outs


def _inproj_kernel(x_ref, g_ref, w_hbm, pw_ref, ps_ref, o_ref, pool_ref, vt_ref,
                   wb_ref, stage, sem, halo_ref, *, tn, seq_tiles):
    i = pl.program_id(0)
    tm = x_ref.shape[0]

    @pl.when(i == 0)
    def _():
        halo_ref[...] = jnp.zeros_like(halo_ref)
        _load_weight_bf16(w_hbm, wb_ref, stage, sem)

    xn = _rms(x_ref[...], g_ref[...]).astype(BF16)
    u = jnp.dot(xn, wb_ref[:, 3 * tn:4 * tn], preferred_element_type=F32)
    v = jnp.dot(xn, wb_ref[:, 2 * tn:3 * tn], preferred_element_type=F32)
    vt_ref[...] = v.T.astype(vt_ref.dtype)
    q = jnp.dot(xn, wb_ref[:, 0 * tn:1 * tn], preferred_element_type=F32) * Q_SCALE
    o_ref[:, 0 * tn:1 * tn] = q.astype(o_ref.dtype)

    tile_in_seq = i % seq_tiles
    prev = jnp.where(tile_in_seq == 0, 0.0, halo_ref[...])
    halo_ref[...] = u[tm - halo_ref.shape[0]:, :]
    seq_pos = tile_in_seq * tm + lax.broadcasted_iota(jnp.int32, (tm, 1), 0)
    c = POOL_GROUP_DIM
    for gi, mixed in enumerate(_pool_mix(u, prev, seq_pos, pw_ref, ps_ref)):
        pool_ref[:, gi * c:(gi + 1) * c] = mixed.astype(pool_ref.dtype)

    k = jnp.dot(xn, wb_ref[:, 1 * tn:2 * tn], preferred_element_type=F32)
    o_ref[:, 1 * tn:2 * tn] = k.astype(o_ref.dtype)


def _inproj(x2, g, w, pool_w, pool_scale, seq_len, tm=512, tn=ATTN_WIDTH, chunk_rows=256):
    t, d = x2.shape
    n = w.shape[1]
    assert n == 4 * tn and tn == POOL_WIDTH and seq_len % tm == 0
    halo = 2 * SUBLANES
    assert max(POOL_WINDOWS) <= halo
    return pl.pallas_call(
        functools.partial(_inproj_kernel, tn=tn, seq_tiles=seq_len // tm),
        grid=(t // tm,),
        in_specs=[
            pl.BlockSpec((tm, d), lambda i: (i, 0)),
            pl.BlockSpec((1, d), lambda i: (0, 0)),
            pl.BlockSpec(memory_space=pl.ANY),
            pl.BlockSpec(pool_w.shape, lambda i: (0, 0, 0)),
            pl.BlockSpec((1, tn), lambda i: (0, 0)),
        ],
        out_specs=[pl.BlockSpec((tm, 2 * tn), lambda i: (i, 0)),
                   pl.BlockSpec((tm, tn), lambda i: (i, 0)),
                   pl.BlockSpec((tn, tm), lambda i: (0, i))],
        out_shape=[jax.ShapeDtypeStruct((t, 2 * tn), BF16), jax.ShapeDtypeStruct((t, tn), BF16),
                   jax.ShapeDtypeStruct((tn, t), BF16)],
        scratch_shapes=[pltpu.VMEM((d, n), BF16), pltpu.VMEM((2, chunk_rows, n), F32),
                        pltpu.SemaphoreType.DMA((2,)), pltpu.VMEM((halo, tn), F32)],
        compiler_params=pltpu.CompilerParams(
            dimension_semantics=("arbitrary",), vmem_limit_bytes=VMEM_LIMIT),
        name="inproj",
    )(x2, g.reshape(1, d), w, pool_w, pool_scale.reshape(1, tn))


def _attn_kernel(lq1_ref, lk1_ref, lq2_ref, lk2_ref, q_ref, k_ref, v_ref, g_ref, o_ref,
                 tab_ref, vt_ref, *, tq):
    h = pl.program_id(0)
    s_len = q_ref.shape[0]
    nq = s_len // tq
    d, dv = DIFF_HEAD_DIM, DIFF_V_DIM

    @pl.when(pl.program_id(1) == 0)
    def _():
        slope = jnp.exp2(jnp.full((1, 1), -8.0 / N_DIFF_HEADS, F32) * (h + 1).astype(F32))
        rel = ((nq - 1) * tq + lax.broadcasted_iota(jnp.int32, (s_len, tq), 1)
               - lax.broadcasted_iota(jnp.int32, (s_len, tq), 0))
        tab_ref[...] = jnp.where(rel >= 0, (-LOG2E * slope) * rel.astype(F32), NEG_INF)

    vt_ref[:dv, :] = v_ref[...]
    vt_ref[dv:, :] = jnp.ones((vt_ref.shape[0] - dv, s_len), BF16)

    lam = (jnp.exp(jnp.sum(lq1_ref[...] * lk1_ref[...], axis=1, keepdims=True))
           - jnp.exp(jnp.sum(lq2_ref[...] * lk2_ref[...], axis=1, keepdims=True)) + LAM_INIT)
    dn = (((1,), (1,)), ((), ()))
    lane = lax.broadcasted_iota(jnp.int32, (tq, 2 * d), 1)

    ck = 4 * tq

    def key_chunks(qi):
        n = (qi + 1) * tq
        return [(lo, min(lo + ck, n)) for lo in range(0, n, ck)]

    def query_halves(qi):
        q = q_ref[qi * tq:(qi + 1) * tq, :]
        zero = jnp.zeros_like(q)
        return (jnp.where(lane < d, q, zero), jnp.where(lane >= d, q, zero))

    def score_chunk(qi, qh, lo, hi):
        base = (nq - 1 - qi) * tq
        bias = tab_ref[base + lo:base + hi, :]
        return [lax.dot_general(k_ref[lo:hi, :], qh[a], dn, preferred_element_type=F32) + bias
                for a in range(2)]

    s_next = [score_chunk(0, query_halves(0), lo, hi) for lo, hi in key_chunks(0)]
    for qi in range(nq):
        s_cur = s_next
        cur_chunks = key_chunks(qi)
        m = [functools.reduce(jnp.maximum, [jnp.max(c[a], axis=0, keepdims=True) for c in s_cur])
             for a in range(2)]
        nxt_chunks = key_chunks(qi + 1) if qi + 1 < nq else []
        qh_next = query_halves(qi + 1) if nxt_chunks else None
        s_next = []
        acc = [None, None]
        for c in range(max(len(cur_chunks), len(nxt_chunks))):
            if c < len(nxt_chunks):
                s_next.append(score_chunk(qi + 1, qh_next, *nxt_chunks[c]))
            if c < len(cur_chunks):
                lo, hi = cur_chunks[c]
                for a in range(2):
                    p = jnp.exp2(s_cur[c][a] - m[a]).astype(BF16)
                    part = jnp.dot(vt_ref[:, lo:hi], p, preferred_element_type=F32)
                    acc[a] = part if acc[a] is None else acc[a] + part
        outs = [acc[a][:dv] / acc[a][dv:dv + 1] for a in range(2)]
        ot = outs[0] - lam * outs[1]
        yt = ot * lax.rsqrt(jnp.mean(ot * ot, axis=0, keepdims=True) + RMS_EPS) * g_ref[...]
        o_ref[qi * tq:(qi + 1) * tq, :] = (yt * (1.0 - LAM_INIT)).T.astype(o_ref.dtype)


def _attention(proj3, v_t, lq1, lk1, lq2, lk2, subln_g, tq=256):
    b, s, _ = proj3.shape
    dv = DIFF_V_DIM
    nh = N_DIFF_HEADS
    ones_rows = 16
    lam_spec = pl.BlockSpec((1, DIFF_HEAD_DIM), lambda hi, bi: (0, 0))
    return pl.pallas_call(
        functools.partial(_attn_kernel, tq=tq),
        grid=(nh, b),
        in_specs=[
            lam_spec, lam_spec, lam_spec, lam_spec,
            pl.BlockSpec((None, s, dv), lambda hi, bi: (bi, 0, hi)),
            pl.BlockSpec((None, s, dv), lambda hi, bi: (bi, 0, nh + hi)),
            pl.BlockSpec((dv, s), lambda hi, bi: (hi, bi)),
            pl.BlockSpec((dv, 1), lambda hi, bi: (0, 0)),
        ],
        out_specs=pl.BlockSpec((None, s, dv), lambda hi, bi: (bi, 0, hi)),
        out_shape=jax.ShapeDtypeStruct((b, s, ATTN_WIDTH), BF16),
        scratch_shapes=[
            pltpu.VMEM((s, tq), F32),
            pltpu.VMEM((dv + ones_rows, s), BF16),
        ],
        compiler_params=pltpu.CompilerParams(
            dimension_semantics=("arbitrary", "arbitrary"),
            vmem_limit_bytes=VMEM_LIMIT),
        name="diff_attn",
    )(lq1, lk1, lq2, lk2, proj3, proj3, v_t, subln_g.reshape(dv, 1))


def _pack_bf16_pair(x):
    bits = lax.bitcast_convert_type(x.astype(BF16).astype(F32), jnp.uint32)
    half = bits.shape[1] // 2
    return bits[:, :half] | (bits[:, half:] >> 16)


def _outproj_kernel(a_ref, p_ref, x_ref, w_hbm, g_ref, wr_ref, br_ref, zbuf,
                    h_ref, hn_ref, ri_ref, rw_ref, xz_hbm, w_ref, stage, sem, zsem,
                    *, n_sub, z_per_step):
    i = pl.program_id(0)

    @pl.when(i == 0)
    def _():
        _load_weight_bf16(w_hbm, w_ref, stage, sem)

    zr = zbuf.shape[0]

    def zero_copy(c):
        row0 = pl.multiple_of((i * z_per_step + c) * zr, zr)
        return pltpu.make_async_copy(zbuf, xz_hbm.at[pl.ds(row0, zr), :], zsem)

    for c in range(z_per_step):
        zero_copy(c).start()

    ka = a_ref.shape[1]
    hs = x_ref.shape[0] // n_sub
    normed = []
    for r in range(n_sub):
        rows = pl.ds(r * hs, hs)
        mixed = (jnp.dot(a_ref[rows, :], w_ref[:ka, :], preferred_element_type=F32)
                 + jnp.dot(p_ref[rows, :], w_ref[ka:, :], preferred_element_type=F32))
        h = x_ref[rows, :] + mixed
        h_ref[rows, :] = h
        hn = _rms(h, g_ref[...])
        hn_ref[rows, :] = _pack_bf16_pair(hn)
        hn_hi = hn.astype(BF16)
        normed.append((hn_hi, (hn - hn_hi.astype(F32)).astype(BF16)))
    for r in range(n_sub):
        rows = pl.ds(r * hs, hs)
        ri, rw = _route(normed[r][0], normed[r][1], wr_ref, br_ref)
        ri_ref[rows, :] = ri
        rw_ref[rows, :] = rw
    for c in range(z_per_step):
        zero_copy(c).wait()


def _route(hn_hi, hn_lo, wr_ref, br_ref):
    both = jnp.dot(hn_hi, wr_ref[...], preferred_element_type=F32)
    logits = (both[:, :LANES] + both[:, LANES:]
              + jnp.dot(hn_lo, wr_ref[:, :LANES], preferred_element_type=F32)) + br_ref[...]
    ng, epg = N_EXPERT_GROUPS, EXPERTS_PER_GROUP
    lane = lax.broadcasted_iota(jnp.int32, logits.shape, 1)
    big = jnp.int32(LANES)
    low = jnp.float32(-3.0e38)
    cm = lane < ng
    c = jnp.where(cm, logits, low)
    cmax = jnp.max(c, axis=1, keepdims=True)
    gsel = jnp.min(jnp.where(c == cmax, lane, big), axis=1, keepdims=True)
    p_group = 1.0 / jnp.sum(jnp.where(cm, jnp.exp(c - cmax), 0.0), axis=1, keepdims=True)
    f_lo = ng + epg * gsel
    fm = (lane >= f_lo) & (lane < f_lo + epg)
    f = jnp.where(fm, logits, low)
    v1 = jnp.max(f, axis=1, keepdims=True)
    i1 = jnp.min(jnp.where(fm & (f == v1), lane, big), axis=1, keepdims=True)
    fm2 = fm & (lane != i1)
    f2 = jnp.where(fm2, logits, low)
    v2 = jnp.max(f2, axis=1, keepdims=True)
    i2 = jnp.min(jnp.where(fm2 & (f2 == v2), lane, big), axis=1, keepdims=True)
    e21 = jnp.exp(v2 - v1)
    w1 = p_group / (1.0 + e21)
    w2 = p_group * e21 / (1.0 + e21)
    return (jnp.where(lane == 0, i1 - ng, jnp.where(lane == 1, i2 - ng, 0)),
            jnp.where(lane == 0, w1, jnp.where(lane == 1, w2, 0.0)))


def _outproj(attn2, pool2, x2, w_out, g2, wr, br, sorted_rows, tm=512, n_sub=2, chunk_rows=512,
             zero_rows=ROW_TILE):
    t, d = x2.shape
    ka, kp = attn2.shape[1], pool2.shape[1]
    steps = t // tm
    assert sorted_rows % (steps * zero_rows) == 0
    row = lambda i: (i, 0)
    const = lambda i: (0, 0)
    return pl.pallas_call(
        functools.partial(_outproj_kernel, n_sub=n_sub, z_per_step=sorted_rows // (steps * zero_rows)),
        grid=(steps,),
        in_specs=[
            pl.BlockSpec((tm, ka), row),
            pl.BlockSpec((tm, kp), row),
            pl.BlockSpec((tm, d), row),
            pl.BlockSpec(memory_space=pl.ANY),
            pl.BlockSpec((1, d), const),
            pl.BlockSpec((d, 2 * LANES), const),
            pl.BlockSpec((1, LANES), const),
            pl.BlockSpec(memory_space=pl.ANY),
        ],
        out_specs=[
            pl.BlockSpec((tm, d), row),
            pl.BlockSpec((tm, d // 2), row),
            pl.BlockSpec((tm, LANES), row),
            pl.BlockSpec((tm, LANES), row),
            pl.BlockSpec(memory_space=pl.ANY),
        ],
        out_shape=[
            jax.ShapeDtypeStruct((t, d), F32),
            jax.ShapeDtypeStruct((t, d // 2), jnp.uint32),
            jax.ShapeDtypeStruct((t, LANES), jnp.int32),
            jax.ShapeDtypeStruct((t, LANES), F32),
            jax.ShapeDtypeStruct((sorted_rows, d // 2), jnp.uint32),
        ],
        scratch_shapes=[pltpu.VMEM((ka + kp, d), BF16), pltpu.VMEM((2, chunk_rows, d), F32),
                        pltpu.SemaphoreType.DMA((2,)),
                        pltpu.SemaphoreType.DMA(())],
        compiler_params=pltpu.CompilerParams(
            dimension_semantics=("arbitrary",), vmem_limit_bytes=VMEM_LIMIT),
        name="outproj_router",
    )(attn2, pool2, x2, w_out, g2.reshape(1, d), wr, br,
      jnp.zeros((zero_rows, d // 2), jnp.uint32))


def _unpack_bf16_pair(words):
    hi = lax.bitcast_convert_type(words & jnp.uint32(0xFFFF0000), F32).astype(BF16)
    lo = lax.bitcast_convert_type(words << 16, F32).astype(BF16)
    return hi, lo


def _expert_kernel(te_ref, nt_ref, first_ref, slot_ref, nxt_ref, nxt2_ref,
                   x_ref, wg_hbm, wu_hbm, wd_hbm, y_ref,
                   wgb, wub, wdb, wsem, *, tm, tiles_per_step):
    nt = nt_ref[0]
    n_slots = wgb.shape[0]
    kh = x_ref.shape[1]
    fh = wdb.shape[1] // 2

    def weight_copies(e, s):
        return ((pltpu.make_async_copy(wg_hbm.at[e], wgb.at[s], wsem.at[s]), 0),
                (pltpu.make_async_copy(wu_hbm.at[e], wub.at[s], wsem.at[s]), 1),
                (pltpu.make_async_copy(wd_hbm.at[e, :fh, :], wdb.at[s, :fh, :], wsem.at[s]), 0),
                (pltpu.make_async_copy(wd_hbm.at[e, fh:, :], wdb.at[s, fh:, :], wsem.at[s]), 1))

    @pl.when(pl.program_id(0) == 0)
    def _():
        for c, prio in weight_copies(te_ref[0], 0):
            c.start(priority=prio)

        @pl.when(nxt_ref[0] >= 0)
        def _():
            for c, prio in weight_copies(nxt_ref[0], 1):
                c.start(priority=prio)

    def tile(j, rows):
        @pl.when(j < nt)
        def _():
            ws = slot_ref[j]

            @pl.when(first_ref[j] == 1)
            def _():
                for c, _ in weight_copies(te_ref[j], ws):
                    c.wait()

                @pl.when(nxt2_ref[j] >= 0)
                def _():
                    for c, prio in weight_copies(nxt2_ref[j], (ws + 2) % n_slots):
                        c.start(priority=prio)

            xa, xb = _unpack_bf16_pair(x_ref[rows, :])
            hg = (jnp.dot(xa, wgb[ws, :kh, :].astype(BF16), preferred_element_type=F32)
                  + jnp.dot(xb, wgb[ws, kh:, :].astype(BF16), preferred_element_type=F32))
            hu = (jnp.dot(xa, wub[ws, :kh, :].astype(BF16), preferred_element_type=F32)
                  + jnp.dot(xb, wub[ws, kh:, :].astype(BF16), preferred_element_type=F32))
            act = (hg / (1.0 + jnp.exp(-hg))) * hu
            y_ref[rows, :] = _pack_bf16_pair(
                jnp.dot(act.astype(BF16), wdb[ws].astype(BF16), preferred_element_type=F32))

        @pl.when(j >= nt)
        def _():
            y_ref[rows, :] = jnp.zeros((tm, kh), y_ref.dtype)

    for sub in range(tiles_per_step):
        tile(pl.program_id(0) * tiles_per_step + sub, pl.ds(sub * tm, tm))


def _experts(plan, xs, w_gate, w_up, w_down, tm=ROW_TILE, tiles_per_step=2):
    d, f = w_gate.shape[1], w_gate.shape[2]
    assert xs.shape[1] * 2 == d
    nt = plan["tile_expert"].shape[0]
    assert nt % tiles_per_step == 0
    rows = tm * tiles_per_step
    any_spec = pl.BlockSpec(memory_space=pl.ANY)
    grid_spec = pltpu.PrefetchScalarGridSpec(
        num_scalar_prefetch=6,
        grid=(nt // tiles_per_step,),
        in_specs=[
            pl.BlockSpec((rows, d // 2),
                         lambda j, te, n, *_: (jnp.minimum(j, (n[0] - 1) // tiles_per_step), 0)),
            any_spec, any_spec, any_spec],
        out_specs=pl.BlockSpec((rows, d // 2), lambda j, *_: (j, 0)),
        scratch_shapes=[
            pltpu.VMEM((WEIGHT_SLOTS, d, f), F32),
            pltpu.VMEM((WEIGHT_SLOTS, d, f), F32),
            pltpu.VMEM((WEIGHT_SLOTS, f, d), F32),
            pltpu.SemaphoreType.DMA((WEIGHT_SLOTS,)),
        ],
    )
    return pl.pallas_call(
        functools.partial(_expert_kernel, tm=tm, tiles_per_step=tiles_per_step),
        grid_spec=grid_spec,
        out_shape=jax.ShapeDtypeStruct((nt * tm, d // 2), jnp.uint32),
        compiler_params=pltpu.CompilerParams(
            dimension_semantics=("arbitrary",), vmem_limit_bytes=VMEM_LIMIT),
        name="experts",
    )(plan["tile_expert"], plan["n_tiles"], plan["first"], plan["slot"], plan["next_expert"],
      plan["next2_expert"], xs, w_gate, w_up, w_down)


def _combine_kernel(pos_ref, h_ref, rw_ref, ys_hbm, g_ref, o_ref, ybuf0, ybuf1, ybuf2, sem):
    i = pl.program_id(0)
    last = pl.num_programs(0) - 1
    tm = h_ref.shape[0]
    kh = ybuf0.shape[2]
    bufs = (ybuf0, ybuf1, ybuf2)
    nbuf = len(bufs)

    def start_gather(tile, buf, sm):
        for r in range(tm):
            for k in range(TOP_K_FINE):
                p = pos_ref[(tile * tm + r) * TOP_K_FINE + k]
                pltpu.make_async_copy(ys_hbm.at[pl.ds(p, 1), :], buf.at[k, pl.ds(r, 1), :],
                                      sm).start(priority=k % 2)

    def wait_gather(buf, sm):
        for k in range(TOP_K_FINE):
            pltpu.make_async_copy(ys_hbm.at[pl.ds(0, tm), :], buf.at[k], sm).wait()

    @pl.when(i == 0)
    def _():
        start_gather(0, ybuf0, sem.at[0])
        start_gather(1, ybuf1, sem.at[1])

    def step(par):
        cur = bufs[par]
        ahead = (par + nbuf - 1) % nbuf
        wait_gather(cur, sem.at[par])
        start_gather(jnp.minimum(i + nbuf - 1, last), bufs[ahead], sem.at[ahead])
        w = rw_ref[...]
        halves = []
        for part in range(2):
            y = None
            for k in range(TOP_K_FINE):
                words = cur[k]
                bits = (words & jnp.uint32(0xFFFF0000)) if part == 0 else (words << 16)
                term = w[:, k:k + 1] * lax.bitcast_convert_type(bits, F32)
                y = term if y is None else y + term
            halves.append(h_ref[:, part * kh:(part + 1) * kh] + y)
        ms = sum(jnp.sum(v * v, axis=-1, keepdims=True) for v in halves) / (2 * kh)
        inv = lax.rsqrt(ms + RMS_EPS)
        for part in range(2):
            o_ref[:, part * kh:(part + 1) * kh] = (
                halves[part] * inv * g_ref[:, part * kh:(part + 1) * kh])

        @pl.when(i == last)
        def _():
            for other in range(nbuf):
                if other != par:
                    wait_gather(bufs[other], sem.at[other])

    for par in range(nbuf):
        pl.when(i % nbuf == par)(functools.partial(step, par))


def _combine(pos, h, rw, ys, g, tm=256):
    t, d = h.shape
    grid_spec = pltpu.PrefetchScalarGridSpec(
        num_scalar_prefetch=1,
        grid=(t // tm,),
        in_specs=[
            pl.BlockSpec((tm, d), lambda i, p: (i, 0)),
            pl.BlockSpec((tm, LANES), lambda i, p: (i, 0)),
            pl.BlockSpec(memory_space=pl.ANY),
            pl.BlockSpec((1, d), lambda i, p: (0, 0)),
        ],
        out_specs=pl.BlockSpec((tm, d), lambda i, p: (i, 0)),
        scratch_shapes=[pltpu.VMEM((TOP_K_FINE, tm, d // 2), jnp.uint32),
                        pltpu.VMEM((TOP_K_FINE, tm, d // 2), jnp.uint32),
                        pltpu.VMEM((TOP_K_FINE, tm, d // 2), jnp.uint32),
                        pltpu.SemaphoreType.DMA((3,))],
    )
    return pl.pallas_call(
        _combine_kernel,
        grid_spec=grid_spec,
        out_shape=jax.ShapeDtypeStruct((t, d), F32),
        compiler_params=pltpu.CompilerParams(
            dimension_semantics=("arbitrary",), vmem_limit_bytes=VMEM_LIMIT),
        name="combine",
    )(pos, h, rw, ys, g.reshape(1, d))


def _rank_kernel(ri_ref, meta_ref, cnt_ref, carry_ref, tri_ref):
    tt = ri_ref.shape[0]
    ne = cnt_ref.shape[0]

    @pl.when(pl.program_id(0) == 0)
    def _():
        carry_ref[...] = jnp.zeros_like(carry_ref)
        earlier = (lax.broadcasted_iota(jnp.int32, (tt, tt), 0)
                   < lax.broadcasted_iota(jnp.int32, (tt, tt), 1))
        tri_ref[...] = jnp.where(earlier, 1.0, 0.0).astype(BF16)

    rit = ri_ref[...].astype(F32).T
    expert = lax.broadcasted_iota(jnp.int32, (ne, tt), 0).astype(F32)
    hits = []
    for k in range(TOP_K_FINE):
        hits.append(expert == rit[k:k + 1, :])
    chosen = jnp.where(hits[0] | hits[1], 1.0, 0.0)
    before = (jnp.dot(chosen.astype(BF16), tri_ref[...], preferred_element_type=F32)
              + carry_ref[:, 0:1])
    for k in range(TOP_K_FINE):
        meta_ref[k:k + 1, :] = rit[k:k + 1, :].astype(jnp.int32)
        rank = jnp.sum(jnp.where(hits[k], before, 0.0), axis=0, keepdims=True)
        meta_ref[TOP_K_FINE + k:TOP_K_FINE + k + 1, :] = rank.astype(jnp.int32)
    meta_ref[2 * TOP_K_FINE:, :] = jnp.zeros((meta_ref.shape[0] - 2 * TOP_K_FINE, tt), jnp.int32)
    carry_ref[...] = carry_ref[...] + jnp.sum(chosen, axis=1, keepdims=True)
    cnt_ref[...] = carry_ref[...].astype(jnp.int32)


def _rank(ri, tt=1024):
    t = ri.shape[0]
    assert TOP_K_FINE == 2
    return pl.pallas_call(
        _rank_kernel,
        grid=(t // tt,),
        in_specs=[pl.BlockSpec((tt, LANES), lambda i: (i, 0))],
        out_specs=[pl.BlockSpec((None, 8, tt), lambda i: (i, 0, 0)),
                   pl.BlockSpec((N_EXPERTS, LANES), lambda i: (0, 0))],
        out_shape=[jax.ShapeDtypeStruct((t // tt, 8, tt), jnp.int32),
                   jax.ShapeDtypeStruct((N_EXPERTS, LANES), jnp.int32)],
        scratch_shapes=[pltpu.VMEM((N_EXPERTS, LANES), F32), pltpu.VMEM((tt, tt), BF16)],
        compiler_params=pltpu.CompilerParams(
            dimension_semantics=("arbitrary",), vmem_limit_bytes=VMEM_LIMIT),
        name="route_rank",
    )(ri)


def _scatter_rows_kernel(pos_ref, src_hbm, zeros_hbm, dst_hbm, st0, st1, st2, sem, lsem, *, tm):
    del zeros_hbm
    i = pl.program_id(0)
    stage = (st0, st1, st2)

    last = pl.num_programs(0) - 1
    nbuf = len(stage)

    def load(tile, b):
        row0 = pl.multiple_of(tile * tm, tm)
        return pltpu.make_async_copy(src_hbm.at[pl.ds(row0, tm), :], stage[b], lsem.at[b])

    def wait_rows(b):
        for _ in range(TOP_K_FINE):
            pltpu.make_async_copy(stage[b], dst_hbm.at[pl.ds(0, tm), :], sem.at[b]).wait()

    @pl.when(i == 0)
    def _():
        load(0, 0).start()

    def step(b):
        nb = (b + 1) % nbuf

        @pl.when(i >= nbuf - 1)
        def _():
            wait_rows(nb)

        @pl.when(i < last)
        def _():
            load(i + 1, nb).start()

        load(i, b).wait()
        for r in range(tm):
            for k in range(TOP_K_FINE):
                p = pos_ref[(i * tm + r) * TOP_K_FINE + k]
                pltpu.make_async_copy(stage[b].at[pl.ds(r, 1), :], dst_hbm.at[pl.ds(p, 1), :],
                                      sem.at[b]).start(priority=k % 2)

        @pl.when(i == last)
        def _():
            for d in range(nbuf - 1):
                @pl.when(i >= d)
                def _(d=d):
                    wait_rows((b - d) % nbuf)

    for b in range(nbuf):
        pl.when(i % nbuf == b)(functools.partial(step, b))


def _scatter_rows(pos, src, zeros, tm=256):
    t, w = src.shape
    any_spec = pl.BlockSpec(memory_space=pl.ANY)
    grid_spec = pltpu.PrefetchScalarGridSpec(
        num_scalar_prefetch=1,
        grid=(t // tm,),
        in_specs=[any_spec, any_spec],
        out_specs=any_spec,
        scratch_shapes=[pltpu.VMEM((tm, w), src.dtype), pltpu.VMEM((tm, w), src.dtype),
                        pltpu.VMEM((tm, w), src.dtype),
                        pltpu.SemaphoreType.DMA((3,)), pltpu.SemaphoreType.DMA((3,))],
    )
    return pl.pallas_call(
        functools.partial(_scatter_rows_kernel, tm=tm),
        grid_spec=grid_spec,
        out_shape=jax.ShapeDtypeStruct(zeros.shape, zeros.dtype),
        input_output_aliases={2: 0},
        compiler_params=pltpu.CompilerParams(dimension_semantics=("arbitrary",)),
        name="scatter_rows",
    )(pos, src, zeros)


def _max_tiles(t, tm=ROW_TILE):
    return t * TOP_K_FINE // tm + N_EXPERTS


def _sort_plan(ri, tm=ROW_TILE):
    t = ri.shape[0]
    k = TOP_K_FINE
    a = t * k
    nt = _max_tiles(t, tm)
    meta, cnt = _rank(ri)
    counts = cnt[:, 0]
    padded = ((counts + tm - 1) // tm) * tm
    off_end = jnp.cumsum(padded)
    off = off_end - padded
    ef = meta[:, 0:k, :]
    eid = jnp.arange(N_EXPERTS, dtype=jnp.int32)
    pos = jnp.sum(jnp.where(ef[..., None] == eid, off, 0), axis=-1) + meta[:, k:2 * k, :]
    pos = jnp.transpose(pos, (0, 2, 1)).reshape(a)
    n_tiles = off_end[-1] // tm
    tile_idx = jnp.arange(nt, dtype=jnp.int32)
    te = jnp.sum((off_end[None, :] <= (tile_idx * tm)[:, None]).astype(jnp.int32), axis=1)
    te = jnp.minimum(te, N_EXPERTS - 1)
    te = jnp.where(tile_idx < n_tiles, te, te[n_tiles - 1]).astype(jnp.int32)
    first = ((tile_idx == 0) | (te != jnp.roll(te, 1))) & (tile_idx < n_tiles)
    slot = (jnp.cumsum(first.astype(jnp.int32)) - 1) % WEIGHT_SLOTS
    eid = jnp.arange(N_EXPERTS, dtype=jnp.int32)
    later = (eid[None, :] > eid[:, None]) & (counts[None, :] > 0)
    next_e = jnp.min(jnp.where(later, eid[None, :], N_EXPERTS), axis=1)
    next_e = jnp.where(next_e < N_EXPERTS, next_e, -1)
    next2_e = jnp.where(next_e >= 0, next_e[jnp.maximum(next_e, 0)], -1)
    return dict(
        next2_expert=next2_e[te].astype(jnp.int32),
        pos=pos.astype(jnp.int32), tile_expert=te,
        n_tiles=n_tiles.reshape(1).astype(jnp.int32), first=first.astype(jnp.int32),
        slot=slot.astype(jnp.int32), next_expert=next_e[te].astype(jnp.int32))


def kernel(x, norm1_g, w_in, lambda_q1, lambda_k1, lambda_q2, lambda_k2, subln_g, pool_w, pool_scale, w_out, norm2_g, w_coarse, b_coarse, w_fine, b_fine, w_gate, w_up, w_down, final_norm_g):
    b, s, d = x.shape
    t = b * s
    assert norm1_g.shape[0] == 1
    x2 = x.reshape(t, d)
    proj, pool, v_t = _inproj(x2, norm1_g[0], w_in[0], pool_w[0], pool_scale[0], s)
    proj3 = proj.reshape(b, s, proj.shape[1])
    attn = _attention(proj3, v_t, lambda_q1, lambda_k1, lambda_q2, lambda_k2, subln_g[0])

    ng, epg = N_EXPERT_GROUPS, EXPERTS_PER_GROUP
    wr = jnp.concatenate(
        [w_coarse[0], jnp.transpose(w_fine[0], (1, 0, 2)).reshape(d, ng * epg)], axis=1)
    wr = jnp.pad(wr, ((0, 0), (0, LANES - wr.shape[1])))
    br = jnp.concatenate([b_coarse[0], b_fine[0].reshape(ng * epg)])
    br = jnp.pad(br, (0, LANES - br.shape[0])).reshape(1, LANES)
    wr_hi = wr.astype(BF16)
    wr_lo = (wr - wr_hi.astype(F32)).astype(BF16)
    sorted_rows = _max_tiles(t) * ROW_TILE
    h, hn, ri, rw, xz = _outproj(attn.reshape(t, ATTN_WIDTH), pool, x2, w_out[0], norm2_g[0],
                                 jnp.concatenate([wr_hi, wr_lo], axis=1), br, sorted_rows)

    plan = _sort_plan(ri)
    xs = _scatter_rows(plan["pos"], hn, xz)
    ys = _experts(plan, xs, w_gate[0], w_up[0], w_down[0])
    out = _combine(plan["pos"], h, rw, ys, final_norm_g)
    return out.reshape(b, s, d)
```

```python
import functools
import math

import jax
import jax.numpy as jnp
from jax import lax
from jax.experimental import pallas as pl
from jax.experimental.pallas import tpu as pltpu

N_DIFF_HEADS = 8
DIFF_HEAD_DIM = 64
DIFF_V_DIM = 2 * DIFF_HEAD_DIM
ATTN_WIDTH = N_DIFF_HEADS * DIFF_V_DIM
POOL_WINDOWS = (2, 4, 8, 16)
POOL_GROUP_DIM = 256
POOL_WIDTH = len(POOL_WINDOWS) * POOL_GROUP_DIM
N_EXPERT_GROUPS = 4
EXPERTS_PER_GROUP = 8
N_EXPERTS = N_EXPERT_GROUPS * EXPERTS_PER_GROUP
TOP_K_FINE = 2
RMS_EPS = 1e-6
NEG_INF = -1e30
LAM_INIT = 0.8 - 0.6 * math.exp(-0.3 * 0)
LOG2E = math.log2(math.e)
Q_SCALE = DIFF_HEAD_DIM ** -0.5 * LOG2E

LANES = 128
SUBLANES = 8
VMEM_LIMIT = 56 * 1024 * 1024
ROW_TILE = 256
WEIGHT_SLOTS = 3

F32 = jnp.float32
BF16 = jnp.bfloat16


def _rms(x, g):
    return x * lax.rsqrt(jnp.mean(x * x, axis=-1, keepdims=True) + RMS_EPS) * g


def _load_weight_bf16(w_hbm, wb_ref, stage, sem):
    ck = stage.shape[1]
    nchunk = w_hbm.shape[0] // ck

    def copy(c):
        return pltpu.make_async_copy(w_hbm.at[c * ck:(c + 1) * ck, :], stage.at[c % 2], sem.at[c % 2])

    copy(0).start()
    for c in range(nchunk):
        if c + 1 < nchunk:
            copy(c + 1).start()
        copy(c).wait()
        wb_ref[c * ck:(c + 1) * ck, :] = stage[c % 2].astype(BF16)


def _pool_mix(u, prev, seq_pos, pw_ref, ps_ref):
    halo = prev.shape[0]
    c = POOL_GROUP_DIM
    ext = jnp.concatenate([prev, u], axis=0)
    outs = []
    for gi, win in enumerate(POOL_WINDOWS):
        assert win & (win - 1) == 0 and win <= halo
        run = ext[:, gi * c:(gi + 1) * c]
        span = 1
        while span < win:
            run = run + pltpu.roll(run, span, axis=0)
            span *= 2
        count = jnp.minimum(seq_pos + 1, win).astype(F32)
        pooled = run[halo:, :] / count - u[:, gi * c:(gi + 1) * c]
        mixed = jnp.dot(pooled.astype(BF16), pw_ref[gi].astype(BF16), preferred_element_type=F32)
        outs.append(mixed * ps_ref[:, gi * c:(gi + 1) * c])
    return outs


def _inproj_kernel(x_ref, g_ref, w_hbm, pw_ref, ps_ref, o_ref, pool_ref, vt_ref,
                   wb_ref, stage, sem, halo_ref, *, tn, seq_tiles):
    i = pl.program_id(0)
    tm = x_ref.shape[0]

    @pl.when(i == 0)
    def _():
        halo_ref[...] = jnp.zeros_like(halo_ref)
        _load_weight_bf16(w_hbm, wb_ref, stage, sem)

    xn = _rms(x_ref[...], g_ref[...]).astype(BF16)
    u = jnp.dot(xn, wb_ref[:, 3 * tn:4 * tn], preferred_element_type=F32)
    v = jnp.dot(xn, wb_ref[:, 2 * tn:3 * tn], preferred_element_type=F32)
    vt_ref[...] = v.T.astype(vt_ref.dtype)
    q = jnp.dot(xn, wb_ref[:, 0 * tn:1 * tn], preferred_element_type=F32) * Q_SCALE
    o_ref[:, 0 * tn:1 * tn] = q.astype(o_ref.dtype)

    tile_in_seq = i % seq_tiles
    prev = jnp.where(tile_in_seq == 0, 0.0, halo_ref[...])
    halo_ref[...] = u[tm - halo_ref.shape[0]:, :]
    seq_pos = tile_in_seq * tm + lax.broadcasted_iota(jnp.int32, (tm, 1), 0)
    c = POOL_GROUP_DIM
    for gi, mixed in enumerate(_pool_mix(u, prev, seq_pos, pw_ref, ps_ref)):
        pool_ref[:, gi * c:(gi + 1) * c] = mixed.astype(pool_ref.dtype)

    k = jnp.dot(xn, wb_ref[:, 1 * tn:2 * tn], preferred_element_type=F32)
    o_ref[:, 1 * tn:2 * tn] = k.astype(o_ref.dtype)


def _inproj(x2, g, w, pool_w, pool_scale, seq_len, tm=512, tn=ATTN_WIDTH, chunk_rows=256):
    t, d = x2.shape
    n = w.shape[1]
    assert n == 4 * tn and tn == POOL_WIDTH and seq_len % tm == 0
    halo = 2 * SUBLANES
    assert max(POOL_WINDOWS) <= halo
    return pl.pallas_call(
        functools.partial(_inproj_kernel, tn=tn, seq_tiles=seq_len // tm),
        grid=(t // tm,),
        in_specs=[
            pl.BlockSpec((tm, d), lambda i: (i, 0)),
            pl.BlockSpec((1, d), lambda i: (0, 0)),
            pl.BlockSpec(memory_space=pl.ANY),
            pl.BlockSpec(pool_w.shape, lambda i: (0, 0, 0)),
            pl.BlockSpec((1, tn), lambda i: (0, 0)),
        ],
        out_specs=[pl.BlockSpec((tm, 2 * tn), lambda i: (i, 0)),
                   pl.BlockSpec((tm, tn), lambda i: (i, 0)),
                   pl.BlockSpec((tn, tm), lambda i: (0, i))],
        out_shape=[jax.ShapeDtypeStruct((t, 2 * tn), BF16), jax.ShapeDtypeStruct((t, tn), BF16),
                   jax.ShapeDtypeStruct((tn, t), BF16)],
        scratch_shapes=[pltpu.VMEM((d, n), BF16), pltpu.VMEM((2, chunk_rows, n), F32),
                        pltpu.SemaphoreType.DMA((2,)), pltpu.VMEM((halo, tn), F32)],
        compiler_params=pltpu.CompilerParams(
            dimension_semantics=("arbitrary",), vmem_limit_bytes=VMEM_LIMIT),
        name="inproj",
    )(x2, g.reshape(1, d), w, pool_w, pool_scale.reshape(1, tn))


def _attn_kernel(lq1_ref, lk1_ref, lq2_ref, lk2_ref, q_ref, k_ref, v_ref, g_ref, o_ref,
                 tab_ref, vt_ref, *, tq):
    h = pl.program_id(0)
    s_len = q_ref.shape[0]
    nq = s_len // tq
    d, dv = DIFF_HEAD_DIM, DIFF_V_DIM

    @pl.when(pl.program_id(1) == 0)
    def _():
        slope = jnp.exp2(jnp.full((1, 1), -8.0 / N_DIFF_HEADS, F32) * (h + 1).astype(F32))
        rel = ((nq - 1) * tq + lax.broadcasted_iota(jnp.int32, (s_len, tq), 1)
               - lax.broadcasted_iota(jnp.int32, (s_len, tq), 0))
        tab_ref[...] = jnp.where(rel >= 0, (-LOG2E * slope) * rel.astype(F32), NEG_INF)

    vt_ref[:dv, :] = v_ref[...]
    vt_ref[dv:, :] = jnp.ones((vt_ref.shape[0] - dv, s_len), BF16)

    lam = (jnp.exp(jnp.sum(lq1_ref[...] * lk1_ref[...], axis=1, keepdims=True))
           - jnp.exp(jnp.sum(lq2_ref[...] * lk2_ref[...], axis=1, keepdims=True)) + LAM_INIT)
    dn = (((1,), (1,)), ((), ()))
    lane = lax.broadcasted_iota(jnp.int32, (tq, 2 * d), 1)

    ck = 4 * tq

    def key_chunks(qi):
        n = (qi + 1) * tq
        return [(lo, min(lo + ck, n)) for lo in range(0, n, ck)]

    def query_halves(qi):
        q = q_ref[qi * tq:(qi + 1) * tq, :]
        zero = jnp.zeros_like(q)
        return (jnp.where(lane < d, q, zero), jnp.where(lane >= d, q, zero))

    def score_chunk(qi, qh, lo, hi):
        base = (nq - 1 - qi) * tq
        bias = tab_ref[base + lo:base + hi, :]
        return [lax.dot_general(k_ref[lo:hi, :], qh[a], dn, preferred_element_type=F32) + bias
                for a in range(2)]

    s_next = [score_chunk(0, query_halves(0), lo, hi) for lo, hi in key_chunks(0)]
    for qi in range(nq):
        s_cur = s_next
        cur_chunks = key_chunks(qi)
        m = [functools.reduce(jnp.maximum, [jnp.max(c[a], axis=0, keepdims=True) for c in s_cur])
             for a in range(2)]
        nxt_chunks = key_chunks(qi + 1) if qi + 1 < nq else []
        qh_next = query_halves(qi + 1) if nxt_chunks else None
        s_next = []
        acc = [None, None]
        for c in range(max(len(cur_chunks), len(nxt_chunks))):
            if c < len(nxt_chunks):
                s_next.append(score_chunk(qi + 1, qh_next, *nxt_chunks[c]))
            if c < len(cur_chunks):
                lo, hi = cur_chunks[c]
                for a in range(2):
                    p = jnp.exp2(s_cur[c][a] - m[a]).astype(BF16)
                    part = jnp.dot(vt_ref[:, lo:hi], p, preferred_element_type=F32)
                    acc[a] = part if acc[a] is None else acc[a] + part
        outs = [acc[a][:dv] / acc[a][dv:dv + 1] for a in range(2)]
        ot = outs[0] - lam * outs[1]
        yt = ot * lax.rsqrt(jnp.mean(ot * ot, axis=0, keepdims=True) + RMS_EPS) * g_ref[...]
        o_ref[qi * tq:(qi + 1) * tq, :] = (yt * (1.0 - LAM_INIT)).T.astype(o_ref.dtype)


def _attention(proj3, v_t, lq1, lk1, lq2, lk2, subln_g, tq=256):
    b, s, _ = proj3.shape
    dv = DIFF_V_DIM
    nh = N_DIFF_HEADS
    ones_rows = 16
    lam_spec = pl.BlockSpec((1, DIFF_HEAD_DIM), lambda hi, bi: (0, 0))
    return pl.pallas_call(
        functools.partial(_attn_kernel, tq=tq),
        grid=(nh, b),
        in_specs=[
            lam_spec, lam_spec, lam_spec, lam_spec,
            pl.BlockSpec((None, s, dv), lambda hi, bi: (bi, 0, hi)),
            pl.BlockSpec((None, s, dv), lambda hi, bi: (bi, 0, nh + hi)),
            pl.BlockSpec((dv, s), lambda hi, bi: (hi, bi)),
            pl.BlockSpec((dv, 1), lambda hi, bi: (0, 0)),
        ],
        out_specs=pl.BlockSpec((None, s, dv), lambda hi, bi: (bi, 0, hi)),
        out_shape=jax.ShapeDtypeStruct((b, s, ATTN_WIDTH), BF16),
        scratch_shapes=[
            pltpu.VMEM((s, tq), F32),
            pltpu.VMEM((dv + ones_rows, s), BF16),
        ],
        compiler_params=pltpu.CompilerParams(
            dimension_semantics=("arbitrary", "arbitrary"),
            vmem_limit_bytes=VMEM_LIMIT),
        name="diff_attn",
    )(lq1, lk1, lq2, lk2, proj3, proj3, v_t, subln_g.reshape(dv, 1))


def _pack_bf16_pair(x):
    bits = lax.bitcast_convert_type(x.astype(BF16).astype(F32), jnp.uint32)
    half = bits.shape[1] // 2
    return bits[:, :half] | (bits[:, half:] >> 16)


def _outproj_kernel(a_ref, p_ref, x_ref, w_hbm, g_ref, wr_ref, br_ref,
                    h_ref, hn_ref, ri_ref, rw_ref, xz_hbm, w_ref, stage, sem, zbuf, zsem,
                    *, n_sub, z_per_step):
    i = pl.program_id(0)

    @pl.when(i == 0)
    def _():
        zbuf[...] = jnp.zeros_like(zbuf)
        _load_weight_bf16(w_hbm, w_ref, stage, sem)

    zr = zbuf.shape[0]

    def zero_copy(c):
        row0 = pl.multiple_of((i * z_per_step + c) * zr, zr)
        return pltpu.make_async_copy(zbuf, xz_hbm.at[pl.ds(row0, zr), :], zsem)

    for c in range(z_per_step):
        zero_copy(c).start()

    ka = a_ref.shape[1]
    hs = x_ref.shape[0] // n_sub
    normed = []
    for r in range(n_sub):
        rows = pl.ds(r * hs, hs)
        mixed = (jnp.dot(a_ref[rows, :], w_ref[:ka, :], preferred_element_type=F32)
                 + jnp.dot(p_ref[rows, :], w_ref[ka:, :], preferred_element_type=F32))
        h = x_ref[rows, :] + mixed
        h_ref[rows, :] = h
        hn = _rms(h, g_ref[...])
        hn_ref[rows, :] = _pack_bf16_pair(hn)
        hn_hi = hn.astype(BF16)
        normed.append((hn_hi, (hn - hn_hi.astype(F32)).astype(BF16)))
    for r in range(n_sub):
        rows = pl.ds(r * hs, hs)
        ri, rw = _route(normed[r][0], normed[r][1], wr_ref, br_ref)
        ri_ref[rows, :] = ri
        rw_ref[rows, :] = rw
    for c in range(z_per_step):
        zero_copy(c).wait()


def _route(hn_hi, hn_lo, wr_ref, br_ref):
    both = jnp.dot(hn_hi, wr_ref[...], preferred_element_type=F32)
    logits = (both[:, :LANES] + both[:, LANES:]
              + jnp.dot(hn_lo, wr_ref[:, :LANES], preferred_element_type=F32)) + br_ref[...]
    ng, epg = N_EXPERT_GROUPS, EXPERTS_PER_GROUP
    lane = lax.broadcasted_iota(jnp.int32, logits.shape, 1)
    big = jnp.int32(LANES)
    low = jnp.float32(-3.0e38)
    cm = lane < ng
    c = jnp.where(cm, logits, low)
    cmax = jnp.max(c, axis=1, keepdims=True)
    gsel = jnp.min(jnp.where(c == cmax, lane, big), axis=1, keepdims=True)
    p_group = 1.0 / jnp.sum(jnp.where(cm, jnp.exp(c - cmax), 0.0), axis=1, keepdims=True)
    f_lo = ng + epg * gsel
    fm = (lane >= f_lo) & (lane < f_lo + epg)
    f = jnp.where(fm, logits, low)
    v1 = jnp.max(f, axis=1, keepdims=True)
    i1 = jnp.min(jnp.where(fm & (f == v1), lane, big), axis=1, keepdims=True)
    fm2 = fm & (lane != i1)
    f2 = jnp.where(fm2, logits, low)
    v2 = jnp.max(f2, axis=1, keepdims=True)
    i2 = jnp.min(jnp.where(fm2 & (f2 == v2), lane, big), axis=1, keepdims=True)
    e21 = jnp.exp(v2 - v1)
    w1 = p_group / (1.0 + e21)
    w2 = p_group * e21 / (1.0 + e21)
    return (jnp.where(lane == 0, i1 - ng, jnp.where(lane == 1, i2 - ng, 0)),
            jnp.where(lane == 0, w1, jnp.where(lane == 1, w2, 0.0)))


def _outproj(attn2, pool2, x2, w_out, g2, wr, br, sorted_rows, tm=512, n_sub=2, chunk_rows=512,
             zero_rows=ROW_TILE):
    t, d = x2.shape
    ka, kp = attn2.shape[1], pool2.shape[1]
    steps = t // tm
    assert sorted_rows % (steps * zero_rows) == 0
    row = lambda i: (i, 0)
    const = lambda i: (0, 0)
    return pl.pallas_call(
        functools.partial(_outproj_kernel, n_sub=n_sub, z_per_step=sorted_rows // (steps * zero_rows)),
        grid=(steps,),
        in_specs=[
            pl.BlockSpec((tm, ka), row),
            pl.BlockSpec((tm, kp), row),
            pl.BlockSpec((tm, d), row),
            pl.BlockSpec(memory_space=pl.ANY),
            pl.BlockSpec((1, d), const),
            pl.BlockSpec((d, 2 * LANES), const),
            pl.BlockSpec((1, LANES), const),
        ],
        out_specs=[
            pl.BlockSpec((tm, d), row),
            pl.BlockSpec((tm, d // 2), row),
            pl.BlockSpec((tm, LANES), row),
            pl.BlockSpec((tm, LANES), row),
            pl.BlockSpec(memory_space=pl.ANY),
        ],
        out_shape=[
            jax.ShapeDtypeStruct((t, d), F32),
            jax.ShapeDtypeStruct((t, d // 2), jnp.uint32),
            jax.ShapeDtypeStruct((t, LANES), jnp.int32),
            jax.ShapeDtypeStruct((t, LANES), F32),
            jax.ShapeDtypeStruct((sorted_rows, d // 2), jnp.uint32),
        ],
        scratch_shapes=[pltpu.VMEM((ka + kp, d), BF16), pltpu.VMEM((2, chunk_rows, d), F32),
                        pltpu.SemaphoreType.DMA((2,)),
                        pltpu.VMEM((zero_rows, d // 2), jnp.uint32), pltpu.SemaphoreType.DMA(())],
        compiler_params=pltpu.CompilerParams(
            dimension_semantics=("arbitrary",), vmem_limit_bytes=VMEM_LIMIT),
        name="outproj_router",
    )(attn2, pool2, x2, w_out, g2.reshape(1, d), wr, br)


def _unpack_bf16_pair(words):
    hi = lax.bitcast_convert_type(words & jnp.uint32(0xFFFF0000), F32).astype(BF16)
    lo = lax.bitcast_convert_type(words << 16, F32).astype(BF16)
    return hi, lo


def _expert_kernel(te_ref, nt_ref, first_ref, slot_ref, nxt_ref, nxt2_ref,
                   x_ref, wg_hbm, wu_hbm, wd_hbm, y_ref,
                   wgb, wub, wdb, wsem, *, tm, tiles_per_step):
    nt = nt_ref[0]
    n_slots = wgb.shape[0]
    kh = x_ref.shape[1]
    fh = wdb.shape[1] // 2

    def weight_copies(e, s):
        return ((pltpu.make_async_copy(wg_hbm.at[e], wgb.at[s], wsem.at[s]), 0),
                (pltpu.make_async_copy(wu_hbm.at[e], wub.at[s], wsem.at[s]), 1),
                (pltpu.make_async_copy(wd_hbm.at[e, :fh, :], wdb.at[s, :fh, :], wsem.at[s]), 0),
                (pltpu.make_async_copy(wd_hbm.at[e, fh:, :], wdb.at[s, fh:, :], wsem.at[s]), 1))

    @pl.when(pl.program_id(0) == 0)
    def _():
        for c, prio in weight_copies(te_ref[0], 0):
            c.start(priority=prio)

        @pl.when(nxt_ref[0] >= 0)
        def _():
            for c, prio in weight_copies(nxt_ref[0], 1):
                c.start(priority=prio)

    def tile(j, rows):
        @pl.when(j < nt)
        def _():
            ws = slot_ref[j]

            @pl.when(first_ref[j] == 1)
            def _():
                for c, _ in weight_copies(te_ref[j], ws):
                    c.wait()

                @pl.when(nxt2_ref[j] >= 0)
                def _():
                    for c, prio in weight_copies(nxt2_ref[j], (ws + 2) % n_slots):
                        c.start(priority=prio)

            xa, xb = _unpack_bf16_pair(x_ref[rows, :])
            hg = (jnp.dot(xa, wgb[ws, :kh, :].astype(BF16), preferred_element_type=F32)
                  + jnp.dot(xb, wgb[ws, kh:, :].astype(BF16), preferred_element_type=F32))
            hu = (jnp.dot(xa, wub[ws, :kh, :].astype(BF16), preferred_element_type=F32)
                  + jnp.dot(xb, wub[ws, kh:, :].astype(BF16), preferred_element_type=F32))
            act = (hg / (1.0 + jnp.exp(-hg))) * hu
            y_ref[rows, :] = _pack_bf16_pair(
                jnp.dot(act.astype(BF16), wdb[ws].astype(BF16), preferred_element_type=F32))

        @pl.when(j >= nt)
        def _():
            y_ref[rows, :] = jnp.zeros((tm, kh), y_ref.dtype)

    for sub in range(tiles_per_step):
        tile(pl.program_id(0) * tiles_per_step + sub, pl.ds(sub * tm, tm))


def _experts(plan, xs, w_gate, w_up, w_down, tm=ROW_TILE, tiles_per_step=2):
    d, f = w_gate.shape[1], w_gate.shape[2]
    assert xs.shape[1] * 2 == d
    nt = plan["tile_expert"].shape[0]
    assert nt % tiles_per_step == 0
    rows = tm * tiles_per_step
    any_spec = pl.BlockSpec(memory_space=pl.ANY)
    grid_spec = pltpu.PrefetchScalarGridSpec(
        num_scalar_prefetch=6,
        grid=(nt // tiles_per_step,),
        in_specs=[
            pl.BlockSpec((rows, d // 2),
                         lambda j, te, n, *_: (jnp.minimum(j, (n[0] - 1) // tiles_per_step), 0)),
            any_spec, any_spec, any_spec],
        out_specs=pl.BlockSpec((rows, d // 2), lambda j, *_: (j, 0)),
        scratch_shapes=[
            pltpu.VMEM((WEIGHT_SLOTS, d, f), F32),
            pltpu.VMEM((WEIGHT_SLOTS, d, f), F32),
            pltpu.VMEM((WEIGHT_SLOTS, f, d), F32),
            pltpu.SemaphoreType.DMA((WEIGHT_SLOTS,)),
        ],
    )
    return pl.pallas_call(
        functools.partial(_expert_kernel, tm=tm, tiles_per_step=tiles_per_step),
        grid_spec=grid_spec,
        out_shape=jax.ShapeDtypeStruct((nt * tm, d // 2), jnp.uint32),
        compiler_params=pltpu.CompilerParams(
            dimension_semantics=("arbitrary",), vmem_limit_bytes=VMEM_LIMIT),
        name="experts",
    )(plan["tile_expert"], plan["n_tiles"], plan["first"], plan["slot"], plan["next_expert"],
      plan["next2_expert"], xs, w_gate, w_up, w_down)


def _combine_kernel(pos_ref, h_ref, rw_ref, ys_hbm, g_ref, o_ref, ybuf0, ybuf1, ybuf2, sem):
    i = pl.program_id(0)
    last = pl.num_programs(0) - 1
    tm = h_ref.shape[0]
    kh = ybuf0.shape[2]
    bufs = (ybuf0, ybuf1, ybuf2)
    nbuf = len(bufs)

    def start_gather(tile, buf, sm, lo=0, hi=tm):
        for r in range(lo, hi):
            for k in range(TOP_K_FINE):
                p = pos_ref[(tile * tm + r) * TOP_K_FINE + k]
                pltpu.make_async_copy(ys_hbm.at[pl.ds(p, 1), :], buf.at[k, pl.ds(r, 1), :],
                                      sm).start(priority=k % 2)

    def wait_gather(buf, sm):
        for k in range(TOP_K_FINE):
            pltpu.make_async_copy(ys_hbm.at[pl.ds(0, tm), :], buf.at[k], sm).wait()

    @pl.when(i == 0)
    def _():
        start_gather(0, ybuf0, sem.at[0])
        start_gather(1, ybuf1, sem.at[1])

    def step(par):
        cur = bufs[par]
        ahead = (par + nbuf - 1) % nbuf
        wait_gather(cur, sem.at[par])
        nxt = jnp.minimum(i + nbuf - 1, last)
        for lo in range(0, tm, SUBLANES):
            rows = slice(lo, lo + SUBLANES)
            w = rw_ref[rows, :]
            words = [cur[k, rows, :] for k in range(TOP_K_FINE)]
            hh = h_ref[rows, :]
            start_gather(nxt, bufs[ahead], sem.at[ahead], lo, lo + SUBLANES)
            halves = []
            for part in range(2):
                y = None
                for k in range(TOP_K_FINE):
                    bits = (words[k] & jnp.uint32(0xFFFF0000)) if part == 0 else (words[k] << 16)
                    term = w[:, k:k + 1] * lax.bitcast_convert_type(bits, F32)
                    y = term if y is None else y + term
                halves.append(hh[:, part * kh:(part + 1) * kh] + y)
            ms = sum(jnp.sum(v * v, axis=-1, keepdims=True) for v in halves) / (2 * kh)
            inv = lax.rsqrt(ms + RMS_EPS)
            for part in range(2):
                o_ref[rows, part * kh:(part + 1) * kh] = (
                    halves[part] * inv * g_ref[:, part * kh:(part + 1) * kh])

        @pl.when(i == last)
        def _():
            for other in range(nbuf):
                if other != par:
                    wait_gather(bufs[other], sem.at[other])

    for par in range(nbuf):
        pl.when(i % nbuf == par)(functools.partial(step, par))


def _combine(pos, h, rw, ys, g, tm=256):
    t, d = h.shape
    grid_spec = pltpu.PrefetchScalarGridSpec(
        num_scalar_prefetch=1,
        grid=(t // tm,),
        in_specs=[
            pl.BlockSpec((tm, d), lambda i, p: (i, 0)),
            pl.BlockSpec((tm, LANES), lambda i, p: (i, 0)),
            pl.BlockSpec(memory_space=pl.ANY),
            pl.BlockSpec((1, d), lambda i, p: (0, 0)),
        ],
        out_specs=pl.BlockSpec((tm, d), lambda i, p: (i, 0)),
        scratch_shapes=[pltpu.VMEM((TOP_K_FINE, tm, d // 2), jnp.uint32),
                        pltpu.VMEM((TOP_K_FINE, tm, d // 2), jnp.uint32),
                        pltpu.VMEM((TOP_K_FINE, tm, d // 2), jnp.uint32),
                        pltpu.SemaphoreType.DMA((3,))],
    )
    return pl.pallas_call(
        _combine_kernel,
        grid_spec=grid_spec,
        out_shape=jax.ShapeDtypeStruct((t, d), F32),
        compiler_params=pltpu.CompilerParams(
            dimension_semantics=("arbitrary",), vmem_limit_bytes=VMEM_LIMIT),
        name="combine",
    )(pos, h, rw, ys, g.reshape(1, d))


def _rank_kernel(ri_ref, meta_ref, cnt_ref, carry_ref, tri_ref):
    tt = ri_ref.shape[0]
    ne = cnt_ref.shape[0]

    @pl.when(pl.program_id(0) == 0)
    def _():
        carry_ref[...] = jnp.zeros_like(carry_ref)
        earlier = (lax.broadcasted_iota(jnp.int32, (tt, tt), 0)
                   < lax.broadcasted_iota(jnp.int32, (tt, tt), 1))
        tri_ref[...] = jnp.where(earlier, 1.0, 0.0).astype(BF16)

    rit = ri_ref[...].astype(F32).T
    expert = lax.broadcasted_iota(jnp.int32, (ne, tt), 0).astype(F32)
    hits = []
    for k in range(TOP_K_FINE):
        hits.append(expert == rit[k:k + 1, :])
    chosen = jnp.where(hits[0] | hits[1], 1.0, 0.0)
    before = (jnp.dot(chosen.astype(BF16), tri_ref[...], preferred_element_type=F32)
              + carry_ref[:, 0:1])
    for k in range(TOP_K_FINE):
        meta_ref[k:k + 1, :] = rit[k:k + 1, :].astype(jnp.int32)
        rank = jnp.sum(jnp.where(hits[k], before, 0.0), axis=0, keepdims=True)
        meta_ref[TOP_K_FINE + k:TOP_K_FINE + k + 1, :] = rank.astype(jnp.int32)
    meta_ref[2 * TOP_K_FINE:, :] = jnp.zeros((meta_ref.shape[0] - 2 * TOP_K_FINE, tt), jnp.int32)
    carry_ref[...] = carry_ref[...] + jnp.sum(chosen, axis=1, keepdims=True)
    cnt_ref[...] = carry_ref[...].astype(jnp.int32)


def _rank(ri, tt=1024):
    t = ri.shape[0]
    assert TOP_K_FINE == 2
    return pl.pallas_call(
        _rank_kernel,
        grid=(t // tt,),
        in_specs=[pl.BlockSpec((tt, LANES), lambda i: (i, 0))],
        out_specs=[pl.BlockSpec((None, 8, tt), lambda i: (i, 0, 0)),
                   pl.BlockSpec((N_EXPERTS, LANES), lambda i: (0, 0))],
        out_shape=[jax.ShapeDtypeStruct((t // tt, 8, tt), jnp.int32),
                   jax.ShapeDtypeStruct((N_EXPERTS, LANES), jnp.int32)],
        scratch_shapes=[pltpu.VMEM((N_EXPERTS, LANES), F32), pltpu.VMEM((tt, tt), BF16)],
        compiler_params=pltpu.CompilerParams(
            dimension_semantics=("arbitrary",), vmem_limit_bytes=VMEM_LIMIT),
        name="route_rank",
    )(ri)


def _scatter_rows_kernel(pos_ref, src_hbm, zeros_hbm, dst_hbm, st0, st1, st2, sem, lsem, *, tm):
    del zeros_hbm
    i = pl.program_id(0)
    stage = (st0, st1, st2)

    last = pl.num_programs(0) - 1
    nbuf = len(stage)

    def load(tile, b):
        row0 = pl.multiple_of(tile * tm, tm)
        return pltpu.make_async_copy(src_hbm.at[pl.ds(row0, tm), :], stage[b], lsem.at[b])

    def wait_rows(b):
        for _ in range(TOP_K_FINE):
            pltpu.make_async_copy(stage[b], dst_hbm.at[pl.ds(0, tm), :], sem.at[b]).wait()

    @pl.when(i == 0)
    def _():
        load(0, 0).start()

    def step(b):
        nb = (b + 1) % nbuf

        @pl.when(i >= nbuf - 1)
        def _():
            wait_rows(nb)

        @pl.when(i < last)
        def _():
            load(i + 1, nb).start()

        load(i, b).wait()
        for r in range(tm):
            for k in range(TOP_K_FINE):
                p = pos_ref[(i * tm + r) * TOP_K_FINE + k]
                pltpu.make_async_copy(stage[b].at[pl.ds(r, 1), :], dst_hbm.at[pl.ds(p, 1), :],
                                      sem.at[b]).start(priority=k % 2)

        @pl.when(i == last)
        def _():
            for d in range(nbuf - 1):
                @pl.when(i >= d)
                def _(d=d):
                    wait_rows((b - d) % nbuf)

    for b in range(nbuf):
        pl.when(i % nbuf == b)(functools.partial(step, b))


def _scatter_rows(pos, src, zeros, tm=256):
    t, w = src.shape
    any_spec = pl.BlockSpec(memory_space=pl.ANY)
    grid_spec = pltpu.PrefetchScalarGridSpec(
        num_scalar_prefetch=1,
        grid=(t // tm,),
        in_specs=[any_spec, any_spec],
        out_specs=any_spec,
        scratch_shapes=[pltpu.VMEM((tm, w), src.dtype), pltpu.VMEM((tm, w), src.dtype),
                        pltpu.VMEM((tm, w), src.dtype),
                        pltpu.SemaphoreType.DMA((3,)), pltpu.SemaphoreType.DMA((3,))],
    )
    return pl.pallas_call(
        functools.partial(_scatter_rows_kernel, tm=tm),
        grid_spec=grid_spec,
        out_shape=jax.ShapeDtypeStruct(zeros.shape, zeros.dtype),
        input_output_aliases={2: 0},
        compiler_params=pltpu.CompilerParams(dimension_semantics=("arbitrary",)),
        name="scatter_rows",
    )(pos, src, zeros)


def _max_tiles(t, tm=ROW_TILE):
    return t * TOP_K_FINE // tm + N_EXPERTS


def _sort_plan(ri, tm=ROW_TILE):
    t = ri.shape[0]
    k = TOP_K_FINE
    a = t * k
    nt = _max_tiles(t, tm)
    meta, cnt = _rank(ri)
    counts = cnt[:, 0]
    padded = ((counts + tm - 1) // tm) * tm
    off_end = jnp.cumsum(padded)
    off = off_end - padded
    ef = meta[:, 0:k, :]
    eid = jnp.arange(N_EXPERTS, dtype=jnp.int32)
    pos = jnp.sum(jnp.where(ef[..., None] == eid, off, 0), axis=-1) + meta[:, k:2 * k, :]
    pos = jnp.transpose(pos, (0, 2, 1)).reshape(a)
    n_tiles = off_end[-1] // tm
    tile_idx = jnp.arange(nt, dtype=jnp.int32)
    te = jnp.sum((off_end[None, :] <= (tile_idx * tm)[:, None]).astype(jnp.int32), axis=1)
    te = jnp.minimum(te, N_EXPERTS - 1)
    te = jnp.where(tile_idx < n_tiles, te, te[n_tiles - 1]).astype(jnp.int32)
    first = ((tile_idx == 0) | (te != jnp.roll(te, 1))) & (tile_idx < n_tiles)
    slot = (jnp.cumsum(first.astype(jnp.int32)) - 1) % WEIGHT_SLOTS
    eid = jnp.arange(N_EXPERTS, dtype=jnp.int32)
    later = (eid[None, :] > eid[:, None]) & (counts[None, :] > 0)
    next_e = jnp.min(jnp.where(later, eid[None, :], N_EXPERTS), axis=1)
    next_e = jnp.where(next_e < N_EXPERTS, next_e, -1)
    next2_e = jnp.where(next_e >= 0, next_e[jnp.maximum(next_e, 0)], -1)
    return dict(
        next2_expert=next2_e[te].astype(jnp.int32),
        pos=pos.astype(jnp.int32), tile_expert=te,
        n_tiles=n_tiles.reshape(1).astype(jnp.int32), first=first.astype(jnp.int32),
        slot=slot.astype(jnp.int32), next_expert=next_e[te].astype(jnp.int32))


def kernel(x, norm1_g, w_in, lambda_q1, lambda_k1, lambda_q2, lambda_k2, subln_g, pool_w, pool_scale, w_out, norm2_g, w_coarse, b_coarse, w_fine, b_fine, w_gate, w_up, w_down, final_norm_g):
    b, s, d = x.shape
    t = b * s
    assert norm1_g.shape[0] == 1
    x2 = x.reshape(t, d)
    proj, pool, v_t = _inproj(x2, norm1_g[0], w_in[0], pool_w[0], pool_scale[0], s)
    proj3 = proj.reshape(b, s, proj.shape[1])
    attn = _attention(proj3, v_t, lambda_q1, lambda_k1, lambda_q2, lambda_k2, subln_g[0])

    ng, epg = N_EXPERT_GROUPS, EXPERTS_PER_GROUP
    wr = jnp.concatenate(
        [w_coarse[0], jnp.transpose(w_fine[0], (1, 0, 2)).reshape(d, ng * epg)], axis=1)
    wr = jnp.pad(wr, ((0, 0), (0, LANES - wr.shape[1])))
    br = jnp.concatenate([b_coarse[0], b_fine[0].reshape(ng * epg)])
    br = jnp.pad(br, (0, LANES - br.shape[0])).reshape(1, LANES)
    wr_hi = wr.astype(BF16)
    wr_lo = (wr - wr_hi.astype(F32)).astype(BF16)
    sorted_rows = _max_tiles(t) * ROW_TILE
    h, hn, ri, rw, xz = _outproj(attn.reshape(t, ATTN_WIDTH), pool, x2, w_out[0], norm2_g[0],
                                 jnp.concatenate([wr_hi, wr_lo], axis=1), br, sorted_rows)

    plan = _sort_plan(ri)
    xs = _scatter_rows(plan["pos"], hn, xz)
    ys = _experts(plan, xs, w_gate[0], w_up[0], w_down[0])
    out = _combine(plan["pos"], h, rw, ys, final_norm_g)
    return out.reshape(b, s, d)
```
